```python
import math
import jax, jax.numpy as jnp
from jax import lax
import numpy as np

D_MODEL = 2048
BATCH = 2
SEQ = 4096
DEPTH = 2
DEC_BATCH = 8
DEC_SEQ = 8
PAST_LEN = 16384
PAGE_SIZE = 128

HEAD_DIM = 128
HEADS_PER_GROUP = 8
WINDOWS = ((128, 1), (512, 4), (2048, 16))
N_ATT_GROUPS = len(WINDOWS)
N_ATT_HEADS = N_ATT_GROUPS * HEADS_PER_GROUP
ATT_WIDTH = HEADS_PER_GROUP * HEAD_DIM
QKV_WIDTH = N_ATT_HEADS * HEAD_DIM
Q_BLOCK = 128
SSM_WIDTH = D_MODEL // 2
SSM_GROUP = 16
N_SSM_GROUPS = SSM_WIDTH // SSM_GROUP
SSM_STATE = 64
IN_WIDTH = SSM_WIDTH + 3 * QKV_WIDTH + 2 * D_MODEL
FFN_HIDDEN = -(-8 * D_MODEL // (3 * 256)) * 256
N_BUCKETS = 32
REL_MAX_DISTANCE = 2048
EPS = 1e-6
NEG_INF = -1e30

kernel_name = "hybrid_s5_dilated_swa_decoder_step"


def rmsnorm(x, g):
    xf = x.astype(jnp.float32)
    y = xf * lax.rsqrt(jnp.mean(xf * xf, axis=-1, keepdims=True) + EPS) * g.astype(jnp.float32)
    return y.astype(x.dtype)


def rel_bucket(dist):
    max_exact = N_BUCKETS // 2
    n = jnp.maximum(dist, 0)
    nf = jnp.maximum(n, 1).astype(jnp.float32)
    large = max_exact + (jnp.log(nf / max_exact) / math.log(REL_MAX_DISTANCE / max_exact)
                         * (N_BUCKETS - max_exact)).astype(jnp.int32)
    large = jnp.minimum(large, N_BUCKETS - 1)
    return jnp.where(n < max_exact, n, large)


def buf_len(window):
    return min(window, PAST_LEN)


def tail_rows(t, rows):
    T = t.shape[1]
    if T < rows:
        t = jnp.pad(t, ((0, 0), (rows - T, 0), (0, 0), (0, 0)))
    return t[:, t.shape[1] - rows:]


def ssm_combine(left, right):
    a_l, b_l = left
    a_r, b_r = right
    return a_r * a_l, a_r * b_l + b_r


def ssm_branch(u, state, p):
    f32 = jnp.float32
    Bt, T, _ = u.shape
    lam = lax.complex(p["ssm_lambda_re"].astype(f32), p["ssm_lambda_im"].astype(f32))
    delta = jnp.exp(p["ssm_log_dt"].astype(f32))[:, None]
    lam_bar = jnp.exp(lam * delta)
    b_bar = ((lam_bar - 1.0) / lam)[..., None] * lax.complex(p["ssm_b_re"].astype(f32),
                                                             p["ssm_b_im"].astype(f32))
    uf = u.astype(f32)
    ug = uf.reshape(Bt, T, N_SSM_GROUPS, SSM_GROUP)
    bu = lax.complex(jnp.einsum('gpc,btgc->btgp', jnp.real(b_bar), ug),
                     jnp.einsum('gpc,btgc->btgp', jnp.imag(b_bar), ug))
    if state is not None:
        h0 = lax.complex(state[0].astype(f32), state[1].astype(f32))
        bu = bu.at[:, 0].add(lam_bar * h0)
    a = jnp.broadcast_to(lam_bar, bu.shape)
    _, hs = lax.associative_scan(ssm_combine, (a, bu), axis=1)
    y = (jnp.einsum('gcp,btgp->btgc', p["ssm_c_re"].astype(f32), jnp.real(hs))
         - jnp.einsum('gcp,btgp->btgc', p["ssm_c_im"].astype(f32), jnp.imag(hs)))
    y = y.reshape(Bt, T, SSM_WIDTH) + p["ssm_d"].astype(f32) * uf
    z = jax.nn.gelu(y)
    out = z * jax.nn.sigmoid(z @ p["w_glu"].astype(f32) + p["b_glu"].astype(f32))
    h_last = hs[:, -1]
    return out.astype(u.dtype), jnp.real(h_last), jnp.imag(h_last)


def dilated_attn_prompt(q, k, v, bias_tab, window, dil):
    Bt, T, H, E = q.shape
    K = window // dil + 1
    M = T // dil
    nb = -(-(K - 1) // Q_BLOCK)
    nblk = -(-M // Q_BLOCK)
    Mp = nblk * Q_BLOCK
    KB = (nb + 1) * Q_BLOCK

    def to_classes(t):
        return t.reshape(Bt, M, dil, H, E).transpose(0, 2, 1, 3, 4)

    def key_blocks(t):
        tp = jnp.pad(t, ((0, 0), (0, 0), (nb * Q_BLOCK, Mp - M), (0, 0), (0, 0)))
        tp = tp.reshape(Bt, dil, nblk + nb, Q_BLOCK, H, E)
        return jnp.concatenate([tp[:, :, j:j + nblk] for j in range(nb + 1)], axis=3)

    qb = jnp.pad(to_classes(q), ((0, 0), (0, 0), (0, Mp - M), (0, 0), (0, 0)))
    qb = qb.reshape(Bt, dil, nblk, Q_BLOCK, H, E)
    kb = key_blocks(to_classes(k))
    vb = key_blocks(to_classes(v))
    logits = jnp.einsum('brnqhe,brnkhe->brnhqk', qb, kb) * (E ** -0.5)
    a_idx = jnp.arange(Q_BLOCK)[:, None]
    c_idx = jnp.arange(KB)[None, :]
    rel = a_idx - c_idx + nb * Q_BLOCK
    key_m = jnp.arange(nblk)[:, None, None] * Q_BLOCK - nb * Q_BLOCK + c_idx[None]
    valid = (rel >= 0)[None] & (rel < K)[None] & (key_m >= 0)
    bias = bias_tab[rel_bucket(jnp.clip(rel, 0, K - 1) * dil)]
    logits = logits + bias.transpose(2, 0, 1)
    logits = jnp.where(valid[None, None, :, None], logits, NEG_INF)
    lse = jax.nn.logsumexp(logits, axis=-1)
    probs = jnp.exp(logits - lse[..., None])
    out = jnp.einsum('brnhqk,brnkhe->brnqhe', probs, vb)
    out = out.reshape(Bt, dil, Mp, H, E)[:, :, :M].transpose(0, 2, 1, 3, 4).reshape(Bt, T, H, E)
    lse = lse.transpose(0, 1, 2, 4, 3).reshape(Bt, dil, Mp, H)[:, :, :M]
    lse = lse.transpose(0, 2, 1, 3).reshape(Bt, T, H)
    return out, lse


def dilated_attn_sample(q, k_all, v_all, bias_tab, window, dil):
    Bt, S, H, E = q.shape
    L = k_all.shape[1] - S
    K = window // dil + 1
    steps = jnp.arange(K)
    idx = L + jnp.arange(S)[:, None] - steps[None, :] * dil
    valid = idx >= 0
    idx_c = jnp.clip(idx, 0, L + S - 1).reshape(-1)
    kg = jnp.take(k_all, idx_c, axis=1).reshape(Bt, S, K, H, E)
    vg = jnp.take(v_all, idx_c, axis=1).reshape(Bt, S, K, H, E)
    logits = jnp.einsum('bshe,bskhe->bhsk', q, kg) * (E ** -0.5)
    bias = bias_tab[rel_bucket(steps * dil)]
    logits = logits + bias.T[None, :, None, :]
    logits = jnp.where(valid[None, None], logits, NEG_INF)
    lse = jax.nn.logsumexp(logits, axis=-1)
    probs = jnp.exp(logits - lse[..., None])
    out = jnp.einsum('bhsk,bskhe->bshe', probs, vg)
    return out, lse.transpose(0, 2, 1)


def trunk_layer(x, ssm_state, kv_bufs, rel_bias, p):
    f32 = jnp.float32
    Bt, T, _ = x.shape
    dt = x.dtype
    h = rmsnorm(x, p["norm_mix"])
    proj = h @ p["w_in"]
    o = SSM_WIDTH
    u = proj[..., :o]
    q = proj[..., o:o + QKV_WIDTH].reshape(Bt, T, N_ATT_GROUPS, HEADS_PER_GROUP, HEAD_DIM)
    k = proj[..., o + QKV_WIDTH:o + 2 * QKV_WIDTH].reshape(Bt, T, N_ATT_GROUPS, HEADS_PER_GROUP, HEAD_DIM)
    v = proj[..., o + 2 * QKV_WIDTH:o + 3 * QKV_WIDTH].reshape(Bt, T, N_ATT_GROUPS, HEADS_PER_GROUP, HEAD_DIM)
    og = o + 3 * QKV_WIDTH
    gate_a = jax.nn.sigmoid(proj[..., og:og + D_MODEL].astype(f32))
    gate_b = jax.nn.sigmoid(proj[..., og + D_MODEL:og + 2 * D_MODEL].astype(f32))

    ya, h_re, h_im = ssm_branch(u, ssm_state, p)

    q = rmsnorm(q, p["q_norm"]).astype(f32)
    k = rmsnorm(k, p["k_norm"])
    outs, lses, new_bufs = [], [], []
    for g, (window, dil) in enumerate(WINDOWS):
        bias_g = rel_bias[:, g * HEADS_PER_GROUP:(g + 1) * HEADS_PER_GROUP].astype(f32)
        kg, vg = k[:, :, g], v[:, :, g]
        if kv_bufs is None:
            out_g, lse_g = dilated_attn_prompt(q[:, :, g], kg.astype(f32), vg.astype(f32), bias_g, window, dil)
            rows = buf_len(window)
            new_bufs.append((tail_rows(kg, rows), tail_rows(vg, rows)))
        else:
            k_buf, v_buf = kv_bufs[g]
            rows = k_buf.shape[1]
            k_all = jnp.concatenate([k_buf.astype(dt), kg], axis=1)
            v_all = jnp.concatenate([v_buf.astype(dt), vg], axis=1)
            out_g, lse_g = dilated_attn_sample(q[:, :, g], k_all.astype(f32), v_all.astype(f32), bias_g, window, dil)
            new_bufs.append((k_all[:, k_all.shape[1] - rows:], v_all[:, v_all.shape[1] - rows:]))
        outs.append(out_g)
        lses.append(lse_g)
    alpha = jax.nn.softmax(jnp.stack(lses, axis=0), axis=0)[..., None]
    yb = jnp.sum(alpha * jnp.stack(outs, axis=0), axis=0).reshape(Bt, T, ATT_WIDTH).astype(dt)

    mix = gate_a * (ya @ p["w_branch_a"]).astype(f32) + gate_b * (yb @ p["w_branch_b"]).astype(f32)
    x = x + (mix.astype(dt) @ p["w_out"])
    h2 = rmsnorm(x, p["norm_ffn"])
    x = x + (jax.nn.silu(h2 @ p["w_ffn_gate"]) * (h2 @ p["w_ffn_up"])) @ p["w_ffn_down"]
    return x, h_re, h_im, new_bufs


def setup_inputs(seed: int = 0) -> dict:
    key = jax.random.key(seed)
    ks = jax.random.split(key, 40)
    f32 = jnp.float32

    def nrm(i, shape, scale):
        return scale * jax.random.normal(ks[i], shape, f32)

    G, P = N_SSM_GROUPS, SSM_STATE
    cshape = lambda w: (DEPTH, DEC_BATCH, buf_len(w), HEADS_PER_GROUP, HEAD_DIM)
    lam_im = math.pi * jnp.arange(P, dtype=f32)[None, None, :] + nrm(14, (DEPTH, G, P), 0.01)
    return {
        "x_prompt": nrm(0, (BATCH, SEQ, D_MODEL), 1.0),
        "x_sample": nrm(1, (DEC_BATCH, DEC_SEQ, D_MODEL), 1.0),
        "state_ssm_re": nrm(2, (DEPTH, DEC_BATCH, G, P), 0.1),
        "state_ssm_im": nrm(3, (DEPTH, DEC_BATCH, G, P), 0.1),
        "cache_k_w128": nrm(4, cshape(128), 1.0),
        "cache_v_w128": nrm(5, cshape(128), 1.0),
        "cache_k_w512": nrm(6, cshape(512), 1.0),
        "cache_v_w512": nrm(7, cshape(512), 1.0),
        "cache_k_w2048": nrm(8, cshape(2048), 1.0),
        "cache_v_w2048": nrm(9, cshape(2048), 1.0),
        "rel_bias": nrm(10, (N_BUCKETS, N_ATT_HEADS), 0.5),
        "norm_mix": 1.0 + nrm(11, (DEPTH, D_MODEL), 0.02),
        "norm_ffn": 1.0 + nrm(12, (DEPTH, D_MODEL), 0.02),
        "q_norm": 1.0 + nrm(15, (DEPTH, HEAD_DIM), 0.02),
        "k_norm": 1.0 + nrm(16, (DEPTH, HEAD_DIM), 0.02),
        "w_in": nrm(17, (DEPTH, D_MODEL, IN_WIDTH), D_MODEL ** -0.5),
        "ssm_lambda_re": -0.5 + nrm(13, (DEPTH, G, P), 0.01),
        "ssm_lambda_im": lam_im,
        "ssm_log_dt": jax.random.uniform(ks[18], (DEPTH, G), f32, math.log(1e-3), math.log(1e-1)),
        "ssm_b_re": nrm(19, (DEPTH, G, P, SSM_GROUP), (2 * SSM_GROUP) ** -0.5),
        "ssm_b_im": nrm(20, (DEPTH, G, P, SSM_GROUP), (2 * SSM_GROUP) ** -0.5),
        "ssm_c_re": nrm(21, (DEPTH, G, SSM_GROUP, P), SSM_STATE ** -0.5),
        "ssm_c_im": nrm(22, (DEPTH, G, SSM_GROUP, P), SSM_STATE ** -0.5),
        "ssm_d": nrm(23, (DEPTH, SSM_WIDTH), 1.0),
        "w_glu": nrm(24, (DEPTH, SSM_WIDTH, SSM_WIDTH), SSM_WIDTH ** -0.5),
        "b_glu": nrm(25, (DEPTH, SSM_WIDTH), 0.01),
        "w_branch_a": nrm(26, (DEPTH, SSM_WIDTH, D_MODEL), SSM_WIDTH ** -0.5),
        "w_branch_b": nrm(27, (DEPTH, ATT_WIDTH, D_MODEL), ATT_WIDTH ** -0.5),
        "w_out": nrm(28, (DEPTH, D_MODEL, D_MODEL), D_MODEL ** -0.5),
        "w_ffn_gate": nrm(29, (DEPTH, D_MODEL, FFN_HIDDEN), D_MODEL ** -0.5),
        "w_ffn_up": nrm(30, (DEPTH, D_MODEL, FFN_HIDDEN), D_MODEL ** -0.5),
        "w_ffn_down": nrm(31, (DEPTH, FFN_HIDDEN, D_MODEL), FFN_HIDDEN ** -0.5),
    }


def reference(x_prompt, x_sample, state_ssm_re, state_ssm_im, cache_k_w128, cache_v_w128,
              cache_k_w512, cache_v_w512, cache_k_w2048, cache_v_w2048, rel_bias, norm_mix, norm_ffn,
              q_norm, k_norm, w_in, ssm_lambda_re, ssm_lambda_im, ssm_log_dt, ssm_b_re, ssm_b_im,
              ssm_c_re, ssm_c_im, ssm_d, w_glu, b_glu, w_branch_a, w_branch_b, w_out,
              w_ffn_gate, w_ffn_up, w_ffn_down):
    caches_k = (cache_k_w128, cache_k_w512, cache_k_w2048)
    caches_v = (cache_v_w128, cache_v_w512, cache_v_w2048)
    xp, xs = x_prompt, x_sample
    p_re, p_im, s_re, s_im = [], [], [], []
    p_k = [[] for _ in WINDOWS]
    p_v = [[] for _ in WINDOWS]
    s_k = [[] for _ in WINDOWS]
    s_v = [[] for _ in WINDOWS]
    for l in range(DEPTH):
        prm = {
            "norm_mix": norm_mix[l], "norm_ffn": norm_ffn[l], "q_norm": q_norm[l], "k_norm": k_norm[l],
            "w_in": w_in[l], "ssm_lambda_re": ssm_lambda_re[l], "ssm_lambda_im": ssm_lambda_im[l],
            "ssm_log_dt": ssm_log_dt[l], "ssm_b_re": ssm_b_re[l], "ssm_b_im": ssm_b_im[l],
            "ssm_c_re": ssm_c_re[l], "ssm_c_im": ssm_c_im[l], "ssm_d": ssm_d[l],
            "w_glu": w_glu[l], "b_glu": b_glu[l], "w_branch_a": w_branch_a[l], "w_branch_b": w_branch_b[l],
            "w_out": w_out[l], "w_ffn_gate": w_ffn_gate[l], "w_ffn_up": w_ffn_up[l], "w_ffn_down": w_ffn_down[l],
        }
        xp, hr, hi, bufs_p = trunk_layer(xp, None, None, rel_bias, prm)
        p_re.append(hr.astype(x_prompt.dtype))
        p_im.append(hi.astype(x_prompt.dtype))
        kv_in = [(caches_k[g][l], caches_v[g][l]) for g in range(N_ATT_GROUPS)]
        xs, hr, hi, bufs_s = trunk_layer(xs, (state_ssm_re[l], state_ssm_im[l]), kv_in, rel_bias, prm)
        s_re.append(hr.astype(state_ssm_re.dtype))
        s_im.append(hi.astype(state_ssm_im.dtype))
        for g in range(N_ATT_GROUPS):
            p_k[g].append(bufs_p[g][0])
            p_v[g].append(bufs_p[g][1])
            s_k[g].append(bufs_s[g][0])
            s_v[g].append(bufs_s[g][1])
    p_state_re, p_state_im = jnp.stack(p_re), jnp.stack(p_im)
    s_state_re, s_state_im = jnp.stack(s_re), jnp.stack(s_im)
    p_k128, p_v128 = jnp.stack(p_k[0]), jnp.stack(p_v[0])
    p_k512, p_v512 = jnp.stack(p_k[1]), jnp.stack(p_v[1])
    p_k2048, p_v2048 = jnp.stack(p_k[2]), jnp.stack(p_v[2])
    s_k128, s_v128 = jnp.stack(s_k[0]), jnp.stack(s_v[0])
    s_k512, s_v512 = jnp.stack(s_k[1]), jnp.stack(s_v[1])
    s_k2048, s_v2048 = jnp.stack(s_k[2]), jnp.stack(s_v[2])
    return (xp, xs, p_state_re, p_state_im, p_k128, p_v128, p_k512, p_v512, p_k2048, p_v2048,
            s_state_re, s_state_im, s_k128, s_v128, s_k512, s_v512, s_k2048, s_v2048)
```

```python
import functools
import math

import numpy as np
import jax
import jax.numpy as jnp
from jax import lax
from jax.experimental import pallas as pl
from jax.experimental.pallas import tpu as pltpu

F32 = jnp.float32
BF16 = jnp.bfloat16

D_MODEL = 2048
HEAD_DIM = 128
HEADS_PER_GROUP = 8
WINDOWS = ((128, 1), (512, 4), (2048, 16))
N_ATT_GROUPS = len(WINDOWS)
N_ATT_HEADS = N_ATT_GROUPS * HEADS_PER_GROUP
ATT_WIDTH = HEADS_PER_GROUP * HEAD_DIM
QKV_WIDTH = N_ATT_HEADS * HEAD_DIM
Q_BLOCK = 128
SSM_WIDTH = D_MODEL // 2
SSM_GROUP = 16
N_SSM_GROUPS = SSM_WIDTH // SSM_GROUP
SSM_STATE = 64
IN_WIDTH = SSM_WIDTH + 3 * QKV_WIDTH + 2 * D_MODEL
FFN_HIDDEN = 5632
N_BUCKETS = 32
REL_MAX_DISTANCE = 2048
EPS = 1e-6
NEG_INF = -1e30

LANES = 128
SUBLANES = 8
VMEM_LIMIT_BYTES = 56 * 1024 * 1024

SSM_CHUNK = 8
SSM_RANGE_GROUPS = LANES // SSM_GROUP
N_SSM_RANGES = N_SSM_GROUPS // SSM_RANGE_GROUPS
RANGE_STATE = SSM_RANGE_GROUPS * SSM_STATE
COL_BLOCK = 1024
FFN_BLOCK = 512


def _cparams(sem):
    return pltpu.CompilerParams(dimension_semantics=sem, vmem_limit_bytes=VMEM_LIMIT_BYTES)


def _dot(a, b):
    return jnp.dot(a, b, preferred_element_type=F32)


def _dot_nt(a, b):
    return lax.dot_general(a, b, (((1,), (1,)), ((), ())), preferred_element_type=F32)


def _ssm_prep_kernel(lre_ref, lim_ref, ldt_ref, btre_ref, btim_ref, cre_ref, cim_ref,
                     kp_ref, bpre_ref, bpim_ref, are_ref, aim_ref, lnre_ref, lnim_ref):
    L = SSM_CHUNK
    lr = lre_ref[...]
    li = lim_ref[...]
    dt = jnp.exp(ldt_ref[...])
    er = jnp.exp(lr * dt)
    lbr = er * jnp.cos(li * dt)
    lbi = er * jnp.sin(li * dt)
    nr = lbr - 1.0
    dd = lr * lr + li * li
    rr = (nr * lr + lbi * li) / dd
    ri = (lbi * lr - nr * li) / dd
    btr = btre_ref[...]
    bti = btim_ref[...]
    bbr = rr * btr - ri * bti
    bbi = rr * bti + ri * btr
    cr = cre_ref[...]
    ci = cim_ref[...]
    pr = [jnp.ones_like(lbr)]
    pi = [jnp.zeros_like(lbr)]
    for _ in range(L):
        pr.append(pr[-1] * lbr - pi[-1] * lbi)
        pi.append(pr[-2] * lbi + pi[-1] * lbr)
    xr = [cr * pr[t] - ci * pi[t] for t in range(L + 1)]
    xi = [cr * pi[t] + ci * pr[t] for t in range(L + 1)]
    xr_k = jnp.concatenate(xr[:L], axis=0)
    xi_k = jnp.concatenate(xi[:L], axis=0)
    hp = lax.Precision.HIGHEST
    kp = (lax.dot_general(bbr, xr_k, (((1,), (1,)), ((), ())), precision=hp, preferred_element_type=F32)
          - lax.dot_general(bbi, xi_k, (((1,), (1,)), ((), ())), precision=hp, preferred_element_type=F32))
    kp_ref[...] = kp
    are_ref[...] = jnp.concatenate(xr[1:], axis=0)
    aim_ref[...] = jnp.concatenate(xi[1:], axis=0)
    bpre_ref[...] = jnp.concatenate([pr[L - 1 - j] * bbr - pi[L - 1 - j] * bbi for j in range(L)], axis=0)
    bpim_ref[...] = jnp.concatenate([pr[L - 1 - j] * bbi + pi[L - 1 - j] * bbr for j in range(L)], axis=0)
    lnre_ref[...] = pr[L]
    lnim_ref[...] = pi[L]


def _ssm_prep(lam_re, lam_im, log_dt, b_re, b_im, c_re, c_im):
    G, P, C, L = N_SSM_GROUPS, SSM_STATE, SSM_GROUP, SSM_CHUNK
    row = lambda a: a.reshape(G, 1, -1)
    bt_re = jnp.swapaxes(b_re, 1, 2)
    bt_im = jnp.swapaxes(b_im, 1, 2)
    vec = pl.BlockSpec((None, 1, P), lambda g: (g, 0, 0))
    mat = pl.BlockSpec((None, C, P), lambda g: (g, 0, 0))
    big = pl.BlockSpec((None, L * C, P), lambda g: (g, 0, 0))
    kp, bpre, bpim, are, aim, lnre, lnim = pl.pallas_call(
        _ssm_prep_kernel,
        grid=(G,),
        in_specs=[vec, vec, pl.BlockSpec((None, 1, 1), lambda g: (g, 0, 0)), mat, mat, mat, mat],
        out_specs=[pl.BlockSpec((None, C, L * C), lambda g: (g, 0, 0)), big, big, big, big, vec, vec],
        out_shape=[jax.ShapeDtypeStruct((G, C, L * C), F32)] + [jax.ShapeDtypeStruct((G, L * C, P), F32)] * 4
                  + [jax.ShapeDtypeStruct((G, 1, P), F32)] * 2,
        compiler_params=_cparams(("arbitrary",)),
        name="ssm_prep",
    )(row(lam_re), row(lam_im), log_dt.reshape(G, 1, 1), bt_re, bt_im, c_re, c_im)

    R, GL = N_SSM_RANGES, SSM_RANGE_GROUPS
    eye = jnp.eye(GL, dtype=F32)
    tau = np.arange(L)[None, :] - np.arange(L)[:, None]
    tmask = jnp.asarray((tau >= 0).astype(np.float32))
    kp6 = kp.reshape(R, GL, C, L, C)
    ksel = kp6[:, :, :, np.clip(tau, 0, L - 1), :] * tmask[None, None, None, :, :, None]
    ktoe = (jnp.transpose(ksel, (0, 3, 1, 2, 4, 5))[:, :, :, :, :, None, :]
            * eye[None, None, :, None, None, :, None])
    ktoe = ktoe.reshape(R, L * LANES, L * LANES).astype(BF16)

    def bcat_part(bp):
        b5 = bp.reshape(R, GL, L, C, P)
        b5 = jnp.transpose(b5, (0, 2, 1, 3, 4))
        out = b5[:, :, :, :, None, :] * eye[None, None, :, None, :, None]
        return out.reshape(R, L * LANES, RANGE_STATE)
    bcat = jnp.concatenate([bcat_part(bpre), bcat_part(bpim)], axis=-1).astype(BF16)

    def acat_part(a):
        a5 = a.reshape(R, GL, L, C, P)
        a5 = jnp.transpose(a5, (0, 1, 4, 2, 3))
        out = a5[:, :, :, :, None, :] * eye[None, :, None, None, :, None]
        return out.reshape(R, RANGE_STATE, L * LANES)
    acat = jnp.concatenate([acat_part(are), -acat_part(aim)], axis=1).astype(BF16)

    half = RANGE_STATE // LANES
    lnr_t = lnre.reshape(R, half, LANES)
    lni_t = lnim.reshape(R, half, LANES)
    lnr = jnp.concatenate([lnr_t, lnr_t], axis=1)
    lni = jnp.concatenate([-lni_t, lni_t], axis=1)
    return ktoe, bcat, acat, lnr, lni


def _head_norm(res, gain):
    outs = []
    for h in range(HEADS_PER_GROUP):
        t = res[:, h * HEAD_DIM:(h + 1) * HEAD_DIM]
        ms = jnp.mean(t * t, axis=-1, keepdims=True)
        outs.append(t * lax.rsqrt(ms + EPS) * gain)
    return outs


def _in_proj_kernel(x_ref, g_ref, w_ref, qn_ref, kn_ref, u_ref, qkv_ref, kvt_ref, gate_ref, h_scr):
    j = pl.program_id(1)

    @pl.when(j == 0)
    def _():
        x = x_ref[...]
        ms = jnp.mean(x * x, axis=-1, keepdims=True)
        h_scr[...] = (x * lax.rsqrt(ms + EPS) * g_ref[...]).astype(BF16)

    res = _dot(h_scr[...], w_ref[...])

    @pl.when(j == 0)
    def _():
        u_ref[...] = res

    @pl.when((j >= 1) & (j <= 3))
    def _():
        for h, t in enumerate(_head_norm(res, qn_ref[...])):
            qkv_ref[h] = t

    @pl.when((j >= 4) & (j <= 6))
    def _():
        for h, t in enumerate(_head_norm(res, kn_ref[...])):
            qkv_ref[h] = t
            kvt_ref[:, h * HEAD_DIM:(h + 1) * HEAD_DIM] = t

    @pl.when((j >= 7) & (j <= 9))
    def _():
        for h in range(HEADS_PER_GROUP):
            qkv_ref[h] = res[:, h * HEAD_DIM:(h + 1) * HEAD_DIM]
        kvt_ref[...] = res

    @pl.when(j >= 10)
    def _():
        gate_ref[...] = jax.nn.sigmoid(res)


def _in_proj(x, gain, w_bf16, q_gain, k_gain, tm):
    M = x.shape[0]
    nj = IN_WIDTH // COL_BLOCK
    n_qkv = 3 * N_ATT_GROUPS
    return pl.pallas_call(
        _in_proj_kernel,
        grid=(M // tm, nj),
        in_specs=[
            pl.BlockSpec((tm, D_MODEL), lambda i, j: (i, 0)),
            pl.BlockSpec((1, D_MODEL), lambda i, j: (0, 0)),
            pl.BlockSpec((D_MODEL, COL_BLOCK), lambda i, j: (0, j)),
            pl.BlockSpec((1, HEAD_DIM), lambda i, j: (0, 0)),
            pl.BlockSpec((1, HEAD_DIM), lambda i, j: (0, 0)),
        ],
        out_specs=[
            pl.BlockSpec((tm, COL_BLOCK), lambda i, j: (i, 0)),
            pl.BlockSpec((HEADS_PER_GROUP, tm, HEAD_DIM), lambda i, j: (jnp.clip(j - 1, 0, n_qkv - 1), i, 0)),
            pl.BlockSpec((tm, COL_BLOCK), lambda i, j: (i, jnp.clip(j - 4, 0, 5))),
            pl.BlockSpec((tm, COL_BLOCK), lambda i, j: (i, jnp.clip(j - 10, 0, 3))),
        ],
        out_shape=[
            jax.ShapeDtypeStruct((M, SSM_WIDTH), F32),
            jax.ShapeDtypeStruct((n_qkv * HEADS_PER_GROUP, M, HEAD_DIM), F32),
            jax.ShapeDtypeStruct((M, 2 * QKV_WIDTH), F32),
            jax.ShapeDtypeStruct((M, 2 * D_MODEL), F32),
        ],
        scratch_shapes=[pltpu.VMEM((tm, D_MODEL), BF16)],
        compiler_params=_cparams(("arbitrary", "arbitrary")),
        name="in_proj",
    )(x, gain.reshape(1, D_MODEL), w_bf16, q_gain.reshape(1, HEAD_DIM), k_gain.reshape(1, HEAD_DIM))


def _ssm_chunk_kernel(nseq, nk, u_ref, h0_ref, ktoe_ref, bcat_ref, acat_ref, lnr_ref, lni_ref,
                      y_ref, hfin_ref, s_scr):
    L = SSM_CHUNK
    rows = nseq * nk
    nsub = 2 * RANGE_STATE // LANES
    ucat = jnp.concatenate([u_ref[pl.ds(j, rows, stride=L), :] for j in range(L)], axis=-1).astype(BF16)
    s = _dot(ucat, bcat_ref[...])
    for n in range(nsub):
        s_scr[pl.ds(n, rows, stride=nsub), :] = s[:, n * LANES:(n + 1) * LANES]
    y_ref_intra = _dot(ucat, ktoe_ref[...])

    lnr = lnr_ref[...]
    lni = lni_ref[...]

    def step(k, hs):
        new = []
        for q in range(nseq):
            off = pl.multiple_of((q * nk + k) * nsub, nsub)
            h = hs[q]
            sk = s_scr[pl.ds(off, nsub), :]
            s_scr[pl.ds(off, nsub), :] = h
            new.append(h * lnr + pltpu.roll(h, nsub // 2, axis=0) * lni + sk)
        return tuple(new)

    hs = lax.fori_loop(0, nk, step, tuple(h0_ref[q] for q in range(nseq)))
    for q in range(nseq):
        hfin_ref[q] = hs[q]

    hprev = jnp.concatenate([s_scr[pl.ds(n, rows, stride=nsub), :] for n in range(nsub)], axis=-1).astype(BF16)
    y = y_ref_intra + _dot(hprev, acat_ref[...])
    for j in range(L):
        y_ref[pl.ds(j, rows, stride=L), :] = y[:, j * LANES:(j + 1) * LANES]


def _ssm_chunk(u, h0, ktoe, bcat, acat, lnr, lni, nseq_total, seq_per_step):
    M = u.shape[0]
    T = M // nseq_total
    nk = T // SSM_CHUNK
    nsteps = nseq_total // seq_per_step
    R = N_SSM_RANGES
    nsub = 2 * RANGE_STATE // LANES
    tm = seq_per_step * T
    wspec = lambda shape: pl.BlockSpec((None,) + shape, lambda r, b: (r, 0, 0))
    return pl.pallas_call(
        functools.partial(_ssm_chunk_kernel, seq_per_step, nk),
        grid=(R, nsteps),
        in_specs=[
            pl.BlockSpec((tm, LANES), lambda r, b: (b, r)),
            pl.BlockSpec((seq_per_step, None, nsub, LANES), lambda r, b: (b, r, 0, 0)),
            wspec((SSM_CHUNK * LANES, SSM_CHUNK * LANES)),
            wspec((SSM_CHUNK * LANES, 2 * RANGE_STATE)),
            wspec((2 * RANGE_STATE, SSM_CHUNK * LANES)),
            wspec((nsub, LANES)),
            wspec((nsub, LANES)),
        ],
        out_specs=[
            pl.BlockSpec((tm, LANES), lambda r, b: (b, r)),
            pl.BlockSpec((seq_per_step, None, nsub, LANES), lambda r, b: (b, r, 0, 0)),
        ],
        out_shape=[
            jax.ShapeDtypeStruct((M, SSM_WIDTH), F32),
            jax.ShapeDtypeStruct((nseq_total, R, nsub, LANES), F32),
        ],
        scratch_shapes=[pltpu.VMEM((seq_per_step * nk * nsub, LANES), F32)],
        compiler_params=_cparams(("arbitrary", "arbitrary")),
        name="ssm_chunk",
    )(u, h0, ktoe, bcat, acat, lnr, lni)


def _state_to_tiles(re, im):
    N = re.shape[0]
    half = RANGE_STATE // LANES
    return jnp.concatenate([re.reshape(N, N_SSM_RANGES, half, LANES),
                            im.reshape(N, N_SSM_RANGES, half, LANES)], axis=2)


def _tiles_to_state(t):
    N = t.shape[0]
    half = RANGE_STATE // LANES
    return (t[:, :, :half].reshape(N, N_SSM_GROUPS, SSM_STATE),
            t[:, :, half:].reshape(N, N_SSM_GROUPS, SSM_STATE))


def _bucket_np(dist):
    max_exact = N_BUCKETS // 2
    n = np.maximum(dist, 0)
    nf = np.maximum(n, 1).astype(np.float64)
    large = max_exact + (np.log(nf / max_exact) / math.log(REL_MAX_DISTANCE / max_exact)
                         * (N_BUCKETS - max_exact)).astype(np.int32)
    large = np.minimum(large, N_BUCKETS - 1)
    return np.where(n < max_exact, n, large)


def _prompt_bias_tiles(rel_bias):
    a = np.arange(Q_BLOCK)[:, None]
    c = np.arange(2 * Q_BLOCK)[None, :]
    rel = a - c + Q_BLOCK
    tiles = []
    for g, (window, dil) in enumerate(WINDOWS):
        K = window // dil + 1
        valid = (rel >= 0) & (rel < K)
        bkt = _bucket_np(np.clip(rel, 0, K - 1) * dil)
        tab = rel_bias[:, g * HEADS_PER_GROUP:(g + 1) * HEADS_PER_GROUP].astype(F32)
        b = jnp.transpose(tab[bkt], (2, 0, 1))
        tiles.append(jnp.where(jnp.asarray(valid)[None], b, NEG_INF))
    return jnp.stack(tiles)


def _attn_prompt_kernel(T, q_ref, k_ref, v_ref, bias_ref, o_ref, m_scr, l_scr, acc_scr):
    g = pl.program_id(2)
    scale = HEAD_DIM ** -0.5
    nblk = T // Q_BLOCK

    def run_group(gi, dil):
        per_class = nblk // dil
        bias = bias_ref[...]
        bias_l = bias[:, :Q_BLOCK]
        bias_r = bias[:, Q_BLOCK:]

        def body(i, carry):
            r = i // per_class
            n = i % per_class
            row_q = r + n * (Q_BLOCK * dil)
            row_p = r + jnp.maximum(n - 1, 0) * (Q_BLOCK * dil)
            sl_q = pl.ds(row_q, Q_BLOCK, stride=dil)
            sl_p = pl.ds(row_p, Q_BLOCK, stride=dil)
            q = q_ref[sl_q, :].astype(BF16)
            kc = k_ref[sl_q, :].astype(BF16)
            kp = k_ref[sl_p, :].astype(BF16)
            s_r = _dot_nt(q, kc) * scale + bias_r
            s_l = _dot_nt(q, kp) * scale + jnp.where(n == 0, NEG_INF, bias_l)
            m = jnp.maximum(jnp.max(s_l, axis=-1, keepdims=True), jnp.max(s_r, axis=-1, keepdims=True))
            p_l = jnp.exp(s_l - m)
            p_r = jnp.exp(s_r - m)
            l = jnp.sum(p_l, axis=-1, keepdims=True) + jnp.sum(p_r, axis=-1, keepdims=True)
            o = (_dot(p_l.astype(BF16), v_ref[sl_p, :].astype(BF16))
                 + _dot(p_r.astype(BF16), v_ref[sl_q, :].astype(BF16)))
            mb = jnp.broadcast_to(m, (Q_BLOCK, HEAD_DIM))
            lb = jnp.broadcast_to(l, (Q_BLOCK, HEAD_DIM))
            if gi == 0:
                m_scr[sl_q, :] = mb
                l_scr[sl_q, :] = lb
                acc_scr[sl_q, :] = o
            else:
                m0 = m_scr[sl_q, :]
                mn = jnp.maximum(m0, mb)
                a0 = jnp.exp(m0 - mn)
                a1 = jnp.exp(mb - mn)
                m_scr[sl_q, :] = mn
                l_scr[sl_q, :] = a0 * l_scr[sl_q, :] + a1 * lb
                acc_scr[sl_q, :] = a0 * acc_scr[sl_q, :] + a1 * o
            return carry

        lax.fori_loop(0, nblk, body, 0)

    for gi, (_, dil) in enumerate(WINDOWS):
        pl.when(g == gi)(functools.partial(run_group, gi, dil))

    @pl.when(g == N_ATT_GROUPS - 1)
    def _():
        o_ref[...] = (acc_scr[...] / l_scr[...]).astype(o_ref.dtype)


def _attn_prompt(qkv, bias_tiles, nbatch):
    M = qkv.shape[1]
    T = M // nbatch
    H, G = HEADS_PER_GROUP, N_ATT_GROUPS

    def qkv_spec(which):
        return pl.BlockSpec((None, T, HEAD_DIM), lambda b, h, g: (which * N_ATT_HEADS + g * H + h, b, 0))

    return pl.pallas_call(
        functools.partial(_attn_prompt_kernel, T),
        grid=(nbatch, H, G),
        in_specs=[qkv_spec(0), qkv_spec(1), qkv_spec(2),
                  pl.BlockSpec((None, None, Q_BLOCK, 2 * Q_BLOCK), lambda b, h, g: (g, h, 0, 0))],
        out_specs=pl.BlockSpec((T, HEAD_DIM), lambda b, h, g: (b, h)),
        out_shape=jax.ShapeDtypeStruct((M, ATT_WIDTH), BF16),
        scratch_shapes=[pltpu.VMEM((T, HEAD_DIM), F32)] * 3,
        compiler_params=_cparams(("arbitrary", "arbitrary", "arbitrary")),
        name="attn_prompt",
    )(qkv, qkv, qkv, bias_tiles)


def _attn_sample_kernel(window, dil, q_ref, ck_ref, cv_ref, nk_ref, nv_ref, bias_ref, o_ref, lse_ref):
    S = q_ref.shape[0]
    K = window // dil + 1
    scale = HEAD_DIM ** -0.5
    bias = bias_ref[...]
    for s in range(S):
        n_c = (window - 1 - s) // dil + 1
        qs = q_ref[s]
        new_rows = [s + j * dil - window for j in range(n_c, K)]
        kk = jnp.concatenate([ck_ref[pl.ds(s, n_c, stride=dil)]] + [nk_ref[pl.ds(i, 1)] for i in new_rows], axis=0)
        vv = jnp.concatenate([cv_ref[pl.ds(s, n_c, stride=dil)]] + [nv_ref[pl.ds(i, 1)] for i in new_rows], axis=0)
        lg = jnp.sum(kk * qs[None], axis=-1, keepdims=True) * scale + bias
        m = jnp.max(lg, axis=0)
        p = jnp.exp(lg - m[None])
        l = jnp.sum(p, axis=0)
        o_ref[s] = jnp.sum(p * vv, axis=0) / l
        lse_ref[s] = m + jnp.log(l)


def _attn_sample(q, cache_k, cache_v, new_k, new_v, bias, layer, window, dil):
    B, S = q.shape[0], q.shape[1]
    H = HEADS_PER_GROUP
    K = window // dil + 1
    small = pl.BlockSpec((None, S, H, HEAD_DIM), lambda b: (b, 0, 0, 0))
    cache = pl.BlockSpec((None, None, window, H, HEAD_DIM), lambda b: (layer, b, 0, 0, 0))
    return pl.pallas_call(
        functools.partial(_attn_sample_kernel, window, dil),
        grid=(B,),
        in_specs=[small, cache, cache, small, small, pl.BlockSpec((K, H, HEAD_DIM), lambda b: (0, 0, 0))],
        out_specs=[small, small],
        out_shape=[jax.ShapeDtypeStruct((B, S, H, HEAD_DIM), F32)] * 2,
        compiler_params=_cparams(("arbitrary",)),
        name="attn_sample_w%d" % window,
    )(q, cache_k, cache_v, new_k, new_v, bias)


def _merge_groups_kernel(o0, o1, o2, l0, l1, l2, y_ref):
    a, b, c = l0[...], l1[...], l2[...]
    m = jnp.maximum(jnp.maximum(a, b), c)
    ea, eb, ec = jnp.exp(a - m), jnp.exp(b - m), jnp.exp(c - m)
    y_ref[...] = ((ea * o0[...] + eb * o1[...] + ec * o2[...]) / (ea + eb + ec)).astype(y_ref.dtype)


def _merge_groups(outs, lses):
    shape = outs[0].shape
    return pl.pallas_call(
        _merge_groups_kernel,
        out_shape=jax.ShapeDtypeStruct(shape, BF16),
        name="merge_groups",
    )(*outs, *lses)


def _sample_bias(rel_bias):
    out = []
    for g, (window, dil) in enumerate(WINDOWS):
        K = window // dil + 1
        steps = (K - 1) - np.arange(K)
        tab = rel_bias[:, g * HEADS_PER_GROUP:(g + 1) * HEADS_PER_GROUP].astype(F32)
        b = tab[_bucket_np(steps * dil)]
        out.append(jnp.broadcast_to(b[:, :, None], (K, HEADS_PER_GROUP, HEAD_DIM)))
    return out


def _gelu_tanh(x):
    return 0.5 * x * (1.0 + jnp.tanh(math.sqrt(2.0 / math.pi) * (x + 0.044715 * (x * x * x))))


def _mix_out_kernel(y_ref, u_ref, yb_ref, ga_ref, gb_ref, x_ref, d_ref, wglu_ref, bglu_ref, wa_ref, wb_ref,
                    wout_ref, gn_ref, x1_ref, h2_ref):
    y = y_ref[...] + d_ref[...] * u_ref[...]
    z = _gelu_tanh(y)
    ya = z * jax.nn.sigmoid(_dot(z.astype(BF16), wglu_ref[...]) + bglu_ref[...])
    mix = (ga_ref[...] * _dot(ya.astype(BF16), wa_ref[...])
           + gb_ref[...] * _dot(yb_ref[...], wb_ref[...]))
    x1 = x_ref[...] + _dot(mix.astype(BF16), wout_ref[...])
    x1_ref[...] = x1
    ms = jnp.mean(x1 * x1, axis=-1, keepdims=True)
    h2_ref[...] = (x1 * lax.rsqrt(ms + EPS) * gn_ref[...]).astype(BF16)


def _mix_out(y_ssm, u, yb, gates, x, ssm_d, wglu, bglu, wa, wb, wout, gain, tm):
    M = x.shape[0]
    row = lambda w: pl.BlockSpec((tm, w), lambda i: (i, 0))
    full = lambda a, b: pl.BlockSpec((a, b), lambda i: (0, 0), pipeline_mode=pl.Buffered(1))
    return pl.pallas_call(
        _mix_out_kernel,
        grid=(M // tm,),
        in_specs=[row(SSM_WIDTH), row(SSM_WIDTH), row(ATT_WIDTH),
                  pl.BlockSpec((tm, D_MODEL), lambda i: (i, 0)),
                  pl.BlockSpec((tm, D_MODEL), lambda i: (i, 1)),
                  row(D_MODEL),
                  full(1, SSM_WIDTH), full(SSM_WIDTH, SSM_WIDTH), full(1, SSM_WIDTH),
                  full(SSM_WIDTH, D_MODEL), full(ATT_WIDTH, D_MODEL), full(D_MODEL, D_MODEL), full(1, D_MODEL)],
        out_specs=[row(D_MODEL), row(D_MODEL)],
        out_shape=[jax.ShapeDtypeStruct((M, D_MODEL), F32), jax.ShapeDtypeStruct((M, D_MODEL), BF16)],
        compiler_params=_cparams(("arbitrary",)),
        name="mix_out",
    )(y_ssm, u, yb, gates, gates, x, ssm_d.reshape(1, -1), wglu, bglu.reshape(1, -1), wa, wb, wout,
      gain.reshape(1, -1))


def _ffn_kernel(h_ref, x_ref, wg_ref, wu_ref, wd_ref, o_ref, acc_ref):
    f = pl.program_id(1)
    h = h_ref[...]
    a = jax.nn.silu(_dot(h, wg_ref[...])) * _dot(h, wu_ref[...])
    part = _dot(a.astype(BF16), wd_ref[...])

    @pl.when(f == 0)
    def _():
        acc_ref[...] = part

    @pl.when(f > 0)
    def _():
        acc_ref[...] += part

    @pl.when(f == pl.num_programs(1) - 1)
    def _():
        o_ref[...] = x_ref[...] + acc_ref[...]


def _ffn(h2, x1, wg, wu, wd, tm):
    M = x1.shape[0]
    nf = FFN_HIDDEN // FFN_BLOCK
    return pl.pallas_call(
        _ffn_kernel,
        grid=(M // tm, nf),
        in_specs=[pl.BlockSpec((tm, D_MODEL), lambda i, f: (i, 0)),
                  pl.BlockSpec((tm, D_MODEL), lambda i, f: (i, 0)),
                  pl.BlockSpec((D_MODEL, FFN_BLOCK), lambda i, f: (0, f)),
                  pl.BlockSpec((D_MODEL, FFN_BLOCK), lambda i, f: (0, f)),
                  pl.BlockSpec((FFN_BLOCK, D_MODEL), lambda i, f: (f, 0))],
        out_specs=pl.BlockSpec((tm, D_MODEL), lambda i, f: (i, 0)),
        out_shape=jax.ShapeDtypeStruct((M, D_MODEL), F32),
        scratch_shapes=[pltpu.VMEM((tm, D_MODEL), F32)],
        compiler_params=_cparams(("arbitrary", "arbitrary")),
        name="ffn",
    )(h2, x1, wg, wu, wd)


def _cache_shift_kernel(*refs):
    n = (len(refs) - 1) // 3
    caches, news, outs, sem = refs[:n], refs[n:2 * n], refs[2 * n:3 * n], refs[3 * n]
    copies = []
    for i in range(n):
        W = caches[i].shape[2]
        S = news[i].shape[2]
        copies.append(pltpu.make_async_copy(caches[i].at[:, :, pl.ds(S, W - S)], outs[i].at[:, :, pl.ds(0, W - S)],
                                            sem.at[2 * i]))
        copies.append(pltpu.make_async_copy(news[i], outs[i].at[:, :, pl.ds(W - S, S)], sem.at[2 * i + 1]))
    for c in copies:
        c.start()
    for c in copies:
        c.wait()


def _cache_shift(caches, news):
    n = len(caches)
    any_spec = pl.BlockSpec(memory_space=pl.ANY)
    return pl.pallas_call(
        _cache_shift_kernel,
        in_specs=[any_spec] * (2 * n),
        out_specs=[any_spec] * n,
        out_shape=[jax.ShapeDtypeStruct(c.shape, c.dtype) for c in caches],
        scratch_shapes=[pltpu.SemaphoreType.DMA((2 * n,))],
        name="cache_shift",
    )(*caches, *news)


def _layer_common(x2d, h0_tiles, nseq, seq_per_step, prm, tm, tm_ffn):
    u, qkv, kvt, gates = _in_proj(x2d, prm["norm_mix"], prm["w_in"], prm["q_norm"], prm["k_norm"], tm)
    y_ssm, hfin = _ssm_chunk(u, h0_tiles, prm["ktoe"], prm["bcat"], prm["acat"], prm["lnr"], prm["lni"],
                             nseq, seq_per_step)
    return u, qkv, kvt, gates, y_ssm, hfin


def _layer_tail(x2d, u, y_ssm, yb, gates, prm, tm_mix, tm_ffn):
    x1, h2 = _mix_out(y_ssm, u, yb, gates, x2d, prm["ssm_d"], prm["w_glu"], prm["b_glu"], prm["w_branch_a"],
                      prm["w_branch_b"], prm["w_out"], prm["norm_ffn"], tm_mix)
    return _ffn(h2, x1, prm["w_ffn_gate"], prm["w_ffn_up"], prm["w_ffn_down"], tm_ffn)


def kernel(x_prompt, x_sample, state_ssm_re, state_ssm_im, cache_k_w128, cache_v_w128, cache_k_w512, cache_v_w512,
           cache_k_w2048, cache_v_w2048, rel_bias, norm_mix, norm_ffn, q_norm, k_norm, w_in, ssm_lambda_re,
           ssm_lambda_im, ssm_log_dt, ssm_b_re, ssm_b_im, ssm_c_re, ssm_c_im, ssm_d, w_glu, b_glu, w_branch_a,
           w_branch_b, w_out, w_ffn_gate, w_ffn_up, w_ffn_down):
    depth = w_in.shape[0]
    B, T, _ = x_prompt.shape
    SB, S, _ = x_sample.shape
    H = HEADS_PER_GROUP
    caches_k = (cache_k_w128, cache_k_w512, cache_k_w2048)
    caches_v = (cache_v_w128, cache_v_w512, cache_v_w2048)

    bias_tiles = _prompt_bias_tiles(rel_bias)
    bias_sample = _sample_bias(rel_bias)

    xp = x_prompt.reshape(B * T, D_MODEL)
    xs = x_sample.reshape(SB * S, D_MODEL)
    p_re, p_im, s_re, s_im = [], [], [], []
    p_k = [[] for _ in WINDOWS]
    p_v = [[] for _ in WINDOWS]
    new_k = [[] for _ in WINDOWS]
    new_v = [[] for _ in WINDOWS]

    for l in range(depth):
        ktoe, bcat, acat, lnr, lni = _ssm_prep(ssm_lambda_re[l], ssm_lambda_im[l], ssm_log_dt[l], ssm_b_re[l],
                                               ssm_b_im[l], ssm_c_re[l], ssm_c_im[l])
        prm = {
            "norm_mix": norm_mix[l], "norm_ffn": norm_ffn[l], "q_norm": q_norm[l], "k_norm": k_norm[l],
            "w_in": w_in[l].astype(BF16), "ssm_d": ssm_d[l], "w_glu": w_glu[l].astype(BF16), "b_glu": b_glu[l],
            "w_branch_a": w_branch_a[l].astype(BF16), "w_branch_b": w_branch_b[l].astype(BF16),
            "w_out": w_out[l].astype(BF16), "w_ffn_gate": w_ffn_gate[l].astype(BF16),
            "w_ffn_up": w_ffn_up[l].astype(BF16), "w_ffn_down": w_ffn_down[l].astype(BF16),
            "ktoe": ktoe, "bcat": bcat, "acat": acat, "lnr": lnr, "lni": lni,
        }

        zeros = jnp.zeros((B, N_SSM_RANGES, 2 * RANGE_STATE // LANES, LANES), F32)
        u, qkv, kvt, gates, y_ssm, hfin = _layer_common(xp, zeros, B, B, prm, 512, 512)
        yb = _attn_prompt(qkv, bias_tiles, B)
        xp = _layer_tail(xp, u, y_ssm, yb, gates, prm, 256, 512)
        hr, hi = _tiles_to_state(hfin)
        p_re.append(hr)
        p_im.append(hi)
        kv5 = kvt.reshape(B, T, 2, N_ATT_GROUPS, H, HEAD_DIM)
        for g, (window, _) in enumerate(WINDOWS):
            p_k[g].append(kv5[:, T - window:, 0, g])
            p_v[g].append(kv5[:, T - window:, 1, g])

        h0 = _state_to_tiles(state_ssm_re[l], state_ssm_im[l])
        u, qkv, kvt, gates, y_ssm, hfin = _layer_common(xs, h0, SB, SB, prm, SB * S, SB * S)
        kv5 = kvt.reshape(SB, S, 2, N_ATT_GROUPS, H, HEAD_DIM)
        q5 = jnp.transpose(qkv[:N_ATT_HEADS].reshape(N_ATT_GROUPS, H, SB, S, HEAD_DIM), (0, 2, 3, 1, 4))
        outs, lses = [], []
        for g, (window, dil) in enumerate(WINDOWS):
            nk_g, nv_g = kv5[:, :, 0, g], kv5[:, :, 1, g]
            new_k[g].append(nk_g)
            new_v[g].append(nv_g)
            o_g, lse_g = _attn_sample(q5[g], caches_k[g], caches_v[g], nk_g, nv_g, bias_sample[g], l, window, dil)
            outs.append(o_g)
            lses.append(lse_g)
        yb = _merge_groups(outs, lses).reshape(SB * S, ATT_WIDTH)
        xs = _layer_tail(xs, u, y_ssm, yb, gates, prm, SB * S, SB * S)
        hr, hi = _tiles_to_state(hfin)
        s_re.append(hr)
        s_im.append(hi)

    shifted = _cache_shift(
        [caches_k[0], caches_v[0], caches_k[1], caches_v[1], caches_k[2], caches_v[2]],
        [jnp.stack(new_k[0]), jnp.stack(new_v[0]), jnp.stack(new_k[1]), jnp.stack(new_v[1]),
         jnp.stack(new_k[2]), jnp.stack(new_v[2])])

    return (xp.reshape(B, T, D_MODEL), xs.reshape(SB, S, D_MODEL),
            jnp.stack(p_re), jnp.stack(p_im),
            jnp.stack(p_k[0]), jnp.stack(p_v[0]), jnp.stack(p_k[1]), jnp.stack(p_v[1]),
            jnp.stack(p_k[2]), jnp.stack(p_v[2]),
            jnp.stack(s_re), jnp.stack(s_im), *shifted)
```

```python
import functools
import math

import numpy as np
import jax
import jax.numpy as jnp
from jax import lax
from jax.experimental import pallas as pl
from jax.experimental.pallas import tpu as pltpu

F32 = jnp.float32
BF16 = jnp.bfloat16

D_MODEL = 2048
HEAD_DIM = 128
HEADS_PER_GROUP = 8
WINDOWS = ((128, 1), (512, 4), (2048, 16))
N_ATT_GROUPS = len(WINDOWS)
N_ATT_HEADS = N_ATT_GROUPS * HEADS_PER_GROUP
ATT_WIDTH = HEADS_PER_GROUP * HEAD_DIM
QKV_WIDTH = N_ATT_HEADS * HEAD_DIM
Q_BLOCK = 128
SSM_WIDTH = D_MODEL // 2
SSM_GROUP = 16
N_SSM_GROUPS = SSM_WIDTH // SSM_GROUP
SSM_STATE = 64
IN_WIDTH = SSM_WIDTH + 3 * QKV_WIDTH + 2 * D_MODEL
FFN_HIDDEN = 5632
N_BUCKETS = 32
REL_MAX_DISTANCE = 2048
EPS = 1e-6
NEG_INF = -1e30

LANES = 128
SUBLANES = 8
VMEM_LIMIT_BYTES = 56 * 1024 * 1024

SSM_CHUNK = 8
SSM_RANGE_GROUPS = LANES // SSM_GROUP
N_SSM_RANGES = N_SSM_GROUPS // SSM_RANGE_GROUPS
RANGE_STATE = SSM_RANGE_GROUPS * SSM_STATE
COL_BLOCK = 1024
FFN_BLOCK = 512
ATTN_UNROLL = 4
CACHE_ROWS = 512


def _cparams(sem):
    return pltpu.CompilerParams(dimension_semantics=sem, vmem_limit_bytes=VMEM_LIMIT_BYTES)


def _dot(a, b):
    return jnp.dot(a, b, preferred_element_type=F32)


def _dot_nt(a, b):
    return lax.dot_general(a, b, (((1,), (1,)), ((), ())), preferred_element_type=F32)


def _ssm_prep_kernel(lre_ref, lim_ref, ldt_ref, btre_ref, btim_ref, cre_ref, cim_ref,
                     kp_ref, bpre_ref, bpim_ref, are_ref, aim_ref, lnre_ref, lnim_ref):
    L = SSM_CHUNK
    lr = lre_ref[...]
    li = lim_ref[...]
    dt = jnp.exp(ldt_ref[...])
    er = jnp.exp(lr * dt)
    lbr = er * jnp.cos(li * dt)
    lbi = er * jnp.sin(li * dt)
    nr = lbr - 1.0
    dd = lr * lr + li * li
    rr = (nr * lr + lbi * li) / dd
    ri = (lbi * lr - nr * li) / dd
    btr = btre_ref[...]
    bti = btim_ref[...]
    bbr = rr * btr - ri * bti
    bbi = rr * bti + ri * btr
    cr = cre_ref[...]
    ci = cim_ref[...]
    pr = [jnp.ones_like(lbr)]
    pi = [jnp.zeros_like(lbr)]
    for _ in range(L):
        pr.append(pr[-1] * lbr - pi[-1] * lbi)
        pi.append(pr[-2] * lbi + pi[-1] * lbr)
    xr = [cr * pr[t] - ci * pi[t] for t in range(L + 1)]
    xi = [cr * pi[t] + ci * pr[t] for t in range(L + 1)]
    xr_k = jnp.concatenate(xr[:L], axis=0)
    xi_k = jnp.concatenate(xi[:L], axis=0)
    hp = lax.Precision.HIGHEST
    kp = (lax.dot_general(bbr, xr_k, (((1,), (1,)), ((), ())), precision=hp, preferred_element_type=F32)
          - lax.dot_general(bbi, xi_k, (((1,), (1,)), ((), ())), precision=hp, preferred_element_type=F32))
    lane = lax.broadcasted_iota(jnp.int32, kp.shape, 1)
    blocks = [kp] + [jnp.where(lane >= j * SSM_GROUP, pltpu.roll(kp, j * SSM_GROUP, axis=1), 0.0)
                     for j in range(1, L)]
    kp_ref[...] = jnp.concatenate(blocks, axis=0)
    are_ref[...] = jnp.concatenate(xr[1:], axis=0)
    aim_ref[...] = jnp.concatenate(xi[1:], axis=0)
    bpre_ref[...] = jnp.concatenate([pr[L - 1 - j] * bbr - pi[L - 1 - j] * bbi for j in range(L)], axis=0)
    bpim_ref[...] = jnp.concatenate([pr[L - 1 - j] * bbi + pi[L - 1 - j] * bbr for j in range(L)], axis=0)
    lnre_ref[...] = pr[L]
    lnim_ref[...] = pi[L]


def _ssm_prep(lam_re, lam_im, log_dt, b_re, b_im, c_re, c_im):
    G, P, C, L = N_SSM_GROUPS, SSM_STATE, SSM_GROUP, SSM_CHUNK
    row = lambda a: a.reshape(G, 1, -1)
    bt_re = jnp.swapaxes(b_re, 1, 2)
    bt_im = jnp.swapaxes(b_im, 1, 2)
    vec = pl.BlockSpec((None, 1, P), lambda g: (g, 0, 0))
    mat = pl.BlockSpec((None, C, P), lambda g: (g, 0, 0))
    big = pl.BlockSpec((None, L * C, P), lambda g: (g, 0, 0))
    kp, bpre, bpim, are, aim, lnre, lnim = pl.pallas_call(
        _ssm_prep_kernel,
        grid=(G,),
        in_specs=[vec, vec, pl.BlockSpec((None, 1, 1), lambda g: (g, 0, 0)), mat, mat, mat, mat],
        out_specs=[pl.BlockSpec((None, L * C, L * C), lambda g: (g, 0, 0)), big, big, big, big, vec, vec],
        out_shape=[jax.ShapeDtypeStruct((G, L * C, L * C), F32)] + [jax.ShapeDtypeStruct((G, L * C, P), F32)] * 4
                  + [jax.ShapeDtypeStruct((G, 1, P), F32)] * 2,
        compiler_params=_cparams(("arbitrary",)),
        name="ssm_prep",
    )(row(lam_re), row(lam_im), log_dt.reshape(G, 1, 1), bt_re, bt_im, c_re, c_im)

    R, GL = N_SSM_RANGES, SSM_RANGE_GROUPS

    def by_range(a, n_outer):
        n_inner = a.shape[1] // n_outer
        a = a.reshape(R, GL, n_outer, n_inner, LANES)
        return jnp.transpose(a, (0, 2, 1, 3, 4)).reshape(R, n_outer * GL * n_inner, LANES).astype(BF16)

    ek = by_range(kp, L)
    eb = by_range(jnp.concatenate([bpre, bpim], axis=-1), L)
    a_t = jnp.swapaxes(jnp.concatenate([are, -aim], axis=-1), 1, 2)
    ea = by_range(a_t, 2)

    half = RANGE_STATE // LANES
    lnr_t = lnre.reshape(R, half, LANES)
    lni_t = lnim.reshape(R, half, LANES)
    lnr = jnp.concatenate([lnr_t, lnr_t], axis=1)
    lni = jnp.concatenate([-lni_t, lni_t], axis=1)
    return ek, eb, ea, lnr, lni


def _ssm_select_matrices():
    q = np.arange(SSM_CHUNK * LANES)
    r = np.arange(LANES)
    sel_k = (r[:, None] // SSM_GROUP == q[None, :] // LANES) & (r[:, None] % SSM_GROUP == q[None, :] % SSM_GROUP)
    sel_b = (r[:, None] // SSM_STATE == q[None, :] // RANGE_STATE) & (r[:, None] % SSM_STATE == q[None, :] % SSM_STATE)
    return jnp.asarray(sel_k, BF16), jnp.asarray(sel_b, BF16)


def _head_norm(res, gain):
    outs = []
    for h in range(HEADS_PER_GROUP):
        t = res[:, h * HEAD_DIM:(h + 1) * HEAD_DIM]
        ms = jnp.mean(t * t, axis=-1, keepdims=True)
        outs.append(t * lax.rsqrt(ms + EPS) * gain)
    return outs


def _in_proj_kernel(x_ref, g_ref, w_ref, qn_ref, kn_ref, u_ref, qkv_ref, gate_ref, h_scr):
    j = pl.program_id(1)

    @pl.when(j == 0)
    def _():
        x = x_ref[...]
        ms = jnp.mean(x * x, axis=-1, keepdims=True)
        h_scr[...] = (x * lax.rsqrt(ms + EPS) * g_ref[...]).astype(BF16)

    res = _dot(h_scr[...], w_ref[...])

    @pl.when(j == 0)
    def _():
        u_ref[...] = res

    @pl.when((j >= 1) & (j <= 3))
    def _():
        for h, t in enumerate(_head_norm(res, qn_ref[...])):
            qkv_ref[h] = t

    @pl.when((j >= 4) & (j <= 6))
    def _():
        for h, t in enumerate(_head_norm(res, kn_ref[...])):
            qkv_ref[h] = t

    @pl.when((j >= 7) & (j <= 9))
    def _():
        for h in range(HEADS_PER_GROUP):
            qkv_ref[h] = res[:, h * HEAD_DIM:(h + 1) * HEAD_DIM]

    @pl.when(j >= 10)
    def _():
        gate_ref[...] = jax.nn.sigmoid(res)


def _in_proj(x, gain, w_bf16, q_gain, k_gain, tm):
    M = x.shape[0]
    nj = IN_WIDTH // COL_BLOCK
    n_qkv = 3 * N_ATT_GROUPS
    return pl.pallas_call(
        _in_proj_kernel,
        grid=(M // tm, nj),
        in_specs=[
            pl.BlockSpec((tm, D_MODEL), lambda i, j: (i, 0)),
            pl.BlockSpec((1, D_MODEL), lambda i, j: (0, 0)),
            pl.BlockSpec((D_MODEL, COL_BLOCK), lambda i, j: (0, j)),
            pl.BlockSpec((1, HEAD_DIM), lambda i, j: (0, 0)),
            pl.BlockSpec((1, HEAD_DIM), lambda i, j: (0, 0)),
        ],
        out_specs=[
            pl.BlockSpec((tm, COL_BLOCK), lambda i, j: (i, 0)),
            pl.BlockSpec((HEADS_PER_GROUP, tm, HEAD_DIM), lambda i, j: (jnp.clip(j - 1, 0, n_qkv - 1), i, 0)),
            pl.BlockSpec((tm, COL_BLOCK), lambda i, j: (i, jnp.clip(j - 10, 0, 3))),
        ],
        out_shape=[
            jax.ShapeDtypeStruct((M, SSM_WIDTH), F32),
            jax.ShapeDtypeStruct((n_qkv * HEADS_PER_GROUP, M, HEAD_DIM), F32),
            jax.ShapeDtypeStruct((M, 2 * D_MODEL), F32),
        ],
        scratch_shapes=[pltpu.VMEM((tm, D_MODEL), BF16)],
        compiler_params=_cparams(("arbitrary", "arbitrary")),
        name="in_proj",
    )(x, gain.reshape(1, D_MODEL), w_bf16, q_gain.reshape(1, HEAD_DIM), k_gain.reshape(1, HEAD_DIM))


def _expand_block_diag(e_ref, sel_ref, row_shift, col_shift, out_scr):
    n = out_scr.shape[0]
    for c in range(n // LANES):
        rows = pl.ds(c * LANES, LANES)
        full = _dot(e_ref[rows, :], sel_ref[...])
        row = lax.broadcasted_iota(jnp.int32, full.shape, 0) + c * LANES
        col = lax.broadcasted_iota(jnp.int32, full.shape, 1)
        keep = ((row >> row_shift) & (SSM_RANGE_GROUPS - 1)) == ((col >> col_shift) & (SSM_RANGE_GROUPS - 1))
        out_scr[rows, :] = jnp.where(keep, full, 0.0).astype(BF16)


def _ssm_chunk_kernel(nseq, nk, u_ref, h0_ref, ek_ref, eb_ref, ea_ref, selk_ref, selb_ref, lnr_ref, lni_ref,
                      y_ref, hfin_ref, s_scr, ktoe_ref, bcat_ref, acat_ref):
    L = SSM_CHUNK
    rows = nseq * nk
    nsub = 2 * RANGE_STATE // LANES
    lg_c = SSM_GROUP.bit_length() - 1
    lg_p = SSM_STATE.bit_length() - 1

    @pl.when(pl.program_id(1) == 0)
    def _():
        _expand_block_diag(ek_ref, selk_ref, lg_c, lg_c, ktoe_ref)
        _expand_block_diag(eb_ref, selb_ref, lg_c, lg_p, bcat_ref)
        _expand_block_diag(ea_ref, selk_ref, lg_p, lg_c, acat_ref)

    ucat = jnp.concatenate([u_ref[pl.ds(j, rows, stride=L), :] for j in range(L)], axis=-1).astype(BF16)
    s = _dot(ucat, bcat_ref[...])
    for n in range(nsub):
        s_scr[pl.ds(n, rows, stride=nsub), :] = s[:, n * LANES:(n + 1) * LANES]
    y_ref_intra = _dot(ucat, ktoe_ref[...])

    lnr = lnr_ref[...]
    lni = lni_ref[...]

    def step(k, hs):
        new = []
        for q in range(nseq):
            off = pl.multiple_of((q * nk + k) * nsub, nsub)
            h = hs[q]
            sk = s_scr[pl.ds(off, nsub), :]
            s_scr[pl.ds(off, nsub), :] = h
            new.append(h * lnr + pltpu.roll(h, nsub // 2, axis=0) * lni + sk)
        return tuple(new)

    hs = lax.fori_loop(0, nk, step, tuple(h0_ref[q] for q in range(nseq)))
    for q in range(nseq):
        hfin_ref[q] = hs[q]

    hprev = jnp.concatenate([s_scr[pl.ds(n, rows, stride=nsub), :] for n in range(nsub)], axis=-1).astype(BF16)
    y = y_ref_intra + _dot(hprev, acat_ref[...])
    for j in range(L):
        y_ref[pl.ds(j, rows, stride=L), :] = y[:, j * LANES:(j + 1) * LANES]


def _ssm_chunk(u, h0, ek, eb, ea, sel_k, sel_b, lnr, lni, nseq_total, seq_per_step):
    M = u.shape[0]
    T = M // nseq_total
    nk = T // SSM_CHUNK
    nsteps = nseq_total // seq_per_step
    R = N_SSM_RANGES
    nsub = 2 * RANGE_STATE // LANES
    tm = seq_per_step * T
    wide = SSM_CHUNK * LANES
    assert wide == 2 * RANGE_STATE
    wspec = lambda shape: pl.BlockSpec((None,) + shape, lambda r, b: (r, 0, 0))
    sel_spec = pl.BlockSpec((LANES, wide), lambda r, b: (0, 0))
    return pl.pallas_call(
        functools.partial(_ssm_chunk_kernel, seq_per_step, nk),
        grid=(R, nsteps),
        in_specs=[
            pl.BlockSpec((tm, LANES), lambda r, b: (b, r)),
            pl.BlockSpec((seq_per_step, None, nsub, LANES), lambda r, b: (b, r, 0, 0)),
            wspec((wide, LANES)),
            wspec((wide, LANES)),
            wspec((wide, LANES)),
            sel_spec,
            sel_spec,
            wspec((nsub, LANES)),
            wspec((nsub, LANES)),
        ],
        out_specs=[
            pl.BlockSpec((tm, LANES), lambda r, b: (b, r)),
            pl.BlockSpec((seq_per_step, None, nsub, LANES), lambda r, b: (b, r, 0, 0)),
        ],
        out_shape=[
            jax.ShapeDtypeStruct((M, SSM_WIDTH), F32),
            jax.ShapeDtypeStruct((nseq_total, R, nsub, LANES), F32),
        ],
        scratch_shapes=[pltpu.VMEM((seq_per_step * nk * nsub, LANES), F32)] + [pltpu.VMEM((wide, wide), BF16)] * 3,
        compiler_params=_cparams(("arbitrary", "arbitrary")),
        name="ssm_chunk",
    )(u, h0, ek, eb, ea, sel_k, sel_b, lnr, lni)


def _state_to_tiles(re, im):
    N = re.shape[0]
    half = RANGE_STATE // LANES
    return jnp.concatenate([re.reshape(N, N_SSM_RANGES, half, LANES),
                            im.reshape(N, N_SSM_RANGES, half, LANES)], axis=2)


def _tiles_to_state(t):
    N = t.shape[0]
    half = RANGE_STATE // LANES
    return (t[:, :, :half].reshape(N, N_SSM_GROUPS, SSM_STATE),
            t[:, :, half:].reshape(N, N_SSM_GROUPS, SSM_STATE))


def _bucket_np(dist):
    max_exact = N_BUCKETS // 2
    n = np.maximum(dist, 0)
    nf = np.maximum(n, 1).astype(np.float64)
    large = max_exact + (np.log(nf / max_exact) / math.log(REL_MAX_DISTANCE / max_exact)
                         * (N_BUCKETS - max_exact)).astype(np.int32)
    large = np.minimum(large, N_BUCKETS - 1)
    return np.where(n < max_exact, n, large)


def _prompt_bias_tiles(rel_bias):
    a = np.arange(Q_BLOCK)[:, None]
    c = np.arange(2 * Q_BLOCK)[None, :]
    rel = a - c + Q_BLOCK
    tiles = []
    for g, (window, dil) in enumerate(WINDOWS):
        K = window // dil + 1
        valid = (rel >= 0) & (rel < K)
        bkt = _bucket_np(np.clip(rel, 0, K - 1) * dil)
        tab = rel_bias[:, g * HEADS_PER_GROUP:(g + 1) * HEADS_PER_GROUP].astype(F32)
        b = jnp.transpose(tab[bkt], (2, 0, 1))
        tiles.append(jnp.where(jnp.asarray(valid)[None], b, NEG_INF))
    return jnp.stack(tiles)


def _attn_prompt_kernel(T, q_ref, k_ref, v_ref, bias_ref, o_ref, m_scr, l_scr, acc_scr):
    g = pl.program_id(2)
    scale = HEAD_DIM ** -0.5
    nblk = T // Q_BLOCK

    def run_group(gi, dil):
        per_class = nblk // dil
        bias = bias_ref[...]
        bias_l = bias[:, :Q_BLOCK]
        bias_r = bias[:, Q_BLOCK:]

        def block(i):
            r = i // per_class
            n = i % per_class
            row_q = r + n * (Q_BLOCK * dil)
            row_p = r + jnp.maximum(n - 1, 0) * (Q_BLOCK * dil)
            sl_q = pl.ds(row_q, Q_BLOCK, stride=dil)
            sl_p = pl.ds(row_p, Q_BLOCK, stride=dil)
            q = q_ref[sl_q, :].astype(BF16)
            kc = k_ref[sl_q, :].astype(BF16)
            kp = k_ref[sl_p, :].astype(BF16)
            s_r = _dot_nt(q, kc) * scale + bias_r
            s_l = _dot_nt(q, kp) * scale + jnp.where(n == 0, NEG_INF, bias_l)
            m = jnp.maximum(jnp.max(s_l, axis=-1, keepdims=True), jnp.max(s_r, axis=-1, keepdims=True))
            p_l = jnp.exp(s_l - m)
            p_r = jnp.exp(s_r - m)
            l = jnp.sum(p_l, axis=-1, keepdims=True) + jnp.sum(p_r, axis=-1, keepdims=True)
            o = (_dot(p_l.astype(BF16), v_ref[sl_p, :].astype(BF16))
                 + _dot(p_r.astype(BF16), v_ref[sl_q, :].astype(BF16)))
            return sl_q, jnp.broadcast_to(m, (Q_BLOCK, HEAD_DIM)), jnp.broadcast_to(l, (Q_BLOCK, HEAD_DIM)), o

        def body(it, carry):
            results = [block(it * ATTN_UNROLL + k) for k in range(ATTN_UNROLL)]
            for sl_q, mb, lb, o in results:
                if gi == 0:
                    m_scr[sl_q, :] = mb
                    l_scr[sl_q, :] = lb
                    acc_scr[sl_q, :] = o
                else:
                    m0 = m_scr[sl_q, :]
                    mn = jnp.maximum(m0, mb)
                    a0 = jnp.exp(m0 - mn)
                    a1 = jnp.exp(mb - mn)
                    m_scr[sl_q, :] = mn
                    l_scr[sl_q, :] = a0 * l_scr[sl_q, :] + a1 * lb
                    acc_scr[sl_q, :] = a0 * acc_scr[sl_q, :] + a1 * o
            return carry

        lax.fori_loop(0, nblk // ATTN_UNROLL, body, 0)

    for gi, (_, dil) in enumerate(WINDOWS):
        pl.when(g == gi)(functools.partial(run_group, gi, dil))

    @pl.when(g == N_ATT_GROUPS - 1)
    def _():
        o_ref[...] = (acc_scr[...] / l_scr[...]).astype(o_ref.dtype)


def _attn_prompt(qkv, bias_tiles, nbatch):
    M = qkv.shape[1]
    T = M // nbatch
    H, G = HEADS_PER_GROUP, N_ATT_GROUPS

    def qkv_spec(which):
        return pl.BlockSpec((None, T, HEAD_DIM), lambda b, h, g: (which * N_ATT_HEADS + g * H + h, b, 0))

    return pl.pallas_call(
        functools.partial(_attn_prompt_kernel, T),
        grid=(nbatch, H, G),
        in_specs=[qkv_spec(0), qkv_spec(1), qkv_spec(2),
                  pl.BlockSpec((None, None, Q_BLOCK, 2 * Q_BLOCK), lambda b, h, g: (g, h, 0, 0))],
        out_specs=pl.BlockSpec((T, HEAD_DIM), lambda b, h, g: (b, h)),
        out_shape=jax.ShapeDtypeStruct((M, ATT_WIDTH), BF16),
        scratch_shapes=[pltpu.VMEM((T, HEAD_DIM), F32)] * 3,
        compiler_params=_cparams(("arbitrary", "arbitrary", "arbitrary")),
        name="attn_prompt",
    )(qkv, qkv, qkv, bias_tiles)


def _attn_sample_kernel(window, dil, q_ref, ck_ref, cv_ref, nk_ref, nv_ref, bias_ref, o_ref, lse_ref):
    S = q_ref.shape[0]
    K = window // dil + 1
    scale = HEAD_DIM ** -0.5
    bias = bias_ref[...]
    for s in range(S):
        n_c = (window - 1 - s) // dil + 1
        qs = q_ref[s]
        new_rows = [s + j * dil - window for j in range(n_c, K)]
        kk = jnp.concatenate([ck_ref[pl.ds(s, n_c, stride=dil)]] + [nk_ref[pl.ds(i, 1)] for i in new_rows], axis=0)
        vv = jnp.concatenate([cv_ref[pl.ds(s, n_c, stride=dil)]] + [nv_ref[pl.ds(i, 1)] for i in new_rows], axis=0)
        lg = jnp.sum(kk * qs[None], axis=-1, keepdims=True) * scale + bias
        m = jnp.max(lg, axis=0)
        p = jnp.exp(lg - m[None])
        l = jnp.sum(p, axis=0)
        o_ref[s] = jnp.sum(p * vv, axis=0) / l
        lse_ref[s] = m + jnp.log(l)


def _attn_sample(q, cache_k, cache_v, new_k, new_v, bias, layer, window, dil):
    B, S = q.shape[0], q.shape[1]
    H = HEADS_PER_GROUP
    K = window // dil + 1
    small = pl.BlockSpec((None, S, H, HEAD_DIM), lambda b: (b, 0, 0, 0))
    cache = pl.BlockSpec((None, None, window, H, HEAD_DIM), lambda b: (layer, b, 0, 0, 0))
    return pl.pallas_call(
        functools.partial(_attn_sample_kernel, window, dil),
        grid=(B,),
        in_specs=[small, cache, cache, small, small, pl.BlockSpec((K, H, HEAD_DIM), lambda b: (0, 0, 0))],
        out_specs=[small, small],
        out_shape=[jax.ShapeDtypeStruct((B, S, H, HEAD_DIM), F32)] * 2,
        compiler_params=_cparams(("arbitrary",)),
        name="attn_sample_w%d" % window,
    )(q, cache_k, cache_v, new_k, new_v, bias)


def _merge_groups_kernel(o0, o1, o2, l0, l1, l2, y_ref):
    a, b, c = l0[...], l1[...], l2[...]
    m = jnp.maximum(jnp.maximum(a, b), c)
    ea, eb, ec = jnp.exp(a - m), jnp.exp(b - m), jnp.exp(c - m)
    y_ref[...] = ((ea * o0[...] + eb * o1[...] + ec * o2[...]) / (ea + eb + ec)).astype(y_ref.dtype)


def _merge_groups(outs, lses):
    shape = outs[0].shape
    return pl.pallas_call(
        _merge_groups_kernel,
        out_shape=jax.ShapeDtypeStruct(shape, BF16),
        name="merge_groups",
    )(*outs, *lses)


def _sample_bias(rel_bias):
    out = []
    for g, (window, dil) in enumerate(WINDOWS):
        K = window // dil + 1
        steps = (K - 1) - np.arange(K)
        tab = rel_bias[:, g * HEADS_PER_GROUP:(g + 1) * HEADS_PER_GROUP].astype(F32)
        b = tab[_bucket_np(steps * dil)]
        out.append(jnp.broadcast_to(b[:, :, None], (K, HEADS_PER_GROUP, HEAD_DIM)))
    return out


def _gelu_tanh(x):
    return 0.5 * x * (1.0 + jnp.tanh(math.sqrt(2.0 / math.pi) * (x + 0.044715 * (x * x * x))))


def _mix_out_kernel(y_ref, u_ref, yb_ref, ga_ref, gb_ref, x_ref, d_ref, wglu_ref, bglu_ref, wa_ref, wb_ref,
                    wout_ref, gn_ref, x1_ref, h2_ref):
    y = y_ref[...] + d_ref[...] * u_ref[...]
    z = _gelu_tanh(y)
    ya = z * jax.nn.sigmoid(_dot(z.astype(BF16), wglu_ref[...]) + bglu_ref[...])
    mix = (ga_ref[...] * _dot(ya.astype(BF16), wa_ref[...])
           + gb_ref[...] * _dot(yb_ref[...], wb_ref[...]))
    x1 = x_ref[...] + _dot(mix.astype(BF16), wout_ref[...])
    x1_ref[...] = x1
    ms = jnp.mean(x1 * x1, axis=-1, keepdims=True)
    h2_ref[...] = (x1 * lax.rsqrt(ms + EPS) * gn_ref[...]).astype(BF16)


def _mix_out(y_ssm, u, yb, gates, x, ssm_d, wglu, bglu, wa, wb, wout, gain, tm):
    M = x.shape[0]
    row = lambda w: pl.BlockSpec((tm, w), lambda i: (i, 0))
    full = lambda a, b: pl.BlockSpec((a, b), lambda i: (0, 0), pipeline_mode=pl.Buffered(1))
    return pl.pallas_call(
        _mix_out_kernel,
        grid=(M // tm,),
        in_specs=[row(SSM_WIDTH), row(SSM_WIDTH), row(ATT_WIDTH),
                  pl.BlockSpec((tm, D_MODEL), lambda i: (i, 0)),
                  pl.BlockSpec((tm, D_MODEL), lambda i: (i, 1)),
                  row(D_MODEL),
                  full(1, SSM_WIDTH), full(SSM_WIDTH, SSM_WIDTH), full(1, SSM_WIDTH),
                  full(SSM_WIDTH, D_MODEL), full(ATT_WIDTH, D_MODEL), full(D_MODEL, D_MODEL), full(1, D_MODEL)],
        out_specs=[row(D_MODEL), row(D_MODEL)],
        out_shape=[jax.ShapeDtypeStruct((M, D_MODEL), F32), jax.ShapeDtypeStruct((M, D_MODEL), BF16)],
        compiler_params=_cparams(("arbitrary",)),
        name="mix_out",
    )(y_ssm, u, yb, gates, gates, x, ssm_d.reshape(1, -1), wglu, bglu.reshape(1, -1), wa, wb, wout,
      gain.reshape(1, -1))


def _ffn_kernel(h_ref, x_ref, wg_ref, wu_ref, wd_ref, o_ref, acc_ref):
    f = pl.program_id(1)
    h = h_ref[...]
    a = jax.nn.silu(_dot(h, wg_ref[...])) * _dot(h, wu_ref[...])
    part = _dot(a.astype(BF16), wd_ref[...])

    @pl.when(f == 0)
    def _():
        acc_ref[...] = part

    @pl.when(f > 0)
    def _():
        acc_ref[...] += part

    @pl.when(f == pl.num_programs(1) - 1)
    def _():
        o_ref[...] = x_ref[...] + acc_ref[...]


def _ffn(h2, x1, wg, wu, wd, tm):
    M = x1.shape[0]
    nf = FFN_HIDDEN // FFN_BLOCK
    return pl.pallas_call(
        _ffn_kernel,
        grid=(M // tm, nf),
        in_specs=[pl.BlockSpec((tm, D_MODEL), lambda i, f: (i, 0)),
                  pl.BlockSpec((tm, D_MODEL), lambda i, f: (i, 0)),
                  pl.BlockSpec((D_MODEL, FFN_BLOCK), lambda i, f: (0, f)),
                  pl.BlockSpec((D_MODEL, FFN_BLOCK), lambda i, f: (0, f)),
                  pl.BlockSpec((FFN_BLOCK, D_MODEL), lambda i, f: (f, 0))],
        out_specs=pl.BlockSpec((tm, D_MODEL), lambda i, f: (i, 0)),
        out_shape=jax.ShapeDtypeStruct((M, D_MODEL), F32),
        scratch_shapes=[pltpu.VMEM((tm, D_MODEL), F32)],
        compiler_params=_cparams(("arbitrary", "arbitrary")),
        name="ffn",
    )(h2, x1, wg, wu, wd)


def _cache_shift_kernel(ck_ref, cv_ref, hk_ref, hv_ref, nk_ref, nv_ref, ok_ref, ov_ref):
    c = pl.program_id(2)
    last = pl.num_programs(2) - 1
    R = ck_ref.shape[0]
    S = nk_ref.shape[0]
    for cache, halo, new, out in ((ck_ref, hk_ref, nk_ref, ok_ref), (cv_ref, hv_ref, nv_ref, ov_ref)):
        out[pl.ds(0, R - S)] = cache[pl.ds(S, R - S)]

        @pl.when(c == last)
        def _():
            out[pl.ds(R - S, S)] = new[...]

        @pl.when(c < last)
        def _():
            out[pl.ds(R - S, S)] = halo[...]


def _cache_shift(cache_k, cache_v, new_k, new_v):
    depth, B, W, H, E = cache_k.shape
    S = new_k.shape[2]
    R = min(W, CACHE_ROWS)
    nchunks = W // R
    blk = pl.BlockSpec((None, None, R, H, E), lambda l, b, c: (l, b, c, 0, 0))
    halo = pl.BlockSpec((None, None, None, S, H, E),
                        lambda l, b, c: (l, b, jnp.minimum((c + 1) * (R // S), W // S - 1), 0, 0, 0))
    new = pl.BlockSpec((None, None, S, H, E), lambda l, b, c: (l, b, 0, 0, 0))
    as_rows = lambda a: a.reshape(depth, B, W // S, S, H, E)
    return pl.pallas_call(
        _cache_shift_kernel,
        grid=(depth, B, nchunks),
        in_specs=[blk, blk, halo, halo, new, new],
        out_specs=[blk, blk],
        out_shape=[jax.ShapeDtypeStruct(cache_k.shape, cache_k.dtype)] * 2,
        compiler_params=_cparams(("arbitrary", "arbitrary", "arbitrary")),
        name="cache_shift_w%d" % W,
    )(cache_k, cache_v, as_rows(cache_k), as_rows(cache_v), new_k, new_v)


def _kv_tails_kernel(src_ref, out_ref):
    H, R = src_ref.shape[0], src_ref.shape[1]
    for h in range(H):
        out_ref[pl.ds(h, R, stride=H), :] = src_ref[h]


def _kv_tails(qkv, g, window, nbatch):
    M = qkv.shape[1]
    T = M // nbatch
    H, E = HEADS_PER_GROUP, HEAD_DIM
    R = min(window, CACHE_ROWS)
    nchunks = window // R
    first = (T - window) // R
    out = pl.pallas_call(
        _kv_tails_kernel,
        grid=(2, nbatch, nchunks),
        in_specs=[pl.BlockSpec((H, R, E), lambda w, b, c: ((w + 1) * N_ATT_GROUPS + g, b * (T // R) + first + c, 0))],
        out_specs=pl.BlockSpec((None, R * H, E), lambda w, b, c: (w, b * nchunks + c, 0)),
        out_shape=jax.ShapeDtypeStruct((2, nbatch * window * H, E), qkv.dtype),
        compiler_params=_cparams(("arbitrary", "arbitrary", "arbitrary")),
        name="kv_tails_w%d" % window,
    )(qkv)
    return out.reshape(2, nbatch, window, H, E)


def _layer_common(x2d, h0_tiles, nseq, seq_per_step, prm, tm):
    u, qkv, gates = _in_proj(x2d, prm["norm_mix"], prm["w_in"], prm["q_norm"], prm["k_norm"], tm)
    y_ssm, hfin = _ssm_chunk(u, h0_tiles, prm["ek"], prm["eb"], prm["ea"], prm["sel_k"], prm["sel_b"],
                             prm["lnr"], prm["lni"], nseq, seq_per_step)
    return u, qkv, gates, y_ssm, hfin


def _layer_tail(x2d, u, y_ssm, yb, gates, prm, tm_mix, tm_ffn):
    x1, h2 = _mix_out(y_ssm, u, yb, gates, x2d, prm["ssm_d"], prm["w_glu"], prm["b_glu"], prm["w_branch_a"],
                      prm["w_branch_b"], prm["w_out"], prm["norm_ffn"], tm_mix)
    return _ffn(h2, x1, prm["w_ffn_gate"], prm["w_ffn_up"], prm["w_ffn_down"], tm_ffn)


def kernel(x_prompt, x_sample, state_ssm_re, state_ssm_im, cache_k_w128, cache_v_w128, cache_k_w512, cache_v_w512,
           cache_k_w2048, cache_v_w2048, rel_bias, norm_mix, norm_ffn, q_norm, k_norm, w_in, ssm_lambda_re,
           ssm_lambda_im, ssm_log_dt, ssm_b_re, ssm_b_im, ssm_c_re, ssm_c_im, ssm_d, w_glu, b_glu, w_branch_a,
           w_branch_b, w_out, w_ffn_gate, w_ffn_up, w_ffn_down):
    depth = w_in.shape[0]
    B, T, _ = x_prompt.shape
    SB, S, _ = x_sample.shape
    H = HEADS_PER_GROUP
    caches_k = (cache_k_w128, cache_k_w512, cache_k_w2048)
    caches_v = (cache_v_w128, cache_v_w512, cache_v_w2048)

    bias_tiles = _prompt_bias_tiles(rel_bias)
    bias_sample = _sample_bias(rel_bias)
    sel_k, sel_b = _ssm_select_matrices()

    xp = x_prompt.reshape(B * T, D_MODEL)
    xs = x_sample.reshape(SB * S, D_MODEL)
    p_re, p_im, s_re, s_im = [], [], [], []
    p_kv = [[] for _ in WINDOWS]
    new_k = [[] for _ in WINDOWS]
    new_v = [[] for _ in WINDOWS]

    for l in range(depth):
        ek, eb, ea, lnr, lni = _ssm_prep(ssm_lambda_re[l], ssm_lambda_im[l], ssm_log_dt[l], ssm_b_re[l],
                                         ssm_b_im[l], ssm_c_re[l], ssm_c_im[l])
        prm = {
            "norm_mix": norm_mix[l], "norm_ffn": norm_ffn[l], "q_norm": q_norm[l], "k_norm": k_norm[l],
            "w_in": w_in[l].astype(BF16), "ssm_d": ssm_d[l], "w_glu": w_glu[l].astype(BF16), "b_glu": b_glu[l],
            "w_branch_a": w_branch_a[l].astype(BF16), "w_branch_b": w_branch_b[l].astype(BF16),
            "w_out": w_out[l].astype(BF16), "w_ffn_gate": w_ffn_gate[l].astype(BF16),
            "w_ffn_up": w_ffn_up[l].astype(BF16), "w_ffn_down": w_ffn_down[l].astype(BF16),
            "ek": ek, "eb": eb, "ea": ea, "sel_k": sel_k, "sel_b": sel_b, "lnr": lnr, "lni": lni,
        }

        zeros = jnp.zeros((B, N_SSM_RANGES, 2 * RANGE_STATE // LANES, LANES), F32)
        u, qkv, gates, y_ssm, hfin = _layer_common(xp, zeros, B, B, prm, 512)
        yb = _attn_prompt(qkv, bias_tiles, B)
        xp = _layer_tail(xp, u, y_ssm, yb, gates, prm, 256, 512)
        hr, hi = _tiles_to_state(hfin)
        p_re.append(hr)
        p_im.append(hi)
        for g, (window, _) in enumerate(WINDOWS):
            p_kv[g].append(_kv_tails(qkv, g, window, B))

        h0 = _state_to_tiles(state_ssm_re[l], state_ssm_im[l])
        u, qkv, gates, y_ssm, hfin = _layer_common(xs, h0, SB, SB, prm, SB * S)
        tok = jnp.transpose(qkv.reshape(3, N_ATT_GROUPS, H, SB, S, HEAD_DIM), (0, 1, 3, 4, 2, 5))
        outs, lses = [], []
        for g, (window, dil) in enumerate(WINDOWS):
            nk_g, nv_g = tok[1, g], tok[2, g]
            new_k[g].append(nk_g)
            new_v[g].append(nv_g)
            o_g, lse_g = _attn_sample(tok[0, g], caches_k[g], caches_v[g], nk_g, nv_g, bias_sample[g], l,
                                      window, dil)
            outs.append(o_g)
            lses.append(lse_g)
        yb = _merge_groups(outs, lses).reshape(SB * S, ATT_WIDTH)
        xs = _layer_tail(xs, u, y_ssm, yb, gates, prm, SB * S, SB * S)
        hr, hi = _tiles_to_state(hfin)
        s_re.append(hr)
        s_im.append(hi)

    shifted = []
    for g in range(N_ATT_GROUPS):
        shifted += _cache_shift(caches_k[g], caches_v[g], jnp.stack(new_k[g]), jnp.stack(new_v[g]))
    p_kv = [jnp.stack(t, axis=1) for t in p_kv]

    return (xp.reshape(B, T, D_MODEL), xs.reshape(SB, S, D_MODEL),
            jnp.stack(p_re), jnp.stack(p_im),
            p_kv[0][0], p_kv[0][1], p_kv[1][0], p_kv[1][1], p_kv[2][0], p_kv[2][1],
            jnp.stack(s_re), jnp.stack(s_im), *shifted)
```

```python
import functools
import math

import numpy as np
import jax
import jax.numpy as jnp
from jax import lax
from jax.experimental import pallas as pl
from jax.experimental.pallas import tpu as pltpu

F32 = jnp.float32
BF16 = jnp.bfloat16

D_MODEL = 2048
HEAD_DIM = 128
HEADS_PER_GROUP = 8
WINDOWS = ((128, 1), (512, 4), (2048, 16))
N_ATT_GROUPS = len(WINDOWS)
N_ATT_HEADS = N_ATT_GROUPS * HEADS_PER_GROUP
ATT_WIDTH = HEADS_PER_GROUP * HEAD_DIM
QKV_WIDTH = N_ATT_HEADS * HEAD_DIM
Q_BLOCK = 128
SSM_WIDTH = D_MODEL // 2
SSM_GROUP = 16
N_SSM_GROUPS = SSM_WIDTH // SSM_GROUP
SSM_STATE = 64
IN_WIDTH = SSM_WIDTH + 3 * QKV_WIDTH + 2 * D_MODEL
FFN_HIDDEN = 5632
N_BUCKETS = 32
REL_MAX_DISTANCE = 2048
EPS = 1e-6
NEG_INF = -1e30

LANES = 128
SUBLANES = 8
VMEM_LIMIT_BYTES = 56 * 1024 * 1024

SSM_CHUNK = 8
SSM_RANGE_GROUPS = LANES // SSM_GROUP
N_SSM_RANGES = N_SSM_GROUPS // SSM_RANGE_GROUPS
RANGE_STATE = SSM_RANGE_GROUPS * SSM_STATE
COL_BLOCK = 1024
FFN_BLOCK = 512
ATTN_UNROLL = 4
ATTN_FIRST_GROUP = 2
CACHE_ROWS = 512
ROW_SUB = 256
FFN_ROW_SUB = 512


def _cparams(sem):
    return pltpu.CompilerParams(dimension_semantics=sem, vmem_limit_bytes=VMEM_LIMIT_BYTES)


def _dot(a, b):
    return jnp.dot(a, b, preferred_element_type=F32)


def _dot_nt(a, b):
    return lax.dot_general(a, b, (((1,), (1,)), ((), ())), preferred_element_type=F32)


def _ssm_prep_kernel(lre_ref, lim_ref, ldt_ref, btre_ref, btim_ref, cre_ref, cim_ref,
                     kp_ref, bpre_ref, bpim_ref, are_ref, aim_ref, lnre_ref, lnim_ref):
    L = SSM_CHUNK
    lr = lre_ref[...]
    li = lim_ref[...]
    dt = jnp.exp(ldt_ref[...])
    er = jnp.exp(lr * dt)
    lbr = er * jnp.cos(li * dt)
    lbi = er * jnp.sin(li * dt)
    nr = lbr - 1.0
    dd = lr * lr + li * li
    rr = (nr * lr + lbi * li) / dd
    ri = (lbi * lr - nr * li) / dd
    btr = btre_ref[...]
    bti = btim_ref[...]
    bbr = rr * btr - ri * bti
    bbi = rr * bti + ri * btr
    cr = cre_ref[...]
    ci = cim_ref[...]
    pr = [jnp.ones_like(lbr)]
    pi = [jnp.zeros_like(lbr)]
    for _ in range(L):
        pr.append(pr[-1] * lbr - pi[-1] * lbi)
        pi.append(pr[-2] * lbi + pi[-1] * lbr)
    xr = [cr * pr[t] - ci * pi[t] for t in range(L + 1)]
    xi = [cr * pi[t] + ci * pr[t] for t in range(L + 1)]
    xr_k = jnp.concatenate(xr[:L], axis=0)
    xi_k = jnp.concatenate(xi[:L], axis=0)
    hp = lax.Precision.HIGHEST
    kp = (lax.dot_general(bbr, xr_k, (((1,), (1,)), ((), ())), precision=hp, preferred_element_type=F32)
          - lax.dot_general(bbi, xi_k, (((1,), (1,)), ((), ())), precision=hp, preferred_element_type=F32))
    lane = lax.broadcasted_iota(jnp.int32, kp.shape, 1)
    blocks = [kp] + [jnp.where(lane >= j * SSM_GROUP, pltpu.roll(kp, j * SSM_GROUP, axis=1), 0.0)
                     for j in range(1, L)]
    kp_ref[...] = jnp.concatenate(blocks, axis=0)
    are_ref[...] = jnp.concatenate(xr[1:], axis=0)
    aim_ref[...] = jnp.concatenate(xi[1:], axis=0)
    bpre_ref[...] = jnp.concatenate([pr[L - 1 - j] * bbr - pi[L - 1 - j] * bbi for j in range(L)], axis=0)
    bpim_ref[...] = jnp.concatenate([pr[L - 1 - j] * bbi + pi[L - 1 - j] * bbr for j in range(L)], axis=0)
    lnre_ref[...] = pr[L]
    lnim_ref[...] = pi[L]


def _ssm_prep(lam_re, lam_im, log_dt, b_re, b_im, c_re, c_im):
    G, P, C, L = N_SSM_GROUPS, SSM_STATE, SSM_GROUP, SSM_CHUNK
    row = lambda a: a.reshape(G, 1, -1)
    bt_re = jnp.swapaxes(b_re, 1, 2)
    bt_im = jnp.swapaxes(b_im, 1, 2)
    vec = pl.BlockSpec((None, 1, P), lambda g: (g, 0, 0))
    mat = pl.BlockSpec((None, C, P), lambda g: (g, 0, 0))
    big = pl.BlockSpec((None, L * C, P), lambda g: (g, 0, 0))
    kp, bpre, bpim, are, aim, lnre, lnim = pl.pallas_call(
        _ssm_prep_kernel,
        grid=(G,),
        in_specs=[vec, vec, pl.BlockSpec((None, 1, 1), lambda g: (g, 0, 0)), mat, mat, mat, mat],
        out_specs=[pl.BlockSpec((None, L * C, L * C), lambda g: (g, 0, 0)), big, big, big, big, vec, vec],
        out_shape=[jax.ShapeDtypeStruct((G, L * C, L * C), F32)] + [jax.ShapeDtypeStruct((G, L * C, P), F32)] * 4
                  + [jax.ShapeDtypeStruct((G, 1, P), F32)] * 2,
        compiler_params=_cparams(("arbitrary",)),
        name="ssm_prep",
    )(row(lam_re), row(lam_im), log_dt.reshape(G, 1, 1), bt_re, bt_im, c_re, c_im)

    R, GL = N_SSM_RANGES, SSM_RANGE_GROUPS

    def by_range(a, n_outer):
        n_inner = a.shape[1] // n_outer
        a = a.reshape(R, GL, n_outer, n_inner, LANES)
        return jnp.transpose(a, (0, 2, 1, 3, 4)).reshape(R, n_outer * GL * n_inner, LANES).astype(BF16)

    ek = by_range(kp, L)
    eb = by_range(jnp.concatenate([bpre, bpim], axis=-1), L)
    a_t = jnp.swapaxes(jnp.concatenate([are, -aim], axis=-1), 1, 2)
    ea = by_range(a_t, 2)

    half = RANGE_STATE // LANES
    lnr_t = lnre.reshape(R, half, LANES)
    lni_t = lnim.reshape(R, half, LANES)
    lnr = jnp.concatenate([lnr_t, lnr_t], axis=1)
    lni = jnp.concatenate([-lni_t, lni_t], axis=1)
    return ek, eb, ea, lnr, lni


def _ssm_select_matrices():
    q = np.arange(SSM_CHUNK * LANES)
    r = np.arange(LANES)
    sel_k = (r[:, None] // SSM_GROUP == q[None, :] // LANES) & (r[:, None] % SSM_GROUP == q[None, :] % SSM_GROUP)
    sel_b = (r[:, None] // SSM_STATE == q[None, :] // RANGE_STATE) & (r[:, None] % SSM_STATE == q[None, :] % SSM_STATE)
    return jnp.asarray(sel_k, BF16), jnp.asarray(sel_b, BF16)


def _head_norm(res, gain):
    outs = []
    for h in range(HEADS_PER_GROUP):
        t = res[:, h * HEAD_DIM:(h + 1) * HEAD_DIM]
        ms = jnp.mean(t * t, axis=-1, keepdims=True)
        outs.append(t * lax.rsqrt(ms + EPS) * gain)
    return outs


def _in_proj_kernel(x_ref, g_ref, w_ref, qn_ref, kn_ref, u_ref, qkv_ref, gate_ref, h_scr):
    j = pl.program_id(1)
    tm = x_ref.shape[0]
    sub = min(tm, ROW_SUB)

    def row_blocks():
        for r in range(tm // sub):
            rows = pl.ds(r * sub, sub)
            yield rows, _dot(h_scr[rows, :], w_ref[...])

    @pl.when(j == 0)
    def _():
        x = x_ref[...]
        ms = jnp.mean(x * x, axis=-1, keepdims=True)
        h_scr[...] = (x * lax.rsqrt(ms + EPS) * g_ref[...]).astype(BF16)
        for rows, res in row_blocks():
            u_ref[rows, :] = res

    def store_heads(gain_ref):
        for rows, res in row_blocks():
            heads = (_head_norm(res, gain_ref[...]) if gain_ref is not None
                     else [res[:, h * HEAD_DIM:(h + 1) * HEAD_DIM] for h in range(HEADS_PER_GROUP)])
            for h, t in enumerate(heads):
                qkv_ref[h, rows, :] = t

    pl.when((j >= 1) & (j <= 3))(functools.partial(store_heads, qn_ref))
    pl.when((j >= 4) & (j <= 6))(functools.partial(store_heads, kn_ref))
    pl.when((j >= 7) & (j <= 9))(functools.partial(store_heads, None))

    @pl.when(j >= 10)
    def _():
        for rows, res in row_blocks():
            gate_ref[rows, :] = jax.nn.sigmoid(res)


def _in_proj(x, gain, w_bf16, q_gain, k_gain, tm):
    M = x.shape[0]
    nj = IN_WIDTH // COL_BLOCK
    n_qkv = 3 * N_ATT_GROUPS
    return pl.pallas_call(
        _in_proj_kernel,
        grid=(M // tm, nj),
        in_specs=[
            pl.BlockSpec((tm, D_MODEL), lambda i, j: (i, 0)),
            pl.BlockSpec((1, D_MODEL), lambda i, j: (0, 0)),
            pl.BlockSpec((D_MODEL, COL_BLOCK), lambda i, j: (0, j)),
            pl.BlockSpec((1, HEAD_DIM), lambda i, j: (0, 0)),
            pl.BlockSpec((1, HEAD_DIM), lambda i, j: (0, 0)),
        ],
        out_specs=[
            pl.BlockSpec((tm, COL_BLOCK), lambda i, j: (i, 0)),
            pl.BlockSpec((HEADS_PER_GROUP, tm, HEAD_DIM), lambda i, j: (jnp.clip(j - 1, 0, n_qkv - 1), i, 0)),
            pl.BlockSpec((tm, COL_BLOCK), lambda i, j: (i, jnp.clip(j - 10, 0, 3))),
        ],
        out_shape=[
            jax.ShapeDtypeStruct((M, SSM_WIDTH), F32),
            jax.ShapeDtypeStruct((n_qkv * HEADS_PER_GROUP, M, HEAD_DIM), F32),
            jax.ShapeDtypeStruct((M, 2 * D_MODEL), F32),
        ],
        scratch_shapes=[pltpu.VMEM((tm, D_MODEL), BF16)],
        compiler_params=_cparams(("arbitrary", "arbitrary")),
        name="in_proj",
    )(x, gain.reshape(1, D_MODEL), w_bf16, q_gain.reshape(1, HEAD_DIM), k_gain.reshape(1, HEAD_DIM))


def _expand_block_diag(e_ref, sel_ref, row_shift, col_shift, out_scr):
    n = out_scr.shape[0]
    for c in range(n // LANES):
        rows = pl.ds(c * LANES, LANES)
        full = _dot(e_ref[rows, :], sel_ref[...])
        row = lax.broadcasted_iota(jnp.int32, full.shape, 0) + c * LANES
        col = lax.broadcasted_iota(jnp.int32, full.shape, 1)
        keep = ((row >> row_shift) & (SSM_RANGE_GROUPS - 1)) == ((col >> col_shift) & (SSM_RANGE_GROUPS - 1))
        out_scr[rows, :] = jnp.where(keep, full, 0.0).astype(BF16)


def _ssm_chunk_kernel(nseq, nk, u_ref, h0_ref, ek_ref, eb_ref, ea_ref, selk_ref, selb_ref, lnr_ref, lni_ref,
                      y_ref, hfin_ref, s_scr, ktoe_ref, bcat_ref, acat_ref):
    L = SSM_CHUNK
    rows = nseq * nk
    nsub = 2 * RANGE_STATE // LANES
    lg_c = SSM_GROUP.bit_length() - 1
    lg_p = SSM_STATE.bit_length() - 1

    @pl.when(pl.program_id(1) == 0)
    def _():
        _expand_block_diag(ek_ref, selk_ref, lg_c, lg_c, ktoe_ref)
        _expand_block_diag(eb_ref, selb_ref, lg_c, lg_p, bcat_ref)
        _expand_block_diag(ea_ref, selk_ref, lg_p, lg_c, acat_ref)

    ucat = jnp.concatenate([u_ref[pl.ds(j, rows, stride=L), :] for j in range(L)], axis=-1).astype(BF16)
    s = _dot(ucat, bcat_ref[...])
    for n in range(nsub):
        s_scr[pl.ds(n, rows, stride=nsub), :] = s[:, n * LANES:(n + 1) * LANES]
    y_ref_intra = _dot(ucat, ktoe_ref[...])

    lnr = lnr_ref[...]
    lni = lni_ref[...]

    def step(k, hs):
        new = []
        for q in range(nseq):
            off = pl.multiple_of((q * nk + k) * nsub, nsub)
            h = hs[q]
            sk = s_scr[pl.ds(off, nsub), :]
            s_scr[pl.ds(off, nsub), :] = h
            new.append(h * lnr + pltpu.roll(h, nsub // 2, axis=0) * lni + sk)
        return tuple(new)

    hs = lax.fori_loop(0, nk, step, tuple(h0_ref[q] for q in range(nseq)))
    for q in range(nseq):
        hfin_ref[q] = hs[q]

    hprev = jnp.concatenate([s_scr[pl.ds(n, rows, stride=nsub), :] for n in range(nsub)], axis=-1).astype(BF16)
    y = y_ref_intra + _dot(hprev, acat_ref[...])
    for j in range(L):
        y_ref[pl.ds(j, rows, stride=L), :] = y[:, j * LANES:(j + 1) * LANES]


def _ssm_chunk(u, h0, ek, eb, ea, sel_k, sel_b, lnr, lni, nseq_total, seq_per_step):
    M = u.shape[0]
    T = M // nseq_total
    nk = T // SSM_CHUNK
    nsteps = nseq_total // seq_per_step
    R = N_SSM_RANGES
    nsub = 2 * RANGE_STATE // LANES
    tm = seq_per_step * T
    wide = SSM_CHUNK * LANES
    assert wide == 2 * RANGE_STATE
    wspec = lambda shape: pl.BlockSpec((None,) + shape, lambda r, b: (r, 0, 0))
    sel_spec = pl.BlockSpec((LANES, wide), lambda r, b: (0, 0))
    return pl.pallas_call(
        functools.partial(_ssm_chunk_kernel, seq_per_step, nk),
        grid=(R, nsteps),
        in_specs=[
            pl.BlockSpec((tm, LANES), lambda r, b: (b, r)),
            pl.BlockSpec((seq_per_step, None, nsub, LANES), lambda r, b: (b, r, 0, 0)),
            wspec((wide, LANES)),
            wspec((wide, LANES)),
            wspec((wide, LANES)),
            sel_spec,
            sel_spec,
            wspec((nsub, LANES)),
            wspec((nsub, LANES)),
        ],
        out_specs=[
            pl.BlockSpec((tm, LANES), lambda r, b: (b, r)),
            pl.BlockSpec((seq_per_step, None, nsub, LANES), lambda r, b: (b, r, 0, 0)),
        ],
        out_shape=[
            jax.ShapeDtypeStruct((M, SSM_WIDTH), F32),
            jax.ShapeDtypeStruct((nseq_total, R, nsub, LANES), F32),
        ],
        scratch_shapes=[pltpu.VMEM((seq_per_step * nk * nsub, LANES), F32)] + [pltpu.VMEM((wide, wide), BF16)] * 3,
        compiler_params=_cparams(("arbitrary", "arbitrary")),
        name="ssm_chunk",
    )(u, h0, ek, eb, ea, sel_k, sel_b, lnr, lni)


def _state_to_tiles(re, im):
    N = re.shape[0]
    half = RANGE_STATE // LANES
    return jnp.concatenate([re.reshape(N, N_SSM_RANGES, half, LANES),
                            im.reshape(N, N_SSM_RANGES, half, LANES)], axis=2)


def _tiles_to_state(t):
    N = t.shape[0]
    half = RANGE_STATE // LANES
    return (t[:, :, :half].reshape(N, N_SSM_GROUPS, SSM_STATE),
            t[:, :, half:].reshape(N, N_SSM_GROUPS, SSM_STATE))


def _bucket_np(dist):
    max_exact = N_BUCKETS // 2
    n = np.maximum(dist, 0)
    nf = np.maximum(n, 1).astype(np.float64)
    large = max_exact + (np.log(nf / max_exact) / math.log(REL_MAX_DISTANCE / max_exact)
                         * (N_BUCKETS - max_exact)).astype(np.int32)
    large = np.minimum(large, N_BUCKETS - 1)
    return np.where(n < max_exact, n, large)


def _prompt_bucket_tiles():
    a = np.arange(Q_BLOCK)[:, None]
    c = np.arange(2 * Q_BLOCK)[None, :]
    rel = a - c + Q_BLOCK
    tiles = []
    for window, dil in WINDOWS:
        K = window // dil + 1
        valid = (rel >= 0) & (rel < K)
        tiles.append(np.where(valid, _bucket_np(np.clip(rel, 0, K - 1) * dil), -1))
    return jnp.asarray(np.stack(tiles), jnp.int32)


def _attn_group_of_step(step):
    return (step + ATTN_FIRST_GROUP) % N_ATT_GROUPS


def _attn_prompt_kernel(T, tab_ref, q_ref, k_ref, v_ref, bkt_ref, o_ref,
                        m_scr, l_scr, acc_scr, s_scr, p_scr, mb_scr, bias_scr):
    h = pl.program_id(1)
    step = pl.program_id(2)
    g = _attn_group_of_step(step)
    scale = HEAD_DIM ** -0.5
    nblk = T // Q_BLOCK

    bkt = bkt_ref[...]
    col = g * HEADS_PER_GROUP + h
    bias = jnp.full(bkt.shape, NEG_INF, F32)
    for t in range(N_BUCKETS):
        bias = jnp.where(bkt == t, tab_ref[t, col], bias)
    bias_scr[...] = bias

    def run_group(first, dil):
        per_class = nblk // dil

        def rows(i):
            r = i // per_class
            n = i % per_class
            row_q = r + n * (Q_BLOCK * dil)
            row_p = r + jnp.maximum(n - 1, 0) * (Q_BLOCK * dil)
            return n, pl.ds(row_q, Q_BLOCK, stride=dil), pl.ds(row_p, Q_BLOCK, stride=dil)

        def scores(i, carry):
            n, sl_q, sl_p = rows(i)
            q = q_ref[sl_q, :].astype(BF16)
            s_r = _dot_nt(q, k_ref[sl_q, :].astype(BF16)) * scale + bias_scr[:, Q_BLOCK:]
            s_l = (_dot_nt(q, k_ref[sl_p, :].astype(BF16)) * scale
                   + jnp.where(n == 0, NEG_INF, bias_scr[:, :Q_BLOCK]))
            s_scr[i, :, :Q_BLOCK] = s_l
            s_scr[i, :, Q_BLOCK:] = s_r
            m = jnp.maximum(jnp.max(s_l, axis=-1, keepdims=True), jnp.max(s_r, axis=-1, keepdims=True))
            mb_scr[i] = jnp.broadcast_to(m, (Q_BLOCK, HEAD_DIM))
            return carry

        def probs(i, carry):
            mb = mb_scr[i]
            p_scr[i, :, :Q_BLOCK] = jnp.exp(s_scr[i, :, :Q_BLOCK] - mb).astype(BF16)
            p_scr[i, :, Q_BLOCK:] = jnp.exp(s_scr[i, :, Q_BLOCK:] - mb).astype(BF16)
            return carry

        def values(i, carry):
            _, sl_q, sl_p = rows(i)
            ones = jnp.ones((Q_BLOCK, HEAD_DIM), BF16)
            v_p = jnp.concatenate([v_ref[sl_p, :].astype(BF16), ones], axis=1)
            v_c = jnp.concatenate([v_ref[sl_q, :].astype(BF16), ones], axis=1)
            ol = _dot(p_scr[i, :, :Q_BLOCK], v_p) + _dot(p_scr[i, :, Q_BLOCK:], v_c)
            o = ol[:, :HEAD_DIM]
            lb = ol[:, HEAD_DIM:]
            mb = mb_scr[i]
            if first:
                m_scr[sl_q, :] = mb
                l_scr[sl_q, :] = lb
                acc_scr[sl_q, :] = o
            else:
                m0 = m_scr[sl_q, :]
                mn = jnp.maximum(m0, mb)
                a0 = jnp.exp(m0 - mn)
                a1 = jnp.exp(mb - mn)
                m_scr[sl_q, :] = mn
                l_scr[sl_q, :] = a0 * l_scr[sl_q, :] + a1 * lb
                acc_scr[sl_q, :] = a0 * acc_scr[sl_q, :] + a1 * o
            return carry

        lax.fori_loop(0, nblk, scores, 0, unroll=ATTN_UNROLL)
        lax.fori_loop(0, nblk, probs, 0, unroll=ATTN_UNROLL)
        lax.fori_loop(0, nblk, values, 0, unroll=ATTN_UNROLL)

    for s in range(N_ATT_GROUPS):
        dil = WINDOWS[(s + ATTN_FIRST_GROUP) % N_ATT_GROUPS][1]
        pl.when(step == s)(functools.partial(run_group, s == 0, dil))

    @pl.when(step == N_ATT_GROUPS - 1)
    def _():
        o_ref[...] = (acc_scr[...] / l_scr[...]).astype(o_ref.dtype)


def _attn_prompt(qkv, rel_bias, bucket_tiles, nbatch):
    M = qkv.shape[1]
    T = M // nbatch
    H, G = HEADS_PER_GROUP, N_ATT_GROUPS
    nblk = T // Q_BLOCK

    def qkv_spec(which):
        return pl.BlockSpec((None, T, HEAD_DIM),
                            lambda b, h, s: (which * N_ATT_HEADS + _attn_group_of_step(s) * H + h, b, 0))

    return pl.pallas_call(
        functools.partial(_attn_prompt_kernel, T),
        grid=(nbatch, H, G),
        in_specs=[pl.BlockSpec(memory_space=pltpu.SMEM),
                  qkv_spec(0), qkv_spec(1), qkv_spec(2),
                  pl.BlockSpec((None, Q_BLOCK, 2 * Q_BLOCK), lambda b, h, s: (_attn_group_of_step(s), 0, 0))],
        out_specs=pl.BlockSpec((T, HEAD_DIM), lambda b, h, s: (b, h)),
        out_shape=jax.ShapeDtypeStruct((M, ATT_WIDTH), BF16),
        scratch_shapes=[pltpu.VMEM((T, HEAD_DIM), F32)] * 3
                       + [pltpu.VMEM((nblk, Q_BLOCK, 2 * Q_BLOCK), F32),
                          pltpu.VMEM((nblk, Q_BLOCK, 2 * Q_BLOCK), BF16),
                          pltpu.VMEM((nblk, Q_BLOCK, HEAD_DIM), F32),
                          pltpu.VMEM((Q_BLOCK, 2 * Q_BLOCK), F32)],
        compiler_params=_cparams(("arbitrary", "arbitrary", "arbitrary")),
        name="attn_prompt",
    )(rel_bias.astype(F32), qkv, qkv, qkv, bucket_tiles)


def _attn_sample_kernel(window, dil, q_ref, ck_ref, cv_ref, nk_ref, nv_ref, bias_ref, o_ref, lse_ref):
    S = q_ref.shape[0]
    K = window // dil + 1
    scale = HEAD_DIM ** -0.5
    bias = bias_ref[...]
    for s in range(S):
        n_c = (window - 1 - s) // dil + 1
        qs = q_ref[s]
        new_rows = [s + j * dil - window for j in range(n_c, K)]
        kk = jnp.concatenate([ck_ref[pl.ds(s, n_c, stride=dil)]] + [nk_ref[pl.ds(i, 1)] for i in new_rows], axis=0)
        vv = jnp.concatenate([cv_ref[pl.ds(s, n_c, stride=dil)]] + [nv_ref[pl.ds(i, 1)] for i in new_rows], axis=0)
        lg = jnp.sum(kk * qs[None], axis=-1, keepdims=True) * scale + bias
        m = jnp.max(lg, axis=0)
        p = jnp.exp(lg - m[None])
        l = jnp.sum(p, axis=0)
        o_ref[s] = jnp.sum(p * vv, axis=0) / l
        lse_ref[s] = m + jnp.log(l)


def _attn_sample(q, cache_k, cache_v, new_k, new_v, bias, layer, window, dil):
    B, S = q.shape[0], q.shape[1]
    H = HEADS_PER_GROUP
    K = window // dil + 1
    small = pl.BlockSpec((None, S, H, HEAD_DIM), lambda b: (b, 0, 0, 0))
    cache = pl.BlockSpec((None, None, window, H, HEAD_DIM), lambda b: (layer, b, 0, 0, 0))
    return pl.pallas_call(
        functools.partial(_attn_sample_kernel, window, dil),
        grid=(B,),
        in_specs=[small, cache, cache, small, small, pl.BlockSpec((K, H, HEAD_DIM), lambda b: (0, 0, 0))],
        out_specs=[small, small],
        out_shape=[jax.ShapeDtypeStruct((B, S, H, HEAD_DIM), F32)] * 2,
        compiler_params=_cparams(("arbitrary",)),
        name="attn_sample_w%d" % window,
    )(q, cache_k, cache_v, new_k, new_v, bias)


def _merge_groups_kernel(o0, o1, o2, l0, l1, l2, y_ref):
    a, b, c = l0[...], l1[...], l2[...]
    m = jnp.maximum(jnp.maximum(a, b), c)
    ea, eb, ec = jnp.exp(a - m), jnp.exp(b - m), jnp.exp(c - m)
    y_ref[...] = ((ea * o0[...] + eb * o1[...] + ec * o2[...]) / (ea + eb + ec)).astype(y_ref.dtype)


def _merge_groups(outs, lses):
    shape = outs[0].shape
    return pl.pallas_call(
        _merge_groups_kernel,
        out_shape=jax.ShapeDtypeStruct(shape, BF16),
        name="merge_groups",
    )(*outs, *lses)


def _sample_bias(rel_bias):
    out = []
    for g, (window, dil) in enumerate(WINDOWS):
        K = window // dil + 1
        steps = (K - 1) - np.arange(K)
        tab = rel_bias[:, g * HEADS_PER_GROUP:(g + 1) * HEADS_PER_GROUP].astype(F32)
        b = tab[_bucket_np(steps * dil)]
        out.append(jnp.broadcast_to(b[:, :, None], (K, HEADS_PER_GROUP, HEAD_DIM)))
    return out


def _gelu_tanh(x):
    return 0.5 * x * (1.0 + jnp.tanh(math.sqrt(2.0 / math.pi) * (x + 0.044715 * (x * x * x))))


def _mix_out_kernel(y_ref, u_ref, yb_ref, ga_ref, gb_ref, x_ref, d_ref, wglu_ref, bglu_ref, wa_ref, wb_ref,
                    wout_ref, gn_ref, x1_ref, h2_ref):
    y = y_ref[...] + d_ref[...] * u_ref[...]
    z = _gelu_tanh(y)
    ya = z * jax.nn.sigmoid(_dot(z.astype(BF16), wglu_ref[...]) + bglu_ref[...])
    mix = (ga_ref[...] * _dot(ya.astype(BF16), wa_ref[...])
           + gb_ref[...] * _dot(yb_ref[...], wb_ref[...]))
    x1 = x_ref[...] + _dot(mix.astype(BF16), wout_ref[...])
    x1_ref[...] = x1
    ms = jnp.mean(x1 * x1, axis=-1, keepdims=True)
    h2_ref[...] = (x1 * lax.rsqrt(ms + EPS) * gn_ref[...]).astype(BF16)


def _mix_out(y_ssm, u, yb, gates, x, ssm_d, wglu, bglu, wa, wb, wout, gain, tm):
    M = x.shape[0]
    row = lambda w: pl.BlockSpec((tm, w), lambda i: (i, 0))
    full = lambda a, b: pl.BlockSpec((a, b), lambda i: (0, 0), pipeline_mode=pl.Buffered(1))
    return pl.pallas_call(
        _mix_out_kernel,
        grid=(M // tm,),
        in_specs=[row(SSM_WIDTH), row(SSM_WIDTH), row(ATT_WIDTH),
                  pl.BlockSpec((tm, D_MODEL), lambda i: (i, 0)),
                  pl.BlockSpec((tm, D_MODEL), lambda i: (i, 1)),
                  row(D_MODEL),
                  full(1, SSM_WIDTH), full(SSM_WIDTH, SSM_WIDTH), full(1, SSM_WIDTH),
                  full(SSM_WIDTH, D_MODEL), full(ATT_WIDTH, D_MODEL), full(D_MODEL, D_MODEL), full(1, D_MODEL)],
        out_specs=[row(D_MODEL), row(D_MODEL)],
        out_shape=[jax.ShapeDtypeStruct((M, D_MODEL), F32), jax.ShapeDtypeStruct((M, D_MODEL), BF16)],
        compiler_params=_cparams(("arbitrary",)),
        name="mix_out",
    )(y_ssm, u, yb, gates, gates, x, ssm_d.reshape(1, -1), wglu, bglu.reshape(1, -1), wa, wb, wout,
      gain.reshape(1, -1))


def _ffn_kernel(h_ref, x_ref, wg_ref, wu_ref, wd_ref, o_ref):
    @pl.when(pl.program_id(1) == 0)
    def _():
        o_ref[...] = x_ref[...]

    tm = h_ref.shape[0]
    sub = min(tm, FFN_ROW_SUB)
    for r in range(tm // sub):
        rows = pl.ds(r * sub, sub)
        h = h_ref[rows, :]
        a = jax.nn.silu(_dot(h, wg_ref[...])) * _dot(h, wu_ref[...])
        o_ref[rows, :] += _dot(a.astype(BF16), wd_ref[...])


def _ffn(h2, x1, wg, wu, wd, tm):
    M = x1.shape[0]
    nf = FFN_HIDDEN // FFN_BLOCK
    return pl.pallas_call(
        _ffn_kernel,
        grid=(M // tm, nf),
        in_specs=[pl.BlockSpec((tm, D_MODEL), lambda i, f: (i, 0)),
                  pl.BlockSpec((tm, D_MODEL), lambda i, f: (i, 0)),
                  pl.BlockSpec((D_MODEL, FFN_BLOCK), lambda i, f: (0, f)),
                  pl.BlockSpec((D_MODEL, FFN_BLOCK), lambda i, f: (0, f)),
                  pl.BlockSpec((FFN_BLOCK, D_MODEL), lambda i, f: (f, 0))],
        out_specs=pl.BlockSpec((tm, D_MODEL), lambda i, f: (i, 0)),
        out_shape=jax.ShapeDtypeStruct((M, D_MODEL), F32),
        compiler_params=_cparams(("arbitrary", "arbitrary")),
        name="ffn",
    )(h2, x1, wg, wu, wd)


def _cache_shift_kernel(ck_ref, cv_ref, hk_ref, hv_ref, nk_ref, nv_ref, ok_ref, ov_ref):
    c = pl.program_id(2)
    last = pl.num_programs(2) - 1
    R = ck_ref.shape[0]
    S = nk_ref.shape[0]
    for cache, halo, new, out in ((ck_ref, hk_ref, nk_ref, ok_ref), (cv_ref, hv_ref, nv_ref, ov_ref)):
        out[pl.ds(0, R - S)] = cache[pl.ds(S, R - S)]

        @pl.when(c == last)
        def _():
            out[pl.ds(R - S, S)] = new[...]

        @pl.when(c < last)
        def _():
            out[pl.ds(R - S, S)] = halo[...]


def _cache_shift(cache_k, cache_v, new_k, new_v):
    depth, B, W, H, E = cache_k.shape
    S = new_k.shape[2]
    R = min(W, CACHE_ROWS)
    nchunks = W // R
    blk = pl.BlockSpec((None, None, R, H, E), lambda l, b, c: (l, b, c, 0, 0))
    halo = pl.BlockSpec((None, None, None, S, H, E),
                        lambda l, b, c: (l, b, jnp.minimum((c + 1) * (R // S), W // S - 1), 0, 0, 0))
    new = pl.BlockSpec((None, None, S, H, E), lambda l, b, c: (l, b, 0, 0, 0))
    as_rows = lambda a: a.reshape(depth, B, W // S, S, H, E)
    return pl.pallas_call(
        _cache_shift_kernel,
        grid=(depth, B, nchunks),
        in_specs=[blk, blk, halo, halo, new, new],
        out_specs=[blk, blk],
        out_shape=[jax.ShapeDtypeStruct(cache_k.shape, cache_k.dtype)] * 2,
        compiler_params=_cparams(("arbitrary", "arbitrary", "arbitrary")),
        name="cache_shift_w%d" % W,
    )(cache_k, cache_v, as_rows(cache_k), as_rows(cache_v), new_k, new_v)


def _kv_tails_kernel(k_ref, v_ref, *rest):
    ok_ref, ov_ref = rest[-2:]
    H, R = k_ref.shape[0], k_ref.shape[1]
    for src, out in ((k_ref, ok_ref), (v_ref, ov_ref)):
        for h in range(H):
            out[pl.ds(h, R, stride=H), :] = src[h]


def _kv_tails(qkv, g, window, nbatch, layer, depth, prev):
    M = qkv.shape[1]
    T = M // nbatch
    H, E = HEADS_PER_GROUP, HEAD_DIM
    R = min(window, CACHE_ROWS)
    nchunks = window // R
    first = (T - window) // R

    def src(which):
        return pl.BlockSpec((H, R, E), lambda b, c: (which * N_ATT_GROUPS + g, b * (T // R) + first + c, 0))

    args, in_specs, aliases = [qkv, qkv], [src(1), src(2)], {}
    if prev is not None:
        args += list(prev)
        in_specs += [pl.BlockSpec(memory_space=pl.ANY)] * 2
        aliases = {2: 0, 3: 1}
    out_spec = pl.BlockSpec((None, R * H, E), lambda b, c: (layer, b * nchunks + c, 0))
    return pl.pallas_call(
        _kv_tails_kernel,
        grid=(nbatch, nchunks),
        in_specs=in_specs,
        out_specs=[out_spec, out_spec],
        out_shape=[jax.ShapeDtypeStruct((depth, nbatch * window * H, E), qkv.dtype)] * 2,
        input_output_aliases=aliases,
        compiler_params=_cparams(("arbitrary", "arbitrary")),
        name="kv_tails_w%d" % window,
    )(*args)


def _layer_common(x2d, h0_tiles, nseq, seq_per_step, prm, tm):
    u, qkv, gates = _in_proj(x2d, prm["norm_mix"], prm["w_in"], prm["q_norm"], prm["k_norm"], tm)
    y_ssm, hfin = _ssm_chunk(u, h0_tiles, prm["ek"], prm["eb"], prm["ea"], prm["sel_k"], prm["sel_b"],
                             prm["lnr"], prm["lni"], nseq, seq_per_step)
    return u, qkv, gates, y_ssm, hfin


def _layer_tail(x2d, u, y_ssm, yb, gates, prm, tm_mix, tm_ffn):
    x1, h2 = _mix_out(y_ssm, u, yb, gates, x2d, prm["ssm_d"], prm["w_glu"], prm["b_glu"], prm["w_branch_a"],
                      prm["w_branch_b"], prm["w_out"], prm["norm_ffn"], tm_mix)
    return _ffn(h2, x1, prm["w_ffn_gate"], prm["w_ffn_up"], prm["w_ffn_down"], tm_ffn)


def kernel(x_prompt, x_sample, state_ssm_re, state_ssm_im, cache_k_w128, cache_v_w128, cache_k_w512, cache_v_w512,
           cache_k_w2048, cache_v_w2048, rel_bias, norm_mix, norm_ffn, q_norm, k_norm, w_in, ssm_lambda_re,
           ssm_lambda_im, ssm_log_dt, ssm_b_re, ssm_b_im, ssm_c_re, ssm_c_im, ssm_d, w_glu, b_glu, w_branch_a,
           w_branch_b, w_out, w_ffn_gate, w_ffn_up, w_ffn_down):
    depth = w_in.shape[0]
    B, T, _ = x_prompt.shape
    SB, S, _ = x_sample.shape
    H = HEADS_PER_GROUP
    caches_k = (cache_k_w128, cache_k_w512, cache_k_w2048)
    caches_v = (cache_v_w128, cache_v_w512, cache_v_w2048)

    bucket_tiles = _prompt_bucket_tiles()
    bias_sample = _sample_bias(rel_bias)
    sel_k, sel_b = _ssm_select_matrices()

    xp = x_prompt.reshape(B * T, D_MODEL)
    xs = x_sample.reshape(SB * S, D_MODEL)
    p_re, p_im, s_re, s_im = [], [], [], []
    p_kv = [None for _ in WINDOWS]
    new_k = [[] for _ in WINDOWS]
    new_v = [[] for _ in WINDOWS]

    for l in range(depth):
        ek, eb, ea, lnr, lni = _ssm_prep(ssm_lambda_re[l], ssm_lambda_im[l], ssm_log_dt[l], ssm_b_re[l],
                                         ssm_b_im[l], ssm_c_re[l], ssm_c_im[l])
        prm = {
            "norm_mix": norm_mix[l], "norm_ffn": norm_ffn[l], "q_norm": q_norm[l], "k_norm": k_norm[l],
            "w_in": w_in[l].astype(BF16), "ssm_d": ssm_d[l], "w_glu": w_glu[l].astype(BF16), "b_glu": b_glu[l],
            "w_branch_a": w_branch_a[l].astype(BF16), "w_branch_b": w_branch_b[l].astype(BF16),
            "w_out": w_out[l].astype(BF16), "w_ffn_gate": w_ffn_gate[l].astype(BF16),
            "w_ffn_up": w_ffn_up[l].astype(BF16), "w_ffn_down": w_ffn_down[l].astype(BF16),
            "ek": ek, "eb": eb, "ea": ea, "sel_k": sel_k, "sel_b": sel_b, "lnr": lnr, "lni": lni,
        }

        zeros = jnp.zeros((B, N_SSM_RANGES, 2 * RANGE_STATE // LANES, LANES), F32)
        u, qkv, gates, y_ssm, hfin = _layer_common(xp, zeros, B, B, prm, 512)
        yb = _attn_prompt(qkv, rel_bias, bucket_tiles, B)
        xp = _layer_tail(xp, u, y_ssm, yb, gates, prm, 256, 512)
        hr, hi = _tiles_to_state(hfin)
        p_re.append(hr)
        p_im.append(hi)
        for g, (window, _) in enumerate(WINDOWS):
            p_kv[g] = _kv_tails(qkv, g, window, B, l, depth, p_kv[g])

        h0 = _state_to_tiles(state_ssm_re[l], state_ssm_im[l])
        u, qkv, gates, y_ssm, hfin = _layer_common(xs, h0, SB, SB, prm, SB * S)
        tok = jnp.transpose(qkv.reshape(3, N_ATT_GROUPS, H, SB, S, HEAD_DIM), (0, 1, 3, 4, 2, 5))
        outs, lses = [], []
        for g, (window, dil) in enumerate(WINDOWS):
            nk_g, nv_g = tok[1, g], tok[2, g]
            new_k[g].append(nk_g)
            new_v[g].append(nv_g)
            o_g, lse_g = _attn_sample(tok[0, g], caches_k[g], caches_v[g], nk_g, nv_g, bias_sample[g], l,
                                      window, dil)
            outs.append(o_g)
            lses.append(lse_g)
        yb = _merge_groups(outs, lses).reshape(SB * S, ATT_WIDTH)
        xs = _layer_tail(xs, u, y_ssm, yb, gates, prm, SB * S, SB * S)
        hr, hi = _tiles_to_state(hfin)
        s_re.append(hr)
        s_im.append(hi)

    shifted = []
    for g in range(N_ATT_GROUPS):
        shifted += _cache_shift(caches_k[g], caches_v[g], jnp.stack(new_k[g]), jnp.stack(new_v[g]))
    tails = [t.reshape(depth, B, window, H, HEAD_DIM) for (window, _), kv in zip(WINDOWS, p_kv) for t in kv]

    return (xp.reshape(B, T, D_MODEL), xs.reshape(SB, S, D_MODEL),
            jnp.stack(p_re), jnp.stack(p_im), *tails,
            jnp.stack(s_re), jnp.stack(s_im), *shifted)
```

```python
import functools
import math

import numpy as np
import jax
import jax.numpy as jnp
from jax import lax
from jax.experimental import pallas as pl
from jax.experimental.pallas import tpu as pltpu

F32 = jnp.float32
BF16 = jnp.bfloat16

D_MODEL = 2048
HEAD_DIM = 128
HEADS_PER_GROUP = 8
WINDOWS = ((128, 1), (512, 4), (2048, 16))
N_ATT_GROUPS = len(WINDOWS)
N_ATT_HEADS = N_ATT_GROUPS * HEADS_PER_GROUP
ATT_WIDTH = HEADS_PER_GROUP * HEAD_DIM
QKV_WIDTH = N_ATT_HEADS * HEAD_DIM
Q_BLOCK = 128
SSM_WIDTH = D_MODEL // 2
SSM_GROUP = 16
N_SSM_GROUPS = SSM_WIDTH // SSM_GROUP
SSM_STATE = 64
IN_WIDTH = SSM_WIDTH + 3 * QKV_WIDTH + 2 * D_MODEL
FFN_HIDDEN = 5632
N_BUCKETS = 32
REL_MAX_DISTANCE = 2048
EPS = 1e-6
NEG_INF = -1e30

LANES = 128
SUBLANES = 8
VMEM_LIMIT_BYTES = 56 * 1024 * 1024

SSM_CHUNK = 8
SSM_RANGE_GROUPS = LANES // SSM_GROUP
N_SSM_RANGES = N_SSM_GROUPS // SSM_RANGE_GROUPS
RANGE_STATE = SSM_RANGE_GROUPS * SSM_STATE
COL_BLOCK = 1024
FFN_BLOCK = 512
ATTN_UNROLL = 4
ATTN_FIRST_GROUP = 2
CACHE_ROWS = 1024
PROMPT_ROWS = 1024
MIX_ROWS = 256
ROW_SUB = 256
FFN_ROW_SUB = 512


def _cparams(sem):
    return pltpu.CompilerParams(dimension_semantics=sem, vmem_limit_bytes=VMEM_LIMIT_BYTES)


def _dot(a, b):
    return jnp.dot(a, b, preferred_element_type=F32)


def _dot_nt(a, b):
    return lax.dot_general(a, b, (((1,), (1,)), ((), ())), preferred_element_type=F32)


def _ssm_prep_kernel(lre_ref, lim_ref, ldt_ref, btre_ref, btim_ref, cre_ref, cim_ref,
                     kp_ref, bpre_ref, bpim_ref, are_ref, aim_ref, lnre_ref, lnim_ref):
    L = SSM_CHUNK
    lr = lre_ref[...]
    li = lim_ref[...]
    dt = jnp.exp(ldt_ref[...])
    er = jnp.exp(lr * dt)
    lbr = er * jnp.cos(li * dt)
    lbi = er * jnp.sin(li * dt)
    nr = lbr - 1.0
    dd = lr * lr + li * li
    rr = (nr * lr + lbi * li) / dd
    ri = (lbi * lr - nr * li) / dd
    btr = btre_ref[...]
    bti = btim_ref[...]
    bbr = rr * btr - ri * bti
    bbi = rr * bti + ri * btr
    cr = cre_ref[...]
    ci = cim_ref[...]
    pr = [jnp.ones_like(lbr)]
    pi = [jnp.zeros_like(lbr)]
    for _ in range(L):
        pr.append(pr[-1] * lbr - pi[-1] * lbi)
        pi.append(pr[-2] * lbi + pi[-1] * lbr)
    xr = [cr * pr[t] - ci * pi[t] for t in range(L + 1)]
    xi = [cr * pi[t] + ci * pr[t] for t in range(L + 1)]
    xr_k = jnp.concatenate(xr[:L], axis=0)
    xi_k = jnp.concatenate(xi[:L], axis=0)
    hp = lax.Precision.HIGHEST
    kp = (lax.dot_general(bbr, xr_k, (((1,), (1,)), ((), ())), precision=hp, preferred_element_type=F32)
          - lax.dot_general(bbi, xi_k, (((1,), (1,)), ((), ())), precision=hp, preferred_element_type=F32))
    lane = lax.broadcasted_iota(jnp.int32, kp.shape, 1)
    blocks = [kp] + [jnp.where(lane >= j * SSM_GROUP, pltpu.roll(kp, j * SSM_GROUP, axis=1), 0.0)
                     for j in range(1, L)]
    kp_ref[...] = jnp.concatenate(blocks, axis=0)
    are_ref[...] = jnp.concatenate(xr[1:], axis=0)
    aim_ref[...] = jnp.concatenate(xi[1:], axis=0)
    bpre_ref[...] = jnp.concatenate([pr[L - 1 - j] * bbr - pi[L - 1 - j] * bbi for j in range(L)], axis=0)
    bpim_ref[...] = jnp.concatenate([pr[L - 1 - j] * bbi + pi[L - 1 - j] * bbr for j in range(L)], axis=0)
    lnre_ref[...] = pr[L]
    lnim_ref[...] = pi[L]


def _ssm_prep(lam_re, lam_im, log_dt, b_re, b_im, c_re, c_im):
    G, P, C, L = N_SSM_GROUPS, SSM_STATE, SSM_GROUP, SSM_CHUNK
    row = lambda a: a.reshape(G, 1, -1)
    bt_re = jnp.swapaxes(b_re, 1, 2)
    bt_im = jnp.swapaxes(b_im, 1, 2)
    vec = pl.BlockSpec((None, 1, P), lambda g: (g, 0, 0))
    mat = pl.BlockSpec((None, C, P), lambda g: (g, 0, 0))
    big = pl.BlockSpec((None, L * C, P), lambda g: (g, 0, 0))
    kp, bpre, bpim, are, aim, lnre, lnim = pl.pallas_call(
        _ssm_prep_kernel,
        grid=(G,),
        in_specs=[vec, vec, pl.BlockSpec((None, 1, 1), lambda g: (g, 0, 0)), mat, mat, mat, mat],
        out_specs=[pl.BlockSpec((None, L * C, L * C), lambda g: (g, 0, 0)), big, big, big, big, vec, vec],
        out_shape=[jax.ShapeDtypeStruct((G, L * C, L * C), F32)] + [jax.ShapeDtypeStruct((G, L * C, P), F32)] * 4
                  + [jax.ShapeDtypeStruct((G, 1, P), F32)] * 2,
        compiler_params=_cparams(("arbitrary",)),
        name="ssm_prep",
    )(row(lam_re), row(lam_im), log_dt.reshape(G, 1, 1), bt_re, bt_im, c_re, c_im)

    R, GL = N_SSM_RANGES, SSM_RANGE_GROUPS

    def by_range(a, n_outer):
        n_inner = a.shape[1] // n_outer
        a = a.reshape(R, GL, n_outer, n_inner, LANES)
        return jnp.transpose(a, (0, 2, 1, 3, 4)).reshape(R, n_outer * GL * n_inner, LANES).astype(BF16)

    ek = by_range(kp, L)
    eb = by_range(jnp.concatenate([bpre, bpim], axis=-1), L)
    a_t = jnp.swapaxes(jnp.concatenate([are, -aim], axis=-1), 1, 2)
    ea = by_range(a_t, 2)

    half = RANGE_STATE // LANES
    lnr_t = lnre.reshape(R, half, LANES)
    lni_t = lnim.reshape(R, half, LANES)
    lnr = jnp.concatenate([lnr_t, lnr_t], axis=1)
    lni = jnp.concatenate([-lni_t, lni_t], axis=1)
    return ek, eb, ea, lnr, lni


def _ssm_select_matrices():
    q = np.arange(SSM_CHUNK * LANES)
    r = np.arange(LANES)
    sel_k = (r[:, None] // SSM_GROUP == q[None, :] // LANES) & (r[:, None] % SSM_GROUP == q[None, :] % SSM_GROUP)
    sel_b = (r[:, None] // SSM_STATE == q[None, :] // RANGE_STATE) & (r[:, None] % SSM_STATE == q[None, :] % SSM_STATE)
    return jnp.asarray(sel_k, BF16), jnp.asarray(sel_b, BF16)


def _head_norm(res, gain):
    outs = []
    for h in range(HEADS_PER_GROUP):
        t = res[:, h * HEAD_DIM:(h + 1) * HEAD_DIM]
        ms = jnp.mean(t * t, axis=-1, keepdims=True)
        outs.append(t * lax.rsqrt(ms + EPS) * gain)
    return outs


def _in_proj_kernel(x_ref, g_ref, w_ref, qn_ref, kn_ref, u_ref, qkv_ref, gate_ref, h_scr):
    j = pl.program_id(1)
    tm = x_ref.shape[0]
    sub = min(tm, ROW_SUB)

    def row_blocks():
        for r in range(tm // sub):
            rows = pl.ds(r * sub, sub)
            yield rows, _dot(h_scr[rows, :], w_ref[...])

    @pl.when(j == 0)
    def _():
        x = x_ref[...]
        ms = jnp.mean(x * x, axis=-1, keepdims=True)
        h_scr[...] = (x * lax.rsqrt(ms + EPS) * g_ref[...]).astype(BF16)
        for rows, res in row_blocks():
            u_ref[rows, :] = res

    def store_heads(gain_ref):
        for rows, res in row_blocks():
            heads = (_head_norm(res, gain_ref[...]) if gain_ref is not None
                     else [res[:, h * HEAD_DIM:(h + 1) * HEAD_DIM] for h in range(HEADS_PER_GROUP)])
            for h, t in enumerate(heads):
                qkv_ref[h, rows, :] = t

    pl.when((j >= 1) & (j <= 3))(functools.partial(store_heads, qn_ref))
    pl.when((j >= 4) & (j <= 6))(functools.partial(store_heads, kn_ref))
    pl.when((j >= 7) & (j <= 9))(functools.partial(store_heads, None))

    @pl.when(j >= 10)
    def _():
        for rows, res in row_blocks():
            gate_ref[rows, :] = jax.nn.sigmoid(res).astype(gate_ref.dtype)


def _in_proj(x, gain, w_bf16, layer, q_gain, k_gain, tm):
    M = x.shape[0]
    nj = IN_WIDTH // COL_BLOCK
    n_qkv = 3 * N_ATT_GROUPS
    return pl.pallas_call(
        _in_proj_kernel,
        grid=(M // tm, nj),
        in_specs=[
            pl.BlockSpec((tm, D_MODEL), lambda i, j: (i, 0), pipeline_mode=pl.Buffered(1)),
            pl.BlockSpec((1, D_MODEL), lambda i, j: (0, 0)),
            pl.BlockSpec((None, D_MODEL, COL_BLOCK), lambda i, j: (layer, 0, j)),
            pl.BlockSpec((1, HEAD_DIM), lambda i, j: (0, 0)),
            pl.BlockSpec((1, HEAD_DIM), lambda i, j: (0, 0)),
        ],
        out_specs=[
            pl.BlockSpec((tm, COL_BLOCK), lambda i, j: (i, 0)),
            pl.BlockSpec((HEADS_PER_GROUP, tm, HEAD_DIM), lambda i, j: (jnp.clip(j - 1, 0, n_qkv - 1), i, 0)),
            pl.BlockSpec((tm, COL_BLOCK), lambda i, j: (i, jnp.clip(j - 10, 0, 3))),
        ],
        out_shape=[
            jax.ShapeDtypeStruct((M, SSM_WIDTH), F32),
            jax.ShapeDtypeStruct((n_qkv * HEADS_PER_GROUP, M, HEAD_DIM), F32),
            jax.ShapeDtypeStruct((M, 2 * D_MODEL), BF16),
        ],
        scratch_shapes=[pltpu.VMEM((tm, D_MODEL), BF16)],
        compiler_params=_cparams(("arbitrary", "arbitrary")),
        name="in_proj",
    )(x, gain.reshape(1, D_MODEL), w_bf16, q_gain.reshape(1, HEAD_DIM), k_gain.reshape(1, HEAD_DIM))


def _expand_block_diag(e_ref, sel_ref, row_shift, col_shift, out_scr):
    n = out_scr.shape[0]
    for c in range(n // LANES):
        rows = pl.ds(c * LANES, LANES)
        full = _dot(e_ref[rows, :], sel_ref[...])
        row = lax.broadcasted_iota(jnp.int32, full.shape, 0) + c * LANES
        col = lax.broadcasted_iota(jnp.int32, full.shape, 1)
        keep = ((row >> row_shift) & (SSM_RANGE_GROUPS - 1)) == ((col >> col_shift) & (SSM_RANGE_GROUPS - 1))
        out_scr[rows, :] = jnp.where(keep, full, 0.0).astype(BF16)


def _ssm_chunk_kernel(nseq, nk, u_ref, h0_ref, ek_ref, eb_ref, ea_ref, selk_ref, selb_ref, lnr_ref, lni_ref,
                      y_ref, hfin_ref, s_scr, ktoe_ref, bcat_ref, acat_ref):
    L = SSM_CHUNK
    rows = nseq * nk
    nsub = 2 * RANGE_STATE // LANES
    lg_c = SSM_GROUP.bit_length() - 1
    lg_p = SSM_STATE.bit_length() - 1

    @pl.when(pl.program_id(1) == 0)
    def _():
        _expand_block_diag(ek_ref, selk_ref, lg_c, lg_c, ktoe_ref)
        _expand_block_diag(eb_ref, selb_ref, lg_c, lg_p, bcat_ref)
        _expand_block_diag(ea_ref, selk_ref, lg_p, lg_c, acat_ref)

    ucat = jnp.concatenate([u_ref[pl.ds(j, rows, stride=L), :] for j in range(L)], axis=-1).astype(BF16)
    s = _dot(ucat, bcat_ref[...])
    for n in range(nsub):
        s_scr[pl.ds(n, rows, stride=nsub), :] = s[:, n * LANES:(n + 1) * LANES]
    y_ref_intra = _dot(ucat, ktoe_ref[...])

    lnr = lnr_ref[...]
    lni = lni_ref[...]

    def step(k, hs):
        new = []
        for q in range(nseq):
            off = pl.multiple_of((q * nk + k) * nsub, nsub)
            h = hs[q]
            sk = s_scr[pl.ds(off, nsub), :]
            s_scr[pl.ds(off, nsub), :] = h
            new.append(h * lnr + pltpu.roll(h, nsub // 2, axis=0) * lni + sk)
        return tuple(new)

    hs = lax.fori_loop(0, nk, step, tuple(h0_ref[q] for q in range(nseq)))
    for q in range(nseq):
        hfin_ref[q] = hs[q]

    hprev = jnp.concatenate([s_scr[pl.ds(n, rows, stride=nsub), :] for n in range(nsub)], axis=-1).astype(BF16)
    y = y_ref_intra + _dot(hprev, acat_ref[...])
    for j in range(L):
        y_ref[pl.ds(j, rows, stride=L), :] = y[:, j * LANES:(j + 1) * LANES]


def _ssm_chunk(u, h0, ek, eb, ea, sel_k, sel_b, lnr, lni, nseq_total, seq_per_step):
    M = u.shape[0]
    T = M // nseq_total
    nk = T // SSM_CHUNK
    nsteps = nseq_total // seq_per_step
    R = N_SSM_RANGES
    nsub = 2 * RANGE_STATE // LANES
    tm = seq_per_step * T
    wide = SSM_CHUNK * LANES
    assert wide == 2 * RANGE_STATE
    wspec = lambda shape: pl.BlockSpec((None,) + shape, lambda r, b: (r, 0, 0))
    sel_spec = pl.BlockSpec((LANES, wide), lambda r, b: (0, 0))
    return pl.pallas_call(
        functools.partial(_ssm_chunk_kernel, seq_per_step, nk),
        grid=(R, nsteps),
        in_specs=[
            pl.BlockSpec((tm, LANES), lambda r, b: (b, r)),
            pl.BlockSpec((seq_per_step, None, nsub, LANES), lambda r, b: (b, r, 0, 0)),
            wspec((wide, LANES)),
            wspec((wide, LANES)),
            wspec((wide, LANES)),
            sel_spec,
            sel_spec,
            wspec((nsub, LANES)),
            wspec((nsub, LANES)),
        ],
        out_specs=[
            pl.BlockSpec((tm, LANES), lambda r, b: (b, r)),
            pl.BlockSpec((seq_per_step, None, nsub, LANES), lambda r, b: (b, r, 0, 0)),
        ],
        out_shape=[
            jax.ShapeDtypeStruct((M, SSM_WIDTH), F32),
            jax.ShapeDtypeStruct((nseq_total, R, nsub, LANES), F32),
        ],
        scratch_shapes=[pltpu.VMEM((seq_per_step * nk * nsub, LANES), F32)] + [pltpu.VMEM((wide, wide), BF16)] * 3,
        compiler_params=_cparams(("arbitrary", "arbitrary")),
        name="ssm_chunk",
    )(u, h0, ek, eb, ea, sel_k, sel_b, lnr, lni)


def _state_to_tiles(re, im):
    N = re.shape[0]
    half = RANGE_STATE // LANES
    return jnp.concatenate([re.reshape(N, N_SSM_RANGES, half, LANES),
                            im.reshape(N, N_SSM_RANGES, half, LANES)], axis=2)


def _tiles_to_state(t):
    N = t.shape[0]
    half = RANGE_STATE // LANES
    return (t[:, :, :half].reshape(N, N_SSM_GROUPS, SSM_STATE),
            t[:, :, half:].reshape(N, N_SSM_GROUPS, SSM_STATE))


def _bucket_np(dist):
    max_exact = N_BUCKETS // 2
    n = np.maximum(dist, 0)
    nf = np.maximum(n, 1).astype(np.float64)
    large = max_exact + (np.log(nf / max_exact) / math.log(REL_MAX_DISTANCE / max_exact)
                         * (N_BUCKETS - max_exact)).astype(np.int32)
    large = np.minimum(large, N_BUCKETS - 1)
    return np.where(n < max_exact, n, large)


def _prompt_bucket_tiles():
    a = np.arange(Q_BLOCK)[:, None]
    c = np.arange(2 * Q_BLOCK)[None, :]
    rel = a - c + Q_BLOCK
    tiles = []
    for window, dil in WINDOWS:
        K = window // dil + 1
        valid = (rel >= 0) & (rel < K)
        tiles.append(np.where(valid, _bucket_np(np.clip(rel, 0, K - 1) * dil), -1))
    return jnp.asarray(np.stack(tiles), jnp.int32)


def _attn_group_of_step(step):
    return (step + ATTN_FIRST_GROUP) % N_ATT_GROUPS


def _attn_prompt_kernel(T, tab_ref, q_ref, k_ref, v_ref, bkt_ref, o_ref,
                        m_scr, l_scr, acc_scr, s_scr, p_scr, mb_scr, bias_scr):
    h = pl.program_id(1)
    step = pl.program_id(2)
    g = _attn_group_of_step(step)
    scale = HEAD_DIM ** -0.5
    nblk = T // Q_BLOCK

    bkt = bkt_ref[...]
    col = g * HEADS_PER_GROUP + h
    bias = jnp.full(bkt.shape, NEG_INF, F32)
    for t in range(N_BUCKETS):
        bias = jnp.where(bkt == t, tab_ref[t, col], bias)
    bias_scr[...] = bias

    def run_group(first, dil):
        per_class = nblk // dil

        def rows(i):
            r = i // per_class
            n = i % per_class
            row_q = r + n * (Q_BLOCK * dil)
            row_p = r + jnp.maximum(n - 1, 0) * (Q_BLOCK * dil)
            return n, pl.ds(row_q, Q_BLOCK, stride=dil), pl.ds(row_p, Q_BLOCK, stride=dil)

        def scores(i, carry):
            n, sl_q, sl_p = rows(i)
            q = q_ref[sl_q, :].astype(BF16)
            s_r = _dot_nt(q, k_ref[sl_q, :].astype(BF16)) * scale + bias_scr[:, Q_BLOCK:]
            s_l = (_dot_nt(q, k_ref[sl_p, :].astype(BF16)) * scale
                   + jnp.where(n == 0, NEG_INF, bias_scr[:, :Q_BLOCK]))
            s_scr[i, :, :Q_BLOCK] = s_l
            s_scr[i, :, Q_BLOCK:] = s_r
            m = jnp.maximum(jnp.max(s_l, axis=-1, keepdims=True), jnp.max(s_r, axis=-1, keepdims=True))
            mb_scr[i] = jnp.broadcast_to(m, (Q_BLOCK, HEAD_DIM))
            return carry

        def probs(i, carry):
            mb = mb_scr[i]
            p_scr[i, :, :Q_BLOCK] = jnp.exp(s_scr[i, :, :Q_BLOCK] - mb).astype(BF16)
            p_scr[i, :, Q_BLOCK:] = jnp.exp(s_scr[i, :, Q_BLOCK:] - mb).astype(BF16)
            return carry

        def values(i, carry):
            _, sl_q, sl_p = rows(i)
            ones = jnp.ones((Q_BLOCK, HEAD_DIM), BF16)
            v_p = jnp.concatenate([v_ref[sl_p, :].astype(BF16), ones], axis=1)
            v_c = jnp.concatenate([v_ref[sl_q, :].astype(BF16), ones], axis=1)
            ol = _dot(p_scr[i, :, :Q_BLOCK], v_p) + _dot(p_scr[i, :, Q_BLOCK:], v_c)
            o = ol[:, :HEAD_DIM]
            lb = ol[:, HEAD_DIM:]
            mb = mb_scr[i]
            if first:
                m_scr[sl_q, :] = mb
                l_scr[sl_q, :] = lb
                acc_scr[sl_q, :] = o
            else:
                m0 = m_scr[sl_q, :]
                mn = jnp.maximum(m0, mb)
                a0 = jnp.exp(m0 - mn)
                a1 = jnp.exp(mb - mn)
                m_scr[sl_q, :] = mn
                l_scr[sl_q, :] = a0 * l_scr[sl_q, :] + a1 * lb
                acc_scr[sl_q, :] = a0 * acc_scr[sl_q, :] + a1 * o
            return carry

        lax.fori_loop(0, nblk, scores, 0, unroll=ATTN_UNROLL)
        lax.fori_loop(0, nblk, probs, 0, unroll=ATTN_UNROLL)
        lax.fori_loop(0, nblk, values, 0, unroll=ATTN_UNROLL)

    for s in range(N_ATT_GROUPS):
        dil = WINDOWS[(s + ATTN_FIRST_GROUP) % N_ATT_GROUPS][1]
        pl.when(step == s)(functools.partial(run_group, s == 0, dil))

    @pl.when(step == N_ATT_GROUPS - 1)
    def _():
        o_ref[...] = (acc_scr[...] / l_scr[...]).astype(o_ref.dtype)


def _attn_prompt(qkv, rel_bias, bucket_tiles, nbatch):
    M = qkv.shape[1]
    T = M // nbatch
    H, G = HEADS_PER_GROUP, N_ATT_GROUPS
    nblk = T // Q_BLOCK

    def qkv_spec(which):
        return pl.BlockSpec((None, T, HEAD_DIM),
                            lambda b, h, s: (which * N_ATT_HEADS + _attn_group_of_step(s) * H + h, b, 0))

    return pl.pallas_call(
        functools.partial(_attn_prompt_kernel, T),
        grid=(nbatch, H, G),
        in_specs=[pl.BlockSpec(memory_space=pltpu.SMEM),
                  qkv_spec(0), qkv_spec(1), qkv_spec(2),
                  pl.BlockSpec((None, Q_BLOCK, 2 * Q_BLOCK), lambda b, h, s: (_attn_group_of_step(s), 0, 0))],
        out_specs=pl.BlockSpec((T, HEAD_DIM), lambda b, h, s: (b, h)),
        out_shape=jax.ShapeDtypeStruct((M, ATT_WIDTH), BF16),
        scratch_shapes=[pltpu.VMEM((T, HEAD_DIM), F32)] * 3
                       + [pltpu.VMEM((nblk, Q_BLOCK, 2 * Q_BLOCK), F32),
                          pltpu.VMEM((nblk, Q_BLOCK, 2 * Q_BLOCK), BF16),
                          pltpu.VMEM((nblk, Q_BLOCK, HEAD_DIM), F32),
                          pltpu.VMEM((Q_BLOCK, 2 * Q_BLOCK), F32)],
        compiler_params=_cparams(("arbitrary", "arbitrary", "arbitrary")),
        name="attn_prompt",
    )(rel_bias.astype(F32), qkv, qkv, qkv, bucket_tiles)


def _attn_sample_kernel(window, dil, q_ref, ck_ref, cv_ref, nk_ref, nv_ref, bias_ref, o_ref, lse_ref):
    S = q_ref.shape[0]
    K = window // dil + 1
    scale = HEAD_DIM ** -0.5
    bias = bias_ref[...]
    for s in range(S):
        n_c = (window - 1 - s) // dil + 1
        qs = q_ref[s]
        new_rows = [s + j * dil - window for j in range(n_c, K)]
        kk = jnp.concatenate([ck_ref[pl.ds(s, n_c, stride=dil)]] + [nk_ref[pl.ds(i, 1)] for i in new_rows], axis=0)
        vv = jnp.concatenate([cv_ref[pl.ds(s, n_c, stride=dil)]] + [nv_ref[pl.ds(i, 1)] for i in new_rows], axis=0)
        lg = jnp.sum(kk * qs[None], axis=-1, keepdims=True) * scale + bias
        m = jnp.max(lg, axis=0)
        p = jnp.exp(lg - m[None])
        l = jnp.sum(p, axis=0)
        o_ref[s] = jnp.sum(p * vv, axis=0) / l
        lse_ref[s] = m + jnp.log(l)


def _attn_sample(q, cache_k, cache_v, new_k, new_v, bias, layer, window, dil):
    B, S = q.shape[0], q.shape[1]
    H = HEADS_PER_GROUP
    K = window // dil + 1
    small = pl.BlockSpec((None, S, H, HEAD_DIM), lambda b: (b, 0, 0, 0))
    cache = pl.BlockSpec((None, None, window, H, HEAD_DIM), lambda b: (layer, b, 0, 0, 0))
    return pl.pallas_call(
        functools.partial(_attn_sample_kernel, window, dil),
        grid=(B,),
        in_specs=[small, cache, cache, small, small, pl.BlockSpec((K, H, HEAD_DIM), lambda b: (0, 0, 0))],
        out_specs=[small, small],
        out_shape=[jax.ShapeDtypeStruct((B, S, H, HEAD_DIM), F32)] * 2,
        compiler_params=_cparams(("arbitrary",)),
        name="attn_sample_w%d" % window,
    )(q, cache_k, cache_v, new_k, new_v, bias)


def _merge_groups_kernel(o0, o1, o2, l0, l1, l2, y_ref):
    a, b, c = l0[...], l1[...], l2[...]
    m = jnp.maximum(jnp.maximum(a, b), c)
    ea, eb, ec = jnp.exp(a - m), jnp.exp(b - m), jnp.exp(c - m)
    y_ref[...] = ((ea * o0[...] + eb * o1[...] + ec * o2[...]) / (ea + eb + ec)).astype(y_ref.dtype)


def _merge_groups(outs, lses):
    shape = outs[0].shape
    return pl.pallas_call(
        _merge_groups_kernel,
        out_shape=jax.ShapeDtypeStruct(shape, BF16),
        name="merge_groups",
    )(*outs, *lses)


def _sample_bias(rel_bias):
    out = []
    for g, (window, dil) in enumerate(WINDOWS):
        K = window // dil + 1
        steps = (K - 1) - np.arange(K)
        tab = rel_bias[:, g * HEADS_PER_GROUP:(g + 1) * HEADS_PER_GROUP].astype(F32)
        b = tab[_bucket_np(steps * dil)]
        out.append(jnp.broadcast_to(b[:, :, None], (K, HEADS_PER_GROUP, HEAD_DIM)))
    return out


def _gelu_tanh(x):
    return 0.5 * x * (1.0 + jnp.tanh(math.sqrt(2.0 / math.pi) * (x + 0.044715 * (x * x * x))))


def _mix_out_kernel(y_ref, u_ref, yb_ref, ga_ref, gb_ref, x_ref, d_ref, wglu_ref, bglu_ref, wa_ref, wb_ref,
                    wout_ref, gn_ref, x1_ref, h2_ref):
    y = y_ref[...] + d_ref[...] * u_ref[...]
    z = _gelu_tanh(y)
    ya = z * jax.nn.sigmoid(_dot(z.astype(BF16), wglu_ref[...]) + bglu_ref[...])
    mix = (ga_ref[...] * _dot(ya.astype(BF16), wa_ref[...])
           + gb_ref[...] * _dot(yb_ref[...], wb_ref[...]))
    x1 = x_ref[...] + _dot(mix.astype(BF16), wout_ref[...])
    x1_ref[...] = x1
    ms = jnp.mean(x1 * x1, axis=-1, keepdims=True)
    h2_ref[...] = (x1 * lax.rsqrt(ms + EPS) * gn_ref[...]).astype(BF16)


def _mix_out(y_ssm, u, yb, gates, x, ssm_d, wglu, bglu, wa, wb, wout, layer, gain, tm):
    M = x.shape[0]
    row = lambda w: pl.BlockSpec((tm, w), lambda i: (i, 0))
    vec = lambda b: pl.BlockSpec((1, b), lambda i: (0, 0), pipeline_mode=pl.Buffered(1))
    full = lambda a, b: pl.BlockSpec((None, a, b), lambda i: (layer, 0, 0), pipeline_mode=pl.Buffered(1))
    return pl.pallas_call(
        _mix_out_kernel,
        grid=(M // tm,),
        in_specs=[row(SSM_WIDTH), row(SSM_WIDTH), row(ATT_WIDTH),
                  pl.BlockSpec((tm, D_MODEL), lambda i: (i, 0)),
                  pl.BlockSpec((tm, D_MODEL), lambda i: (i, 1)),
                  row(D_MODEL),
                  vec(SSM_WIDTH), full(SSM_WIDTH, SSM_WIDTH), vec(SSM_WIDTH),
                  full(SSM_WIDTH, D_MODEL), full(ATT_WIDTH, D_MODEL), full(D_MODEL, D_MODEL), vec(D_MODEL)],
        out_specs=[row(D_MODEL), row(D_MODEL)],
        out_shape=[jax.ShapeDtypeStruct((M, D_MODEL), F32), jax.ShapeDtypeStruct((M, D_MODEL), BF16)],
        compiler_params=_cparams(("arbitrary",)),
        name="mix_out",
    )(y_ssm, u, yb, gates, gates, x, ssm_d.reshape(1, -1), wglu, bglu.reshape(1, -1), wa, wb, wout,
      gain.reshape(1, -1))


def _ffn_kernel(h_ref, x_ref, wg_ref, wu_ref, wd_ref, o_ref):
    @pl.when(pl.program_id(1) == 0)
    def _():
        o_ref[...] = x_ref[...]

    tm = h_ref.shape[0]
    sub = min(tm, FFN_ROW_SUB)
    for r in range(tm // sub):
        rows = pl.ds(r * sub, sub)
        h = h_ref[rows, :]
        a = jax.nn.silu(_dot(h, wg_ref[...])) * _dot(h, wu_ref[...])
        o_ref[rows, :] += _dot(a.astype(BF16), wd_ref[...])


def _ffn(h2, x1, wg, wu, wd, layer, tm):
    M = x1.shape[0]
    nf = FFN_HIDDEN // FFN_BLOCK
    return pl.pallas_call(
        _ffn_kernel,
        grid=(M // tm, nf),
        in_specs=[pl.BlockSpec((tm, D_MODEL), lambda i, f: (i, 0)),
                  pl.BlockSpec((tm, D_MODEL), lambda i, f: (i, 0), pipeline_mode=pl.Buffered(1)),
                  pl.BlockSpec((None, D_MODEL, FFN_BLOCK), lambda i, f: (layer, 0, f)),
                  pl.BlockSpec((None, D_MODEL, FFN_BLOCK), lambda i, f: (layer, 0, f)),
                  pl.BlockSpec((None, FFN_BLOCK, D_MODEL), lambda i, f: (layer, f, 0))],
        out_specs=pl.BlockSpec((tm, D_MODEL), lambda i, f: (i, 0)),
        out_shape=jax.ShapeDtypeStruct((M, D_MODEL), F32),
        compiler_params=_cparams(("arbitrary", "arbitrary")),
        name="ffn",
    )(h2, x1, wg, wu, wd)


def _cache_shift_kernel(ck_ref, cv_ref, hk_ref, hv_ref, nk_ref, nv_ref, ok_ref, ov_ref):
    c = pl.program_id(2)
    last = pl.num_programs(2) - 1
    R = ck_ref.shape[0]
    S = nk_ref.shape[0]
    for cache, halo, new, out in ((ck_ref, hk_ref, nk_ref, ok_ref), (cv_ref, hv_ref, nv_ref, ov_ref)):
        out[pl.ds(0, R - S)] = cache[pl.ds(S, R - S)]

        @pl.when(c == last)
        def _():
            out[pl.ds(R - S, S)] = new[...]

        @pl.when(c < last)
        def _():
            out[pl.ds(R - S, S)] = halo[...]


def _cache_shift(cache_k, cache_v, new_k, new_v):
    depth, B, W, H, E = cache_k.shape
    S = new_k.shape[2]
    R = min(W, CACHE_ROWS)
    nchunks = W // R
    blk = pl.BlockSpec((None, None, R, H, E), lambda l, b, c: (l, b, c, 0, 0))
    halo = pl.BlockSpec((None, None, None, S, H, E),
                        lambda l, b, c: (l, b, jnp.minimum((c + 1) * (R // S), W // S - 1), 0, 0, 0))
    new = pl.BlockSpec((None, None, S, H, E), lambda l, b, c: (l, b, 0, 0, 0))
    as_rows = lambda a: a.reshape(depth, B, W // S, S, H, E)
    return pl.pallas_call(
        _cache_shift_kernel,
        grid=(depth, B, nchunks),
        in_specs=[blk, blk, halo, halo, new, new],
        out_specs=[blk, blk],
        out_shape=[jax.ShapeDtypeStruct(cache_k.shape, cache_k.dtype)] * 2,
        compiler_params=_cparams(("arbitrary", "arbitrary", "arbitrary")),
        name="cache_shift_w%d" % W,
    )(cache_k, cache_v, as_rows(cache_k), as_rows(cache_v), new_k, new_v)


def _kv_tails_kernel(k_ref, v_ref, *rest):
    ok_ref, ov_ref = rest[-2:]
    H, R = k_ref.shape[0], k_ref.shape[1]
    for src, out in ((k_ref, ok_ref), (v_ref, ov_ref)):
        for h in range(H):
            out[pl.ds(h, R, stride=H), :] = src[h]


def _kv_tails(qkv, g, window, nbatch, layer, depth, prev):
    M = qkv.shape[1]
    T = M // nbatch
    H, E = HEADS_PER_GROUP, HEAD_DIM
    R = min(window, CACHE_ROWS)
    nchunks = window // R
    first = (T - window) // R

    def src(which):
        return pl.BlockSpec((H, R, E), lambda b, c: (which * N_ATT_GROUPS + g, b * (T // R) + first + c, 0))

    args, in_specs, aliases = [qkv, qkv], [src(1), src(2)], {}
    if prev is not None:
        args += list(prev)
        in_specs += [pl.BlockSpec(memory_space=pl.ANY)] * 2
        aliases = {2: 0, 3: 1}
    out_spec = pl.BlockSpec((None, R * H, E), lambda b, c: (layer, b * nchunks + c, 0))
    return pl.pallas_call(
        _kv_tails_kernel,
        grid=(nbatch, nchunks),
        in_specs=in_specs,
        out_specs=[out_spec, out_spec],
        out_shape=[jax.ShapeDtypeStruct((depth, nbatch * window * H, E), qkv.dtype)] * 2,
        input_output_aliases=aliases,
        compiler_params=_cparams(("arbitrary", "arbitrary")),
        name="kv_tails_w%d" % window,
    )(*args)


def _layer_common(x2d, h0_tiles, nseq, seq_per_step, prm, tm):
    u, qkv, gates = _in_proj(x2d, prm["norm_mix"], prm["w_in"], prm["layer"], prm["q_norm"], prm["k_norm"], tm)
    y_ssm, hfin = _ssm_chunk(u, h0_tiles, prm["ek"], prm["eb"], prm["ea"], prm["sel_k"], prm["sel_b"],
                             prm["lnr"], prm["lni"], nseq, seq_per_step)
    return u, qkv, gates, y_ssm, hfin


def _layer_tail(x2d, u, y_ssm, yb, gates, prm, tm_mix, tm_ffn):
    x1, h2 = _mix_out(y_ssm, u, yb, gates, x2d, prm["ssm_d"], prm["w_glu"], prm["b_glu"], prm["w_branch_a"],
                      prm["w_branch_b"], prm["w_out"], prm["layer"], prm["norm_ffn"], tm_mix)
    return _ffn(h2, x1, prm["w_ffn_gate"], prm["w_ffn_up"], prm["w_ffn_down"], prm["layer"], tm_ffn)


def kernel(x_prompt, x_sample, state_ssm_re, state_ssm_im, cache_k_w128, cache_v_w128, cache_k_w512, cache_v_w512,
           cache_k_w2048, cache_v_w2048, rel_bias, norm_mix, norm_ffn, q_norm, k_norm, w_in, ssm_lambda_re,
           ssm_lambda_im, ssm_log_dt, ssm_b_re, ssm_b_im, ssm_c_re, ssm_c_im, ssm_d, w_glu, b_glu, w_branch_a,
           w_branch_b, w_out, w_ffn_gate, w_ffn_up, w_ffn_down):
    depth = w_in.shape[0]
    B, T, _ = x_prompt.shape
    SB, S, _ = x_sample.shape
    H = HEADS_PER_GROUP
    caches_k = (cache_k_w128, cache_k_w512, cache_k_w2048)
    caches_v = (cache_v_w128, cache_v_w512, cache_v_w2048)

    bucket_tiles = _prompt_bucket_tiles()
    bias_sample = _sample_bias(rel_bias)
    sel_k, sel_b = _ssm_select_matrices()

    xp = x_prompt.reshape(B * T, D_MODEL)
    xs = x_sample.reshape(SB * S, D_MODEL)
    p_re, p_im, s_re, s_im = [], [], [], []
    p_kv = [None for _ in WINDOWS]
    new_k = [[] for _ in WINDOWS]
    new_v = [[] for _ in WINDOWS]

    weights = {
        "w_in": w_in.astype(BF16), "w_glu": w_glu.astype(BF16), "w_branch_a": w_branch_a.astype(BF16),
        "w_branch_b": w_branch_b.astype(BF16), "w_out": w_out.astype(BF16), "w_ffn_gate": w_ffn_gate.astype(BF16),
        "w_ffn_up": w_ffn_up.astype(BF16), "w_ffn_down": w_ffn_down.astype(BF16),
    }

    for l in range(depth):
        ek, eb, ea, lnr, lni = _ssm_prep(ssm_lambda_re[l], ssm_lambda_im[l], ssm_log_dt[l], ssm_b_re[l],
                                         ssm_b_im[l], ssm_c_re[l], ssm_c_im[l])
        prm = dict(weights)
        prm.update({
            "layer": l, "norm_mix": norm_mix[l], "norm_ffn": norm_ffn[l], "q_norm": q_norm[l], "k_norm": k_norm[l],
            "ssm_d": ssm_d[l], "b_glu": b_glu[l],
            "ek": ek, "eb": eb, "ea": ea, "sel_k": sel_k, "sel_b": sel_b, "lnr": lnr, "lni": lni,
        })

        zeros = jnp.zeros((B, N_SSM_RANGES, 2 * RANGE_STATE // LANES, LANES), F32)
        u, qkv, gates, y_ssm, hfin = _layer_common(xp, zeros, B, B, prm, PROMPT_ROWS)
        yb = _attn_prompt(qkv, rel_bias, bucket_tiles, B)
        xp = _layer_tail(xp, u, y_ssm, yb, gates, prm, MIX_ROWS, PROMPT_ROWS)
        hr, hi = _tiles_to_state(hfin)
        p_re.append(hr)
        p_im.append(hi)
        for g, (window, _) in enumerate(WINDOWS):
            p_kv[g] = _kv_tails(qkv, g, window, B, l, depth, p_kv[g])

        h0 = _state_to_tiles(state_ssm_re[l], state_ssm_im[l])
        u, qkv, gates, y_ssm, hfin = _layer_common(xs, h0, SB, SB, prm, SB * S)
        tok = jnp.transpose(qkv.reshape(3, N_ATT_GROUPS, H, SB, S, HEAD_DIM), (0, 1, 3, 4, 2, 5))
        outs, lses = [], []
        for g, (window, dil) in enumerate(WINDOWS):
            nk_g, nv_g = tok[1, g], tok[2, g]
            new_k[g].append(nk_g)
            new_v[g].append(nv_g)
            o_g, lse_g = _attn_sample(tok[0, g], caches_k[g], caches_v[g], nk_g, nv_g, bias_sample[g], l,
                                      window, dil)
            outs.append(o_g)
            lses.append(lse_g)
        yb = _merge_groups(outs, lses).reshape(SB * S, ATT_WIDTH)
        xs = _layer_tail(xs, u, y_ssm, yb, gates, prm, SB * S, SB * S)
        hr, hi = _tiles_to_state(hfin)
        s_re.append(hr)
        s_im.append(hi)

    shifted = []
    for g in range(N_ATT_GROUPS):
        shifted += _cache_shift(caches_k[g], caches_v[g], jnp.stack(new_k[g]), jnp.stack(new_v[g]))
    tails = [t.reshape(depth, B, window, H, HEAD_DIM) for (window, _), kv in zip(WINDOWS, p_kv) for t in kv]

    return (xp.reshape(B, T, D_MODEL), xs.reshape(SB, S, D_MODEL),
            jnp.stack(p_re), jnp.stack(p_im), *tails,
            jnp.stack(s_re), jnp.stack(s_im), *shifted)
```

```python
import functools
import math

import numpy as np
import jax
import jax.numpy as jnp
from jax import lax
from jax.experimental import pallas as pl
from jax.experimental.pallas import tpu as pltpu

F32 = jnp.float32
BF16 = jnp.bfloat16

D_MODEL = 2048
HEAD_DIM = 128
HEADS_PER_GROUP = 8
WINDOWS = ((128, 1), (512, 4), (2048, 16))
N_ATT_GROUPS = len(WINDOWS)
N_ATT_HEADS = N_ATT_GROUPS * HEADS_PER_GROUP
ATT_WIDTH = HEADS_PER_GROUP * HEAD_DIM
QKV_WIDTH = N_ATT_HEADS * HEAD_DIM
Q_BLOCK = 128
SSM_WIDTH = D_MODEL // 2
SSM_GROUP = 16
N_SSM_GROUPS = SSM_WIDTH // SSM_GROUP
SSM_STATE = 64
IN_WIDTH = SSM_WIDTH + 3 * QKV_WIDTH + 2 * D_MODEL
FFN_HIDDEN = 5632
N_BUCKETS = 32
REL_MAX_DISTANCE = 2048
EPS = 1e-6
NEG_INF = -1e30

LANES = 128
SUBLANES = 8
VMEM_LIMIT_BYTES = 56 * 1024 * 1024

SSM_CHUNK = 8
SSM_RANGE_GROUPS = LANES // SSM_GROUP
N_SSM_RANGES = N_SSM_GROUPS // SSM_RANGE_GROUPS
RANGE_STATE = SSM_RANGE_GROUPS * SSM_STATE
COL_BLOCK = 1024
FFN_BLOCK = 512
ATTN_UNROLL = 16
ATTN_FIRST_GROUP = 2
CACHE_ROWS = 1024
PROMPT_ROWS = 1024
MIX_ROWS = 256
ROW_SUB = 256
FFN_ROW_SUB = 512


def _cparams(sem):
    return pltpu.CompilerParams(dimension_semantics=sem, vmem_limit_bytes=VMEM_LIMIT_BYTES)


def _dot(a, b):
    return jnp.dot(a, b, preferred_element_type=F32)


def _dot_nt(a, b):
    return lax.dot_general(a, b, (((1,), (1,)), ((), ())), preferred_element_type=F32)


def _ssm_prep_kernel(lre_ref, lim_ref, ldt_ref, btre_ref, btim_ref, cre_ref, cim_ref,
                     kp_ref, bpre_ref, bpim_ref, are_ref, aim_ref, lnre_ref, lnim_ref):
    L = SSM_CHUNK
    lr = lre_ref[...]
    li = lim_ref[...]
    dt = jnp.exp(ldt_ref[...])
    er = jnp.exp(lr * dt)
    lbr = er * jnp.cos(li * dt)
    lbi = er * jnp.sin(li * dt)
    nr = lbr - 1.0
    dd = lr * lr + li * li
    rr = (nr * lr + lbi * li) / dd
    ri = (lbi * lr - nr * li) / dd
    btr = btre_ref[...]
    bti = btim_ref[...]
    bbr = rr * btr - ri * bti
    bbi = rr * bti + ri * btr
    cr = cre_ref[...]
    ci = cim_ref[...]
    pr = [jnp.ones_like(lbr)]
    pi = [jnp.zeros_like(lbr)]
    for _ in range(L):
        pr.append(pr[-1] * lbr - pi[-1] * lbi)
        pi.append(pr[-2] * lbi + pi[-1] * lbr)
    xr = [cr * pr[t] - ci * pi[t] for t in range(L + 1)]
    xi = [cr * pi[t] + ci * pr[t] for t in range(L + 1)]
    xr_k = jnp.concatenate(xr[:L], axis=0)
    xi_k = jnp.concatenate(xi[:L], axis=0)
    hp = lax.Precision.HIGHEST
    kp = (lax.dot_general(bbr, xr_k, (((1,), (1,)), ((), ())), precision=hp, preferred_element_type=F32)
          - lax.dot_general(bbi, xi_k, (((1,), (1,)), ((), ())), precision=hp, preferred_element_type=F32))
    lane = lax.broadcasted_iota(jnp.int32, kp.shape, 1)
    blocks = [kp] + [jnp.where(lane >= j * SSM_GROUP, pltpu.roll(kp, j * SSM_GROUP, axis=1), 0.0)
                     for j in range(1, L)]
    kp_ref[...] = jnp.concatenate(blocks, axis=0)
    are_ref[...] = jnp.concatenate(xr[1:], axis=0)
    aim_ref[...] = jnp.concatenate(xi[1:], axis=0)
    bpre_ref[...] = jnp.concatenate([pr[L - 1 - j] * bbr - pi[L - 1 - j] * bbi for j in range(L)], axis=0)
    bpim_ref[...] = jnp.concatenate([pr[L - 1 - j] * bbi + pi[L - 1 - j] * bbr for j in range(L)], axis=0)
    lnre_ref[...] = pr[L]
    lnim_ref[...] = pi[L]


def _ssm_prep(lam_re, lam_im, log_dt, b_re, b_im, c_re, c_im):
    G, P, C, L = N_SSM_GROUPS, SSM_STATE, SSM_GROUP, SSM_CHUNK
    row = lambda a: a.reshape(G, 1, -1)
    bt_re = jnp.swapaxes(b_re, 1, 2)
    bt_im = jnp.swapaxes(b_im, 1, 2)
    vec = pl.BlockSpec((None, 1, P), lambda g: (g, 0, 0))
    mat = pl.BlockSpec((None, C, P), lambda g: (g, 0, 0))
    big = pl.BlockSpec((None, L * C, P), lambda g: (g, 0, 0))
    kp, bpre, bpim, are, aim, lnre, lnim = pl.pallas_call(
        _ssm_prep_kernel,
        grid=(G,),
        in_specs=[vec, vec, pl.BlockSpec((None, 1, 1), lambda g: (g, 0, 0)), mat, mat, mat, mat],
        out_specs=[pl.BlockSpec((None, L * C, L * C), lambda g: (g, 0, 0)), big, big, big, big, vec, vec],
        out_shape=[jax.ShapeDtypeStruct((G, L * C, L * C), F32)] + [jax.ShapeDtypeStruct((G, L * C, P), F32)] * 4
                  + [jax.ShapeDtypeStruct((G, 1, P), F32)] * 2,
        compiler_params=_cparams(("arbitrary",)),
        name="ssm_prep",
    )(row(lam_re), row(lam_im), log_dt.reshape(G, 1, 1), bt_re, bt_im, c_re, c_im)

    R, GL = N_SSM_RANGES, SSM_RANGE_GROUPS

    def by_range(a, n_outer):
        n_inner = a.shape[1] // n_outer
        a = a.reshape(R, GL, n_outer, n_inner, LANES)
        return jnp.transpose(a, (0, 2, 1, 3, 4)).reshape(R, n_outer * GL * n_inner, LANES).astype(BF16)

    ek = by_range(kp, L)
    eb = by_range(jnp.concatenate([bpre, bpim], axis=-1), L)
    a_t = jnp.swapaxes(jnp.concatenate([are, -aim], axis=-1), 1, 2)
    ea = by_range(a_t, 2)

    half = RANGE_STATE // LANES
    lnr_t = lnre.reshape(R, half, LANES)
    lni_t = lnim.reshape(R, half, LANES)
    lnr = jnp.concatenate([lnr_t, lnr_t], axis=1)
    lni = jnp.concatenate([-lni_t, lni_t], axis=1)
    return ek, eb, ea, lnr, lni


def _ssm_select_matrices():
    q = np.arange(SSM_CHUNK * LANES)
    r = np.arange(LANES)
    sel_k = (r[:, None] // SSM_GROUP == q[None, :] // LANES) & (r[:, None] % SSM_GROUP == q[None, :] % SSM_GROUP)
    sel_b = (r[:, None] // SSM_STATE == q[None, :] // RANGE_STATE) & (r[:, None] % SSM_STATE == q[None, :] % SSM_STATE)
    return jnp.asarray(sel_k, BF16), jnp.asarray(sel_b, BF16)


def _head_norm(res, gain):
    outs = []
    for h in range(HEADS_PER_GROUP):
        t = res[:, h * HEAD_DIM:(h + 1) * HEAD_DIM]
        ms = jnp.mean(t * t, axis=-1, keepdims=True)
        outs.append(t * lax.rsqrt(ms + EPS) * gain)
    return outs


def _in_proj_kernel(x_ref, g_ref, w_ref, qn_ref, kn_ref, u_ref, qkv_ref, gate_ref, h_scr):
    j = pl.program_id(1)
    tm = x_ref.shape[0]
    sub = min(tm, ROW_SUB)

    def row_blocks():
        for r in range(tm // sub):
            rows = pl.ds(r * sub, sub)
            yield rows, _dot(h_scr[rows, :], w_ref[...])

    @pl.when(j == 0)
    def _():
        x = x_ref[...]
        ms = jnp.mean(x * x, axis=-1, keepdims=True)
        h_scr[...] = (x * lax.rsqrt(ms + EPS) * g_ref[...]).astype(BF16)
        for rows, res in row_blocks():
            u_ref[rows, :] = res

    def store_heads(gain_ref):
        for rows, res in row_blocks():
            heads = (_head_norm(res, gain_ref[...]) if gain_ref is not None
                     else [res[:, h * HEAD_DIM:(h + 1) * HEAD_DIM] for h in range(HEADS_PER_GROUP)])
            for h, t in enumerate(heads):
                qkv_ref[h, rows, :] = t

    pl.when((j >= 1) & (j <= 3))(functools.partial(store_heads, qn_ref))
    pl.when((j >= 4) & (j <= 6))(functools.partial(store_heads, kn_ref))
    pl.when((j >= 7) & (j <= 9))(functools.partial(store_heads, None))

    @pl.when(j >= 10)
    def _():
        for rows, res in row_blocks():
            gate_ref[rows, :] = jax.nn.sigmoid(res).astype(gate_ref.dtype)


def _in_proj(x, gain, w_bf16, layer, q_gain, k_gain, tm):
    M = x.shape[0]
    nj = IN_WIDTH // COL_BLOCK
    n_qkv = 3 * N_ATT_GROUPS
    return pl.pallas_call(
        _in_proj_kernel,
        grid=(M // tm, nj),
        in_specs=[
            pl.BlockSpec((tm, D_MODEL), lambda i, j: (i, 0), pipeline_mode=pl.Buffered(1)),
            pl.BlockSpec((1, D_MODEL), lambda i, j: (0, 0)),
            pl.BlockSpec((None, D_MODEL, COL_BLOCK), lambda i, j: (layer, 0, j)),
            pl.BlockSpec((1, HEAD_DIM), lambda i, j: (0, 0)),
            pl.BlockSpec((1, HEAD_DIM), lambda i, j: (0, 0)),
        ],
        out_specs=[
            pl.BlockSpec((tm, COL_BLOCK), lambda i, j: (i, 0)),
            pl.BlockSpec((HEADS_PER_GROUP, tm, HEAD_DIM), lambda i, j: (jnp.clip(j - 1, 0, n_qkv - 1), i, 0)),
            pl.BlockSpec((tm, COL_BLOCK), lambda i, j: (i, jnp.clip(j - 10, 0, 3))),
        ],
        out_shape=[
            jax.ShapeDtypeStruct((M, SSM_WIDTH), F32),
            jax.ShapeDtypeStruct((n_qkv * HEADS_PER_GROUP, M, HEAD_DIM), F32),
            jax.ShapeDtypeStruct((M, 2 * D_MODEL), BF16),
        ],
        scratch_shapes=[pltpu.VMEM((tm, D_MODEL), BF16)],
        compiler_params=_cparams(("arbitrary", "arbitrary")),
        name="in_proj",
    )(x, gain.reshape(1, D_MODEL), w_bf16, q_gain.reshape(1, HEAD_DIM), k_gain.reshape(1, HEAD_DIM))


def _expand_block_diag(e_ref, sel_ref, row_shift, col_shift, out_scr):
    n = out_scr.shape[0]
    for c in range(n // LANES):
        rows = pl.ds(c * LANES, LANES)
        full = _dot(e_ref[rows, :], sel_ref[...])
        row = lax.broadcasted_iota(jnp.int32, full.shape, 0) + c * LANES
        col = lax.broadcasted_iota(jnp.int32, full.shape, 1)
        keep = ((row >> row_shift) & (SSM_RANGE_GROUPS - 1)) == ((col >> col_shift) & (SSM_RANGE_GROUPS - 1))
        out_scr[rows, :] = jnp.where(keep, full, 0.0).astype(BF16)


def _ssm_chunk_kernel(nseq, nk, u_ref, h0_ref, ek_ref, eb_ref, ea_ref, selk_ref, selb_ref, lnr_ref, lni_ref,
                      y_ref, hfin_ref, s_scr, ktoe_ref, bcat_ref, acat_ref):
    L = SSM_CHUNK
    rows = nseq * nk
    nsub = 2 * RANGE_STATE // LANES
    lg_c = SSM_GROUP.bit_length() - 1
    lg_p = SSM_STATE.bit_length() - 1

    @pl.when(pl.program_id(1) == 0)
    def _():
        _expand_block_diag(ek_ref, selk_ref, lg_c, lg_c, ktoe_ref)
        _expand_block_diag(eb_ref, selb_ref, lg_c, lg_p, bcat_ref)
        _expand_block_diag(ea_ref, selk_ref, lg_p, lg_c, acat_ref)

    ucat = jnp.concatenate([u_ref[pl.ds(j, rows, stride=L), :] for j in range(L)], axis=-1).astype(BF16)
    s = _dot(ucat, bcat_ref[...])
    for n in range(nsub):
        s_scr[pl.ds(n, rows, stride=nsub), :] = s[:, n * LANES:(n + 1) * LANES]
    y_ref_intra = _dot(ucat, ktoe_ref[...])

    lnr = lnr_ref[...]
    lni = lni_ref[...]

    def step(k, hs):
        new = []
        for q in range(nseq):
            off = pl.multiple_of((q * nk + k) * nsub, nsub)
            h = hs[q]
            sk = s_scr[pl.ds(off, nsub), :]
            s_scr[pl.ds(off, nsub), :] = h
            new.append(h * lnr + pltpu.roll(h, nsub // 2, axis=0) * lni + sk)
        return tuple(new)

    hs = lax.fori_loop(0, nk, step, tuple(h0_ref[q] for q in range(nseq)))
    for q in range(nseq):
        hfin_ref[q] = hs[q]

    hprev = jnp.concatenate([s_scr[pl.ds(n, rows, stride=nsub), :] for n in range(nsub)], axis=-1).astype(BF16)
    y = y_ref_intra + _dot(hprev, acat_ref[...])
    for j in range(L):
        y_ref[pl.ds(j, rows, stride=L), :] = y[:, j * LANES:(j + 1) * LANES]


def _ssm_chunk(u, h0, ek, eb, ea, sel_k, sel_b, lnr, lni, nseq_total, seq_per_step):
    M = u.shape[0]
    T = M // nseq_total
    nk = T // SSM_CHUNK
    nsteps = nseq_total // seq_per_step
    R = N_SSM_RANGES
    nsub = 2 * RANGE_STATE // LANES
    tm = seq_per_step * T
    wide = SSM_CHUNK * LANES
    assert wide == 2 * RANGE_STATE
    wspec = lambda shape: pl.BlockSpec((None,) + shape, lambda r, b: (r, 0, 0))
    sel_spec = pl.BlockSpec((LANES, wide), lambda r, b: (0, 0))
    return pl.pallas_call(
        functools.partial(_ssm_chunk_kernel, seq_per_step, nk),
        grid=(R, nsteps),
        in_specs=[
            pl.BlockSpec((tm, LANES), lambda r, b: (b, r)),
            pl.BlockSpec((seq_per_step, None, nsub, LANES), lambda r, b: (b, r, 0, 0)),
            wspec((wide, LANES)),
            wspec((wide, LANES)),
            wspec((wide, LANES)),
            sel_spec,
            sel_spec,
            wspec((nsub, LANES)),
            wspec((nsub, LANES)),
        ],
        out_specs=[
            pl.BlockSpec((tm, LANES), lambda r, b: (b, r)),
            pl.BlockSpec((seq_per_step, None, nsub, LANES), lambda r, b: (b, r, 0, 0)),
        ],
        out_shape=[
            jax.ShapeDtypeStruct((M, SSM_WIDTH), F32),
            jax.ShapeDtypeStruct((nseq_total, R, nsub, LANES), F32),
        ],
        scratch_shapes=[pltpu.VMEM((seq_per_step * nk * nsub, LANES), F32)] + [pltpu.VMEM((wide, wide), BF16)] * 3,
        compiler_params=_cparams(("arbitrary", "arbitrary")),
        name="ssm_chunk",
    )(u, h0, ek, eb, ea, sel_k, sel_b, lnr, lni)


def _state_to_tiles(re, im):
    N = re.shape[0]
    half = RANGE_STATE // LANES
    return jnp.concatenate([re.reshape(N, N_SSM_RANGES, half, LANES),
                            im.reshape(N, N_SSM_RANGES, half, LANES)], axis=2)


def _tiles_to_state(t):
    N = t.shape[0]
    half = RANGE_STATE // LANES
    return (t[:, :, :half].reshape(N, N_SSM_GROUPS, SSM_STATE),
            t[:, :, half:].reshape(N, N_SSM_GROUPS, SSM_STATE))


def _bucket_np(dist):
    max_exact = N_BUCKETS // 2
    n = np.maximum(dist, 0)
    nf = np.maximum(n, 1).astype(np.float64)
    large = max_exact + (np.log(nf / max_exact) / math.log(REL_MAX_DISTANCE / max_exact)
                         * (N_BUCKETS - max_exact)).astype(np.int32)
    large = np.minimum(large, N_BUCKETS - 1)
    return np.where(n < max_exact, n, large)


def _prompt_bucket_tiles():
    a = np.arange(Q_BLOCK)[:, None]
    c = np.arange(2 * Q_BLOCK)[None, :]
    rel = a - c + Q_BLOCK
    tiles = []
    for window, dil in WINDOWS:
        K = window // dil + 1
        valid = (rel >= 0) & (rel < K)
        tiles.append(np.where(valid, _bucket_np(np.clip(rel, 0, K - 1) * dil), -1))
    return jnp.asarray(np.stack(tiles), jnp.int32)


def _attn_group_of_step(step):
    return (step + ATTN_FIRST_GROUP) % N_ATT_GROUPS


def _attn_prompt_kernel(T, tab_ref, q_ref, k_ref, v_ref, bkt_ref, o_ref,
                        m_scr, l_scr, acc_scr, s_scr, p_scr, mb_scr, bias_scr):
    h = pl.program_id(1)
    step = pl.program_id(2)
    g = _attn_group_of_step(step)
    scale = HEAD_DIM ** -0.5
    nblk = T // Q_BLOCK

    bkt = bkt_ref[...]
    col = g * HEADS_PER_GROUP + h
    bias = jnp.full(bkt.shape, NEG_INF, F32)
    for t in range(N_BUCKETS):
        bias = jnp.where(bkt == t, tab_ref[t, col], bias)
    bias_scr[...] = bias

    def run_group(first, dil):
        per_class = nblk // dil

        def rows(i):
            r = i // per_class
            n = i % per_class
            row_q = r + n * (Q_BLOCK * dil)
            row_p = r + jnp.maximum(n - 1, 0) * (Q_BLOCK * dil)
            return n, pl.ds(row_q, Q_BLOCK, stride=dil), pl.ds(row_p, Q_BLOCK, stride=dil)

        def scores(i, carry):
            n, sl_q, sl_p = rows(i)
            q = q_ref[sl_q, :].astype(BF16)
            s_r = _dot_nt(q, k_ref[sl_q, :].astype(BF16)) * scale + bias_scr[:, Q_BLOCK:]
            s_l = (_dot_nt(q, k_ref[sl_p, :].astype(BF16)) * scale
                   + jnp.where(n == 0, NEG_INF, bias_scr[:, :Q_BLOCK]))
            s_scr[i, :, :Q_BLOCK] = s_l
            s_scr[i, :, Q_BLOCK:] = s_r
            m = jnp.maximum(jnp.max(s_l, axis=-1, keepdims=True), jnp.max(s_r, axis=-1, keepdims=True))
            mb_scr[i] = jnp.broadcast_to(m, (Q_BLOCK, HEAD_DIM))
            return carry

        def probs(i, carry):
            mb = mb_scr[i]
            p_scr[i, :, :Q_BLOCK] = jnp.exp(s_scr[i, :, :Q_BLOCK] - mb).astype(BF16)
            p_scr[i, :, Q_BLOCK:] = jnp.exp(s_scr[i, :, Q_BLOCK:] - mb).astype(BF16)
            return carry

        def values(i, carry):
            _, sl_q, sl_p = rows(i)
            ones = jnp.ones((Q_BLOCK, HEAD_DIM), BF16)
            v_p = jnp.concatenate([v_ref[sl_p, :].astype(BF16), ones], axis=1)
            v_c = jnp.concatenate([v_ref[sl_q, :].astype(BF16), ones], axis=1)
            ol = _dot(p_scr[i, :, :Q_BLOCK], v_p) + _dot(p_scr[i, :, Q_BLOCK:], v_c)
            o = ol[:, :HEAD_DIM]
            lb = ol[:, HEAD_DIM:]
            mb = mb_scr[i]
            if first:
                m_scr[sl_q, :] = mb
                l_scr[sl_q, :] = lb
                acc_scr[sl_q, :] = o
            else:
                m0 = m_scr[sl_q, :]
                mn = jnp.maximum(m0, mb)
                a0 = jnp.exp(m0 - mn)
                a1 = jnp.exp(mb - mn)
                m_scr[sl_q, :] = mn
                l_scr[sl_q, :] = a0 * l_scr[sl_q, :] + a1 * lb
                acc_scr[sl_q, :] = a0 * acc_scr[sl_q, :] + a1 * o
            return carry

        lax.fori_loop(0, nblk, scores, 0, unroll=ATTN_UNROLL)
        lax.fori_loop(0, nblk, probs, 0, unroll=ATTN_UNROLL)
        lax.fori_loop(0, nblk, values, 0, unroll=ATTN_UNROLL)

    for s in range(N_ATT_GROUPS):
        dil = WINDOWS[(s + ATTN_FIRST_GROUP) % N_ATT_GROUPS][1]
        pl.when(step == s)(functools.partial(run_group, s == 0, dil))

    @pl.when(step == N_ATT_GROUPS - 1)
    def _():
        o_ref[...] = (acc_scr[...] / l_scr[...]).astype(o_ref.dtype)


def _attn_prompt(qkv, rel_bias, bucket_tiles, nbatch):
    M = qkv.shape[1]
    T = M // nbatch
    H, G = HEADS_PER_GROUP, N_ATT_GROUPS
    nblk = T // Q_BLOCK

    def qkv_spec(which):
        return pl.BlockSpec((None, T, HEAD_DIM),
                            lambda b, h, s: (which * N_ATT_HEADS + _attn_group_of_step(s) * H + h, b, 0))

    return pl.pallas_call(
        functools.partial(_attn_prompt_kernel, T),
        grid=(nbatch, H, G),
        in_specs=[pl.BlockSpec(memory_space=pltpu.SMEM),
                  qkv_spec(0), qkv_spec(1), qkv_spec(2),
                  pl.BlockSpec((None, Q_BLOCK, 2 * Q_BLOCK), lambda b, h, s: (_attn_group_of_step(s), 0, 0))],
        out_specs=pl.BlockSpec((T, HEAD_DIM), lambda b, h, s: (b, h)),
        out_shape=jax.ShapeDtypeStruct((M, ATT_WIDTH), BF16),
        scratch_shapes=[pltpu.VMEM((T, HEAD_DIM), F32)] * 3
                       + [pltpu.VMEM((nblk, Q_BLOCK, 2 * Q_BLOCK), F32),
                          pltpu.VMEM((nblk, Q_BLOCK, 2 * Q_BLOCK), BF16),
                          pltpu.VMEM((nblk, Q_BLOCK, HEAD_DIM), F32),
                          pltpu.VMEM((Q_BLOCK, 2 * Q_BLOCK), F32)],
        compiler_params=_cparams(("arbitrary", "arbitrary", "arbitrary")),
        name="attn_prompt",
    )(rel_bias.astype(F32), qkv, qkv, qkv, bucket_tiles)


def _attn_sample_kernel(window, dil, q_ref, ck_ref, cv_ref, nk_ref, nv_ref, bias_ref, o_ref, lse_ref):
    S = q_ref.shape[0]
    K = window // dil + 1
    scale = HEAD_DIM ** -0.5
    bias = bias_ref[...]
    for s in range(S):
        n_c = (window - 1 - s) // dil + 1
        qs = q_ref[s]
        new_rows = [s + j * dil - window for j in range(n_c, K)]
        kk = jnp.concatenate([ck_ref[pl.ds(s, n_c, stride=dil)]] + [nk_ref[pl.ds(i, 1)] for i in new_rows], axis=0)
        vv = jnp.concatenate([cv_ref[pl.ds(s, n_c, stride=dil)]] + [nv_ref[pl.ds(i, 1)] for i in new_rows], axis=0)
        lg = jnp.sum(kk * qs[None], axis=-1, keepdims=True) * scale + bias
        m = jnp.max(lg, axis=0)
        p = jnp.exp(lg - m[None])
        l = jnp.sum(p, axis=0)
        o_ref[s] = jnp.sum(p * vv, axis=0) / l
        lse_ref[s] = m + jnp.log(l)


def _attn_sample(q, cache_k, cache_v, new_k, new_v, bias, layer, window, dil):
    B, S = q.shape[0], q.shape[1]
    H = HEADS_PER_GROUP
    K = window // dil + 1
    small = pl.BlockSpec((None, S, H, HEAD_DIM), lambda b: (b, 0, 0, 0))
    cache = pl.BlockSpec((None, None, window, H, HEAD_DIM), lambda b: (layer, b, 0, 0, 0))
    return pl.pallas_call(
        functools.partial(_attn_sample_kernel, window, dil),
        grid=(B,),
        in_specs=[small, cache, cache, small, small, pl.BlockSpec((K, H, HEAD_DIM), lambda b: (0, 0, 0))],
        out_specs=[small, small],
        out_shape=[jax.ShapeDtypeStruct((B, S, H, HEAD_DIM), F32)] * 2,
        compiler_params=_cparams(("arbitrary",)),
        name="attn_sample_w%d" % window,
    )(q, cache_k, cache_v, new_k, new_v, bias)


def _merge_groups_kernel(o0, o1, o2, l0, l1, l2, y_ref):
    a, b, c = l0[...], l1[...], l2[...]
    m = jnp.maximum(jnp.maximum(a, b), c)
    ea, eb, ec = jnp.exp(a - m), jnp.exp(b - m), jnp.exp(c - m)
    y_ref[...] = ((ea * o0[...] + eb * o1[...] + ec * o2[...]) / (ea + eb + ec)).astype(y_ref.dtype)


def _merge_groups(outs, lses):
    shape = outs[0].shape
    return pl.pallas_call(
        _merge_groups_kernel,
        out_shape=jax.ShapeDtypeStruct(shape, BF16),
        name="merge_groups",
    )(*outs, *lses)


def _sample_bias(rel_bias):
    out = []
    for g, (window, dil) in enumerate(WINDOWS):
        K = window // dil + 1
        steps = (K - 1) - np.arange(K)
        tab = rel_bias[:, g * HEADS_PER_GROUP:(g + 1) * HEADS_PER_GROUP].astype(F32)
        b = tab[_bucket_np(steps * dil)]
        out.append(jnp.broadcast_to(b[:, :, None], (K, HEADS_PER_GROUP, HEAD_DIM)))
    return out


def _gelu_tanh(x):
    return 0.5 * x * (1.0 + jnp.tanh(math.sqrt(2.0 / math.pi) * (x + 0.044715 * (x * x * x))))


def _mix_out_kernel(y_ref, u_ref, yb_ref, ga_ref, gb_ref, x_ref, d_ref, wglu_ref, bglu_ref, wa_ref, wb_ref,
                    wout_ref, gn_ref, x1_ref, h2_ref):
    y = y_ref[...] + d_ref[...] * u_ref[...]
    z = _gelu_tanh(y)
    ya = z * jax.nn.sigmoid(_dot(z.astype(BF16), wglu_ref[...]) + bglu_ref[...])
    mix = (ga_ref[...] * _dot(ya.astype(BF16), wa_ref[...])
           + gb_ref[...] * _dot(yb_ref[...], wb_ref[...]))
    x1 = x_ref[...] + _dot(mix.astype(BF16), wout_ref[...])
    x1_ref[...] = x1
    ms = jnp.mean(x1 * x1, axis=-1, keepdims=True)
    h2_ref[...] = (x1 * lax.rsqrt(ms + EPS) * gn_ref[...]).astype(BF16)


def _mix_out(y_ssm, u, yb, gates, x, ssm_d, wglu, bglu, wa, wb, wout, layer, gain, tm):
    M = x.shape[0]
    row = lambda w: pl.BlockSpec((tm, w), lambda i: (i, 0))
    vec = lambda b: pl.BlockSpec((1, b), lambda i: (0, 0), pipeline_mode=pl.Buffered(1))
    full = lambda a, b: pl.BlockSpec((None, a, b), lambda i: (layer, 0, 0), pipeline_mode=pl.Buffered(1))
    return pl.pallas_call(
        _mix_out_kernel,
        grid=(M // tm,),
        in_specs=[row(SSM_WIDTH), row(SSM_WIDTH), row(ATT_WIDTH),
                  pl.BlockSpec((tm, D_MODEL), lambda i: (i, 0)),
                  pl.BlockSpec((tm, D_MODEL), lambda i: (i, 1)),
                  row(D_MODEL),
                  vec(SSM_WIDTH), full(SSM_WIDTH, SSM_WIDTH), vec(SSM_WIDTH),
                  full(SSM_WIDTH, D_MODEL), full(ATT_WIDTH, D_MODEL), full(D_MODEL, D_MODEL), vec(D_MODEL)],
        out_specs=[row(D_MODEL), row(D_MODEL)],
        out_shape=[jax.ShapeDtypeStruct((M, D_MODEL), F32), jax.ShapeDtypeStruct((M, D_MODEL), BF16)],
        compiler_params=_cparams(("arbitrary",)),
        name="mix_out",
    )(y_ssm, u, yb, gates, gates, x, ssm_d.reshape(1, -1), wglu, bglu.reshape(1, -1), wa, wb, wout,
      gain.reshape(1, -1))


def _ffn_kernel(h_ref, x_ref, wg_ref, wu_ref, wd_ref, o_ref):
    @pl.when(pl.program_id(1) == 0)
    def _():
        o_ref[...] = x_ref[...]

    tm = h_ref.shape[0]
    sub = min(tm, FFN_ROW_SUB)
    for r in range(tm // sub):
        rows = pl.ds(r * sub, sub)
        h = h_ref[rows, :]
        a = jax.nn.silu(_dot(h, wg_ref[...])) * _dot(h, wu_ref[...])
        o_ref[rows, :] += _dot(a.astype(BF16), wd_ref[...])


def _ffn(h2, x1, wg, wu, wd, layer, tm):
    M = x1.shape[0]
    nf = FFN_HIDDEN // FFN_BLOCK
    return pl.pallas_call(
        _ffn_kernel,
        grid=(M // tm, nf),
        in_specs=[pl.BlockSpec((tm, D_MODEL), lambda i, f: (i, 0)),
                  pl.BlockSpec((tm, D_MODEL), lambda i, f: (i, 0), pipeline_mode=pl.Buffered(1)),
                  pl.BlockSpec((None, D_MODEL, FFN_BLOCK), lambda i, f: (layer, 0, f)),
                  pl.BlockSpec((None, D_MODEL, FFN_BLOCK), lambda i, f: (layer, 0, f)),
                  pl.BlockSpec((None, FFN_BLOCK, D_MODEL), lambda i, f: (layer, f, 0))],
        out_specs=pl.BlockSpec((tm, D_MODEL), lambda i, f: (i, 0)),
        out_shape=jax.ShapeDtypeStruct((M, D_MODEL), F32),
        compiler_params=_cparams(("arbitrary", "arbitrary")),
        name="ffn",
    )(h2, x1, wg, wu, wd)


def _cache_shift_kernel(ck_ref, cv_ref, hk_ref, hv_ref, nk_ref, nv_ref, ok_ref, ov_ref):
    c = pl.program_id(2)
    last = pl.num_programs(2) - 1
    R = ck_ref.shape[0]
    S = nk_ref.shape[0]
    for cache, halo, new, out in ((ck_ref, hk_ref, nk_ref, ok_ref), (cv_ref, hv_ref, nv_ref, ov_ref)):
        out[pl.ds(0, R - S)] = cache[pl.ds(S, R - S)]

        @pl.when(c == last)
        def _():
            out[pl.ds(R - S, S)] = new[...]

        @pl.when(c < last)
        def _():
            out[pl.ds(R - S, S)] = halo[...]


def _cache_shift(cache_k, cache_v, new_k, new_v):
    depth, B, W, H, E = cache_k.shape
    S = new_k.shape[2]
    R = min(W, CACHE_ROWS)
    nchunks = W // R
    blk = pl.BlockSpec((None, None, R, H, E), lambda l, b, c: (l, b, c, 0, 0))
    halo = pl.BlockSpec((None, None, None, S, H, E),
                        lambda l, b, c: (l, b, jnp.minimum((c + 1) * (R // S), W // S - 1), 0, 0, 0))
    new = pl.BlockSpec((None, None, S, H, E), lambda l, b, c: (l, b, 0, 0, 0))
    as_rows = lambda a: a.reshape(depth, B, W // S, S, H, E)
    return pl.pallas_call(
        _cache_shift_kernel,
        grid=(depth, B, nchunks),
        in_specs=[blk, blk, halo, halo, new, new],
        out_specs=[blk, blk],
        out_shape=[jax.ShapeDtypeStruct(cache_k.shape, cache_k.dtype)] * 2,
        compiler_params=_cparams(("arbitrary", "arbitrary", "arbitrary")),
        name="cache_shift_w%d" % W,
    )(cache_k, cache_v, as_rows(cache_k), as_rows(cache_v), new_k, new_v)


def _kv_tails_kernel(*refs):
    depth = (len(refs) - 2) // 2
    ok_ref, ov_ref = refs[-2:]
    H, R = refs[0].shape[0], refs[0].shape[1]
    for l in range(depth):
        @pl.when(pl.program_id(0) == l)
        def _():
            for src, out in ((refs[2 * l], ok_ref), (refs[2 * l + 1], ov_ref)):
                for h in range(H):
                    out[pl.ds(h, R, stride=H), :] = src[h]


def _kv_tails(qkv_layers, g, window, nbatch):
    depth = len(qkv_layers)
    M = qkv_layers[0].shape[1]
    T = M // nbatch
    H, E = HEADS_PER_GROUP, HEAD_DIM
    R = min(window, CACHE_ROWS)
    nchunks = window // R
    first = (T - window) // R

    def src(layer, which):
        def index(l, b, c):
            bb = jnp.where(l < layer, 0, jnp.where(l > layer, nbatch - 1, b))
            cc = jnp.where(l < layer, 0, jnp.where(l > layer, nchunks - 1, c))
            return (which * N_ATT_GROUPS + g, bb * (T // R) + first + cc, 0)
        return pl.BlockSpec((H, R, E), index)

    args, in_specs = [], []
    for layer, qkv in enumerate(qkv_layers):
        args += [qkv, qkv]
        in_specs += [src(layer, 1), src(layer, 2)]
    out_spec = pl.BlockSpec((None, R * H, E), lambda l, b, c: (l, b * nchunks + c, 0))
    return pl.pallas_call(
        _kv_tails_kernel,
        grid=(depth, nbatch, nchunks),
        in_specs=in_specs,
        out_specs=[out_spec, out_spec],
        out_shape=[jax.ShapeDtypeStruct((depth, nbatch * window * H, E), qkv_layers[0].dtype)] * 2,
        compiler_params=_cparams(("arbitrary", "arbitrary", "arbitrary")),
        name="kv_tails_w%d" % window,
    )(*args)


def _layer_common(x2d, h0_tiles, nseq, seq_per_step, prm, tm):
    u, qkv, gates = _in_proj(x2d, prm["norm_mix"], prm["w_in"], prm["layer"], prm["q_norm"], prm["k_norm"], tm)
    y_ssm, hfin = _ssm_chunk(u, h0_tiles, prm["ek"], prm["eb"], prm["ea"], prm["sel_k"], prm["sel_b"],
                             prm["lnr"], prm["lni"], nseq, seq_per_step)
    return u, qkv, gates, y_ssm, hfin


def _layer_tail(x2d, u, y_ssm, yb, gates, prm, tm_mix, tm_ffn):
    x1, h2 = _mix_out(y_ssm, u, yb, gates, x2d, prm["ssm_d"], prm["w_glu"], prm["b_glu"], prm["w_branch_a"],
                      prm["w_branch_b"], prm["w_out"], prm["layer"], prm["norm_ffn"], tm_mix)
    return _ffn(h2, x1, prm["w_ffn_gate"], prm["w_ffn_up"], prm["w_ffn_down"], prm["layer"], tm_ffn)


def kernel(x_prompt, x_sample, state_ssm_re, state_ssm_im, cache_k_w128, cache_v_w128, cache_k_w512, cache_v_w512,
           cache_k_w2048, cache_v_w2048, rel_bias, norm_mix, norm_ffn, q_norm, k_norm, w_in, ssm_lambda_re,
           ssm_lambda_im, ssm_log_dt, ssm_b_re, ssm_b_im, ssm_c_re, ssm_c_im, ssm_d, w_glu, b_glu, w_branch_a,
           w_branch_b, w_out, w_ffn_gate, w_ffn_up, w_ffn_down):
    depth = w_in.shape[0]
    B, T, _ = x_prompt.shape
    SB, S, _ = x_sample.shape
    H = HEADS_PER_GROUP
    caches_k = (cache_k_w128, cache_k_w512, cache_k_w2048)
    caches_v = (cache_v_w128, cache_v_w512, cache_v_w2048)

    bucket_tiles = _prompt_bucket_tiles()
    bias_sample = _sample_bias(rel_bias)
    sel_k, sel_b = _ssm_select_matrices()

    xp = x_prompt.reshape(B * T, D_MODEL)
    xs = x_sample.reshape(SB * S, D_MODEL)
    p_re, p_im, s_re, s_im = [], [], [], []
    qkv_prompt = []
    new_k = [[] for _ in WINDOWS]
    new_v = [[] for _ in WINDOWS]

    weights = {
        "w_in": w_in.astype(BF16), "w_glu": w_glu.astype(BF16), "w_branch_a": w_branch_a.astype(BF16),
        "w_branch_b": w_branch_b.astype(BF16), "w_out": w_out.astype(BF16), "w_ffn_gate": w_ffn_gate.astype(BF16),
        "w_ffn_up": w_ffn_up.astype(BF16), "w_ffn_down": w_ffn_down.astype(BF16),
    }

    for l in range(depth):
        ek, eb, ea, lnr, lni = _ssm_prep(ssm_lambda_re[l], ssm_lambda_im[l], ssm_log_dt[l], ssm_b_re[l],
                                         ssm_b_im[l], ssm_c_re[l], ssm_c_im[l])
        prm = dict(weights)
        prm.update({
            "layer": l, "norm_mix": norm_mix[l], "norm_ffn": norm_ffn[l], "q_norm": q_norm[l], "k_norm": k_norm[l],
            "ssm_d": ssm_d[l], "b_glu": b_glu[l],
            "ek": ek, "eb": eb, "ea": ea, "sel_k": sel_k, "sel_b": sel_b, "lnr": lnr, "lni": lni,
        })

        zeros = jnp.zeros((B, N_SSM_RANGES, 2 * RANGE_STATE // LANES, LANES), F32)
        u, qkv, gates, y_ssm, hfin = _layer_common(xp, zeros, B, B, prm, PROMPT_ROWS)
        yb = _attn_prompt(qkv, rel_bias, bucket_tiles, B)
        xp = _layer_tail(xp, u, y_ssm, yb, gates, prm, MIX_ROWS, PROMPT_ROWS)
        hr, hi = _tiles_to_state(hfin)
        p_re.append(hr)
        p_im.append(hi)
        qkv_prompt.append(qkv)

        h0 = _state_to_tiles(state_ssm_re[l], state_ssm_im[l])
        u, qkv, gates, y_ssm, hfin = _layer_common(xs, h0, SB, SB, prm, SB * S)
        tok = jnp.transpose(qkv.reshape(3, N_ATT_GROUPS, H, SB, S, HEAD_DIM), (0, 1, 3, 4, 2, 5))
        outs, lses = [], []
        for g, (window, dil) in enumerate(WINDOWS):
            nk_g, nv_g = tok[1, g], tok[2, g]
            new_k[g].append(nk_g)
            new_v[g].append(nv_g)
            o_g, lse_g = _attn_sample(tok[0, g], caches_k[g], caches_v[g], nk_g, nv_g, bias_sample[g], l,
                                      window, dil)
            outs.append(o_g)
            lses.append(lse_g)
        yb = _merge_groups(outs, lses).reshape(SB * S, ATT_WIDTH)
        xs = _layer_tail(xs, u, y_ssm, yb, gates, prm, SB * S, SB * S)
        hr, hi = _tiles_to_state(hfin)
        s_re.append(hr)
        s_im.append(hi)

    shifted = []
    for g in range(N_ATT_GROUPS):
        shifted += _cache_shift(caches_k[g], caches_v[g], jnp.stack(new_k[g]), jnp.stack(new_v[g]))
    tails = [t.reshape(depth, B, window, H, HEAD_DIM)
             for g, (window, _) in enumerate(WINDOWS) for t in _kv_tails(qkv_prompt, g, window, B)]

    return (xp.reshape(B, T, D_MODEL), xs.reshape(SB, S, D_MODEL),
            jnp.stack(p_re), jnp.stack(p_im), *tails,
            jnp.stack(s_re), jnp.stack(s_im), *shifted)
```

```python
import functools
import math

import numpy as np
import jax
import jax.numpy as jnp
from jax import lax
from jax.experimental import pallas as pl
from jax.experimental.pallas import tpu as pltpu

F32 = jnp.float32
BF16 = jnp.bfloat16

D_MODEL = 2048
HEAD_DIM = 128
HEADS_PER_GROUP = 8
WINDOWS = ((128, 1), (512, 4), (2048, 16))
N_ATT_GROUPS = len(WINDOWS)
N_ATT_HEADS = N_ATT_GROUPS * HEADS_PER_GROUP
ATT_WIDTH = HEADS_PER_GROUP * HEAD_DIM
QKV_WIDTH = N_ATT_HEADS * HEAD_DIM
Q_BLOCK = 128
SSM_WIDTH = D_MODEL // 2
SSM_GROUP = 16
N_SSM_GROUPS = SSM_WIDTH // SSM_GROUP
SSM_STATE = 64
IN_WIDTH = SSM_WIDTH + 3 * QKV_WIDTH + 2 * D_MODEL
FFN_HIDDEN = 5632
N_BUCKETS = 32
REL_MAX_DISTANCE = 2048
EPS = 1e-6
NEG_INF = -1e30

LANES = 128
SUBLANES = 8
VMEM_LIMIT_BYTES = 60 * 1024 * 1024

SSM_CHUNK = 8
SSM_RANGE_GROUPS = LANES // SSM_GROUP
N_SSM_RANGES = N_SSM_GROUPS // SSM_RANGE_GROUPS
RANGE_STATE = SSM_RANGE_GROUPS * SSM_STATE
COL_BLOCK = 512
FFN_BLOCK = 256
ATTN_UNROLL = 16
ATTN_FIRST_GROUP = 2
CACHE_ROWS = 1024
IN_PROJ_ROWS = 2048
FFN_ROWS = 1024
MIX_ROWS = 256
ROW_SUB = 256
FFN_ROW_SUB = 512


def _cparams(sem):
    return pltpu.CompilerParams(dimension_semantics=sem, vmem_limit_bytes=VMEM_LIMIT_BYTES)


def _dot(a, b):
    return jnp.dot(a, b, preferred_element_type=F32)


def _dot_nt(a, b):
    return lax.dot_general(a, b, (((1,), (1,)), ((), ())), preferred_element_type=F32)


def _ssm_prep_kernel(lre_ref, lim_ref, ldt_ref, btre_ref, btim_ref, cre_ref, cim_ref,
                     kp_ref, bpre_ref, bpim_ref, are_ref, aim_ref, lnre_ref, lnim_ref):
    L = SSM_CHUNK
    lr = lre_ref[...]
    li = lim_ref[...]
    dt = jnp.exp(ldt_ref[...])
    er = jnp.exp(lr * dt)
    lbr = er * jnp.cos(li * dt)
    lbi = er * jnp.sin(li * dt)
    nr = lbr - 1.0
    dd = lr * lr + li * li
    rr = (nr * lr + lbi * li) / dd
    ri = (lbi * lr - nr * li) / dd
    btr = btre_ref[...]
    bti = btim_ref[...]
    bbr = rr * btr - ri * bti
    bbi = rr * bti + ri * btr
    cr = cre_ref[...]
    ci = cim_ref[...]
    pr = [jnp.ones_like(lbr)]
    pi = [jnp.zeros_like(lbr)]
    for _ in range(L):
        pr.append(pr[-1] * lbr - pi[-1] * lbi)
        pi.append(pr[-2] * lbi + pi[-1] * lbr)
    xr = [cr * pr[t] - ci * pi[t] for t in range(L + 1)]
    xi = [cr * pi[t] + ci * pr[t] for t in range(L + 1)]
    xr_k = jnp.concatenate(xr[:L], axis=0)
    xi_k = jnp.concatenate(xi[:L], axis=0)
    hp = lax.Precision.HIGHEST
    kp = (lax.dot_general(bbr, xr_k, (((1,), (1,)), ((), ())), precision=hp, preferred_element_type=F32)
          - lax.dot_general(bbi, xi_k, (((1,), (1,)), ((), ())), precision=hp, preferred_element_type=F32))
    lane = lax.broadcasted_iota(jnp.int32, kp.shape, 1)
    blocks = [kp] + [jnp.where(lane >= j * SSM_GROUP, pltpu.roll(kp, j * SSM_GROUP, axis=1), 0.0)
                     for j in range(1, L)]
    kp_ref[...] = jnp.concatenate(blocks, axis=0)
    are_ref[...] = jnp.concatenate(xr[1:], axis=0)
    aim_ref[...] = jnp.concatenate(xi[1:], axis=0)
    bpre_ref[...] = jnp.concatenate([pr[L - 1 - j] * bbr - pi[L - 1 - j] * bbi for j in range(L)], axis=0)
    bpim_ref[...] = jnp.concatenate([pr[L - 1 - j] * bbi + pi[L - 1 - j] * bbr for j in range(L)], axis=0)
    lnre_ref[...] = pr[L]
    lnim_ref[...] = pi[L]


def _ssm_prep(lam_re, lam_im, log_dt, b_re, b_im, c_re, c_im):
    G, P, C, L = N_SSM_GROUPS, SSM_STATE, SSM_GROUP, SSM_CHUNK
    row = lambda a: a.reshape(G, 1, -1)
    bt_re = jnp.swapaxes(b_re, 1, 2)
    bt_im = jnp.swapaxes(b_im, 1, 2)
    vec = pl.BlockSpec((None, 1, P), lambda g: (g, 0, 0))
    mat = pl.BlockSpec((None, C, P), lambda g: (g, 0, 0))
    big = pl.BlockSpec((None, L * C, P), lambda g: (g, 0, 0))
    kp, bpre, bpim, are, aim, lnre, lnim = pl.pallas_call(
        _ssm_prep_kernel,
        grid=(G,),
        in_specs=[vec, vec, pl.BlockSpec((None, 1, 1), lambda g: (g, 0, 0)), mat, mat, mat, mat],
        out_specs=[pl.BlockSpec((None, L * C, L * C), lambda g: (g, 0, 0)), big, big, big, big, vec, vec],
        out_shape=[jax.ShapeDtypeStruct((G, L * C, L * C), F32)] + [jax.ShapeDtypeStruct((G, L * C, P), F32)] * 4
                  + [jax.ShapeDtypeStruct((G, 1, P), F32)] * 2,
        compiler_params=_cparams(("arbitrary",)),
        name="ssm_prep",
    )(row(lam_re), row(lam_im), log_dt.reshape(G, 1, 1), bt_re, bt_im, c_re, c_im)

    R, GL = N_SSM_RANGES, SSM_RANGE_GROUPS

    def by_range(a, n_outer):
        n_inner = a.shape[1] // n_outer
        a = a.reshape(R, GL, n_outer, n_inner, LANES)
        return jnp.transpose(a, (0, 2, 1, 3, 4)).reshape(R, n_outer * GL * n_inner, LANES).astype(BF16)

    ek = by_range(kp, L)
    eb = by_range(jnp.concatenate([bpre, bpim], axis=-1), L)
    a_t = jnp.swapaxes(jnp.concatenate([are, -aim], axis=-1), 1, 2)
    ea = by_range(a_t, 2)

    half = RANGE_STATE // LANES
    lnr_t = lnre.reshape(R, half, LANES)
    lni_t = lnim.reshape(R, half, LANES)
    lnr = jnp.concatenate([lnr_t, lnr_t], axis=1)
    lni = jnp.concatenate([-lni_t, lni_t], axis=1)
    return ek, eb, ea, lnr, lni


def _ssm_select_matrices():
    q = np.arange(SSM_CHUNK * LANES)
    r = np.arange(LANES)
    sel_k = (r[:, None] // SSM_GROUP == q[None, :] // LANES) & (r[:, None] % SSM_GROUP == q[None, :] % SSM_GROUP)
    sel_b = (r[:, None] // SSM_STATE == q[None, :] // RANGE_STATE) & (r[:, None] % SSM_STATE == q[None, :] % SSM_STATE)
    return jnp.asarray(sel_k, BF16), jnp.asarray(sel_b, BF16)


HEADS_PER_BLOCK = COL_BLOCK // HEAD_DIM
_SEG_U = 0
_SEG_Q = SSM_WIDTH // COL_BLOCK
_SEG_K = _SEG_Q + QKV_WIDTH // COL_BLOCK
_SEG_V = _SEG_K + QKV_WIDTH // COL_BLOCK
_SEG_GATE = _SEG_V + QKV_WIDTH // COL_BLOCK
_SEG_END = IN_WIDTH // COL_BLOCK


def _head_norm(res, gain):
    outs = []
    for h in range(HEADS_PER_BLOCK):
        t = res[:, h * HEAD_DIM:(h + 1) * HEAD_DIM]
        ms = jnp.mean(t * t, axis=-1, keepdims=True)
        outs.append(t * lax.rsqrt(ms + EPS) * gain)
    return outs


def _in_proj_kernel(x_ref, g_ref, w_ref, qn_ref, kn_ref, u_ref, qkv_ref, gate_ref, h_scr, w_scr):
    j = pl.program_id(1)
    tm = x_ref.shape[0]
    sub = min(tm, ROW_SUB)

    def row_blocks():
        w_scr[...] = w_ref[...].astype(BF16)
        for r in range(tm // sub):
            rows = pl.ds(r * sub, sub)
            yield rows, _dot(h_scr[rows, :], w_scr[...])

    @pl.when(j == 0)
    def _():
        for r in range(tm // sub):
            rows = pl.ds(r * sub, sub)
            x = x_ref[rows, :]
            ms = jnp.mean(x * x, axis=-1, keepdims=True)
            h_scr[rows, :] = (x * lax.rsqrt(ms + EPS) * g_ref[...]).astype(BF16)

    @pl.when(j < _SEG_Q)
    def _():
        for rows, res in row_blocks():
            u_ref[rows, :] = res

    def store_heads(gain_ref):
        for rows, res in row_blocks():
            heads = (_head_norm(res, gain_ref[...]) if gain_ref is not None
                     else [res[:, h * HEAD_DIM:(h + 1) * HEAD_DIM] for h in range(HEADS_PER_BLOCK)])
            for h, t in enumerate(heads):
                qkv_ref[h, rows, :] = t

    pl.when((j >= _SEG_Q) & (j < _SEG_K))(functools.partial(store_heads, qn_ref))
    pl.when((j >= _SEG_K) & (j < _SEG_V))(functools.partial(store_heads, kn_ref))
    pl.when((j >= _SEG_V) & (j < _SEG_GATE))(functools.partial(store_heads, None))

    @pl.when(j >= _SEG_GATE)
    def _():
        for rows, res in row_blocks():
            gate_ref[rows, :] = jax.nn.sigmoid(res).astype(gate_ref.dtype)


def _in_proj(x, gain, w, layer, q_gain, k_gain, tm):
    M = x.shape[0]
    n_heads = 3 * N_ATT_HEADS
    return pl.pallas_call(
        _in_proj_kernel,
        grid=(M // tm, _SEG_END),
        in_specs=[
            pl.BlockSpec((tm, D_MODEL), lambda i, j: (i, 0), pipeline_mode=pl.Buffered(1)),
            pl.BlockSpec((1, D_MODEL), lambda i, j: (0, 0)),
            pl.BlockSpec((None, D_MODEL, COL_BLOCK), lambda i, j: (layer, 0, j)),
            pl.BlockSpec((1, HEAD_DIM), lambda i, j: (0, 0)),
            pl.BlockSpec((1, HEAD_DIM), lambda i, j: (0, 0)),
        ],
        out_specs=[
            pl.BlockSpec((tm, COL_BLOCK), lambda i, j: (i, jnp.clip(j, 0, _SEG_Q - 1))),
            pl.BlockSpec((HEADS_PER_BLOCK, tm, HEAD_DIM),
                         lambda i, j: (jnp.clip(j - _SEG_Q, 0, _SEG_GATE - _SEG_Q - 1), i, 0)),
            pl.BlockSpec((tm, COL_BLOCK), lambda i, j: (i, jnp.clip(j - _SEG_GATE, 0, _SEG_END - _SEG_GATE - 1))),
        ],
        out_shape=[
            jax.ShapeDtypeStruct((M, SSM_WIDTH), F32),
            jax.ShapeDtypeStruct((n_heads, M, HEAD_DIM), F32),
            jax.ShapeDtypeStruct((M, 2 * D_MODEL), BF16),
        ],
        scratch_shapes=[pltpu.VMEM((tm, D_MODEL), BF16), pltpu.VMEM((D_MODEL, COL_BLOCK), BF16)],
        compiler_params=_cparams(("arbitrary", "arbitrary")),
        name="in_proj",
    )(x, gain.reshape(1, D_MODEL), w, q_gain.reshape(1, HEAD_DIM), k_gain.reshape(1, HEAD_DIM))


def _expand_block_diag(e_ref, sel_ref, row_shift, col_shift, out_scr):
    n = out_scr.shape[0]
    for c in range(n // LANES):
        rows = pl.ds(c * LANES, LANES)
        full = _dot(e_ref[rows, :], sel_ref[...])
        row = lax.broadcasted_iota(jnp.int32, full.shape, 0) + c * LANES
        col = lax.broadcasted_iota(jnp.int32, full.shape, 1)
        keep = ((row >> row_shift) & (SSM_RANGE_GROUPS - 1)) == ((col >> col_shift) & (SSM_RANGE_GROUPS - 1))
        out_scr[rows, :] = jnp.where(keep, full, 0.0).astype(BF16)


def _ssm_chunk_kernel(nseq, nk, u_ref, h0_ref, ek_ref, eb_ref, ea_ref, selk_ref, selb_ref, lnr_ref, lni_ref,
                      y_ref, hfin_ref, s_scr, ktoe_ref, bcat_ref, acat_ref):
    L = SSM_CHUNK
    rows = nseq * nk
    nsub = 2 * RANGE_STATE // LANES
    lg_c = SSM_GROUP.bit_length() - 1
    lg_p = SSM_STATE.bit_length() - 1

    @pl.when(pl.program_id(1) == 0)
    def _():
        _expand_block_diag(ek_ref, selk_ref, lg_c, lg_c, ktoe_ref)
        _expand_block_diag(eb_ref, selb_ref, lg_c, lg_p, bcat_ref)
        _expand_block_diag(ea_ref, selk_ref, lg_p, lg_c, acat_ref)

    ucat = jnp.concatenate([u_ref[pl.ds(j, rows, stride=L), :] for j in range(L)], axis=-1).astype(BF16)
    s = _dot(ucat, bcat_ref[...])
    for n in range(nsub):
        s_scr[pl.ds(n, rows, stride=nsub), :] = s[:, n * LANES:(n + 1) * LANES]
    y_ref_intra = _dot(ucat, ktoe_ref[...])

    lnr = lnr_ref[...]
    lni = lni_ref[...]

    def step(k, hs):
        new = []
        for q in range(nseq):
            off = pl.multiple_of((q * nk + k) * nsub, nsub)
            h = hs[q]
            sk = s_scr[pl.ds(off, nsub), :]
            s_scr[pl.ds(off, nsub), :] = h
            new.append(h * lnr + pltpu.roll(h, nsub // 2, axis=0) * lni + sk)
        return tuple(new)

    hs = lax.fori_loop(0, nk, step, tuple(h0_ref[q] for q in range(nseq)))
    for q in range(nseq):
        hfin_ref[q] = hs[q]

    hprev = jnp.concatenate([s_scr[pl.ds(n, rows, stride=nsub), :] for n in range(nsub)], axis=-1).astype(BF16)
    y = y_ref_intra + _dot(hprev, acat_ref[...])
    for j in range(L):
        y_ref[pl.ds(j, rows, stride=L), :] = y[:, j * LANES:(j + 1) * LANES]


def _ssm_chunk(u, h0, ek, eb, ea, sel_k, sel_b, lnr, lni, nseq_total, seq_per_step):
    M = u.shape[0]
    T = M // nseq_total
    nk = T // SSM_CHUNK
    nsteps = nseq_total // seq_per_step
    R = N_SSM_RANGES
    nsub = 2 * RANGE_STATE // LANES
    tm = seq_per_step * T
    wide = SSM_CHUNK * LANES
    assert wide == 2 * RANGE_STATE
    wspec = lambda shape: pl.BlockSpec((None,) + shape, lambda r, b: (r, 0, 0))
    sel_spec = pl.BlockSpec((LANES, wide), lambda r, b: (0, 0))
    return pl.pallas_call(
        functools.partial(_ssm_chunk_kernel, seq_per_step, nk),
        grid=(R, nsteps),
        in_specs=[
            pl.BlockSpec((tm, LANES), lambda r, b: (b, r)),
            pl.BlockSpec((seq_per_step, None, nsub, LANES), lambda r, b: (b, r, 0, 0)),
            wspec((wide, LANES)),
            wspec((wide, LANES)),
            wspec((wide, LANES)),
            sel_spec,
            sel_spec,
            wspec((nsub, LANES)),
            wspec((nsub, LANES)),
        ],
        out_specs=[
            pl.BlockSpec((tm, LANES), lambda r, b: (b, r)),
            pl.BlockSpec((seq_per_step, None, nsub, LANES), lambda r, b: (b, r, 0, 0)),
        ],
        out_shape=[
            jax.ShapeDtypeStruct((M, SSM_WIDTH), F32),
            jax.ShapeDtypeStruct((nseq_total, R, nsub, LANES), F32),
        ],
        scratch_shapes=[pltpu.VMEM((seq_per_step * nk * nsub, LANES), F32)] + [pltpu.VMEM((wide, wide), BF16)] * 3,
        compiler_params=_cparams(("arbitrary", "arbitrary")),
        name="ssm_chunk",
    )(u, h0, ek, eb, ea, sel_k, sel_b, lnr, lni)


def _state_to_tiles(re, im):
    N = re.shape[0]
    half = RANGE_STATE // LANES
    return jnp.concatenate([re.reshape(N, N_SSM_RANGES, half, LANES),
                            im.reshape(N, N_SSM_RANGES, half, LANES)], axis=2)


def _tiles_to_state(t):
    N = t.shape[0]
    half = RANGE_STATE // LANES
    return (t[:, :, :half].reshape(N, N_SSM_GROUPS, SSM_STATE),
            t[:, :, half:].reshape(N, N_SSM_GROUPS, SSM_STATE))


def _bucket_np(dist):
    max_exact = N_BUCKETS // 2
    n = np.maximum(dist, 0)
    nf = np.maximum(n, 1).astype(np.float64)
    large = max_exact + (np.log(nf / max_exact) / math.log(REL_MAX_DISTANCE / max_exact)
                         * (N_BUCKETS - max_exact)).astype(np.int32)
    large = np.minimum(large, N_BUCKETS - 1)
    return np.where(n < max_exact, n, large)


def _prompt_bucket_tiles():
    a = np.arange(Q_BLOCK)[:, None]
    c = np.arange(2 * Q_BLOCK)[None, :]
    rel = a - c + Q_BLOCK
    tiles = []
    for window, dil in WINDOWS:
        K = window // dil + 1
        valid = (rel >= 0) & (rel < K)
        tiles.append(np.where(valid, _bucket_np(np.clip(rel, 0, K - 1) * dil), -1))
    return jnp.asarray(np.stack(tiles), jnp.int32)


def _attn_group_of_step(step):
    return (step + ATTN_FIRST_GROUP) % N_ATT_GROUPS


def _attn_prompt_kernel(T, tab_ref, q_ref, k_ref, v_ref, bkt_ref, o_ref,
                        m_scr, l_scr, acc_scr, s_scr, p_scr, mb_scr, bias_scr):
    h = pl.program_id(1)
    step = pl.program_id(2)
    g = _attn_group_of_step(step)
    scale = HEAD_DIM ** -0.5
    nblk = T // Q_BLOCK

    bkt = bkt_ref[...]
    col = g * HEADS_PER_GROUP + h
    bias = jnp.full(bkt.shape, NEG_INF, F32)
    for t in range(N_BUCKETS):
        bias = jnp.where(bkt == t, tab_ref[t, col], bias)
    bias_scr[...] = bias

    def run_group(first, dil):
        per_class = nblk // dil

        def rows(i):
            r = i // per_class
            n = i % per_class
            row_q = r + n * (Q_BLOCK * dil)
            row_p = r + jnp.maximum(n - 1, 0) * (Q_BLOCK * dil)
            return n, pl.ds(row_q, Q_BLOCK, stride=dil), pl.ds(row_p, Q_BLOCK, stride=dil)

        def scores(i, carry):
            n, sl_q, sl_p = rows(i)
            q = q_ref[sl_q, :].astype(BF16)
            s_r = _dot_nt(q, k_ref[sl_q, :].astype(BF16)) * scale + bias_scr[:, Q_BLOCK:]
            s_l = (_dot_nt(q, k_ref[sl_p, :].astype(BF16)) * scale
                   + jnp.where(n == 0, NEG_INF, bias_scr[:, :Q_BLOCK]))
            s_scr[i, :, :Q_BLOCK] = s_l
            s_scr[i, :, Q_BLOCK:] = s_r
            m = jnp.maximum(jnp.max(s_l, axis=-1, keepdims=True), jnp.max(s_r, axis=-1, keepdims=True))
            mb_scr[i] = jnp.broadcast_to(m, (Q_BLOCK, HEAD_DIM))
            return carry

        def probs(i, carry):
            mb = mb_scr[i]
            p_scr[i, :, :Q_BLOCK] = jnp.exp(s_scr[i, :, :Q_BLOCK] - mb).astype(BF16)
            p_scr[i, :, Q_BLOCK:] = jnp.exp(s_scr[i, :, Q_BLOCK:] - mb).astype(BF16)
            return carry

        def values(i, carry):
            _, sl_q, sl_p = rows(i)
            ones = jnp.ones((Q_BLOCK, HEAD_DIM), BF16)
            v_p = jnp.concatenate([v_ref[sl_p, :].astype(BF16), ones], axis=1)
            v_c = jnp.concatenate([v_ref[sl_q, :].astype(BF16), ones], axis=1)
            ol = _dot(p_scr[i, :, :Q_BLOCK], v_p) + _dot(p_scr[i, :, Q_BLOCK:], v_c)
            o = ol[:, :HEAD_DIM]
            lb = ol[:, HEAD_DIM:]
            mb = mb_scr[i]
            if first:
                m_scr[sl_q, :] = mb
                l_scr[sl_q, :] = lb
                acc_scr[sl_q, :] = o
            else:
                m0 = m_scr[sl_q, :]
                mn = jnp.maximum(m0, mb)
                a0 = jnp.exp(m0 - mn)
                a1 = jnp.exp(mb - mn)
                m_scr[sl_q, :] = mn
                l_scr[sl_q, :] = a0 * l_scr[sl_q, :] + a1 * lb
                acc_scr[sl_q, :] = a0 * acc_scr[sl_q, :] + a1 * o
            return carry

        lax.fori_loop(0, nblk, scores, 0, unroll=ATTN_UNROLL)
        lax.fori_loop(0, nblk, probs, 0, unroll=ATTN_UNROLL)
        lax.fori_loop(0, nblk, values, 0, unroll=ATTN_UNROLL)

    for s in range(N_ATT_GROUPS):
        dil = WINDOWS[(s + ATTN_FIRST_GROUP) % N_ATT_GROUPS][1]
        pl.when(step == s)(functools.partial(run_group, s == 0, dil))

    @pl.when(step == N_ATT_GROUPS - 1)
    def _():
        o_ref[...] = (acc_scr[...] / l_scr[...]).astype(o_ref.dtype)


def _attn_prompt(qkv, rel_bias, bucket_tiles, nbatch):
    M = qkv.shape[1]
    T = M // nbatch
    H, G = HEADS_PER_GROUP, N_ATT_GROUPS
    nblk = T // Q_BLOCK

    def qkv_spec(which):
        return pl.BlockSpec((None, T, HEAD_DIM),
                            lambda b, h, s: (which * N_ATT_HEADS + _attn_group_of_step(s) * H + h, b, 0))

    return pl.pallas_call(
        functools.partial(_attn_prompt_kernel, T),
        grid=(nbatch, H, G),
        in_specs=[pl.BlockSpec(memory_space=pltpu.SMEM),
                  qkv_spec(0), qkv_spec(1), qkv_spec(2),
                  pl.BlockSpec((None, Q_BLOCK, 2 * Q_BLOCK), lambda b, h, s: (_attn_group_of_step(s), 0, 0))],
        out_specs=pl.BlockSpec((T, HEAD_DIM), lambda b, h, s: (b, h)),
        out_shape=jax.ShapeDtypeStruct((M, ATT_WIDTH), BF16),
        scratch_shapes=[pltpu.VMEM((T, HEAD_DIM), F32)] * 3
                       + [pltpu.VMEM((nblk, Q_BLOCK, 2 * Q_BLOCK), F32),
                          pltpu.VMEM((nblk, Q_BLOCK, 2 * Q_BLOCK), BF16),
                          pltpu.VMEM((nblk, Q_BLOCK, HEAD_DIM), F32),
                          pltpu.VMEM((Q_BLOCK, 2 * Q_BLOCK), F32)],
        compiler_params=_cparams(("arbitrary", "arbitrary", "arbitrary")),
        name="attn_prompt",
    )(rel_bias.astype(F32), qkv, qkv, qkv, bucket_tiles)


def _attn_sample_kernel(window, dil, q_ref, ck_ref, cv_ref, nk_ref, nv_ref, bias_ref, o_ref, lse_ref):
    S = q_ref.shape[0]
    K = window // dil + 1
    scale = HEAD_DIM ** -0.5
    bias = bias_ref[...]
    for s in range(S):
        n_c = (window - 1 - s) // dil + 1
        qs = q_ref[s]
        new_rows = [s + j * dil - window for j in range(n_c, K)]
        kk = jnp.concatenate([ck_ref[pl.ds(s, n_c, stride=dil)]] + [nk_ref[pl.ds(i, 1)] for i in new_rows], axis=0)
        vv = jnp.concatenate([cv_ref[pl.ds(s, n_c, stride=dil)]] + [nv_ref[pl.ds(i, 1)] for i in new_rows], axis=0)
        lg = jnp.sum(kk * qs[None], axis=-1, keepdims=True) * scale + bias
        m = jnp.max(lg, axis=0)
        p = jnp.exp(lg - m[None])
        l = jnp.sum(p, axis=0)
        o_ref[s] = jnp.sum(p * vv, axis=0) / l
        lse_ref[s] = m + jnp.log(l)


def _attn_sample(q, cache_k, cache_v, new_k, new_v, bias, layer, window, dil):
    B, S = q.shape[0], q.shape[1]
    H = HEADS_PER_GROUP
    K = window // dil + 1
    small = pl.BlockSpec((None, S, H, HEAD_DIM), lambda b: (b, 0, 0, 0))
    cache = pl.BlockSpec((None, None, window, H, HEAD_DIM), lambda b: (layer, b, 0, 0, 0))
    return pl.pallas_call(
        functools.partial(_attn_sample_kernel, window, dil),
        grid=(B,),
        in_specs=[small, cache, cache, small, small, pl.BlockSpec((K, H, HEAD_DIM), lambda b: (0, 0, 0))],
        out_specs=[small, small],
        out_shape=[jax.ShapeDtypeStruct((B, S, H, HEAD_DIM), F32)] * 2,
        compiler_params=_cparams(("arbitrary",)),
        name="attn_sample_w%d" % window,
    )(q, cache_k, cache_v, new_k, new_v, bias)


def _merge_groups_kernel(o0, o1, o2, l0, l1, l2, y_ref):
    a, b, c = l0[...], l1[...], l2[...]
    m = jnp.maximum(jnp.maximum(a, b), c)
    ea, eb, ec = jnp.exp(a - m), jnp.exp(b - m), jnp.exp(c - m)
    y_ref[...] = ((ea * o0[...] + eb * o1[...] + ec * o2[...]) / (ea + eb + ec)).astype(y_ref.dtype)


def _merge_groups(outs, lses):
    shape = outs[0].shape
    return pl.pallas_call(
        _merge_groups_kernel,
        out_shape=jax.ShapeDtypeStruct(shape, BF16),
        name="merge_groups",
    )(*outs, *lses)


def _sample_bias(rel_bias):
    out = []
    for g, (window, dil) in enumerate(WINDOWS):
        K = window // dil + 1
        steps = (K - 1) - np.arange(K)
        tab = rel_bias[:, g * HEADS_PER_GROUP:(g + 1) * HEADS_PER_GROUP].astype(F32)
        b = tab[_bucket_np(steps * dil)]
        out.append(jnp.broadcast_to(b[:, :, None], (K, HEADS_PER_GROUP, HEAD_DIM)))
    return out


def _gelu_tanh(x):
    return 0.5 * x * (1.0 + jnp.tanh(math.sqrt(2.0 / math.pi) * (x + 0.044715 * (x * x * x))))


def _mix_out_kernel(y_ref, u_ref, yb_ref, ga_ref, gb_ref, x_ref, d_ref, wglu_ref, bglu_ref, wa_ref, wb_ref,
                    wout_ref, gn_ref, x1_ref, h2_ref):
    y = y_ref[...] + d_ref[...] * u_ref[...]
    z = _gelu_tanh(y)
    ya = z * jax.nn.sigmoid(_dot(z.astype(BF16), wglu_ref[...]) + bglu_ref[...])
    mix = (ga_ref[...] * _dot(ya.astype(BF16), wa_ref[...])
           + gb_ref[...] * _dot(yb_ref[...], wb_ref[...]))
    x1 = x_ref[...] + _dot(mix.astype(BF16), wout_ref[...])
    x1_ref[...] = x1
    ms = jnp.mean(x1 * x1, axis=-1, keepdims=True)
    h2_ref[...] = (x1 * lax.rsqrt(ms + EPS) * gn_ref[...]).astype(BF16)


def _mix_out(y_ssm, u, yb, gates, x, ssm_d, wglu, bglu, wa, wb, wout, layer, gain, tm):
    M = x.shape[0]
    row = lambda w: pl.BlockSpec((tm, w), lambda i: (i, 0))
    vec = lambda b: pl.BlockSpec((1, b), lambda i: (0, 0), pipeline_mode=pl.Buffered(1))
    full = lambda a, b: pl.BlockSpec((None, a, b), lambda i: (layer, 0, 0), pipeline_mode=pl.Buffered(1))
    return pl.pallas_call(
        _mix_out_kernel,
        grid=(M // tm,),
        in_specs=[row(SSM_WIDTH), row(SSM_WIDTH), row(ATT_WIDTH),
                  pl.BlockSpec((tm, D_MODEL), lambda i: (i, 0)),
                  pl.BlockSpec((tm, D_MODEL), lambda i: (i, 1)),
                  row(D_MODEL),
                  vec(SSM_WIDTH), full(SSM_WIDTH, SSM_WIDTH), vec(SSM_WIDTH),
                  full(SSM_WIDTH, D_MODEL), full(ATT_WIDTH, D_MODEL), full(D_MODEL, D_MODEL), vec(D_MODEL)],
        out_specs=[row(D_MODEL), row(D_MODEL)],
        out_shape=[jax.ShapeDtypeStruct((M, D_MODEL), F32), jax.ShapeDtypeStruct((M, D_MODEL), BF16)],
        compiler_params=_cparams(("arbitrary",)),
        name="mix_out",
    )(y_ssm, u, yb, gates, gates, x, ssm_d.reshape(1, -1), wglu, bglu.reshape(1, -1), wa, wb, wout,
      gain.reshape(1, -1))


def _ffn_kernel(h_ref, x_ref, wg_ref, wu_ref, wd_ref, o_ref, wg_scr, wu_scr, wd_scr):
    @pl.when(pl.program_id(1) == 0)
    def _():
        o_ref[...] = x_ref[...]

    tm = h_ref.shape[0]
    sub = min(tm, FFN_ROW_SUB)
    wg_scr[...] = wg_ref[...].astype(BF16)
    wu_scr[...] = wu_ref[...].astype(BF16)
    wd_scr[...] = wd_ref[...].astype(BF16)
    for r in range(tm // sub):
        rows = pl.ds(r * sub, sub)
        h = h_ref[rows, :]
        a = jax.nn.silu(_dot(h, wg_scr[...])) * _dot(h, wu_scr[...])
        o_ref[rows, :] += _dot(a.astype(BF16), wd_scr[...])


def _ffn(h2, x1, wg, wu, wd, layer, tm):
    M = x1.shape[0]
    nf = FFN_HIDDEN // FFN_BLOCK
    return pl.pallas_call(
        _ffn_kernel,
        grid=(M // tm, nf),
        in_specs=[pl.BlockSpec((tm, D_MODEL), lambda i, f: (i, 0)),
                  pl.BlockSpec((tm, D_MODEL), lambda i, f: (i, 0), pipeline_mode=pl.Buffered(1)),
                  pl.BlockSpec((None, D_MODEL, FFN_BLOCK), lambda i, f: (layer, 0, f)),
                  pl.BlockSpec((None, D_MODEL, FFN_BLOCK), lambda i, f: (layer, 0, f)),
                  pl.BlockSpec((None, FFN_BLOCK, D_MODEL), lambda i, f: (layer, f, 0))],
        out_specs=pl.BlockSpec((tm, D_MODEL), lambda i, f: (i, 0)),
        out_shape=jax.ShapeDtypeStruct((M, D_MODEL), F32),
        scratch_shapes=[pltpu.VMEM((D_MODEL, FFN_BLOCK), BF16), pltpu.VMEM((D_MODEL, FFN_BLOCK), BF16),
                        pltpu.VMEM((FFN_BLOCK, D_MODEL), BF16)],
        compiler_params=_cparams(("arbitrary", "arbitrary")),
        name="ffn",
    )(h2, x1, wg, wu, wd)


def _cache_shift_kernel(ck_ref, cv_ref, hk_ref, hv_ref, nk_ref, nv_ref, ok_ref, ov_ref):
    c = pl.program_id(2)
    last = pl.num_programs(2) - 1
    R = ck_ref.shape[0]
    S = nk_ref.shape[0]
    for cache, halo, new, out in ((ck_ref, hk_ref, nk_ref, ok_ref), (cv_ref, hv_ref, nv_ref, ov_ref)):
        out[pl.ds(0, R - S)] = cache[pl.ds(S, R - S)]

        @pl.when(c == last)
        def _():
            out[pl.ds(R - S, S)] = new[...]

        @pl.when(c < last)
        def _():
            out[pl.ds(R - S, S)] = halo[...]


def _cache_shift(cache_k, cache_v, new_k, new_v):
    depth, B, W, H, E = cache_k.shape
    S = new_k.shape[2]
    R = min(W, CACHE_ROWS)
    nchunks = W // R
    blk = pl.BlockSpec((None, None, R, H, E), lambda l, b, c: (l, b, c, 0, 0))
    halo = pl.BlockSpec((None, None, None, S, H, E),
                        lambda l, b, c: (l, b, jnp.minimum((c + 1) * (R // S), W // S - 1), 0, 0, 0))
    new = pl.BlockSpec((None, None, S, H, E), lambda l, b, c: (l, b, 0, 0, 0))
    as_rows = lambda a: a.reshape(depth, B, W // S, S, H, E)
    return pl.pallas_call(
        _cache_shift_kernel,
        grid=(depth, B, nchunks),
        in_specs=[blk, blk, halo, halo, new, new],
        out_specs=[blk, blk],
        out_shape=[jax.ShapeDtypeStruct(cache_k.shape, cache_k.dtype)] * 2,
        compiler_params=_cparams(("arbitrary", "arbitrary", "arbitrary")),
        name="cache_shift_w%d" % W,
    )(cache_k, cache_v, as_rows(cache_k), as_rows(cache_v), new_k, new_v)


def _kv_tails_kernel(*refs):
    depth = (len(refs) - 2) // 2
    ok_ref, ov_ref = refs[-2:]
    H, R = refs[0].shape[0], refs[0].shape[1]
    for l in range(depth):
        @pl.when(pl.program_id(0) == l)
        def _():
            for src, out in ((refs[2 * l], ok_ref), (refs[2 * l + 1], ov_ref)):
                for h in range(H):
                    out[pl.ds(h, R, stride=H), :] = src[h]


def _kv_tails(qkv_layers, g, window, nbatch):
    depth = len(qkv_layers)
    M = qkv_layers[0].shape[1]
    T = M // nbatch
    H, E = HEADS_PER_GROUP, HEAD_DIM
    R = min(window, CACHE_ROWS)
    nchunks = window // R
    first = (T - window) // R

    def src(layer, which):
        def index(l, b, c):
            bb = jnp.where(l < layer, 0, jnp.where(l > layer, nbatch - 1, b))
            cc = jnp.where(l < layer, 0, jnp.where(l > layer, nchunks - 1, c))
            return (which * N_ATT_GROUPS + g, bb * (T // R) + first + cc, 0)
        return pl.BlockSpec((H, R, E), index)

    args, in_specs = [], []
    for layer, qkv in enumerate(qkv_layers):
        args += [qkv, qkv]
        in_specs += [src(layer, 1), src(layer, 2)]
    out_spec = pl.BlockSpec((None, R * H, E), lambda l, b, c: (l, b * nchunks + c, 0))
    return pl.pallas_call(
        _kv_tails_kernel,
        grid=(depth, nbatch, nchunks),
        in_specs=in_specs,
        out_specs=[out_spec, out_spec],
        out_shape=[jax.ShapeDtypeStruct((depth, nbatch * window * H, E), qkv_layers[0].dtype)] * 2,
        compiler_params=_cparams(("arbitrary", "arbitrary", "arbitrary")),
        name="kv_tails_w%d" % window,
    )(*args)


def _layer_common(x2d, h0_tiles, nseq, seq_per_step, prm, tm):
    u, qkv, gates = _in_proj(x2d, prm["norm_mix"], prm["w_in"], prm["layer"], prm["q_norm"], prm["k_norm"], tm)
    y_ssm, hfin = _ssm_chunk(u, h0_tiles, prm["ek"], prm["eb"], prm["ea"], prm["sel_k"], prm["sel_b"],
                             prm["lnr"], prm["lni"], nseq, seq_per_step)
    return u, qkv, gates, y_ssm, hfin


def _layer_tail(x2d, u, y_ssm, yb, gates, prm, tm_mix, tm_ffn):
    x1, h2 = _mix_out(y_ssm, u, yb, gates, x2d, prm["ssm_d"], prm["w_glu"], prm["b_glu"], prm["w_branch_a"],
                      prm["w_branch_b"], prm["w_out"], prm["layer"], prm["norm_ffn"], tm_mix)
    return _ffn(h2, x1, prm["w_ffn_gate"], prm["w_ffn_up"], prm["w_ffn_down"], prm["layer"], tm_ffn)


def kernel(x_prompt, x_sample, state_ssm_re, state_ssm_im, cache_k_w128, cache_v_w128, cache_k_w512, cache_v_w512,
           cache_k_w2048, cache_v_w2048, rel_bias, norm_mix, norm_ffn, q_norm, k_norm, w_in, ssm_lambda_re,
           ssm_lambda_im, ssm_log_dt, ssm_b_re, ssm_b_im, ssm_c_re, ssm_c_im, ssm_d, w_glu, b_glu, w_branch_a,
           w_branch_b, w_out, w_ffn_gate, w_ffn_up, w_ffn_down):
    depth = w_in.shape[0]
    B, T, _ = x_prompt.shape
    SB, S, _ = x_sample.shape
    H = HEADS_PER_GROUP
    caches_k = (cache_k_w128, cache_k_w512, cache_k_w2048)
    caches_v = (cache_v_w128, cache_v_w512, cache_v_w2048)

    bucket_tiles = _prompt_bucket_tiles()
    bias_sample = _sample_bias(rel_bias)
    sel_k, sel_b = _ssm_select_matrices()

    xp = x_prompt.reshape(B * T, D_MODEL)
    xs = x_sample.reshape(SB * S, D_MODEL)
    p_re, p_im, s_re, s_im = [], [], [], []
    qkv_prompt = []
    new_k = [[] for _ in WINDOWS]
    new_v = [[] for _ in WINDOWS]

    weights = {
        "w_in": w_in, "w_glu": w_glu.astype(BF16), "w_branch_a": w_branch_a.astype(BF16),
        "w_branch_b": w_branch_b.astype(BF16), "w_out": w_out.astype(BF16), "w_ffn_gate": w_ffn_gate,
        "w_ffn_up": w_ffn_up, "w_ffn_down": w_ffn_down,
    }

    for l in range(depth):
        ek, eb, ea, lnr, lni = _ssm_prep(ssm_lambda_re[l], ssm_lambda_im[l], ssm_log_dt[l], ssm_b_re[l],
                                         ssm_b_im[l], ssm_c_re[l], ssm_c_im[l])
        prm = dict(weights)
        prm.update({
            "layer": l, "norm_mix": norm_mix[l], "norm_ffn": norm_ffn[l], "q_norm": q_norm[l], "k_norm": k_norm[l],
            "ssm_d": ssm_d[l], "b_glu": b_glu[l],
            "ek": ek, "eb": eb, "ea": ea, "sel_k": sel_k, "sel_b": sel_b, "lnr": lnr, "lni": lni,
        })

        zeros = jnp.zeros((B, N_SSM_RANGES, 2 * RANGE_STATE // LANES, LANES), F32)
        u, qkv, gates, y_ssm, hfin = _layer_common(xp, zeros, B, B, prm, IN_PROJ_ROWS)
        yb = _attn_prompt(qkv, rel_bias, bucket_tiles, B)
        xp = _layer_tail(xp, u, y_ssm, yb, gates, prm, MIX_ROWS, FFN_ROWS)
        hr, hi = _tiles_to_state(hfin)
        p_re.append(hr)
        p_im.append(hi)
        qkv_prompt.append(qkv)

        h0 = _state_to_tiles(state_ssm_re[l], state_ssm_im[l])
        u, qkv, gates, y_ssm, hfin = _layer_common(xs, h0, SB, SB, prm, SB * S)
        tok = jnp.transpose(qkv.reshape(3, N_ATT_GROUPS, H, SB, S, HEAD_DIM), (0, 1, 3, 4, 2, 5))
        outs, lses = [], []
        for g, (window, dil) in enumerate(WINDOWS):
            nk_g, nv_g = tok[1, g], tok[2, g]
            new_k[g].append(nk_g)
            new_v[g].append(nv_g)
            o_g, lse_g = _attn_sample(tok[0, g], caches_k[g], caches_v[g], nk_g, nv_g, bias_sample[g], l,
                                      window, dil)
            outs.append(o_g)
            lses.append(lse_g)
        yb = _merge_groups(outs, lses).reshape(SB * S, ATT_WIDTH)
        xs = _layer_tail(xs, u, y_ssm, yb, gates, prm, SB * S, SB * S)
        hr, hi = _tiles_to_state(hfin)
        s_re.append(hr)
        s_im.append(hi)

    shifted = []
    for g in range(N_ATT_GROUPS):
        shifted += _cache_shift(caches_k[g], caches_v[g], jnp.stack(new_k[g]), jnp.stack(new_v[g]))
    tails = [t.reshape(depth, B, window, H, HEAD_DIM)
             for g, (window, _) in enumerate(WINDOWS) for t in _kv_tails(qkv_prompt, g, window, B)]

    return (xp.reshape(B, T, D_MODEL), xs.reshape(SB, S, D_MODEL),
            jnp.stack(p_re), jnp.stack(p_im), *tails,
            jnp.stack(s_re), jnp.stack(s_im), *shifted)
```

```python
import functools
import math

import numpy as np
import jax
import jax.numpy as jnp
from jax import lax
from jax.experimental import pallas as pl
from jax.experimental.pallas import tpu as pltpu

F32 = jnp.float32
BF16 = jnp.bfloat16

D_MODEL = 2048
HEAD_DIM = 128
HEADS_PER_GROUP = 8
WINDOWS = ((128, 1), (512, 4), (2048, 16))
N_ATT_GROUPS = len(WINDOWS)
N_ATT_HEADS = N_ATT_GROUPS * HEADS_PER_GROUP
ATT_WIDTH = HEADS_PER_GROUP * HEAD_DIM
QKV_WIDTH = N_ATT_HEADS * HEAD_DIM
Q_BLOCK = 128
SSM_WIDTH = D_MODEL // 2
SSM_GROUP = 16
N_SSM_GROUPS = SSM_WIDTH // SSM_GROUP
SSM_STATE = 64
IN_WIDTH = SSM_WIDTH + 3 * QKV_WIDTH + 2 * D_MODEL
FFN_HIDDEN = 5632
N_BUCKETS = 32
REL_MAX_DISTANCE = 2048
EPS = 1e-6
NEG_INF = -1e30

LANES = 128
SUBLANES = 8
VMEM_LIMIT_BYTES = 60 * 1024 * 1024

SSM_CHUNK = 8
SSM_RANGE_GROUPS = LANES // SSM_GROUP
N_SSM_RANGES = N_SSM_GROUPS // SSM_RANGE_GROUPS
RANGE_STATE = SSM_RANGE_GROUPS * SSM_STATE
COL_BLOCK = 512
FFN_BLOCK = 256
ATTN_UNROLL = 16
ATTN_FIRST_GROUP = 2
CACHE_ROWS = 1024
IN_PROJ_ROWS = 2048
FFN_ROWS = 1024
MIX_ROWS = 256
ROW_SUB = 256
FFN_ROW_SUB = 512


def _cparams(sem):
    return pltpu.CompilerParams(dimension_semantics=sem, vmem_limit_bytes=VMEM_LIMIT_BYTES)


def _dot(a, b):
    return jnp.dot(a, b, preferred_element_type=F32)


def _dot_nt(a, b):
    return lax.dot_general(a, b, (((1,), (1,)), ((), ())), preferred_element_type=F32)


def _ssm_prep_kernel(lre_ref, lim_ref, ldt_ref, btre_ref, btim_ref, cre_ref, cim_ref,
                     kp_ref, bpre_ref, bpim_ref, are_ref, aim_ref, lnre_ref, lnim_ref):
    L = SSM_CHUNK
    lr = lre_ref[...]
    li = lim_ref[...]
    dt = jnp.exp(ldt_ref[...])
    er = jnp.exp(lr * dt)
    lbr = er * jnp.cos(li * dt)
    lbi = er * jnp.sin(li * dt)
    nr = lbr - 1.0
    dd = lr * lr + li * li
    rr = (nr * lr + lbi * li) / dd
    ri = (lbi * lr - nr * li) / dd
    btr = btre_ref[...]
    bti = btim_ref[...]
    bbr = rr * btr - ri * bti
    bbi = rr * bti + ri * btr
    cr = cre_ref[...]
    ci = cim_ref[...]
    pr = [jnp.ones_like(lbr)]
    pi = [jnp.zeros_like(lbr)]
    for _ in range(L):
        pr.append(pr[-1] * lbr - pi[-1] * lbi)
        pi.append(pr[-2] * lbi + pi[-1] * lbr)
    xr = [cr * pr[t] - ci * pi[t] for t in range(L + 1)]
    xi = [cr * pi[t] + ci * pr[t] for t in range(L + 1)]
    xr_k = jnp.concatenate(xr[:L], axis=0)
    xi_k = jnp.concatenate(xi[:L], axis=0)
    hp = lax.Precision.HIGHEST
    kp = (lax.dot_general(bbr, xr_k, (((1,), (1,)), ((), ())), precision=hp, preferred_element_type=F32)
          - lax.dot_general(bbi, xi_k, (((1,), (1,)), ((), ())), precision=hp, preferred_element_type=F32))
    lane = lax.broadcasted_iota(jnp.int32, kp.shape, 1)
    blocks = [kp] + [jnp.where(lane >= j * SSM_GROUP, pltpu.roll(kp, j * SSM_GROUP, axis=1), 0.0)
                     for j in range(1, L)]
    kp_ref[...] = jnp.concatenate(blocks, axis=0)
    are_ref[...] = jnp.concatenate(xr[1:], axis=0)
    aim_ref[...] = jnp.concatenate(xi[1:], axis=0)
    bpre_ref[...] = jnp.concatenate([pr[L - 1 - j] * bbr - pi[L - 1 - j] * bbi for j in range(L)], axis=0)
    bpim_ref[...] = jnp.concatenate([pr[L - 1 - j] * bbi + pi[L - 1 - j] * bbr for j in range(L)], axis=0)
    lnre_ref[...] = pr[L]
    lnim_ref[...] = pi[L]


def _ssm_prep(lam_re, lam_im, log_dt, b_re, b_im, c_re, c_im):
    G, P, C, L = N_SSM_GROUPS, SSM_STATE, SSM_GROUP, SSM_CHUNK
    row = lambda a: a.reshape(G, 1, -1)
    bt_re = jnp.swapaxes(b_re, 1, 2)
    bt_im = jnp.swapaxes(b_im, 1, 2)
    vec = pl.BlockSpec((None, 1, P), lambda g: (g, 0, 0))
    mat = pl.BlockSpec((None, C, P), lambda g: (g, 0, 0))
    big = pl.BlockSpec((None, L * C, P), lambda g: (g, 0, 0))
    kp, bpre, bpim, are, aim, lnre, lnim = pl.pallas_call(
        _ssm_prep_kernel,
        grid=(G,),
        in_specs=[vec, vec, pl.BlockSpec((None, 1, 1), lambda g: (g, 0, 0)), mat, mat, mat, mat],
        out_specs=[pl.BlockSpec((None, L * C, L * C), lambda g: (g, 0, 0)), big, big, big, big, vec, vec],
        out_shape=[jax.ShapeDtypeStruct((G, L * C, L * C), F32)] + [jax.ShapeDtypeStruct((G, L * C, P), F32)] * 4
                  + [jax.ShapeDtypeStruct((G, 1, P), F32)] * 2,
        compiler_params=_cparams(("arbitrary",)),
        name="ssm_prep",
    )(row(lam_re), row(lam_im), log_dt.reshape(G, 1, 1), bt_re, bt_im, c_re, c_im)

    R, GL = N_SSM_RANGES, SSM_RANGE_GROUPS

    def by_range(a, n_outer):
        n_inner = a.shape[1] // n_outer
        a = a.reshape(R, GL, n_outer, n_inner, LANES)
        return jnp.transpose(a, (0, 2, 1, 3, 4)).reshape(R, n_outer * GL * n_inner, LANES).astype(BF16)

    ek = by_range(kp, L)
    eb = by_range(jnp.concatenate([bpre, bpim], axis=-1), L)
    a_t = jnp.swapaxes(jnp.concatenate([are, -aim], axis=-1), 1, 2)
    ea = by_range(a_t, 2)

    half = RANGE_STATE // LANES
    lnr_t = lnre.reshape(R, half, LANES)
    lni_t = lnim.reshape(R, half, LANES)
    lnr = jnp.concatenate([lnr_t, lnr_t], axis=1)
    lni = jnp.concatenate([-lni_t, lni_t], axis=1)
    return ek, eb, ea, lnr, lni


def _ssm_select_matrices():
    q = np.arange(SSM_CHUNK * LANES)
    r = np.arange(LANES)
    sel_k = (r[:, None] // SSM_GROUP == q[None, :] // LANES) & (r[:, None] % SSM_GROUP == q[None, :] % SSM_GROUP)
    sel_b = (r[:, None] // SSM_STATE == q[None, :] // RANGE_STATE) & (r[:, None] % SSM_STATE == q[None, :] % SSM_STATE)
    return jnp.asarray(sel_k, BF16), jnp.asarray(sel_b, BF16)


HEADS_PER_BLOCK = COL_BLOCK // HEAD_DIM
_SEG_U = 0
_SEG_Q = SSM_WIDTH // COL_BLOCK
_SEG_K = _SEG_Q + QKV_WIDTH // COL_BLOCK
_SEG_V = _SEG_K + QKV_WIDTH // COL_BLOCK
_SEG_GATE = _SEG_V + QKV_WIDTH // COL_BLOCK
_SEG_END = IN_WIDTH // COL_BLOCK


def _head_norm(res, gain):
    outs = []
    for h in range(HEADS_PER_BLOCK):
        t = res[:, h * HEAD_DIM:(h + 1) * HEAD_DIM]
        ms = jnp.mean(t * t, axis=-1, keepdims=True)
        outs.append(t * lax.rsqrt(ms + EPS) * gain)
    return outs


def _rmsnorm_rows(x, gain):
    ms = jnp.mean(x * x, axis=-1, keepdims=True)
    return (x * lax.rsqrt(ms + EPS) * gain).astype(BF16)


def _rmsnorm_kernel(x_ref, g_ref, o_ref):
    o_ref[...] = _rmsnorm_rows(x_ref[...], g_ref[...])


def _rmsnorm(x, gain, tm):
    M, D = x.shape
    return pl.pallas_call(
        _rmsnorm_kernel,
        grid=(M // tm,),
        in_specs=[pl.BlockSpec((tm, D), lambda i: (i, 0)), pl.BlockSpec((1, D), lambda i: (0, 0))],
        out_specs=pl.BlockSpec((tm, D), lambda i: (i, 0)),
        out_shape=jax.ShapeDtypeStruct((M, D), BF16),
        compiler_params=_cparams(("arbitrary",)),
        name="rmsnorm",
    )(x, gain.reshape(1, D))


def _in_proj_kernel(h_ref, hs_ref, w_ref, qn_ref, kn_ref,
                    u_ref, qkv_ref, gate_ref, us_ref, qkvs_ref, gates_s_ref, w_scr):
    i = pl.program_id(0)
    j = pl.program_id(1)
    tm = h_ref.shape[0]
    sub = min(tm, ROW_SUB)

    def row_blocks():
        w_scr[...] = w_ref[...].astype(BF16)
        for r in range(tm // sub):
            rows = pl.ds(r * sub, sub)
            yield rows, _dot(h_ref[rows, :], w_scr[...])

    def sample_res():
        return _dot(hs_ref[...], w_scr[...])

    @pl.when(j < _SEG_Q)
    def _():
        for rows, res in row_blocks():
            u_ref[rows, :] = res

        @pl.when(i == 0)
        def _():
            us_ref[j] = sample_res()

    def split_heads(res, gain_ref):
        if gain_ref is not None:
            return _head_norm(res, gain_ref[...])
        return [res[:, h * HEAD_DIM:(h + 1) * HEAD_DIM] for h in range(HEADS_PER_BLOCK)]

    def store_heads(gain_ref):
        for rows, res in row_blocks():
            for h, t in enumerate(split_heads(res, gain_ref)):
                qkv_ref[h, rows, :] = t

        @pl.when(i == 0)
        def _():
            base = (j - _SEG_Q) * HEADS_PER_BLOCK
            for h, t in enumerate(split_heads(sample_res(), gain_ref)):
                qkvs_ref[base + h] = t

    pl.when((j >= _SEG_Q) & (j < _SEG_K))(functools.partial(store_heads, qn_ref))
    pl.when((j >= _SEG_K) & (j < _SEG_V))(functools.partial(store_heads, kn_ref))
    pl.when((j >= _SEG_V) & (j < _SEG_GATE))(functools.partial(store_heads, None))

    @pl.when(j >= _SEG_GATE)
    def _():
        for rows, res in row_blocks():
            gate_ref[rows, :] = jax.nn.sigmoid(res).astype(gate_ref.dtype)

        @pl.when(i == 0)
        def _():
            gates_s_ref[j - _SEG_GATE] = jax.nn.sigmoid(sample_res()).astype(gates_s_ref.dtype)


def _in_proj(h, hs, w, layer, q_gain, k_gain, tm):
    M, Ms = h.shape[0], hs.shape[0]
    n_heads = 3 * N_ATT_HEADS
    n_gate = _SEG_END - _SEG_GATE
    whole = lambda shape: pl.BlockSpec(shape, lambda i, j: (0,) * len(shape))
    u, qkv, gates, us, qkvs, gates_s = pl.pallas_call(
        _in_proj_kernel,
        grid=(M // tm, _SEG_END),
        in_specs=[
            pl.BlockSpec((tm, D_MODEL), lambda i, j: (i, 0), pipeline_mode=pl.Buffered(1)),
            whole((Ms, D_MODEL)),
            pl.BlockSpec((None, D_MODEL, COL_BLOCK), lambda i, j: (layer, 0, j)),
            pl.BlockSpec((1, HEAD_DIM), lambda i, j: (0, 0)),
            pl.BlockSpec((1, HEAD_DIM), lambda i, j: (0, 0)),
        ],
        out_specs=[
            pl.BlockSpec((tm, COL_BLOCK), lambda i, j: (i, jnp.clip(j, 0, _SEG_Q - 1))),
            pl.BlockSpec((HEADS_PER_BLOCK, tm, HEAD_DIM),
                         lambda i, j: (jnp.clip(j - _SEG_Q, 0, _SEG_GATE - _SEG_Q - 1), i, 0)),
            pl.BlockSpec((tm, COL_BLOCK), lambda i, j: (i, jnp.clip(j - _SEG_GATE, 0, _SEG_END - _SEG_GATE - 1))),
            whole((_SEG_Q, Ms, COL_BLOCK)),
            whole((n_heads, Ms, HEAD_DIM)),
            whole((n_gate, Ms, COL_BLOCK)),
        ],
        out_shape=[
            jax.ShapeDtypeStruct((M, SSM_WIDTH), F32),
            jax.ShapeDtypeStruct((n_heads, M, HEAD_DIM), F32),
            jax.ShapeDtypeStruct((M, 2 * D_MODEL), BF16),
            jax.ShapeDtypeStruct((_SEG_Q, Ms, COL_BLOCK), F32),
            jax.ShapeDtypeStruct((n_heads, Ms, HEAD_DIM), F32),
            jax.ShapeDtypeStruct((n_gate, Ms, COL_BLOCK), BF16),
        ],
        scratch_shapes=[pltpu.VMEM((D_MODEL, COL_BLOCK), BF16)],
        compiler_params=_cparams(("arbitrary", "arbitrary")),
        name="in_proj",
    )(h, hs, w, q_gain.reshape(1, HEAD_DIM), k_gain.reshape(1, HEAD_DIM))
    us = jnp.swapaxes(us, 0, 1).reshape(Ms, SSM_WIDTH)
    gates_s = jnp.swapaxes(gates_s, 0, 1).reshape(Ms, 2 * D_MODEL)
    return (u, qkv, gates), (us, qkvs, gates_s)


def _expand_block_diag(e_ref, sel_ref, row_shift, col_shift, out_scr):
    n = out_scr.shape[0]
    for c in range(n // LANES):
        rows = pl.ds(c * LANES, LANES)
        full = _dot(e_ref[rows, :], sel_ref[...])
        row = lax.broadcasted_iota(jnp.int32, full.shape, 0) + c * LANES
        col = lax.broadcasted_iota(jnp.int32, full.shape, 1)
        keep = ((row >> row_shift) & (SSM_RANGE_GROUPS - 1)) == ((col >> col_shift) & (SSM_RANGE_GROUPS - 1))
        out_scr[rows, :] = jnp.where(keep, full, 0.0).astype(BF16)


def _ssm_chunk_kernel(nseq, nk, u_ref, h0_ref, ek_ref, eb_ref, ea_ref, selk_ref, selb_ref, lnr_ref, lni_ref,
                      y_ref, hfin_ref, s_scr, ktoe_ref, bcat_ref, acat_ref):
    L = SSM_CHUNK
    rows = nseq * nk
    nsub = 2 * RANGE_STATE // LANES
    lg_c = SSM_GROUP.bit_length() - 1
    lg_p = SSM_STATE.bit_length() - 1

    @pl.when(pl.program_id(1) == 0)
    def _():
        _expand_block_diag(ek_ref, selk_ref, lg_c, lg_c, ktoe_ref)
        _expand_block_diag(eb_ref, selb_ref, lg_c, lg_p, bcat_ref)
        _expand_block_diag(ea_ref, selk_ref, lg_p, lg_c, acat_ref)

    ucat = jnp.concatenate([u_ref[pl.ds(j, rows, stride=L), :] for j in range(L)], axis=-1).astype(BF16)
    s = _dot(ucat, bcat_ref[...])
    for n in range(nsub):
        s_scr[pl.ds(n, rows, stride=nsub), :] = s[:, n * LANES:(n + 1) * LANES]
    y_ref_intra = _dot(ucat, ktoe_ref[...])

    lnr = lnr_ref[...]
    lni = lni_ref[...]

    def step(k, hs):
        new = []
        for q in range(nseq):
            off = pl.multiple_of((q * nk + k) * nsub, nsub)
            h = hs[q]
            sk = s_scr[pl.ds(off, nsub), :]
            s_scr[pl.ds(off, nsub), :] = h
            new.append(h * lnr + pltpu.roll(h, nsub // 2, axis=0) * lni + sk)
        return tuple(new)

    hs = lax.fori_loop(0, nk, step, tuple(h0_ref[q] for q in range(nseq)))
    for q in range(nseq):
        hfin_ref[q] = hs[q]

    hprev = jnp.concatenate([s_scr[pl.ds(n, rows, stride=nsub), :] for n in range(nsub)], axis=-1).astype(BF16)
    y = y_ref_intra + _dot(hprev, acat_ref[...])
    for j in range(L):
        y_ref[pl.ds(j, rows, stride=L), :] = y[:, j * LANES:(j + 1) * LANES]


def _ssm_chunk(u, h0, ek, eb, ea, sel_k, sel_b, lnr, lni, nseq_total, seq_per_step):
    M = u.shape[0]
    T = M // nseq_total
    nk = T // SSM_CHUNK
    nsteps = nseq_total // seq_per_step
    R = N_SSM_RANGES
    nsub = 2 * RANGE_STATE // LANES
    tm = seq_per_step * T
    wide = SSM_CHUNK * LANES
    assert wide == 2 * RANGE_STATE
    wspec = lambda shape: pl.BlockSpec((None,) + shape, lambda r, b: (r, 0, 0))
    sel_spec = pl.BlockSpec((LANES, wide), lambda r, b: (0, 0))
    return pl.pallas_call(
        functools.partial(_ssm_chunk_kernel, seq_per_step, nk),
        grid=(R, nsteps),
        in_specs=[
            pl.BlockSpec((tm, LANES), lambda r, b: (b, r)),
            pl.BlockSpec((seq_per_step, None, nsub, LANES), lambda r, b: (b, r, 0, 0)),
            wspec((wide, LANES)),
            wspec((wide, LANES)),
            wspec((wide, LANES)),
            sel_spec,
            sel_spec,
            wspec((nsub, LANES)),
            wspec((nsub, LANES)),
        ],
        out_specs=[
            pl.BlockSpec((tm, LANES), lambda r, b: (b, r)),
            pl.BlockSpec((seq_per_step, None, nsub, LANES), lambda r, b: (b, r, 0, 0)),
        ],
        out_shape=[
            jax.ShapeDtypeStruct((M, SSM_WIDTH), F32),
            jax.ShapeDtypeStruct((nseq_total, R, nsub, LANES), F32),
        ],
        scratch_shapes=[pltpu.VMEM((seq_per_step * nk * nsub, LANES), F32)] + [pltpu.VMEM((wide, wide), BF16)] * 3,
        compiler_params=_cparams(("arbitrary", "arbitrary")),
        name="ssm_chunk",
    )(u, h0, ek, eb, ea, sel_k, sel_b, lnr, lni)


def _state_to_tiles(re, im):
    N = re.shape[0]
    half = RANGE_STATE // LANES
    return jnp.concatenate([re.reshape(N, N_SSM_RANGES, half, LANES),
                            im.reshape(N, N_SSM_RANGES, half, LANES)], axis=2)


def _tiles_to_state(t):
    N = t.shape[0]
    half = RANGE_STATE // LANES
    return (t[:, :, :half].reshape(N, N_SSM_GROUPS, SSM_STATE),
            t[:, :, half:].reshape(N, N_SSM_GROUPS, SSM_STATE))


def _bucket_np(dist):
    max_exact = N_BUCKETS // 2
    n = np.maximum(dist, 0)
    nf = np.maximum(n, 1).astype(np.float64)
    large = max_exact + (np.log(nf / max_exact) / math.log(REL_MAX_DISTANCE / max_exact)
                         * (N_BUCKETS - max_exact)).astype(np.int32)
    large = np.minimum(large, N_BUCKETS - 1)
    return np.where(n < max_exact, n, large)


def _prompt_bucket_tiles():
    a = np.arange(Q_BLOCK)[:, None]
    c = np.arange(2 * Q_BLOCK)[None, :]
    rel = a - c + Q_BLOCK
    tiles = []
    for window, dil in WINDOWS:
        K = window // dil + 1
        valid = (rel >= 0) & (rel < K)
        tiles.append(np.where(valid, _bucket_np(np.clip(rel, 0, K - 1) * dil), -1))
    return jnp.asarray(np.stack(tiles), jnp.int32)


def _attn_group_of_step(step):
    return (step + ATTN_FIRST_GROUP) % N_ATT_GROUPS


def _attn_prompt_kernel(T, tab_ref, q_ref, k_ref, v_ref, bkt_ref, o_ref,
                        m_scr, l_scr, acc_scr, s_scr, p_scr, mb_scr, bias_scr):
    h = pl.program_id(1)
    step = pl.program_id(2)
    g = _attn_group_of_step(step)
    scale = HEAD_DIM ** -0.5
    nblk = T // Q_BLOCK

    bkt = bkt_ref[...]
    col = g * HEADS_PER_GROUP + h
    bias = jnp.full(bkt.shape, NEG_INF, F32)
    for t in range(N_BUCKETS):
        bias = jnp.where(bkt == t, tab_ref[t, col], bias)
    bias_scr[...] = bias

    def run_group(first, dil):
        per_class = nblk // dil

        def rows(i):
            r = i // per_class
            n = i % per_class
            row_q = r + n * (Q_BLOCK * dil)
            row_p = r + jnp.maximum(n - 1, 0) * (Q_BLOCK * dil)
            return n, pl.ds(row_q, Q_BLOCK, stride=dil), pl.ds(row_p, Q_BLOCK, stride=dil)

        def scores(i, carry):
            n, sl_q, sl_p = rows(i)
            q = q_ref[sl_q, :].astype(BF16)
            s_r = _dot_nt(q, k_ref[sl_q, :].astype(BF16)) * scale + bias_scr[:, Q_BLOCK:]
            s_l = (_dot_nt(q, k_ref[sl_p, :].astype(BF16)) * scale
                   + jnp.where(n == 0, NEG_INF, bias_scr[:, :Q_BLOCK]))
            s_scr[i, :, :Q_BLOCK] = s_l
            s_scr[i, :, Q_BLOCK:] = s_r
            m = jnp.maximum(jnp.max(s_l, axis=-1, keepdims=True), jnp.max(s_r, axis=-1, keepdims=True))
            mb_scr[i] = jnp.broadcast_to(m, (Q_BLOCK, HEAD_DIM))
            return carry

        def probs(i, carry):
            mb = mb_scr[i]
            p_scr[i, :, :Q_BLOCK] = jnp.exp(s_scr[i, :, :Q_BLOCK] - mb).astype(BF16)
            p_scr[i, :, Q_BLOCK:] = jnp.exp(s_scr[i, :, Q_BLOCK:] - mb).astype(BF16)
            return carry

        def values(i, carry):
            _, sl_q, sl_p = rows(i)
            ones = jnp.ones((Q_BLOCK, HEAD_DIM), BF16)
            v_p = jnp.concatenate([v_ref[sl_p, :].astype(BF16), ones], axis=1)
            v_c = jnp.concatenate([v_ref[sl_q, :].astype(BF16), ones], axis=1)
            ol = _dot(p_scr[i, :, :Q_BLOCK], v_p) + _dot(p_scr[i, :, Q_BLOCK:], v_c)
            o = ol[:, :HEAD_DIM]
            lb = ol[:, HEAD_DIM:]
            mb = mb_scr[i]
            if first:
                m_scr[sl_q, :] = mb
                l_scr[sl_q, :] = lb
                acc_scr[sl_q, :] = o
            else:
                m0 = m_scr[sl_q, :]
                mn = jnp.maximum(m0, mb)
                a0 = jnp.exp(m0 - mn)
                a1 = jnp.exp(mb - mn)
                m_scr[sl_q, :] = mn
                l_scr[sl_q, :] = a0 * l_scr[sl_q, :] + a1 * lb
                acc_scr[sl_q, :] = a0 * acc_scr[sl_q, :] + a1 * o
            return carry

        lax.fori_loop(0, nblk, scores, 0, unroll=ATTN_UNROLL)
        lax.fori_loop(0, nblk, probs, 0, unroll=ATTN_UNROLL)
        lax.fori_loop(0, nblk, values, 0, unroll=ATTN_UNROLL)

    for s in range(N_ATT_GROUPS):
        dil = WINDOWS[(s + ATTN_FIRST_GROUP) % N_ATT_GROUPS][1]
        pl.when(step == s)(functools.partial(run_group, s == 0, dil))

    @pl.when(step == N_ATT_GROUPS - 1)
    def _():
        o_ref[...] = (acc_scr[...] / l_scr[...]).astype(o_ref.dtype)


def _attn_prompt(qkv, rel_bias, bucket_tiles, nbatch):
    M = qkv.shape[1]
    T = M // nbatch
    H, G = HEADS_PER_GROUP, N_ATT_GROUPS
    nblk = T // Q_BLOCK

    def qkv_spec(which):
        return pl.BlockSpec((None, T, HEAD_DIM),
                            lambda b, h, s: (which * N_ATT_HEADS + _attn_group_of_step(s) * H + h, b, 0))

    return pl.pallas_call(
        functools.partial(_attn_prompt_kernel, T),
        grid=(nbatch, H, G),
        in_specs=[pl.BlockSpec(memory_space=pltpu.SMEM),
                  qkv_spec(0), qkv_spec(1), qkv_spec(2),
                  pl.BlockSpec((None, Q_BLOCK, 2 * Q_BLOCK), lambda b, h, s: (_attn_group_of_step(s), 0, 0))],
        out_specs=pl.BlockSpec((T, HEAD_DIM), lambda b, h, s: (b, h)),
        out_shape=jax.ShapeDtypeStruct((M, ATT_WIDTH), BF16),
        scratch_shapes=[pltpu.VMEM((T, HEAD_DIM), F32)] * 3
                       + [pltpu.VMEM((nblk, Q_BLOCK, 2 * Q_BLOCK), F32),
                          pltpu.VMEM((nblk, Q_BLOCK, 2 * Q_BLOCK), BF16),
                          pltpu.VMEM((nblk, Q_BLOCK, HEAD_DIM), F32),
                          pltpu.VMEM((Q_BLOCK, 2 * Q_BLOCK), F32)],
        compiler_params=_cparams(("arbitrary", "arbitrary", "arbitrary")),
        name="attn_prompt",
    )(rel_bias.astype(F32), qkv, qkv, qkv, bucket_tiles)


def _attn_sample_kernel(window, dil, q_ref, ck_ref, cv_ref, nk_ref, nv_ref, bias_ref, o_ref, lse_ref):
    S = q_ref.shape[0]
    K = window // dil + 1
    scale = HEAD_DIM ** -0.5
    bias = bias_ref[...]
    for s in range(S):
        n_c = (window - 1 - s) // dil + 1
        qs = q_ref[s]
        new_rows = [s + j * dil - window for j in range(n_c, K)]
        kk = jnp.concatenate([ck_ref[pl.ds(s, n_c, stride=dil)]] + [nk_ref[pl.ds(i, 1)] for i in new_rows], axis=0)
        vv = jnp.concatenate([cv_ref[pl.ds(s, n_c, stride=dil)]] + [nv_ref[pl.ds(i, 1)] for i in new_rows], axis=0)
        lg = jnp.sum(kk * qs[None], axis=-1, keepdims=True) * scale + bias
        m = jnp.max(lg, axis=0)
        p = jnp.exp(lg - m[None])
        l = jnp.sum(p, axis=0)
        o_ref[s] = jnp.sum(p * vv, axis=0) / l
        lse_ref[s] = m + jnp.log(l)


def _attn_sample(q, cache_k, cache_v, new_k, new_v, bias, layer, window, dil):
    B, S = q.shape[0], q.shape[1]
    H = HEADS_PER_GROUP
    K = window // dil + 1
    small = pl.BlockSpec((None, S, H, HEAD_DIM), lambda b: (b, 0, 0, 0))
    cache = pl.BlockSpec((None, None, window, H, HEAD_DIM), lambda b: (layer, b, 0, 0, 0))
    return pl.pallas_call(
        functools.partial(_attn_sample_kernel, window, dil),
        grid=(B,),
        in_specs=[small, cache, cache, small, small, pl.BlockSpec((K, H, HEAD_DIM), lambda b: (0, 0, 0))],
        out_specs=[small, small],
        out_shape=[jax.ShapeDtypeStruct((B, S, H, HEAD_DIM), F32)] * 2,
        compiler_params=_cparams(("arbitrary",)),
        name="attn_sample_w%d" % window,
    )(q, cache_k, cache_v, new_k, new_v, bias)


def _merge_groups_kernel(o0, o1, o2, l0, l1, l2, y_ref):
    a, b, c = l0[...], l1[...], l2[...]
    m = jnp.maximum(jnp.maximum(a, b), c)
    ea, eb, ec = jnp.exp(a - m), jnp.exp(b - m), jnp.exp(c - m)
    y_ref[...] = ((ea * o0[...] + eb * o1[...] + ec * o2[...]) / (ea + eb + ec)).astype(y_ref.dtype)


def _merge_groups(outs, lses):
    shape = outs[0].shape
    return pl.pallas_call(
        _merge_groups_kernel,
        out_shape=jax.ShapeDtypeStruct(shape, BF16),
        name="merge_groups",
    )(*outs, *lses)


def _sample_bias(rel_bias):
    out = []
    for g, (window, dil) in enumerate(WINDOWS):
        K = window // dil + 1
        steps = (K - 1) - np.arange(K)
        tab = rel_bias[:, g * HEADS_PER_GROUP:(g + 1) * HEADS_PER_GROUP].astype(F32)
        b = tab[_bucket_np(steps * dil)]
        out.append(jnp.broadcast_to(b[:, :, None], (K, HEADS_PER_GROUP, HEAD_DIM)))
    return out


def _gelu_tanh(x):
    return 0.5 * x * (1.0 + jnp.tanh(math.sqrt(2.0 / math.pi) * (x + 0.044715 * (x * x * x))))


def _mix_out_kernel(y_ref, u_ref, yb_ref, ga_ref, gb_ref, x_ref, d_ref, wglu_ref, bglu_ref, wa_ref, wb_ref,
                    wout_ref, gn_ref, x1_ref, h2_ref):
    y = y_ref[...] + d_ref[...] * u_ref[...]
    z = _gelu_tanh(y)
    ya = z * jax.nn.sigmoid(_dot(z.astype(BF16), wglu_ref[...]) + bglu_ref[...])
    mix = (ga_ref[...] * _dot(ya.astype(BF16), wa_ref[...])
           + gb_ref[...] * _dot(yb_ref[...], wb_ref[...]))
    x1 = x_ref[...] + _dot(mix.astype(BF16), wout_ref[...])
    x1_ref[...] = x1
    ms = jnp.mean(x1 * x1, axis=-1, keepdims=True)
    h2_ref[...] = (x1 * lax.rsqrt(ms + EPS) * gn_ref[...]).astype(BF16)


def _mix_out(y_ssm, u, yb, gates, x, ssm_d, wglu, bglu, wa, wb, wout, layer, gain, tm):
    M = x.shape[0]
    row = lambda w: pl.BlockSpec((tm, w), lambda i: (i, 0))
    vec = lambda b: pl.BlockSpec((1, b), lambda i: (0, 0), pipeline_mode=pl.Buffered(1))
    full = lambda a, b: pl.BlockSpec((None, a, b), lambda i: (layer, 0, 0), pipeline_mode=pl.Buffered(1))
    return pl.pallas_call(
        _mix_out_kernel,
        grid=(M // tm,),
        in_specs=[row(SSM_WIDTH), row(SSM_WIDTH), row(ATT_WIDTH),
                  pl.BlockSpec((tm, D_MODEL), lambda i: (i, 0)),
                  pl.BlockSpec((tm, D_MODEL), lambda i: (i, 1)),
                  row(D_MODEL),
                  vec(SSM_WIDTH), full(SSM_WIDTH, SSM_WIDTH), vec(SSM_WIDTH),
                  full(SSM_WIDTH, D_MODEL), full(ATT_WIDTH, D_MODEL), full(D_MODEL, D_MODEL), vec(D_MODEL)],
        out_specs=[row(D_MODEL), row(D_MODEL)],
        out_shape=[jax.ShapeDtypeStruct((M, D_MODEL), F32), jax.ShapeDtypeStruct((M, D_MODEL), BF16)],
        compiler_params=_cparams(("arbitrary",)),
        name="mix_out",
    )(y_ssm, u, yb, gates, gates, x, ssm_d.reshape(1, -1), wglu, bglu.reshape(1, -1), wa, wb, wout,
      gain.reshape(1, -1))


def _ffn_kernel(emit_norm, h_ref, x_ref, hs_ref, xs_ref, wg_ref, wu_ref, wd_ref, gn_ref, *rest):
    if emit_norm:
        o_ref, os_ref, hn_ref, hns_ref, wg_scr, wu_scr, wd_scr = rest
    else:
        o_ref, os_ref, wg_scr, wu_scr, wd_scr = rest
    i = pl.program_id(0)
    f = pl.program_id(1)
    last = pl.num_programs(1) - 1

    @pl.when(f == 0)
    def _():
        o_ref[...] = x_ref[...]

    @pl.when((f == 0) & (i == 0))
    def _():
        os_ref[...] = xs_ref[...]

    def swiglu(h):
        a = jax.nn.silu(_dot(h, wg_scr[...])) * _dot(h, wu_scr[...])
        return _dot(a.astype(BF16), wd_scr[...])

    tm = h_ref.shape[0]
    sub = min(tm, FFN_ROW_SUB)
    wg_scr[...] = wg_ref[...].astype(BF16)
    wu_scr[...] = wu_ref[...].astype(BF16)
    wd_scr[...] = wd_ref[...].astype(BF16)
    for r in range(tm // sub):
        rows = pl.ds(r * sub, sub)
        o_ref[rows, :] += swiglu(h_ref[rows, :])

    @pl.when(i == 0)
    def _():
        os_ref[...] += swiglu(hs_ref[...])

    if emit_norm:
        @pl.when(f == last)
        def _():
            for r in range(tm // sub):
                rows = pl.ds(r * sub, sub)
                hn_ref[rows, :] = _rmsnorm_rows(o_ref[rows, :], gn_ref[...])

        @pl.when((f == last) & (i == 0))
        def _():
            hns_ref[...] = _rmsnorm_rows(os_ref[...], gn_ref[...])


def _ffn(h2, x1, h2s, x1s, wg, wu, wd, layer, next_gain, tm):
    M, Ms = x1.shape[0], x1s.shape[0]
    nf = FFN_HIDDEN // FFN_BLOCK
    emit_norm = next_gain is not None
    gain = next_gain if emit_norm else jnp.ones((D_MODEL,), F32)
    row = pl.BlockSpec((tm, D_MODEL), lambda i, f: (i, 0))
    whole = pl.BlockSpec((Ms, D_MODEL), lambda i, f: (0, 0))
    out_specs = [row, whole]
    out_shape = [jax.ShapeDtypeStruct((M, D_MODEL), F32), jax.ShapeDtypeStruct((Ms, D_MODEL), F32)]
    if emit_norm:
        out_specs += [row, whole]
        out_shape += [jax.ShapeDtypeStruct((M, D_MODEL), BF16), jax.ShapeDtypeStruct((Ms, D_MODEL), BF16)]
    return pl.pallas_call(
        functools.partial(_ffn_kernel, emit_norm),
        grid=(M // tm, nf),
        in_specs=[row,
                  pl.BlockSpec((tm, D_MODEL), lambda i, f: (i, 0), pipeline_mode=pl.Buffered(1)),
                  whole, whole,
                  pl.BlockSpec((None, D_MODEL, FFN_BLOCK), lambda i, f: (layer, 0, f)),
                  pl.BlockSpec((None, D_MODEL, FFN_BLOCK), lambda i, f: (layer, 0, f)),
                  pl.BlockSpec((None, FFN_BLOCK, D_MODEL), lambda i, f: (layer, f, 0)),
                  pl.BlockSpec((1, D_MODEL), lambda i, f: (0, 0))],
        out_specs=out_specs,
        out_shape=out_shape,
        scratch_shapes=[pltpu.VMEM((D_MODEL, FFN_BLOCK), BF16), pltpu.VMEM((D_MODEL, FFN_BLOCK), BF16),
                        pltpu.VMEM((FFN_BLOCK, D_MODEL), BF16)],
        compiler_params=_cparams(("arbitrary", "arbitrary")),
        name="ffn",
    )(h2, x1, h2s, x1s, wg, wu, wd, gain.reshape(1, D_MODEL))


def _cache_shift_kernel(ck_ref, cv_ref, hk_ref, hv_ref, nk_ref, nv_ref, ok_ref, ov_ref):
    c = pl.program_id(2)
    last = pl.num_programs(2) - 1
    R = ck_ref.shape[0]
    S = nk_ref.shape[0]
    for cache, halo, new, out in ((ck_ref, hk_ref, nk_ref, ok_ref), (cv_ref, hv_ref, nv_ref, ov_ref)):
        out[pl.ds(0, R - S)] = cache[pl.ds(S, R - S)]

        @pl.when(c == last)
        def _():
            out[pl.ds(R - S, S)] = new[...]

        @pl.when(c < last)
        def _():
            out[pl.ds(R - S, S)] = halo[...]


def _cache_shift(cache_k, cache_v, new_k, new_v):
    depth, B, W, H, E = cache_k.shape
    S = new_k.shape[2]
    R = min(W, CACHE_ROWS)
    nchunks = W // R
    blk = pl.BlockSpec((None, None, R, H, E), lambda l, b, c: (l, b, c, 0, 0))
    halo = pl.BlockSpec((None, None, None, S, H, E),
                        lambda l, b, c: (l, b, jnp.minimum((c + 1) * (R // S), W // S - 1), 0, 0, 0))
    new = pl.BlockSpec((None, None, S, H, E), lambda l, b, c: (l, b, 0, 0, 0))
    as_rows = lambda a: a.reshape(depth, B, W // S, S, H, E)
    return pl.pallas_call(
        _cache_shift_kernel,
        grid=(depth, B, nchunks),
        in_specs=[blk, blk, halo, halo, new, new],
        out_specs=[blk, blk],
        out_shape=[jax.ShapeDtypeStruct(cache_k.shape, cache_k.dtype)] * 2,
        compiler_params=_cparams(("arbitrary", "arbitrary", "arbitrary")),
        name="cache_shift_w%d" % W,
    )(cache_k, cache_v, as_rows(cache_k), as_rows(cache_v), new_k, new_v)


def _kv_tails_kernel(*refs):
    depth = (len(refs) - 2) // 2
    ok_ref, ov_ref = refs[-2:]
    H, R = refs[0].shape[0], refs[0].shape[1]
    for l in range(depth):
        @pl.when(pl.program_id(0) == l)
        def _():
            for src, out in ((refs[2 * l], ok_ref), (refs[2 * l + 1], ov_ref)):
                for h in range(H):
                    out[pl.ds(h, R, stride=H), :] = src[h]


def _kv_tails(qkv_layers, g, window, nbatch):
    depth = len(qkv_layers)
    M = qkv_layers[0].shape[1]
    T = M // nbatch
    H, E = HEADS_PER_GROUP, HEAD_DIM
    R = min(window, CACHE_ROWS)
    nchunks = window // R
    first = (T - window) // R

    def src(layer, which):
        def index(l, b, c):
            bb = jnp.where(l < layer, 0, jnp.where(l > layer, nbatch - 1, b))
            cc = jnp.where(l < layer, 0, jnp.where(l > layer, nchunks - 1, c))
            return (which * N_ATT_GROUPS + g, bb * (T // R) + first + cc, 0)
        return pl.BlockSpec((H, R, E), index)

    args, in_specs = [], []
    for layer, qkv in enumerate(qkv_layers):
        args += [qkv, qkv]
        in_specs += [src(layer, 1), src(layer, 2)]
    out_spec = pl.BlockSpec((None, R * H, E), lambda l, b, c: (l, b * nchunks + c, 0))
    return pl.pallas_call(
        _kv_tails_kernel,
        grid=(depth, nbatch, nchunks),
        in_specs=in_specs,
        out_specs=[out_spec, out_spec],
        out_shape=[jax.ShapeDtypeStruct((depth, nbatch * window * H, E), qkv_layers[0].dtype)] * 2,
        compiler_params=_cparams(("arbitrary", "arbitrary", "arbitrary")),
        name="kv_tails_w%d" % window,
    )(*args)


def _ssm(u, h0_tiles, nseq, prm):
    return _ssm_chunk(u, h0_tiles, prm["ek"], prm["eb"], prm["ea"], prm["sel_k"], prm["sel_b"],
                      prm["lnr"], prm["lni"], nseq, nseq)


def _mix(x2d, u, y_ssm, yb, gates, prm, tm):
    return _mix_out(y_ssm, u, yb, gates, x2d, prm["ssm_d"], prm["w_glu"], prm["b_glu"], prm["w_branch_a"],
                    prm["w_branch_b"], prm["w_out"], prm["layer"], prm["norm_ffn"], tm)


def kernel(x_prompt, x_sample, state_ssm_re, state_ssm_im, cache_k_w128, cache_v_w128, cache_k_w512, cache_v_w512,
           cache_k_w2048, cache_v_w2048, rel_bias, norm_mix, norm_ffn, q_norm, k_norm, w_in, ssm_lambda_re,
           ssm_lambda_im, ssm_log_dt, ssm_b_re, ssm_b_im, ssm_c_re, ssm_c_im, ssm_d, w_glu, b_glu, w_branch_a,
           w_branch_b, w_out, w_ffn_gate, w_ffn_up, w_ffn_down):
    depth = w_in.shape[0]
    B, T, _ = x_prompt.shape
    SB, S, _ = x_sample.shape
    H = HEADS_PER_GROUP
    caches_k = (cache_k_w128, cache_k_w512, cache_k_w2048)
    caches_v = (cache_v_w128, cache_v_w512, cache_v_w2048)

    bucket_tiles = _prompt_bucket_tiles()
    bias_sample = _sample_bias(rel_bias)
    sel_k, sel_b = _ssm_select_matrices()

    xp = x_prompt.reshape(B * T, D_MODEL)
    xs = x_sample.reshape(SB * S, D_MODEL)
    p_re, p_im, s_re, s_im = [], [], [], []
    qkv_prompt = []
    new_k = [[] for _ in WINDOWS]
    new_v = [[] for _ in WINDOWS]

    weights = {
        "w_glu": w_glu.astype(BF16), "w_branch_a": w_branch_a.astype(BF16),
        "w_branch_b": w_branch_b.astype(BF16), "w_out": w_out.astype(BF16),
    }

    hp = _rmsnorm(xp, norm_mix[0], FFN_ROWS)
    hs = _rmsnorm(xs, norm_mix[0], SB * S)

    for l in range(depth):
        ek, eb, ea, lnr, lni = _ssm_prep(ssm_lambda_re[l], ssm_lambda_im[l], ssm_log_dt[l], ssm_b_re[l],
                                         ssm_b_im[l], ssm_c_re[l], ssm_c_im[l])
        prm = dict(weights)
        prm.update({
            "layer": l, "norm_ffn": norm_ffn[l], "ssm_d": ssm_d[l], "b_glu": b_glu[l],
            "ek": ek, "eb": eb, "ea": ea, "sel_k": sel_k, "sel_b": sel_b, "lnr": lnr, "lni": lni,
        })

        (u, qkv, gates), (us, qkvs, gates_s) = _in_proj(hp, hs, w_in, l, q_norm[l], k_norm[l], IN_PROJ_ROWS)

        zeros = jnp.zeros((B, N_SSM_RANGES, 2 * RANGE_STATE // LANES, LANES), F32)
        y_ssm, hfin = _ssm(u, zeros, B, prm)
        yb = _attn_prompt(qkv, rel_bias, bucket_tiles, B)
        x1, h2 = _mix(xp, u, y_ssm, yb, gates, prm, MIX_ROWS)
        hr, hi = _tiles_to_state(hfin)
        p_re.append(hr)
        p_im.append(hi)
        qkv_prompt.append(qkv)

        h0 = _state_to_tiles(state_ssm_re[l], state_ssm_im[l])
        ys_ssm, hfin = _ssm(us, h0, SB, prm)
        tok = jnp.transpose(qkvs.reshape(3, N_ATT_GROUPS, H, SB, S, HEAD_DIM), (0, 1, 3, 4, 2, 5))
        outs, lses = [], []
        for g, (window, dil) in enumerate(WINDOWS):
            nk_g, nv_g = tok[1, g], tok[2, g]
            new_k[g].append(nk_g)
            new_v[g].append(nv_g)
            o_g, lse_g = _attn_sample(tok[0, g], caches_k[g], caches_v[g], nk_g, nv_g, bias_sample[g], l,
                                      window, dil)
            outs.append(o_g)
            lses.append(lse_g)
        ybs = _merge_groups(outs, lses).reshape(SB * S, ATT_WIDTH)
        x1s, h2s = _mix(xs, us, ys_ssm, ybs, gates_s, prm, SB * S)
        hr, hi = _tiles_to_state(hfin)
        s_re.append(hr)
        s_im.append(hi)

        next_gain = norm_mix[l + 1] if l + 1 < depth else None
        res = _ffn(h2, x1, h2s, x1s, w_ffn_gate, w_ffn_up, w_ffn_down, l, next_gain, FFN_ROWS)
        if next_gain is None:
            xp, xs = res
        else:
            xp, xs, hp, hs = res

    shifted = []
    for g in range(N_ATT_GROUPS):
        shifted += _cache_shift(caches_k[g], caches_v[g], jnp.stack(new_k[g]), jnp.stack(new_v[g]))
    tails = [t.reshape(depth, B, window, H, HEAD_DIM)
             for g, (window, _) in enumerate(WINDOWS) for t in _kv_tails(qkv_prompt, g, window, B)]

    return (xp.reshape(B, T, D_MODEL), xs.reshape(SB, S, D_MODEL),
            jnp.stack(p_re), jnp.stack(p_im), *tails,
            jnp.stack(s_re), jnp.stack(s_im), *shifted)
```

```python
import functools
import math

import numpy as np
import jax
import jax.numpy as jnp
from jax import lax
from jax.experimental import pallas as pl
from jax.experimental.pallas import tpu as pltpu

F32 = jnp.float32
BF16 = jnp.bfloat16

D_MODEL = 2048
HEAD_DIM = 128
HEADS_PER_GROUP = 8
WINDOWS = ((128, 1), (512, 4), (2048, 16))
N_ATT_GROUPS = len(WINDOWS)
N_ATT_HEADS = N_ATT_GROUPS * HEADS_PER_GROUP
ATT_WIDTH = HEADS_PER_GROUP * HEAD_DIM
QKV_WIDTH = N_ATT_HEADS * HEAD_DIM
Q_BLOCK = 128
SSM_WIDTH = D_MODEL // 2
SSM_GROUP = 16
N_SSM_GROUPS = SSM_WIDTH // SSM_GROUP
SSM_STATE = 64
IN_WIDTH = SSM_WIDTH + 3 * QKV_WIDTH + 2 * D_MODEL
FFN_HIDDEN = 5632
N_BUCKETS = 32
REL_MAX_DISTANCE = 2048
EPS = 1e-6
NEG_INF = -1e30

LANES = 128
SUBLANES = 8
VMEM_LIMIT_BYTES = 60 * 1024 * 1024

SSM_CHUNK = 8
SSM_RANGE_GROUPS = LANES // SSM_GROUP
N_SSM_RANGES = N_SSM_GROUPS // SSM_RANGE_GROUPS
RANGE_STATE = SSM_RANGE_GROUPS * SSM_STATE
COL_BLOCK = 512
FFN_BLOCK = 256
ATTN_UNROLL = 16
ATTN_FIRST_GROUP = 2
CACHE_ROWS = 1024
IN_PROJ_ROWS = 2048
FFN_ROWS = 1024
MIX_ROWS = 256
ROW_SUB = 256
FFN_ROW_SUB = 512


def _cparams(sem):
    return pltpu.CompilerParams(dimension_semantics=sem, vmem_limit_bytes=VMEM_LIMIT_BYTES)


def _dot(a, b):
    return jnp.dot(a, b, preferred_element_type=F32)


def _dot_nt(a, b):
    return lax.dot_general(a, b, (((1,), (1,)), ((), ())), preferred_element_type=F32)


def _ssm_prep_kernel(*refs):
    for g in range(SSM_RANGE_GROUPS):
        _ssm_prep_group(*[ref.at[g] for ref in refs])


def _ssm_prep_group(lre_ref, lim_ref, ldt_ref, btre_ref, btim_ref, cre_ref, cim_ref,
                    kp_ref, bpre_ref, bpim_ref, are_ref, aim_ref, lnre_ref, lnim_ref):
    L = SSM_CHUNK
    lr = lre_ref[...]
    li = lim_ref[...]
    dt = jnp.exp(ldt_ref[...])
    er = jnp.exp(lr * dt)
    lbr = er * jnp.cos(li * dt)
    lbi = er * jnp.sin(li * dt)
    nr = lbr - 1.0
    dd = lr * lr + li * li
    rr = (nr * lr + lbi * li) / dd
    ri = (lbi * lr - nr * li) / dd
    btr = btre_ref[...]
    bti = btim_ref[...]
    bbr = rr * btr - ri * bti
    bbi = rr * bti + ri * btr
    cr = cre_ref[...]
    ci = cim_ref[...]
    pr = [jnp.ones_like(lbr)]
    pi = [jnp.zeros_like(lbr)]
    for _ in range(L):
        pr.append(pr[-1] * lbr - pi[-1] * lbi)
        pi.append(pr[-2] * lbi + pi[-1] * lbr)
    xr = [cr * pr[t] - ci * pi[t] for t in range(L + 1)]
    xi = [cr * pi[t] + ci * pr[t] for t in range(L + 1)]
    xr_k = jnp.concatenate(xr[:L], axis=0)
    xi_k = jnp.concatenate(xi[:L], axis=0)
    hp = lax.Precision.HIGHEST
    kp = (lax.dot_general(bbr, xr_k, (((1,), (1,)), ((), ())), precision=hp, preferred_element_type=F32)
          - lax.dot_general(bbi, xi_k, (((1,), (1,)), ((), ())), precision=hp, preferred_element_type=F32))
    lane = lax.broadcasted_iota(jnp.int32, kp.shape, 1)
    blocks = [kp] + [jnp.where(lane >= j * SSM_GROUP, pltpu.roll(kp, j * SSM_GROUP, axis=1), 0.0)
                     for j in range(1, L)]
    kp_ref[...] = jnp.concatenate(blocks, axis=0)
    are_ref[...] = jnp.concatenate(xr[1:], axis=0)
    aim_ref[...] = jnp.concatenate(xi[1:], axis=0)
    bpre_ref[...] = jnp.concatenate([pr[L - 1 - j] * bbr - pi[L - 1 - j] * bbi for j in range(L)], axis=0)
    bpim_ref[...] = jnp.concatenate([pr[L - 1 - j] * bbi + pi[L - 1 - j] * bbr for j in range(L)], axis=0)
    lnre_ref[...] = pr[L]
    lnim_ref[...] = pi[L]


def _ssm_prep(lam_re, lam_im, log_dt, b_re, b_im, c_re, c_im):
    G, P, C, L = N_SSM_GROUPS, SSM_STATE, SSM_GROUP, SSM_CHUNK
    row = lambda a: a.reshape(G, 1, -1)
    bt_re = jnp.swapaxes(b_re, 1, 2)
    bt_im = jnp.swapaxes(b_im, 1, 2)
    GL = SSM_RANGE_GROUPS
    vec = pl.BlockSpec((GL, 1, P), lambda r: (r, 0, 0))
    mat = pl.BlockSpec((GL, C, P), lambda r: (r, 0, 0))
    big = pl.BlockSpec((GL, L * C, P), lambda r: (r, 0, 0))
    kp, bpre, bpim, are, aim, lnre, lnim = pl.pallas_call(
        _ssm_prep_kernel,
        grid=(G // GL,),
        in_specs=[vec, vec, pl.BlockSpec((GL, 1, 1), lambda r: (r, 0, 0)), mat, mat, mat, mat],
        out_specs=[pl.BlockSpec((GL, L * C, L * C), lambda r: (r, 0, 0)), big, big, big, big, vec, vec],
        out_shape=[jax.ShapeDtypeStruct((G, L * C, L * C), F32)] + [jax.ShapeDtypeStruct((G, L * C, P), F32)] * 4
                  + [jax.ShapeDtypeStruct((G, 1, P), F32)] * 2,
        compiler_params=_cparams(("arbitrary",)),
        name="ssm_prep",
    )(row(lam_re), row(lam_im), log_dt.reshape(G, 1, 1), bt_re, bt_im, c_re, c_im)

    R, GL = N_SSM_RANGES, SSM_RANGE_GROUPS

    def by_range(a, n_outer):
        n_inner = a.shape[1] // n_outer
        a = a.reshape(R, GL, n_outer, n_inner, LANES)
        return jnp.transpose(a, (0, 2, 1, 3, 4)).reshape(R, n_outer * GL * n_inner, LANES).astype(BF16)

    ek = by_range(kp, L)
    eb = by_range(jnp.concatenate([bpre, bpim], axis=-1), L)
    a_t = jnp.swapaxes(jnp.concatenate([are, -aim], axis=-1), 1, 2)
    ea = by_range(a_t, 2)

    half = RANGE_STATE // LANES
    lnr_t = lnre.reshape(R, half, LANES)
    lni_t = lnim.reshape(R, half, LANES)
    lnr = jnp.concatenate([lnr_t, lnr_t], axis=1)
    lni = jnp.concatenate([-lni_t, lni_t], axis=1)
    return ek, eb, ea, lnr, lni


def _ssm_select_matrices():
    q = np.arange(SSM_CHUNK * LANES)
    r = np.arange(LANES)
    sel_k = (r[:, None] // SSM_GROUP == q[None, :] // LANES) & (r[:, None] % SSM_GROUP == q[None, :] % SSM_GROUP)
    sel_b = (r[:, None] // SSM_STATE == q[None, :] // RANGE_STATE) & (r[:, None] % SSM_STATE == q[None, :] % SSM_STATE)
    return jnp.asarray(sel_k, BF16), jnp.asarray(sel_b, BF16)


HEADS_PER_BLOCK = COL_BLOCK // HEAD_DIM
_SEG_U = 0
_SEG_Q = SSM_WIDTH // COL_BLOCK
_SEG_K = _SEG_Q + QKV_WIDTH // COL_BLOCK
_SEG_V = _SEG_K + QKV_WIDTH // COL_BLOCK
_SEG_GATE = _SEG_V + QKV_WIDTH // COL_BLOCK
_SEG_END = IN_WIDTH // COL_BLOCK


def _head_norm(res, gain):
    outs = []
    for h in range(HEADS_PER_BLOCK):
        t = res[:, h * HEAD_DIM:(h + 1) * HEAD_DIM]
        ms = jnp.mean(t * t, axis=-1, keepdims=True)
        outs.append(t * lax.rsqrt(ms + EPS) * gain)
    return outs


def _rmsnorm_rows(x, gain):
    ms = jnp.mean(x * x, axis=-1, keepdims=True)
    return (x * lax.rsqrt(ms + EPS) * gain).astype(BF16)


def _rmsnorm_kernel(x_ref, g_ref, o_ref):
    o_ref[...] = _rmsnorm_rows(x_ref[...], g_ref[...])


def _rmsnorm(x, gain, tm):
    M, D = x.shape
    return pl.pallas_call(
        _rmsnorm_kernel,
        grid=(M // tm,),
        in_specs=[pl.BlockSpec((tm, D), lambda i: (i, 0)), pl.BlockSpec((1, D), lambda i: (0, 0))],
        out_specs=pl.BlockSpec((tm, D), lambda i: (i, 0)),
        out_shape=jax.ShapeDtypeStruct((M, D), BF16),
        compiler_params=_cparams(("arbitrary",)),
        name="rmsnorm",
    )(x, gain.reshape(1, D))


def _in_proj_kernel(h_ref, hs_ref, w_ref, qn_ref, kn_ref,
                    u_ref, qkv_ref, gate_ref, us_ref, qkvs_ref, gates_s_ref, w_scr):
    i = pl.program_id(0)
    j = pl.program_id(1)
    tm = h_ref.shape[0]
    sub = min(tm, ROW_SUB)

    def row_blocks():
        w_scr[...] = w_ref[...].astype(BF16)
        for r in range(tm // sub):
            rows = pl.ds(r * sub, sub)
            yield rows, _dot(h_ref[rows, :], w_scr[...])

    def sample_res():
        return _dot(hs_ref[...], w_scr[...])

    @pl.when(j < _SEG_Q)
    def _():
        for rows, res in row_blocks():
            u_ref[rows, :] = res

        @pl.when(i == 0)
        def _():
            us_ref[j] = sample_res()

    def split_heads(res, gain_ref):
        if gain_ref is not None:
            return _head_norm(res, gain_ref[...])
        return [res[:, h * HEAD_DIM:(h + 1) * HEAD_DIM] for h in range(HEADS_PER_BLOCK)]

    def store_heads(gain_ref):
        for rows, res in row_blocks():
            for h, t in enumerate(split_heads(res, gain_ref)):
                qkv_ref[h, rows, :] = t

        @pl.when(i == 0)
        def _():
            base = (j - _SEG_Q) * HEADS_PER_BLOCK
            for h, t in enumerate(split_heads(sample_res(), gain_ref)):
                qkvs_ref[base + h] = t

    pl.when((j >= _SEG_Q) & (j < _SEG_K))(functools.partial(store_heads, qn_ref))
    pl.when((j >= _SEG_K) & (j < _SEG_V))(functools.partial(store_heads, kn_ref))
    pl.when((j >= _SEG_V) & (j < _SEG_GATE))(functools.partial(store_heads, None))

    @pl.when(j >= _SEG_GATE)
    def _():
        for rows, res in row_blocks():
            gate_ref[rows, :] = jax.nn.sigmoid(res).astype(gate_ref.dtype)

        @pl.when(i == 0)
        def _():
            gates_s_ref[j - _SEG_GATE] = jax.nn.sigmoid(sample_res()).astype(gates_s_ref.dtype)


def _in_proj(h, hs, w, layer, q_gain, k_gain, tm):
    M, Ms = h.shape[0], hs.shape[0]
    n_heads = 3 * N_ATT_HEADS
    n_gate = _SEG_END - _SEG_GATE
    whole = lambda shape: pl.BlockSpec(shape, lambda i, j: (0,) * len(shape))
    u, qkv, gates, us, qkvs, gates_s = pl.pallas_call(
        _in_proj_kernel,
        grid=(M // tm, _SEG_END),
        in_specs=[
            pl.BlockSpec((tm, D_MODEL), lambda i, j: (i, 0), pipeline_mode=pl.Buffered(1)),
            whole((Ms, D_MODEL)),
            pl.BlockSpec((None, D_MODEL, COL_BLOCK), lambda i, j: (layer, 0, j)),
            pl.BlockSpec((1, HEAD_DIM), lambda i, j: (0, 0)),
            pl.BlockSpec((1, HEAD_DIM), lambda i, j: (0, 0)),
        ],
        out_specs=[
            pl.BlockSpec((tm, COL_BLOCK), lambda i, j: (i, jnp.clip(j, 0, _SEG_Q - 1))),
            pl.BlockSpec((HEADS_PER_BLOCK, tm, HEAD_DIM),
                         lambda i, j: (jnp.clip(j - _SEG_Q, 0, _SEG_GATE - _SEG_Q - 1), i, 0)),
            pl.BlockSpec((tm, COL_BLOCK), lambda i, j: (i, jnp.clip(j - _SEG_GATE, 0, _SEG_END - _SEG_GATE - 1))),
            whole((_SEG_Q, Ms, COL_BLOCK)),
            whole((n_heads, Ms, HEAD_DIM)),
            whole((n_gate, Ms, COL_BLOCK)),
        ],
        out_shape=[
            jax.ShapeDtypeStruct((M, SSM_WIDTH), F32),
            jax.ShapeDtypeStruct((n_heads, M, HEAD_DIM), F32),
            jax.ShapeDtypeStruct((M, 2 * D_MODEL), BF16),
            jax.ShapeDtypeStruct((_SEG_Q, Ms, COL_BLOCK), F32),
            jax.ShapeDtypeStruct((n_heads, Ms, HEAD_DIM), F32),
            jax.ShapeDtypeStruct((n_gate, Ms, COL_BLOCK), BF16),
        ],
        scratch_shapes=[pltpu.VMEM((D_MODEL, COL_BLOCK), BF16)],
        compiler_params=_cparams(("arbitrary", "arbitrary")),
        name="in_proj",
    )(h, hs, w, q_gain.reshape(1, HEAD_DIM), k_gain.reshape(1, HEAD_DIM))
    us = jnp.swapaxes(us, 0, 1).reshape(Ms, SSM_WIDTH)
    gates_s = jnp.swapaxes(gates_s, 0, 1).reshape(Ms, 2 * D_MODEL)
    return (u, qkv, gates), (us, qkvs, gates_s)


def _expand_block_diag(e_ref, sel_ref, row_shift, col_shift, out_scr):
    n = out_scr.shape[0]
    for c in range(n // LANES):
        rows = pl.ds(c * LANES, LANES)
        full = _dot(e_ref[rows, :], sel_ref[...])
        row = lax.broadcasted_iota(jnp.int32, full.shape, 0) + c * LANES
        col = lax.broadcasted_iota(jnp.int32, full.shape, 1)
        keep = ((row >> row_shift) & (SSM_RANGE_GROUPS - 1)) == ((col >> col_shift) & (SSM_RANGE_GROUPS - 1))
        out_scr[rows, :] = jnp.where(keep, full, 0.0).astype(BF16)


def _ssm_chunk_kernel(nseq, nk, u_ref, h0_ref, ek_ref, eb_ref, ea_ref, selk_ref, selb_ref, lnr_ref, lni_ref,
                      y_ref, hfin_ref, s_scr, ktoe_ref, bcat_ref, acat_ref):
    L = SSM_CHUNK
    rows = nseq * nk
    nsub = 2 * RANGE_STATE // LANES
    lg_c = SSM_GROUP.bit_length() - 1
    lg_p = SSM_STATE.bit_length() - 1

    @pl.when(pl.program_id(1) == 0)
    def _():
        _expand_block_diag(ek_ref, selk_ref, lg_c, lg_c, ktoe_ref)
        _expand_block_diag(eb_ref, selb_ref, lg_c, lg_p, bcat_ref)
        _expand_block_diag(ea_ref, selk_ref, lg_p, lg_c, acat_ref)

    ucat = jnp.concatenate([u_ref[pl.ds(j, rows, stride=L), :] for j in range(L)], axis=-1).astype(BF16)
    s = _dot(ucat, bcat_ref[...])
    for n in range(nsub):
        s_scr[pl.ds(n, rows, stride=nsub), :] = s[:, n * LANES:(n + 1) * LANES]
    y_ref_intra = _dot(ucat, ktoe_ref[...])

    lnr = lnr_ref[...]
    lni = lni_ref[...]

    def step(k, hs):
        new = []
        for q in range(nseq):
            off = pl.multiple_of((q * nk + k) * nsub, nsub)
            h = hs[q]
            sk = s_scr[pl.ds(off, nsub), :]
            s_scr[pl.ds(off, nsub), :] = h
            new.append(h * lnr + pltpu.roll(h, nsub // 2, axis=0) * lni + sk)
        return tuple(new)

    hs = lax.fori_loop(0, nk, step, tuple(h0_ref[q] for q in range(nseq)))
    for q in range(nseq):
        hfin_ref[q] = hs[q]

    hprev = jnp.concatenate([s_scr[pl.ds(n, rows, stride=nsub), :] for n in range(nsub)], axis=-1).astype(BF16)
    y = y_ref_intra + _dot(hprev, acat_ref[...])
    for j in range(L):
        y_ref[pl.ds(j, rows, stride=L), :] = y[:, j * LANES:(j + 1) * LANES]


def _ssm_chunk(u, h0, ek, eb, ea, sel_k, sel_b, lnr, lni, nseq_total, seq_per_step):
    M = u.shape[0]
    T = M // nseq_total
    nk = T // SSM_CHUNK
    nsteps = nseq_total // seq_per_step
    R = N_SSM_RANGES
    nsub = 2 * RANGE_STATE // LANES
    tm = seq_per_step * T
    wide = SSM_CHUNK * LANES
    assert wide == 2 * RANGE_STATE
    wspec = lambda shape: pl.BlockSpec((None,) + shape, lambda r, b: (r, 0, 0))
    sel_spec = pl.BlockSpec((LANES, wide), lambda r, b: (0, 0))
    return pl.pallas_call(
        functools.partial(_ssm_chunk_kernel, seq_per_step, nk),
        grid=(R, nsteps),
        in_specs=[
            pl.BlockSpec((tm, LANES), lambda r, b: (b, r)),
            pl.BlockSpec((seq_per_step, None, nsub, LANES), lambda r, b: (b, r, 0, 0)),
            wspec((wide, LANES)),
            wspec((wide, LANES)),
            wspec((wide, LANES)),
            sel_spec,
            sel_spec,
            wspec((nsub, LANES)),
            wspec((nsub, LANES)),
        ],
        out_specs=[
            pl.BlockSpec((tm, LANES), lambda r, b: (b, r)),
            pl.BlockSpec((seq_per_step, None, nsub, LANES), lambda r, b: (b, r, 0, 0)),
        ],
        out_shape=[
            jax.ShapeDtypeStruct((M, SSM_WIDTH), F32),
            jax.ShapeDtypeStruct((nseq_total, R, nsub, LANES), F32),
        ],
        scratch_shapes=[pltpu.VMEM((seq_per_step * nk * nsub, LANES), F32)] + [pltpu.VMEM((wide, wide), BF16)] * 3,
        compiler_params=_cparams(("arbitrary", "arbitrary")),
        name="ssm_chunk",
    )(u, h0, ek, eb, ea, sel_k, sel_b, lnr, lni)


def _state_to_tiles(re, im):
    N = re.shape[0]
    half = RANGE_STATE // LANES
    return jnp.concatenate([re.reshape(N, N_SSM_RANGES, half, LANES),
                            im.reshape(N, N_SSM_RANGES, half, LANES)], axis=2)


def _tiles_to_state(t):
    N = t.shape[0]
    half = RANGE_STATE // LANES
    return (t[:, :, :half].reshape(N, N_SSM_GROUPS, SSM_STATE),
            t[:, :, half:].reshape(N, N_SSM_GROUPS, SSM_STATE))


def _bucket_np(dist):
    max_exact = N_BUCKETS // 2
    n = np.maximum(dist, 0)
    nf = np.maximum(n, 1).astype(np.float64)
    large = max_exact + (np.log(nf / max_exact) / math.log(REL_MAX_DISTANCE / max_exact)
                         * (N_BUCKETS - max_exact)).astype(np.int32)
    large = np.minimum(large, N_BUCKETS - 1)
    return np.where(n < max_exact, n, large)


def _prompt_bucket_tiles():
    a = np.arange(Q_BLOCK)[:, None]
    c = np.arange(2 * Q_BLOCK)[None, :]
    rel = a - c + Q_BLOCK
    tiles = []
    for window, dil in WINDOWS:
        K = window // dil + 1
        valid = (rel >= 0) & (rel < K)
        tiles.append(np.where(valid, _bucket_np(np.clip(rel, 0, K - 1) * dil), -1))
    return jnp.asarray(np.stack(tiles), jnp.int32)


def _attn_group_of_step(step):
    return (step + ATTN_FIRST_GROUP) % N_ATT_GROUPS


def _attn_prompt_kernel(T, tab_ref, q_ref, k_ref, v_ref, bkt_ref, o_ref,
                        m_scr, l_scr, acc_scr, s_scr, p_scr, mb_scr, bias_scr):
    h = pl.program_id(1)
    step = pl.program_id(2)
    g = _attn_group_of_step(step)
    scale = HEAD_DIM ** -0.5
    nblk = T // Q_BLOCK

    bkt = bkt_ref[...]
    col = g * HEADS_PER_GROUP + h
    bias = jnp.full(bkt.shape, NEG_INF, F32)
    for t in range(N_BUCKETS):
        bias = jnp.where(bkt == t, tab_ref[t, col], bias)
    bias_scr[...] = bias

    def run_group(first, dil):
        per_class = nblk // dil
        run = min(per_class, ATTN_UNROLL)
        runs_per_iter = ATTN_UNROLL // run
        whole_class = run == per_class

        def block_rows(r, n):
            return pl.ds(r + n * (Q_BLOCK * dil), Q_BLOCK, stride=dil)

        def runs(it):
            for j in range(runs_per_iter):
                i0 = it * ATTN_UNROLL + j * run
                yield i0, i0 // per_class, (0 if whole_class else i0 % per_class)

        def tiles(ref, r, n0, augment):
            out = []
            for u in range(-1, run):
                if u < 0 and whole_class:
                    out.append(None)
                    continue
                n = jnp.maximum(n0 + u, 0) if u < 0 else n0 + u
                t = ref[block_rows(r, n), :].astype(BF16)
                if augment:
                    t = jnp.concatenate([t, jnp.ones((Q_BLOCK, HEAD_DIM), BF16)], axis=1)
                out.append(t)
            return out

        def scores(it, carry):
            for i0, r, n0 in runs(it):
                kt = tiles(k_ref, r, n0, False)
                for u in range(run):
                    q = q_ref[block_rows(r, n0 + u), :].astype(BF16)
                    s_r = _dot_nt(q, kt[u + 1]) * scale + bias_scr[:, Q_BLOCK:]
                    if kt[u] is None:
                        s_l = jnp.full((Q_BLOCK, Q_BLOCK), NEG_INF, F32)
                    else:
                        bias_l = bias_scr[:, :Q_BLOCK]
                        if u == 0:
                            bias_l = jnp.where(n0 == 0, NEG_INF, bias_l)
                        s_l = _dot_nt(q, kt[u]) * scale + bias_l
                    s_scr[i0 + u, :, :Q_BLOCK] = s_l
                    s_scr[i0 + u, :, Q_BLOCK:] = s_r
                    m = jnp.maximum(jnp.max(s_l, axis=-1, keepdims=True), jnp.max(s_r, axis=-1, keepdims=True))
                    mb_scr[i0 + u] = jnp.broadcast_to(m, (Q_BLOCK, HEAD_DIM))
            return carry

        def probs(i, carry):
            mb = mb_scr[i]
            p_scr[i, :, :Q_BLOCK] = jnp.exp(s_scr[i, :, :Q_BLOCK] - mb).astype(BF16)
            p_scr[i, :, Q_BLOCK:] = jnp.exp(s_scr[i, :, Q_BLOCK:] - mb).astype(BF16)
            return carry

        def values(it, carry):
            for i0, r, n0 in runs(it):
                vt = tiles(v_ref, r, n0, True)
                for u in range(run):
                    i = i0 + u
                    ol = _dot(p_scr[i, :, Q_BLOCK:], vt[u + 1])
                    if vt[u] is not None:
                        ol = ol + _dot(p_scr[i, :, :Q_BLOCK], vt[u])
                    o = ol[:, :HEAD_DIM]
                    lb = ol[:, HEAD_DIM:]
                    mb = mb_scr[i]
                    sl_q = block_rows(r, n0 + u)
                    if first:
                        m_scr[sl_q, :] = mb
                        l_scr[sl_q, :] = lb
                        acc_scr[sl_q, :] = o
                    else:
                        m0 = m_scr[sl_q, :]
                        mn = jnp.maximum(m0, mb)
                        a0 = jnp.exp(m0 - mn)
                        a1 = jnp.exp(mb - mn)
                        m_scr[sl_q, :] = mn
                        l_scr[sl_q, :] = a0 * l_scr[sl_q, :] + a1 * lb
                        acc_scr[sl_q, :] = a0 * acc_scr[sl_q, :] + a1 * o
            return carry

        lax.fori_loop(0, nblk // ATTN_UNROLL, scores, 0)
        lax.fori_loop(0, nblk, probs, 0, unroll=ATTN_UNROLL)
        lax.fori_loop(0, nblk // ATTN_UNROLL, values, 0)

    for s in range(N_ATT_GROUPS):
        dil = WINDOWS[(s + ATTN_FIRST_GROUP) % N_ATT_GROUPS][1]
        pl.when(step == s)(functools.partial(run_group, s == 0, dil))

    @pl.when(step == N_ATT_GROUPS - 1)
    def _():
        o_ref[...] = (acc_scr[...] / l_scr[...]).astype(o_ref.dtype)


def _attn_prompt(qkv, rel_bias, bucket_tiles, nbatch):
    M = qkv.shape[1]
    T = M // nbatch
    H, G = HEADS_PER_GROUP, N_ATT_GROUPS
    nblk = T // Q_BLOCK

    def qkv_spec(which):
        return pl.BlockSpec((None, T, HEAD_DIM),
                            lambda b, h, s: (which * N_ATT_HEADS + _attn_group_of_step(s) * H + h, b, 0))

    return pl.pallas_call(
        functools.partial(_attn_prompt_kernel, T),
        grid=(nbatch, H, G),
        in_specs=[pl.BlockSpec(memory_space=pltpu.SMEM),
                  qkv_spec(0), qkv_spec(1), qkv_spec(2),
                  pl.BlockSpec((None, Q_BLOCK, 2 * Q_BLOCK), lambda b, h, s: (_attn_group_of_step(s), 0, 0))],
        out_specs=pl.BlockSpec((T, HEAD_DIM), lambda b, h, s: (b, h)),
        out_shape=jax.ShapeDtypeStruct((M, ATT_WIDTH), BF16),
        scratch_shapes=[pltpu.VMEM((T, HEAD_DIM), F32)] * 3
                       + [pltpu.VMEM((nblk, Q_BLOCK, 2 * Q_BLOCK), F32),
                          pltpu.VMEM((nblk, Q_BLOCK, 2 * Q_BLOCK), BF16),
                          pltpu.VMEM((nblk, Q_BLOCK, HEAD_DIM), F32),
                          pltpu.VMEM((Q_BLOCK, 2 * Q_BLOCK), F32)],
        compiler_params=_cparams(("arbitrary", "arbitrary", "arbitrary")),
        name="attn_prompt",
    )(rel_bias.astype(F32), qkv, qkv, qkv, bucket_tiles)


def _attn_sample_kernel(window, dil, q_ref, ck_ref, cv_ref, nk_ref, nv_ref, bias_ref, o_ref, lse_ref):
    S = q_ref.shape[0]
    K = window // dil + 1
    scale = HEAD_DIM ** -0.5
    bias = bias_ref[...]
    for s in range(S):
        n_c = (window - 1 - s) // dil + 1
        qs = q_ref[s]
        new_rows = [s + j * dil - window for j in range(n_c, K)]
        kk = jnp.concatenate([ck_ref[pl.ds(s, n_c, stride=dil)]] + [nk_ref[pl.ds(i, 1)] for i in new_rows], axis=0)
        vv = jnp.concatenate([cv_ref[pl.ds(s, n_c, stride=dil)]] + [nv_ref[pl.ds(i, 1)] for i in new_rows], axis=0)
        lg = jnp.sum(kk * qs[None], axis=-1, keepdims=True) * scale + bias
        m = jnp.max(lg, axis=0)
        p = jnp.exp(lg - m[None])
        l = jnp.sum(p, axis=0)
        o_ref[s] = jnp.sum(p * vv, axis=0) / l
        lse_ref[s] = m + jnp.log(l)


def _attn_sample(q, cache_k, cache_v, new_k, new_v, bias, layer, window, dil):
    B, S = q.shape[0], q.shape[1]
    H = HEADS_PER_GROUP
    K = window // dil + 1
    small = pl.BlockSpec((None, S, H, HEAD_DIM), lambda b: (b, 0, 0, 0))
    cache = pl.BlockSpec((None, None, window, H, HEAD_DIM), lambda b: (layer, b, 0, 0, 0))
    return pl.pallas_call(
        functools.partial(_attn_sample_kernel, window, dil),
        grid=(B,),
        in_specs=[small, cache, cache, small, small, pl.BlockSpec((K, H, HEAD_DIM), lambda b: (0, 0, 0))],
        out_specs=[small, small],
        out_shape=[jax.ShapeDtypeStruct((B, S, H, HEAD_DIM), F32)] * 2,
        compiler_params=_cparams(("arbitrary",)),
        name="attn_sample_w%d" % window,
    )(q, cache_k, cache_v, new_k, new_v, bias)


def _merge_groups_kernel(o0, o1, o2, l0, l1, l2, y_ref):
    a, b, c = l0[...], l1[...], l2[...]
    m = jnp.maximum(jnp.maximum(a, b), c)
    ea, eb, ec = jnp.exp(a - m), jnp.exp(b - m), jnp.exp(c - m)
    y_ref[...] = ((ea * o0[...] + eb * o1[...] + ec * o2[...]) / (ea + eb + ec)).astype(y_ref.dtype)


def _merge_groups(outs, lses):
    shape = outs[0].shape
    return pl.pallas_call(
        _merge_groups_kernel,
        out_shape=jax.ShapeDtypeStruct(shape, BF16),
        name="merge_groups",
    )(*outs, *lses)


def _sample_bias(rel_bias):
    out = []
    for g, (window, dil) in enumerate(WINDOWS):
        K = window // dil + 1
        steps = (K - 1) - np.arange(K)
        tab = rel_bias[:, g * HEADS_PER_GROUP:(g + 1) * HEADS_PER_GROUP].astype(F32)
        b = tab[_bucket_np(steps * dil)]
        out.append(jnp.broadcast_to(b[:, :, None], (K, HEADS_PER_GROUP, HEAD_DIM)))
    return out


def _gelu_tanh(x):
    return 0.5 * x * (1.0 + jnp.tanh(math.sqrt(2.0 / math.pi) * (x + 0.044715 * (x * x * x))))


def _mix_out_kernel(y_ref, u_ref, yb_ref, ga_ref, gb_ref, x_ref, d_ref, wglu_ref, bglu_ref, wa_ref, wb_ref,
                    wout_ref, gn_ref, x1_ref, h2_ref):
    y = y_ref[...] + d_ref[...] * u_ref[...]
    z = _gelu_tanh(y)
    ya = z * jax.nn.sigmoid(_dot(z.astype(BF16), wglu_ref[...]) + bglu_ref[...])
    mix = (ga_ref[...] * _dot(ya.astype(BF16), wa_ref[...])
           + gb_ref[...] * _dot(yb_ref[...], wb_ref[...]))
    x1 = x_ref[...] + _dot(mix.astype(BF16), wout_ref[...])
    x1_ref[...] = x1
    ms = jnp.mean(x1 * x1, axis=-1, keepdims=True)
    h2_ref[...] = (x1 * lax.rsqrt(ms + EPS) * gn_ref[...]).astype(BF16)


def _mix_out(y_ssm, u, yb, gates, x, ssm_d, wglu, bglu, wa, wb, wout, layer, gain, tm):
    M = x.shape[0]
    row = lambda w: pl.BlockSpec((tm, w), lambda i: (i, 0))
    vec = lambda b: pl.BlockSpec((1, b), lambda i: (0, 0), pipeline_mode=pl.Buffered(1))
    full = lambda a, b: pl.BlockSpec((None, a, b), lambda i: (layer, 0, 0), pipeline_mode=pl.Buffered(1))
    return pl.pallas_call(
        _mix_out_kernel,
        grid=(M // tm,),
        in_specs=[row(SSM_WIDTH), row(SSM_WIDTH), row(ATT_WIDTH),
                  pl.BlockSpec((tm, D_MODEL), lambda i: (i, 0)),
                  pl.BlockSpec((tm, D_MODEL), lambda i: (i, 1)),
                  row(D_MODEL),
                  vec(SSM_WIDTH), full(SSM_WIDTH, SSM_WIDTH), vec(SSM_WIDTH),
                  full(SSM_WIDTH, D_MODEL), full(ATT_WIDTH, D_MODEL), full(D_MODEL, D_MODEL), vec(D_MODEL)],
        out_specs=[row(D_MODEL), row(D_MODEL)],
        out_shape=[jax.ShapeDtypeStruct((M, D_MODEL), F32), jax.ShapeDtypeStruct((M, D_MODEL), BF16)],
        compiler_params=_cparams(("arbitrary",)),
        name="mix_out",
    )(y_ssm, u, yb, gates, gates, x, ssm_d.reshape(1, -1), wglu, bglu.reshape(1, -1), wa, wb, wout,
      gain.reshape(1, -1))


def _ffn_kernel(emit_norm, h_ref, x_ref, hs_ref, xs_ref, wg_ref, wu_ref, wd_ref, gn_ref, *rest):
    if emit_norm:
        o_ref, os_ref, hn_ref, hns_ref, wg_scr, wu_scr, wd_scr = rest
    else:
        o_ref, os_ref, wg_scr, wu_scr, wd_scr = rest
    i = pl.program_id(0)
    f = pl.program_id(1)
    last = pl.num_programs(1) - 1

    @pl.when(f == 0)
    def _():
        o_ref[...] = x_ref[...]

    @pl.when((f == 0) & (i == 0))
    def _():
        os_ref[...] = xs_ref[...]

    def swiglu(h):
        a = jax.nn.silu(_dot(h, wg_scr[...])) * _dot(h, wu_scr[...])
        return _dot(a.astype(BF16), wd_scr[...])

    tm = h_ref.shape[0]
    sub = min(tm, FFN_ROW_SUB)
    wg_scr[...] = wg_ref[...].astype(BF16)
    wu_scr[...] = wu_ref[...].astype(BF16)
    wd_scr[...] = wd_ref[...].astype(BF16)
    for r in range(tm // sub):
        rows = pl.ds(r * sub, sub)
        o_ref[rows, :] += swiglu(h_ref[rows, :])

    @pl.when(i == 0)
    def _():
        os_ref[...] += swiglu(hs_ref[...])

    if emit_norm:
        @pl.when(f == last)
        def _():
            for r in range(tm // sub):
                rows = pl.ds(r * sub, sub)
                hn_ref[rows, :] = _rmsnorm_rows(o_ref[rows, :], gn_ref[...])

        @pl.when((f == last) & (i == 0))
        def _():
            hns_ref[...] = _rmsnorm_rows(os_ref[...], gn_ref[...])


def _ffn(h2, x1, h2s, x1s, wg, wu, wd, layer, next_gain, tm):
    M, Ms = x1.shape[0], x1s.shape[0]
    nf = FFN_HIDDEN // FFN_BLOCK
    emit_norm = next_gain is not None
    gain = next_gain if emit_norm else jnp.ones((D_MODEL,), F32)
    row = pl.BlockSpec((tm, D_MODEL), lambda i, f: (i, 0))
    whole = pl.BlockSpec((Ms, D_MODEL), lambda i, f: (0, 0))
    out_specs = [row, whole]
    out_shape = [jax.ShapeDtypeStruct((M, D_MODEL), F32), jax.ShapeDtypeStruct((Ms, D_MODEL), F32)]
    if emit_norm:
        out_specs += [row, whole]
        out_shape += [jax.ShapeDtypeStruct((M, D_MODEL), BF16), jax.ShapeDtypeStruct((Ms, D_MODEL), BF16)]
    return pl.pallas_call(
        functools.partial(_ffn_kernel, emit_norm),
        grid=(M // tm, nf),
        in_specs=[row,
                  pl.BlockSpec((tm, D_MODEL), lambda i, f: (i, 0), pipeline_mode=pl.Buffered(1)),
                  whole, whole,
                  pl.BlockSpec((None, D_MODEL, FFN_BLOCK), lambda i, f: (layer, 0, f)),
                  pl.BlockSpec((None, D_MODEL, FFN_BLOCK), lambda i, f: (layer, 0, f)),
                  pl.BlockSpec((None, FFN_BLOCK, D_MODEL), lambda i, f: (layer, f, 0)),
                  pl.BlockSpec((1, D_MODEL), lambda i, f: (0, 0))],
        out_specs=out_specs,
        out_shape=out_shape,
        scratch_shapes=[pltpu.VMEM((D_MODEL, FFN_BLOCK), BF16), pltpu.VMEM((D_MODEL, FFN_BLOCK), BF16),
                        pltpu.VMEM((FFN_BLOCK, D_MODEL), BF16)],
        compiler_params=_cparams(("arbitrary", "arbitrary")),
        name="ffn",
    )(h2, x1, h2s, x1s, wg, wu, wd, gain.reshape(1, D_MODEL))


def _cache_shift_kernel(ck_ref, cv_ref, hk_ref, hv_ref, nk_ref, nv_ref, ok_ref, ov_ref):
    c = pl.program_id(2)
    last = pl.num_programs(2) - 1
    R = ck_ref.shape[0]
    S = nk_ref.shape[0]
    for cache, halo, new, out in ((ck_ref, hk_ref, nk_ref, ok_ref), (cv_ref, hv_ref, nv_ref, ov_ref)):
        out[pl.ds(0, R - S)] = cache[pl.ds(S, R - S)]

        @pl.when(c == last)
        def _():
            out[pl.ds(R - S, S)] = new[...]

        @pl.when(c < last)
        def _():
            out[pl.ds(R - S, S)] = halo[...]


def _cache_shift(cache_k, cache_v, new_k, new_v):
    depth, B, W, H, E = cache_k.shape
    S = new_k.shape[2]
    R = min(W, CACHE_ROWS)
    nchunks = W // R
    blk = pl.BlockSpec((None, None, R, H, E), lambda l, b, c: (l, b, c, 0, 0))
    halo = pl.BlockSpec((None, None, None, S, H, E),
                        lambda l, b, c: (l, b, jnp.minimum((c + 1) * (R // S), W // S - 1), 0, 0, 0))
    new = pl.BlockSpec((None, None, S, H, E), lambda l, b, c: (l, b, 0, 0, 0))
    as_rows = lambda a: a.reshape(depth, B, W // S, S, H, E)
    return pl.pallas_call(
        _cache_shift_kernel,
        grid=(depth, B, nchunks),
        in_specs=[blk, blk, halo, halo, new, new],
        out_specs=[blk, blk],
        out_shape=[jax.ShapeDtypeStruct(cache_k.shape, cache_k.dtype)] * 2,
        compiler_params=_cparams(("arbitrary", "arbitrary", "arbitrary")),
        name="cache_shift_w%d" % W,
    )(cache_k, cache_v, as_rows(cache_k), as_rows(cache_v), new_k, new_v)


def _kv_tails_kernel(*refs):
    depth = (len(refs) - 2) // 2
    ok_ref, ov_ref = refs[-2:]
    H, R = refs[0].shape[0], refs[0].shape[1]
    for l in range(depth):
        @pl.when(pl.program_id(0) == l)
        def _():
            for src, out in ((refs[2 * l], ok_ref), (refs[2 * l + 1], ov_ref)):
                for h in range(H):
                    out[pl.ds(h, R, stride=H), :] = src[h]


def _kv_tails(qkv_layers, g, window, nbatch):
    depth = len(qkv_layers)
    M = qkv_layers[0].shape[1]
    T = M // nbatch
    H, E = HEADS_PER_GROUP, HEAD_DIM
    R = min(window, CACHE_ROWS)
    nchunks = window // R
    first = (T - window) // R

    def src(layer, which):
        def index(l, b, c):
            bb = jnp.where(l < layer, 0, jnp.where(l > layer, nbatch - 1, b))
            cc = jnp.where(l < layer, 0, jnp.where(l > layer, nchunks - 1, c))
            return (which * N_ATT_GROUPS + g, bb * (T // R) + first + cc, 0)
        return pl.BlockSpec((H, R, E), index)

    args, in_specs = [], []
    for layer, qkv in enumerate(qkv_layers):
        args += [qkv, qkv]
        in_specs += [src(layer, 1), src(layer, 2)]
    out_spec = pl.BlockSpec((None, R * H, E), lambda l, b, c: (l, b * nchunks + c, 0))
    return pl.pallas_call(
        _kv_tails_kernel,
        grid=(depth, nbatch, nchunks),
        in_specs=in_specs,
        out_specs=[out_spec, out_spec],
        out_shape=[jax.ShapeDtypeStruct((depth, nbatch * window * H, E), qkv_layers[0].dtype)] * 2,
        compiler_params=_cparams(("arbitrary", "arbitrary", "arbitrary")),
        name="kv_tails_w%d" % window,
    )(*args)


def _ssm(u, h0_tiles, nseq, prm):
    return _ssm_chunk(u, h0_tiles, prm["ek"], prm["eb"], prm["ea"], prm["sel_k"], prm["sel_b"],
                      prm["lnr"], prm["lni"], nseq, nseq)


def _mix(x2d, u, y_ssm, yb, gates, prm, tm):
    return _mix_out(y_ssm, u, yb, gates, x2d, prm["ssm_d"], prm["w_glu"], prm["b_glu"], prm["w_branch_a"],
                    prm["w_branch_b"], prm["w_out"], prm["layer"], prm["norm_ffn"], tm)


def kernel(x_prompt, x_sample, state_ssm_re, state_ssm_im, cache_k_w128, cache_v_w128, cache_k_w512, cache_v_w512,
           cache_k_w2048, cache_v_w2048, rel_bias, norm_mix, norm_ffn, q_norm, k_norm, w_in, ssm_lambda_re,
           ssm_lambda_im, ssm_log_dt, ssm_b_re, ssm_b_im, ssm_c_re, ssm_c_im, ssm_d, w_glu, b_glu, w_branch_a,
           w_branch_b, w_out, w_ffn_gate, w_ffn_up, w_ffn_down):
    depth = w_in.shape[0]
    B, T, _ = x_prompt.shape
    SB, S, _ = x_sample.shape
    H = HEADS_PER_GROUP
    caches_k = (cache_k_w128, cache_k_w512, cache_k_w2048)
    caches_v = (cache_v_w128, cache_v_w512, cache_v_w2048)

    bucket_tiles = _prompt_bucket_tiles()
    bias_sample = _sample_bias(rel_bias)
    sel_k, sel_b = _ssm_select_matrices()

    xp = x_prompt.reshape(B * T, D_MODEL)
    xs = x_sample.reshape(SB * S, D_MODEL)
    p_re, p_im, s_re, s_im = [], [], [], []
    qkv_prompt = []
    new_k = [[] for _ in WINDOWS]
    new_v = [[] for _ in WINDOWS]

    weights = {
        "w_glu": w_glu.astype(BF16), "w_branch_a": w_branch_a.astype(BF16),
        "w_branch_b": w_branch_b.astype(BF16), "w_out": w_out.astype(BF16),
    }

    hp = _rmsnorm(xp, norm_mix[0], FFN_ROWS)
    hs = _rmsnorm(xs, norm_mix[0], SB * S)

    for l in range(depth):
        ek, eb, ea, lnr, lni = _ssm_prep(ssm_lambda_re[l], ssm_lambda_im[l], ssm_log_dt[l], ssm_b_re[l],
                                         ssm_b_im[l], ssm_c_re[l], ssm_c_im[l])
        prm = dict(weights)
        prm.update({
            "layer": l, "norm_ffn": norm_ffn[l], "ssm_d": ssm_d[l], "b_glu": b_glu[l],
            "ek": ek, "eb": eb, "ea": ea, "sel_k": sel_k, "sel_b": sel_b, "lnr": lnr, "lni": lni,
        })

        (u, qkv, gates), (us, qkvs, gates_s) = _in_proj(hp, hs, w_in, l, q_norm[l], k_norm[l], IN_PROJ_ROWS)

        zeros = jnp.zeros((B, N_SSM_RANGES, 2 * RANGE_STATE // LANES, LANES), F32)
        y_ssm, hfin = _ssm(u, zeros, B, prm)
        yb = _attn_prompt(qkv, rel_bias, bucket_tiles, B)
        x1, h2 = _mix(xp, u, y_ssm, yb, gates, prm, MIX_ROWS)
        hr, hi = _tiles_to_state(hfin)
        p_re.append(hr)
        p_im.append(hi)
        qkv_prompt.append(qkv)

        h0 = _state_to_tiles(state_ssm_re[l], state_ssm_im[l])
        ys_ssm, hfin = _ssm(us, h0, SB, prm)
        tok = jnp.transpose(qkvs.reshape(3, N_ATT_GROUPS, H, SB, S, HEAD_DIM), (0, 1, 3, 4, 2, 5))
        outs, lses = [], []
        for g, (window, dil) in enumerate(WINDOWS):
            nk_g, nv_g = tok[1, g], tok[2, g]
            new_k[g].append(nk_g)
            new_v[g].append(nv_g)
            o_g, lse_g = _attn_sample(tok[0, g], caches_k[g], caches_v[g], nk_g, nv_g, bias_sample[g], l,
                                      window, dil)
            outs.append(o_g)
            lses.append(lse_g)
        ybs = _merge_groups(outs, lses).reshape(SB * S, ATT_WIDTH)
        x1s, h2s = _mix(xs, us, ys_ssm, ybs, gates_s, prm, SB * S)
        hr, hi = _tiles_to_state(hfin)
        s_re.append(hr)
        s_im.append(hi)

        next_gain = norm_mix[l + 1] if l + 1 < depth else None
        res = _ffn(h2, x1, h2s, x1s, w_ffn_gate, w_ffn_up, w_ffn_down, l, next_gain, FFN_ROWS)
        if next_gain is None:
            xp, xs = res
        else:
            xp, xs, hp, hs = res

    shifted = []
    for g in range(N_ATT_GROUPS):
        shifted += _cache_shift(caches_k[g], caches_v[g], jnp.stack(new_k[g]), jnp.stack(new_v[g]))
    tails = [t.reshape(depth, B, window, H, HEAD_DIM)
             for g, (window, _) in enumerate(WINDOWS) for t in _kv_tails(qkv_prompt, g, window, B)]

    return (xp.reshape(B, T, D_MODEL), xs.reshape(SB, S, D_MODEL),
            jnp.stack(p_re), jnp.stack(p_im), *tails,
            jnp.stack(s_re), jnp.stack(s_im), *shifted)
```

```python
import functools
import math

import numpy as np
import jax
import jax.numpy as jnp
from jax import lax
from jax.experimental import pallas as pl
from jax.experimental.pallas import tpu as pltpu

F32 = jnp.float32
BF16 = jnp.bfloat16

D_MODEL = 2048
HEAD_DIM = 128
HEADS_PER_GROUP = 8
WINDOWS = ((128, 1), (512, 4), (2048, 16))
N_ATT_GROUPS = len(WINDOWS)
N_ATT_HEADS = N_ATT_GROUPS * HEADS_PER_GROUP
ATT_WIDTH = HEADS_PER_GROUP * HEAD_DIM
QKV_WIDTH = N_ATT_HEADS * HEAD_DIM
Q_BLOCK = 128
SSM_WIDTH = D_MODEL // 2
SSM_GROUP = 16
N_SSM_GROUPS = SSM_WIDTH // SSM_GROUP
SSM_STATE = 64
IN_WIDTH = SSM_WIDTH + 3 * QKV_WIDTH + 2 * D_MODEL
FFN_HIDDEN = 5632
N_BUCKETS = 32
REL_MAX_DISTANCE = 2048
EPS = 1e-6
NEG_INF = -1e30

LANES = 128
SUBLANES = 8
VMEM_LIMIT_BYTES = 60 * 1024 * 1024

SSM_CHUNK = 8
SSM_RANGE_GROUPS = LANES // SSM_GROUP
N_SSM_RANGES = N_SSM_GROUPS // SSM_RANGE_GROUPS
RANGE_STATE = SSM_RANGE_GROUPS * SSM_STATE
COL_BLOCK = 512
FFN_BLOCK = 256
ATTN_UNROLL = 16
ATTN_FIRST_GROUP = 2
CACHE_ROWS = 1024
IN_PROJ_ROWS = 2048
FFN_ROWS = 1024
MIX_ROWS = 256
ROW_SUB = 256
FFN_ROW_SUB = 512
MIX_ROW_SUB = 256


def _cparams(sem):
    return pltpu.CompilerParams(dimension_semantics=sem, vmem_limit_bytes=VMEM_LIMIT_BYTES)


def _dot(a, b):
    return jnp.dot(a, b, preferred_element_type=F32)


def _dot_nt(a, b):
    return lax.dot_general(a, b, (((1,), (1,)), ((), ())), preferred_element_type=F32)


def _ssm_prep_kernel(*refs):
    for g in range(SSM_RANGE_GROUPS):
        _ssm_prep_group(*[ref.at[g] for ref in refs])


def _ssm_prep_group(lre_ref, lim_ref, ldt_ref, btre_ref, btim_ref, cre_ref, cim_ref,
                    kp_ref, bpre_ref, bpim_ref, are_ref, aim_ref, lnre_ref, lnim_ref):
    L = SSM_CHUNK
    lr = lre_ref[...]
    li = lim_ref[...]
    dt = jnp.exp(ldt_ref[...])
    er = jnp.exp(lr * dt)
    lbr = er * jnp.cos(li * dt)
    lbi = er * jnp.sin(li * dt)
    nr = lbr - 1.0
    dd = lr * lr + li * li
    rr = (nr * lr + lbi * li) / dd
    ri = (lbi * lr - nr * li) / dd
    btr = btre_ref[...]
    bti = btim_ref[...]
    bbr = rr * btr - ri * bti
    bbi = rr * bti + ri * btr
    cr = cre_ref[...]
    ci = cim_ref[...]
    pr = [jnp.ones_like(lbr)]
    pi = [jnp.zeros_like(lbr)]
    for _ in range(L):
        pr.append(pr[-1] * lbr - pi[-1] * lbi)
        pi.append(pr[-2] * lbi + pi[-1] * lbr)
    xr = [cr * pr[t] - ci * pi[t] for t in range(L + 1)]
    xi = [cr * pi[t] + ci * pr[t] for t in range(L + 1)]
    xr_k = jnp.concatenate(xr[:L], axis=0)
    xi_k = jnp.concatenate(xi[:L], axis=0)
    hp = lax.Precision.HIGHEST
    kp = (lax.dot_general(bbr, xr_k, (((1,), (1,)), ((), ())), precision=hp, preferred_element_type=F32)
          - lax.dot_general(bbi, xi_k, (((1,), (1,)), ((), ())), precision=hp, preferred_element_type=F32))
    lane = lax.broadcasted_iota(jnp.int32, kp.shape, 1)
    blocks = [kp] + [jnp.where(lane >= j * SSM_GROUP, pltpu.roll(kp, j * SSM_GROUP, axis=1), 0.0)
                     for j in range(1, L)]
    kp_ref[...] = jnp.concatenate(blocks, axis=0)
    are_ref[...] = jnp.concatenate(xr[1:], axis=0)
    aim_ref[...] = jnp.concatenate(xi[1:], axis=0)
    bpre_ref[...] = jnp.concatenate([pr[L - 1 - j] * bbr - pi[L - 1 - j] * bbi for j in range(L)], axis=0)
    bpim_ref[...] = jnp.concatenate([pr[L - 1 - j] * bbi + pi[L - 1 - j] * bbr for j in range(L)], axis=0)
    lnre_ref[...] = pr[L]
    lnim_ref[...] = pi[L]


def _ssm_prep(lam_re, lam_im, log_dt, b_re, b_im, c_re, c_im):
    G, P, C, L = N_SSM_GROUPS, SSM_STATE, SSM_GROUP, SSM_CHUNK
    row = lambda a: a.reshape(G, 1, -1)
    bt_re = jnp.swapaxes(b_re, 1, 2)
    bt_im = jnp.swapaxes(b_im, 1, 2)
    GL = SSM_RANGE_GROUPS
    vec = pl.BlockSpec((GL, 1, P), lambda r: (r, 0, 0))
    mat = pl.BlockSpec((GL, C, P), lambda r: (r, 0, 0))
    big = pl.BlockSpec((GL, L * C, P), lambda r: (r, 0, 0))
    kp, bpre, bpim, are, aim, lnre, lnim = pl.pallas_call(
        _ssm_prep_kernel,
        grid=(G // GL,),
        in_specs=[vec, vec, pl.BlockSpec((GL, 1, 1), lambda r: (r, 0, 0)), mat, mat, mat, mat],
        out_specs=[pl.BlockSpec((GL, L * C, L * C), lambda r: (r, 0, 0)), big, big, big, big, vec, vec],
        out_shape=[jax.ShapeDtypeStruct((G, L * C, L * C), F32)] + [jax.ShapeDtypeStruct((G, L * C, P), F32)] * 4
                  + [jax.ShapeDtypeStruct((G, 1, P), F32)] * 2,
        compiler_params=_cparams(("arbitrary",)),
        name="ssm_prep",
    )(row(lam_re), row(lam_im), log_dt.reshape(G, 1, 1), bt_re, bt_im, c_re, c_im)

    R, GL = N_SSM_RANGES, SSM_RANGE_GROUPS

    def by_range(a, n_outer):
        n_inner = a.shape[1] // n_outer
        a = a.reshape(R, GL, n_outer, n_inner, LANES)
        return jnp.transpose(a, (0, 2, 1, 3, 4)).reshape(R, n_outer * GL * n_inner, LANES).astype(BF16)

    ek = by_range(kp, L)
    eb = by_range(jnp.concatenate([bpre, bpim], axis=-1), L)
    a_t = jnp.swapaxes(jnp.concatenate([are, -aim], axis=-1), 1, 2)
    ea = by_range(a_t, 2)

    half = RANGE_STATE // LANES
    lnr_t = lnre.reshape(R, half, LANES)
    lni_t = lnim.reshape(R, half, LANES)
    lnr = jnp.concatenate([lnr_t, lnr_t], axis=1)
    lni = jnp.concatenate([-lni_t, lni_t], axis=1)
    return ek, eb, ea, lnr, lni


def _ssm_select_matrices():
    q = np.arange(SSM_CHUNK * LANES)
    r = np.arange(LANES)
    sel_k = (r[:, None] // SSM_GROUP == q[None, :] // LANES) & (r[:, None] % SSM_GROUP == q[None, :] % SSM_GROUP)
    sel_b = (r[:, None] // SSM_STATE == q[None, :] // RANGE_STATE) & (r[:, None] % SSM_STATE == q[None, :] % SSM_STATE)
    return jnp.asarray(sel_k, BF16), jnp.asarray(sel_b, BF16)


HEADS_PER_BLOCK = COL_BLOCK // HEAD_DIM
_SEG_U = 0
_SEG_Q = SSM_WIDTH // COL_BLOCK
_SEG_K = _SEG_Q + QKV_WIDTH // COL_BLOCK
_SEG_V = _SEG_K + QKV_WIDTH // COL_BLOCK
_SEG_GATE = _SEG_V + QKV_WIDTH // COL_BLOCK
_SEG_END = IN_WIDTH // COL_BLOCK


def _head_norm(res, gain):
    outs = []
    for h in range(HEADS_PER_BLOCK):
        t = res[:, h * HEAD_DIM:(h + 1) * HEAD_DIM]
        ms = jnp.mean(t * t, axis=-1, keepdims=True)
        outs.append(t * lax.rsqrt(ms + EPS) * gain)
    return outs


def _rmsnorm_rows(x, gain):
    ms = jnp.mean(x * x, axis=-1, keepdims=True)
    return (x * lax.rsqrt(ms + EPS) * gain).astype(BF16)


def _rmsnorm_kernel(x_ref, g_ref, o_ref):
    o_ref[...] = _rmsnorm_rows(x_ref[...], g_ref[...])


def _rmsnorm(x, gain, tm):
    M, D = x.shape
    return pl.pallas_call(
        _rmsnorm_kernel,
        grid=(M // tm,),
        in_specs=[pl.BlockSpec((tm, D), lambda i: (i, 0)), pl.BlockSpec((1, D), lambda i: (0, 0))],
        out_specs=pl.BlockSpec((tm, D), lambda i: (i, 0)),
        out_shape=jax.ShapeDtypeStruct((M, D), BF16),
        compiler_params=_cparams(("arbitrary",)),
        name="rmsnorm",
    )(x, gain.reshape(1, D))


def _in_proj_kernel(h_ref, hs_ref, w_ref, qn_ref, kn_ref,
                    u_ref, qkv_ref, gate_ref, us_ref, qkvs_ref, gates_s_ref, w_scr):
    i = pl.program_id(0)
    j = pl.program_id(1)
    tm = h_ref.shape[0]
    sub = min(tm, ROW_SUB)

    def row_blocks():
        w_scr[...] = w_ref[...].astype(BF16)
        for r in range(tm // sub):
            rows = pl.ds(r * sub, sub)
            yield rows, _dot(h_ref[rows, :], w_scr[...])

    def sample_res():
        return _dot(hs_ref[...], w_scr[...])

    @pl.when(j < _SEG_Q)
    def _():
        for rows, res in row_blocks():
            u_ref[rows, :] = res

        @pl.when(i == 0)
        def _():
            us_ref[j] = sample_res()

    def split_heads(res, gain_ref):
        if gain_ref is not None:
            return _head_norm(res, gain_ref[...])
        return [res[:, h * HEAD_DIM:(h + 1) * HEAD_DIM] for h in range(HEADS_PER_BLOCK)]

    def store_heads(gain_ref):
        for rows, res in row_blocks():
            for h, t in enumerate(split_heads(res, gain_ref)):
                qkv_ref[h, rows, :] = t

        @pl.when(i == 0)
        def _():
            base = (j - _SEG_Q) * HEADS_PER_BLOCK
            for h, t in enumerate(split_heads(sample_res(), gain_ref)):
                qkvs_ref[base + h] = t

    pl.when((j >= _SEG_Q) & (j < _SEG_K))(functools.partial(store_heads, qn_ref))
    pl.when((j >= _SEG_K) & (j < _SEG_V))(functools.partial(store_heads, kn_ref))
    pl.when((j >= _SEG_V) & (j < _SEG_GATE))(functools.partial(store_heads, None))

    @pl.when(j >= _SEG_GATE)
    def _():
        for rows, res in row_blocks():
            gate_ref[rows, :] = jax.nn.sigmoid(res).astype(gate_ref.dtype)

        @pl.when(i == 0)
        def _():
            gates_s_ref[j - _SEG_GATE] = jax.nn.sigmoid(sample_res()).astype(gates_s_ref.dtype)


def _in_proj(h, hs, w, layer, q_gain, k_gain, tm):
    M, Ms = h.shape[0], hs.shape[0]
    n_heads = 3 * N_ATT_HEADS
    n_gate = _SEG_END - _SEG_GATE
    whole = lambda shape: pl.BlockSpec(shape, lambda i, j: (0,) * len(shape))
    u, qkv, gates, us, qkvs, gates_s = pl.pallas_call(
        _in_proj_kernel,
        grid=(M // tm, _SEG_END),
        in_specs=[
            pl.BlockSpec((tm, D_MODEL), lambda i, j: (i, 0), pipeline_mode=pl.Buffered(1)),
            whole((Ms, D_MODEL)),
            pl.BlockSpec((None, D_MODEL, COL_BLOCK), lambda i, j: (layer, 0, j)),
            pl.BlockSpec((1, HEAD_DIM), lambda i, j: (0, 0)),
            pl.BlockSpec((1, HEAD_DIM), lambda i, j: (0, 0)),
        ],
        out_specs=[
            pl.BlockSpec((tm, COL_BLOCK), lambda i, j: (i, jnp.clip(j, 0, _SEG_Q - 1))),
            pl.BlockSpec((HEADS_PER_BLOCK, tm, HEAD_DIM),
                         lambda i, j: (jnp.clip(j - _SEG_Q, 0, _SEG_GATE - _SEG_Q - 1), i, 0)),
            pl.BlockSpec((tm, COL_BLOCK), lambda i, j: (i, jnp.clip(j - _SEG_GATE, 0, _SEG_END - _SEG_GATE - 1))),
            whole((_SEG_Q, Ms, COL_BLOCK)),
            whole((n_heads, Ms, HEAD_DIM)),
            whole((n_gate, Ms, COL_BLOCK)),
        ],
        out_shape=[
            jax.ShapeDtypeStruct((M, SSM_WIDTH), F32),
            jax.ShapeDtypeStruct((n_heads, M, HEAD_DIM), F32),
            jax.ShapeDtypeStruct((M, 2 * D_MODEL), BF16),
            jax.ShapeDtypeStruct((_SEG_Q, Ms, COL_BLOCK), F32),
            jax.ShapeDtypeStruct((n_heads, Ms, HEAD_DIM), F32),
            jax.ShapeDtypeStruct((n_gate, Ms, COL_BLOCK), BF16),
        ],
        scratch_shapes=[pltpu.VMEM((D_MODEL, COL_BLOCK), BF16)],
        compiler_params=_cparams(("arbitrary", "arbitrary")),
        name="in_proj",
    )(h, hs, w, q_gain.reshape(1, HEAD_DIM), k_gain.reshape(1, HEAD_DIM))
    us = jnp.swapaxes(us, 0, 1).reshape(Ms, SSM_WIDTH)
    gates_s = jnp.swapaxes(gates_s, 0, 1).reshape(Ms, 2 * D_MODEL)
    return (u, qkv, gates), (us, qkvs, gates_s)


def _expand_block_diag(e_ref, sel_ref, row_shift, col_shift, out_scr):
    n = out_scr.shape[0]
    for c in range(n // LANES):
        rows = pl.ds(c * LANES, LANES)
        full = _dot(e_ref[rows, :], sel_ref[...])
        row = lax.broadcasted_iota(jnp.int32, full.shape, 0) + c * LANES
        col = lax.broadcasted_iota(jnp.int32, full.shape, 1)
        keep = ((row >> row_shift) & (SSM_RANGE_GROUPS - 1)) == ((col >> col_shift) & (SSM_RANGE_GROUPS - 1))
        out_scr[rows, :] = jnp.where(keep, full, 0.0).astype(BF16)


def _ssm_chunk_kernel(nseq, nk, u_ref, h0_ref, ek_ref, eb_ref, ea_ref, selk_ref, selb_ref, lnr_ref, lni_ref,
                      y_ref, hfin_ref, s_scr, ktoe_ref, bcat_ref, acat_ref):
    L = SSM_CHUNK
    rows = nseq * nk
    nsub = 2 * RANGE_STATE // LANES
    lg_c = SSM_GROUP.bit_length() - 1
    lg_p = SSM_STATE.bit_length() - 1

    @pl.when(pl.program_id(1) == 0)
    def _():
        _expand_block_diag(ek_ref, selk_ref, lg_c, lg_c, ktoe_ref)
        _expand_block_diag(eb_ref, selb_ref, lg_c, lg_p, bcat_ref)
        _expand_block_diag(ea_ref, selk_ref, lg_p, lg_c, acat_ref)

    ucat = jnp.concatenate([u_ref[pl.ds(j, rows, stride=L), :] for j in range(L)], axis=-1).astype(BF16)
    s = _dot(ucat, bcat_ref[...])
    for n in range(nsub):
        s_scr[pl.ds(n, rows, stride=nsub), :] = s[:, n * LANES:(n + 1) * LANES]
    y_ref_intra = _dot(ucat, ktoe_ref[...])

    lnr = lnr_ref[...]
    lni = lni_ref[...]

    def step(k, hs):
        new = []
        for q in range(nseq):
            off = pl.multiple_of((q * nk + k) * nsub, nsub)
            h = hs[q]
            sk = s_scr[pl.ds(off, nsub), :]
            s_scr[pl.ds(off, nsub), :] = h
            new.append(h * lnr + pltpu.roll(h, nsub // 2, axis=0) * lni + sk)
        return tuple(new)

    hs = lax.fori_loop(0, nk, step, tuple(h0_ref[q] for q in range(nseq)))
    for q in range(nseq):
        hfin_ref[q] = hs[q]

    hprev = jnp.concatenate([s_scr[pl.ds(n, rows, stride=nsub), :] for n in range(nsub)], axis=-1).astype(BF16)
    y = y_ref_intra + _dot(hprev, acat_ref[...])
    for j in range(L):
        y_ref[pl.ds(j, rows, stride=L), :] = y[:, j * LANES:(j + 1) * LANES]


def _ssm_chunk(u, h0, ek, eb, ea, sel_k, sel_b, lnr, lni, nseq_total, seq_per_step):
    M = u.shape[0]
    T = M // nseq_total
    nk = T // SSM_CHUNK
    nsteps = nseq_total // seq_per_step
    R = N_SSM_RANGES
    nsub = 2 * RANGE_STATE // LANES
    tm = seq_per_step * T
    wide = SSM_CHUNK * LANES
    assert wide == 2 * RANGE_STATE
    wspec = lambda shape: pl.BlockSpec((None,) + shape, lambda r, b: (r, 0, 0))
    sel_spec = pl.BlockSpec((LANES, wide), lambda r, b: (0, 0))
    return pl.pallas_call(
        functools.partial(_ssm_chunk_kernel, seq_per_step, nk),
        grid=(R, nsteps),
        in_specs=[
            pl.BlockSpec((tm, LANES), lambda r, b: (b, r)),
            pl.BlockSpec((seq_per_step, None, nsub, LANES), lambda r, b: (b, r, 0, 0)),
            wspec((wide, LANES)),
            wspec((wide, LANES)),
            wspec((wide, LANES)),
            sel_spec,
            sel_spec,
            wspec((nsub, LANES)),
            wspec((nsub, LANES)),
        ],
        out_specs=[
            pl.BlockSpec((tm, LANES), lambda r, b: (b, r)),
            pl.BlockSpec((seq_per_step, None, nsub, LANES), lambda r, b: (b, r, 0, 0)),
        ],
        out_shape=[
            jax.ShapeDtypeStruct((M, SSM_WIDTH), F32),
            jax.ShapeDtypeStruct((nseq_total, R, nsub, LANES), F32),
        ],
        scratch_shapes=[pltpu.VMEM((seq_per_step * nk * nsub, LANES), F32)] + [pltpu.VMEM((wide, wide), BF16)] * 3,
        compiler_params=_cparams(("arbitrary", "arbitrary")),
        name="ssm_chunk",
    )(u, h0, ek, eb, ea, sel_k, sel_b, lnr, lni)


def _state_to_tiles(re, im):
    N = re.shape[0]
    half = RANGE_STATE // LANES
    return jnp.concatenate([re.reshape(N, N_SSM_RANGES, half, LANES),
                            im.reshape(N, N_SSM_RANGES, half, LANES)], axis=2)


def _tiles_to_state(t):
    N = t.shape[0]
    half = RANGE_STATE // LANES
    return (t[:, :, :half].reshape(N, N_SSM_GROUPS, SSM_STATE),
            t[:, :, half:].reshape(N, N_SSM_GROUPS, SSM_STATE))


def _bucket_np(dist):
    max_exact = N_BUCKETS // 2
    n = np.maximum(dist, 0)
    nf = np.maximum(n, 1).astype(np.float64)
    large = max_exact + (np.log(nf / max_exact) / math.log(REL_MAX_DISTANCE / max_exact)
                         * (N_BUCKETS - max_exact)).astype(np.int32)
    large = np.minimum(large, N_BUCKETS - 1)
    return np.where(n < max_exact, n, large)


def _prompt_bucket_tiles():
    a = np.arange(Q_BLOCK)[:, None]
    c = np.arange(2 * Q_BLOCK)[None, :]
    rel = a - c + Q_BLOCK
    tiles = []
    for window, dil in WINDOWS:
        K = window // dil + 1
        valid = (rel >= 0) & (rel < K)
        tiles.append(np.where(valid, _bucket_np(np.clip(rel, 0, K - 1) * dil), -1))
    return jnp.asarray(np.stack(tiles), jnp.int32)


def _attn_group_of_step(step):
    return (step + ATTN_FIRST_GROUP) % N_ATT_GROUPS


def _attn_prompt_kernel(T, tab_ref, q_ref, k_ref, v_ref, bkt_ref, o_ref,
                        m_scr, l_scr, acc_scr, s_scr, p_scr, mb_scr, bias_scr):
    h = pl.program_id(1)
    step = pl.program_id(2)
    g = _attn_group_of_step(step)
    scale = HEAD_DIM ** -0.5
    nblk = T // Q_BLOCK

    bkt = bkt_ref[...]
    col = g * HEADS_PER_GROUP + h
    bias = jnp.full(bkt.shape, NEG_INF, F32)
    for t in range(N_BUCKETS):
        bias = jnp.where(bkt == t, tab_ref[t, col], bias)
    bias_scr[...] = bias

    def run_group(first, dil):
        per_class = nblk // dil
        run = min(per_class, ATTN_UNROLL)
        runs_per_iter = ATTN_UNROLL // run
        whole_class = run == per_class

        def block_rows(r, n):
            return pl.ds(r + n * (Q_BLOCK * dil), Q_BLOCK, stride=dil)

        def runs(it):
            for j in range(runs_per_iter):
                i0 = it * ATTN_UNROLL + j * run
                yield i0, i0 // per_class, (0 if whole_class else i0 % per_class)

        def tiles(ref, r, n0, augment):
            out = []
            for u in range(-1, run):
                if u < 0 and whole_class:
                    out.append(None)
                    continue
                n = jnp.maximum(n0 + u, 0) if u < 0 else n0 + u
                t = ref[block_rows(r, n), :].astype(BF16)
                if augment:
                    t = jnp.concatenate([t, jnp.ones((Q_BLOCK, HEAD_DIM), BF16)], axis=1)
                out.append(t)
            return out

        def scores(it, carry):
            for i0, r, n0 in runs(it):
                kt = tiles(k_ref, r, n0, False)
                for u in range(run):
                    q = q_ref[block_rows(r, n0 + u), :].astype(BF16)
                    s_r = _dot_nt(q, kt[u + 1]) * scale + bias_scr[:, Q_BLOCK:]
                    if kt[u] is None:
                        s_l = jnp.full((Q_BLOCK, Q_BLOCK), NEG_INF, F32)
                    else:
                        bias_l = bias_scr[:, :Q_BLOCK]
                        if u == 0:
                            bias_l = jnp.where(n0 == 0, NEG_INF, bias_l)
                        s_l = _dot_nt(q, kt[u]) * scale + bias_l
                    s_scr[i0 + u, :, :Q_BLOCK] = s_l
                    s_scr[i0 + u, :, Q_BLOCK:] = s_r
                    m = jnp.maximum(jnp.max(s_l, axis=-1, keepdims=True), jnp.max(s_r, axis=-1, keepdims=True))
                    mb_scr[i0 + u] = jnp.broadcast_to(m, (Q_BLOCK, HEAD_DIM))
            return carry

        def probs(i, carry):
            mb = mb_scr[i]
            p_scr[i, :, :Q_BLOCK] = jnp.exp(s_scr[i, :, :Q_BLOCK] - mb).astype(BF16)
            p_scr[i, :, Q_BLOCK:] = jnp.exp(s_scr[i, :, Q_BLOCK:] - mb).astype(BF16)
            return carry

        def values(it, carry):
            for i0, r, n0 in runs(it):
                vt = tiles(v_ref, r, n0, True)
                for u in range(run):
                    i = i0 + u
                    ol = _dot(p_scr[i, :, Q_BLOCK:], vt[u + 1])
                    if vt[u] is not None:
                        ol = ol + _dot(p_scr[i, :, :Q_BLOCK], vt[u])
                    o = ol[:, :HEAD_DIM]
                    lb = ol[:, HEAD_DIM:]
                    mb = mb_scr[i]
                    sl_q = block_rows(r, n0 + u)
                    if first:
                        m_scr[sl_q, :] = mb
                        l_scr[sl_q, :] = lb
                        acc_scr[sl_q, :] = o
                    else:
                        m0 = m_scr[sl_q, :]
                        mn = jnp.maximum(m0, mb)
                        a0 = jnp.exp(m0 - mn)
                        a1 = jnp.exp(mb - mn)
                        m_scr[sl_q, :] = mn
                        l_scr[sl_q, :] = a0 * l_scr[sl_q, :] + a1 * lb
                        acc_scr[sl_q, :] = a0 * acc_scr[sl_q, :] + a1 * o
            return carry

        lax.fori_loop(0, nblk // ATTN_UNROLL, scores, 0)
        lax.fori_loop(0, nblk, probs, 0, unroll=ATTN_UNROLL)
        lax.fori_loop(0, nblk // ATTN_UNROLL, values, 0)

    for s in range(N_ATT_GROUPS):
        dil = WINDOWS[(s + ATTN_FIRST_GROUP) % N_ATT_GROUPS][1]
        pl.when(step == s)(functools.partial(run_group, s == 0, dil))

    @pl.when(step == N_ATT_GROUPS - 1)
    def _():
        o_ref[...] = (acc_scr[...] / l_scr[...]).astype(o_ref.dtype)


def _attn_prompt(qkv, rel_bias, bucket_tiles, nbatch):
    M = qkv.shape[1]
    T = M // nbatch
    H, G = HEADS_PER_GROUP, N_ATT_GROUPS
    nblk = T // Q_BLOCK

    def qkv_spec(which):
        return pl.BlockSpec((None, T, HEAD_DIM),
                            lambda b, h, s: (which * N_ATT_HEADS + _attn_group_of_step(s) * H + h, b, 0))

    return pl.pallas_call(
        functools.partial(_attn_prompt_kernel, T),
        grid=(nbatch, H, G),
        in_specs=[pl.BlockSpec(memory_space=pltpu.SMEM),
                  qkv_spec(0), qkv_spec(1), qkv_spec(2),
                  pl.BlockSpec((None, Q_BLOCK, 2 * Q_BLOCK), lambda b, h, s: (_attn_group_of_step(s), 0, 0))],
        out_specs=pl.BlockSpec((T, HEAD_DIM), lambda b, h, s: (b, h)),
        out_shape=jax.ShapeDtypeStruct((M, ATT_WIDTH), BF16),
        scratch_shapes=[pltpu.VMEM((T, HEAD_DIM), F32)] * 3
                       + [pltpu.VMEM((nblk, Q_BLOCK, 2 * Q_BLOCK), F32),
                          pltpu.VMEM((nblk, Q_BLOCK, 2 * Q_BLOCK), BF16),
                          pltpu.VMEM((nblk, Q_BLOCK, HEAD_DIM), F32),
                          pltpu.VMEM((Q_BLOCK, 2 * Q_BLOCK), F32)],
        compiler_params=_cparams(("arbitrary", "arbitrary", "arbitrary")),
        name="attn_prompt",
    )(rel_bias.astype(F32), qkv, qkv, qkv, bucket_tiles)


def _attn_sample_kernel(window, dil, q_ref, ck_ref, cv_ref, nk_ref, nv_ref, bias_ref, o_ref, lse_ref):
    S = q_ref.shape[0]
    K = window // dil + 1
    scale = HEAD_DIM ** -0.5
    bias = bias_ref[...]
    by_residue = len(ck_ref.shape) == 4

    def buffered(ref, s, n):
        return ref[pl.ds(0, n), s] if by_residue else ref[pl.ds(s, n, stride=dil)]

    for s in range(S):
        n_c = (window - 1 - s) // dil + 1
        qs = q_ref[s]
        new_rows = [s + j * dil - window for j in range(n_c, K)]
        kk = jnp.concatenate([buffered(ck_ref, s, n_c)] + [nk_ref[pl.ds(i, 1)] for i in new_rows], axis=0)
        vv = jnp.concatenate([buffered(cv_ref, s, n_c)] + [nv_ref[pl.ds(i, 1)] for i in new_rows], axis=0)
        lg = jnp.sum(kk * qs[None], axis=-1, keepdims=True) * scale + bias
        m = jnp.max(lg, axis=0)
        p = jnp.exp(lg - m[None])
        l = jnp.sum(p, axis=0)
        o_ref[s] = jnp.sum(p * vv, axis=0) / l
        lse_ref[s] = m + jnp.log(l)


def _attn_sample(q, cache_k, cache_v, new_k, new_v, bias, layer, window, dil):
    B, S = q.shape[0], q.shape[1]
    H = HEADS_PER_GROUP
    K = window // dil + 1
    small = pl.BlockSpec((None, S, H, HEAD_DIM), lambda b: (b, 0, 0, 0))
    if dil > S:
        depth = cache_k.shape[0]
        cache_k = cache_k.reshape(depth, B, window // dil, dil, H, HEAD_DIM)
        cache_v = cache_v.reshape(depth, B, window // dil, dil, H, HEAD_DIM)
        cache = pl.BlockSpec((None, None, window // dil, S, H, HEAD_DIM), lambda b: (layer, b, 0, 0, 0, 0))
    else:
        cache = pl.BlockSpec((None, None, window, H, HEAD_DIM), lambda b: (layer, b, 0, 0, 0))
    return pl.pallas_call(
        functools.partial(_attn_sample_kernel, window, dil),
        grid=(B,),
        in_specs=[small, cache, cache, small, small, pl.BlockSpec((K, H, HEAD_DIM), lambda b: (0, 0, 0))],
        out_specs=[small, small],
        out_shape=[jax.ShapeDtypeStruct((B, S, H, HEAD_DIM), F32)] * 2,
        compiler_params=_cparams(("arbitrary",)),
        name="attn_sample_w%d" % window,
    )(q, cache_k, cache_v, new_k, new_v, bias)


def _merge_groups_kernel(o0, o1, o2, l0, l1, l2, y_ref):
    a, b, c = l0[...], l1[...], l2[...]
    m = jnp.maximum(jnp.maximum(a, b), c)
    ea, eb, ec = jnp.exp(a - m), jnp.exp(b - m), jnp.exp(c - m)
    y_ref[...] = ((ea * o0[...] + eb * o1[...] + ec * o2[...]) / (ea + eb + ec)).astype(y_ref.dtype)


def _merge_groups(outs, lses):
    shape = outs[0].shape
    return pl.pallas_call(
        _merge_groups_kernel,
        out_shape=jax.ShapeDtypeStruct(shape, BF16),
        name="merge_groups",
    )(*outs, *lses)


def _sample_bias(rel_bias):
    out = []
    for g, (window, dil) in enumerate(WINDOWS):
        K = window // dil + 1
        steps = (K - 1) - np.arange(K)
        tab = rel_bias[:, g * HEADS_PER_GROUP:(g + 1) * HEADS_PER_GROUP].astype(F32)
        b = tab[_bucket_np(steps * dil)]
        out.append(jnp.broadcast_to(b[:, :, None], (K, HEADS_PER_GROUP, HEAD_DIM)))
    return out


def _gelu_tanh(x):
    return 0.5 * x * (1.0 + jnp.tanh(math.sqrt(2.0 / math.pi) * (x + 0.044715 * (x * x * x))))


def _mix_out_kernel(y_ref, u_ref, yb_ref, ga_ref, gb_ref, x_ref, d_ref, wglu_ref, bglu_ref, wa_ref, wb_ref,
                    wout_ref, gn_ref, x1_ref, h2_ref):
    tm = x_ref.shape[0]
    sub = min(tm, MIX_ROW_SUB)
    for r in range(tm // sub):
        rows = pl.ds(r * sub, sub)
        y = y_ref[rows, :] + d_ref[...] * u_ref[rows, :]
        z = _gelu_tanh(y)
        ya = z * jax.nn.sigmoid(_dot(z.astype(BF16), wglu_ref[...]) + bglu_ref[...])
        mix = (ga_ref[rows, :] * _dot(ya.astype(BF16), wa_ref[...])
               + gb_ref[rows, :] * _dot(yb_ref[rows, :], wb_ref[...]))
        x1 = x_ref[rows, :] + _dot(mix.astype(BF16), wout_ref[...])
        x1_ref[rows, :] = x1
        h2_ref[rows, :] = _rmsnorm_rows(x1, gn_ref[...])


def _mix_out(y_ssm, u, yb, gates, x, ssm_d, wglu, bglu, wa, wb, wout, layer, gain, tm):
    M = x.shape[0]
    row = lambda w: pl.BlockSpec((tm, w), lambda i: (i, 0))
    vec = lambda b: pl.BlockSpec((1, b), lambda i: (0, 0), pipeline_mode=pl.Buffered(1))
    full = lambda a, b: pl.BlockSpec((None, a, b), lambda i: (layer, 0, 0), pipeline_mode=pl.Buffered(1))
    return pl.pallas_call(
        _mix_out_kernel,
        grid=(M // tm,),
        in_specs=[row(SSM_WIDTH), row(SSM_WIDTH), row(ATT_WIDTH),
                  pl.BlockSpec((tm, D_MODEL), lambda i: (i, 0)),
                  pl.BlockSpec((tm, D_MODEL), lambda i: (i, 1)),
                  row(D_MODEL),
                  vec(SSM_WIDTH), full(SSM_WIDTH, SSM_WIDTH), vec(SSM_WIDTH),
                  full(SSM_WIDTH, D_MODEL), full(ATT_WIDTH, D_MODEL), full(D_MODEL, D_MODEL), vec(D_MODEL)],
        out_specs=[row(D_MODEL), row(D_MODEL)],
        out_shape=[jax.ShapeDtypeStruct((M, D_MODEL), F32), jax.ShapeDtypeStruct((M, D_MODEL), BF16)],
        compiler_params=_cparams(("arbitrary",)),
        name="mix_out",
    )(y_ssm, u, yb, gates, gates, x, ssm_d.reshape(1, -1), wglu, bglu.reshape(1, -1), wa, wb, wout,
      gain.reshape(1, -1))


def _ffn_kernel(emit_norm, h_ref, x_ref, hs_ref, xs_ref, wg_ref, wu_ref, wd_ref, gn_ref, *rest):
    if emit_norm:
        o_ref, os_ref, hn_ref, hns_ref, wg_scr, wu_scr, wd_scr = rest
    else:
        o_ref, os_ref, wg_scr, wu_scr, wd_scr = rest
    i = pl.program_id(0)
    f = pl.program_id(1)
    last = pl.num_programs(1) - 1

    @pl.when(f == 0)
    def _():
        o_ref[...] = x_ref[...]

    @pl.when((f == 0) & (i == 0))
    def _():
        os_ref[...] = xs_ref[...]

    def swiglu(h):
        a = jax.nn.silu(_dot(h, wg_scr[...])) * _dot(h, wu_scr[...])
        return _dot(a.astype(BF16), wd_scr[...])

    tm = h_ref.shape[0]
    sub = min(tm, FFN_ROW_SUB)
    wg_scr[...] = wg_ref[...].astype(BF16)
    wu_scr[...] = wu_ref[...].astype(BF16)
    wd_scr[...] = wd_ref[...].astype(BF16)
    for r in range(tm // sub):
        rows = pl.ds(r * sub, sub)
        o_ref[rows, :] += swiglu(h_ref[rows, :])

    @pl.when(i == 0)
    def _():
        os_ref[...] += swiglu(hs_ref[...])

    if emit_norm:
        @pl.when(f == last)
        def _():
            for r in range(tm // sub):
                rows = pl.ds(r * sub, sub)
                hn_ref[rows, :] = _rmsnorm_rows(o_ref[rows, :], gn_ref[...])

        @pl.when((f == last) & (i == 0))
        def _():
            hns_ref[...] = _rmsnorm_rows(os_ref[...], gn_ref[...])


def _ffn(h2, x1, h2s, x1s, wg, wu, wd, layer, next_gain, tm):
    M, Ms = x1.shape[0], x1s.shape[0]
    nf = FFN_HIDDEN // FFN_BLOCK
    emit_norm = next_gain is not None
    gain = next_gain if emit_norm else jnp.ones((D_MODEL,), F32)
    row = pl.BlockSpec((tm, D_MODEL), lambda i, f: (i, 0))
    whole = pl.BlockSpec((Ms, D_MODEL), lambda i, f: (0, 0))
    out_specs = [row, whole]
    out_shape = [jax.ShapeDtypeStruct((M, D_MODEL), F32), jax.ShapeDtypeStruct((Ms, D_MODEL), F32)]
    if emit_norm:
        out_specs += [row, whole]
        out_shape += [jax.ShapeDtypeStruct((M, D_MODEL), BF16), jax.ShapeDtypeStruct((Ms, D_MODEL), BF16)]
    return pl.pallas_call(
        functools.partial(_ffn_kernel, emit_norm),
        grid=(M // tm, nf),
        in_specs=[row,
                  pl.BlockSpec((tm, D_MODEL), lambda i, f: (i, 0), pipeline_mode=pl.Buffered(1)),
                  whole, whole,
                  pl.BlockSpec((None, D_MODEL, FFN_BLOCK), lambda i, f: (layer, 0, f)),
                  pl.BlockSpec((None, D_MODEL, FFN_BLOCK), lambda i, f: (layer, 0, f)),
                  pl.BlockSpec((None, FFN_BLOCK, D_MODEL), lambda i, f: (layer, f, 0)),
                  pl.BlockSpec((1, D_MODEL), lambda i, f: (0, 0))],
        out_specs=out_specs,
        out_shape=out_shape,
        scratch_shapes=[pltpu.VMEM((D_MODEL, FFN_BLOCK), BF16), pltpu.VMEM((D_MODEL, FFN_BLOCK), BF16),
                        pltpu.VMEM((FFN_BLOCK, D_MODEL), BF16)],
        compiler_params=_cparams(("arbitrary", "arbitrary")),
        name="ffn",
    )(h2, x1, h2s, x1s, wg, wu, wd, gain.reshape(1, D_MODEL))


def _cache_shift_kernel(ck_ref, cv_ref, hk_ref, hv_ref, nk_ref, nv_ref, ok_ref, ov_ref):
    c = pl.program_id(2)
    last = pl.num_programs(2) - 1
    R = ck_ref.shape[0]
    S = nk_ref.shape[0]
    for cache, halo, new, out in ((ck_ref, hk_ref, nk_ref, ok_ref), (cv_ref, hv_ref, nv_ref, ov_ref)):
        out[pl.ds(0, R - S)] = cache[pl.ds(S, R - S)]

        @pl.when(c == last)
        def _():
            out[pl.ds(R - S, S)] = new[...]

        @pl.when(c < last)
        def _():
            out[pl.ds(R - S, S)] = halo[...]


def _cache_shift(cache_k, cache_v, new_k, new_v):
    depth, B, W, H, E = cache_k.shape
    S = new_k.shape[2]
    R = min(W, CACHE_ROWS)
    nchunks = W // R
    blk = pl.BlockSpec((None, None, R, H, E), lambda l, b, c: (l, b, c, 0, 0))
    halo = pl.BlockSpec((None, None, None, S, H, E),
                        lambda l, b, c: (l, b, jnp.minimum((c + 1) * (R // S), W // S - 1), 0, 0, 0))
    new = pl.BlockSpec((None, None, S, H, E), lambda l, b, c: (l, b, 0, 0, 0))
    as_rows = lambda a: a.reshape(depth, B, W // S, S, H, E)
    return pl.pallas_call(
        _cache_shift_kernel,
        grid=(depth, B, nchunks),
        in_specs=[blk, blk, halo, halo, new, new],
        out_specs=[blk, blk],
        out_shape=[jax.ShapeDtypeStruct(cache_k.shape, cache_k.dtype)] * 2,
        compiler_params=_cparams(("arbitrary", "arbitrary", "arbitrary")),
        name="cache_shift_w%d" % W,
    )(cache_k, cache_v, as_rows(cache_k), as_rows(cache_v), new_k, new_v)


def _kv_tails_kernel(*refs):
    depth = (len(refs) - 2) // 2
    ok_ref, ov_ref = refs[-2:]
    H, R = refs[0].shape[0], refs[0].shape[1]
    for l in range(depth):
        @pl.when(pl.program_id(0) == l)
        def _():
            for src, out in ((refs[2 * l], ok_ref), (refs[2 * l + 1], ov_ref)):
                for h in range(H):
                    out[pl.ds(h, R, stride=H), :] = src[h]


def _kv_tails(qkv_layers, g, window, nbatch):
    depth = len(qkv_layers)
    M = qkv_layers[0].shape[1]
    T = M // nbatch
    H, E = HEADS_PER_GROUP, HEAD_DIM
    R = min(window, CACHE_ROWS)
    nchunks = window // R
    first = (T - window) // R

    def src(layer, which):
        def index(l, b, c):
            bb = jnp.where(l < layer, 0, jnp.where(l > layer, nbatch - 1, b))
            cc = jnp.where(l < layer, 0, jnp.where(l > layer, nchunks - 1, c))
            return (which * N_ATT_GROUPS + g, bb * (T // R) + first + cc, 0)
        return pl.BlockSpec((H, R, E), index)

    args, in_specs = [], []
    for layer, qkv in enumerate(qkv_layers):
        args += [qkv, qkv]
        in_specs += [src(layer, 1), src(layer, 2)]
    out_spec = pl.BlockSpec((None, R * H, E), lambda l, b, c: (l, b * nchunks + c, 0))
    return pl.pallas_call(
        _kv_tails_kernel,
        grid=(depth, nbatch, nchunks),
        in_specs=in_specs,
        out_specs=[out_spec, out_spec],
        out_shape=[jax.ShapeDtypeStruct((depth, nbatch * window * H, E), qkv_layers[0].dtype)] * 2,
        compiler_params=_cparams(("arbitrary", "arbitrary", "arbitrary")),
        name="kv_tails_w%d" % window,
    )(*args)


def _ssm(u, h0_tiles, nseq, prm):
    return _ssm_chunk(u, h0_tiles, prm["ek"], prm["eb"], prm["ea"], prm["sel_k"], prm["sel_b"],
                      prm["lnr"], prm["lni"], nseq, nseq)


def _mix(x2d, u, y_ssm, yb, gates, prm, tm):
    return _mix_out(y_ssm, u, yb, gates, x2d, prm["ssm_d"], prm["w_glu"], prm["b_glu"], prm["w_branch_a"],
                    prm["w_branch_b"], prm["w_out"], prm["layer"], prm["norm_ffn"], tm)


def kernel(x_prompt, x_sample, state_ssm_re, state_ssm_im, cache_k_w128, cache_v_w128, cache_k_w512, cache_v_w512,
           cache_k_w2048, cache_v_w2048, rel_bias, norm_mix, norm_ffn, q_norm, k_norm, w_in, ssm_lambda_re,
           ssm_lambda_im, ssm_log_dt, ssm_b_re, ssm_b_im, ssm_c_re, ssm_c_im, ssm_d, w_glu, b_glu, w_branch_a,
           w_branch_b, w_out, w_ffn_gate, w_ffn_up, w_ffn_down):
    depth = w_in.shape[0]
    B, T, _ = x_prompt.shape
    SB, S, _ = x_sample.shape
    H = HEADS_PER_GROUP
    caches_k = (cache_k_w128, cache_k_w512, cache_k_w2048)
    caches_v = (cache_v_w128, cache_v_w512, cache_v_w2048)

    bucket_tiles = _prompt_bucket_tiles()
    bias_sample = _sample_bias(rel_bias)
    sel_k, sel_b = _ssm_select_matrices()

    xp = x_prompt.reshape(B * T, D_MODEL)
    xs = x_sample.reshape(SB * S, D_MODEL)
    p_re, p_im, s_re, s_im = [], [], [], []
    qkv_prompt = []
    new_k = [[] for _ in WINDOWS]
    new_v = [[] for _ in WINDOWS]

    weights = {
        "w_glu": w_glu.astype(BF16), "w_branch_a": w_branch_a.astype(BF16),
        "w_branch_b": w_branch_b.astype(BF16), "w_out": w_out.astype(BF16),
    }

    hp = _rmsnorm(xp, norm_mix[0], FFN_ROWS)
    hs = _rmsnorm(xs, norm_mix[0], SB * S)

    for l in range(depth):
        ek, eb, ea, lnr, lni = _ssm_prep(ssm_lambda_re[l], ssm_lambda_im[l], ssm_log_dt[l], ssm_b_re[l],
                                         ssm_b_im[l], ssm_c_re[l], ssm_c_im[l])
        prm = dict(weights)
        prm.update({
            "layer": l, "norm_ffn": norm_ffn[l], "ssm_d": ssm_d[l], "b_glu": b_glu[l],
            "ek": ek, "eb": eb, "ea": ea, "sel_k": sel_k, "sel_b": sel_b, "lnr": lnr, "lni": lni,
        })

        (u, qkv, gates), (us, qkvs, gates_s) = _in_proj(hp, hs, w_in, l, q_norm[l], k_norm[l], IN_PROJ_ROWS)

        zeros = jnp.zeros((B, N_SSM_RANGES, 2 * RANGE_STATE // LANES, LANES), F32)
        y_ssm, hfin = _ssm(u, zeros, B, prm)
        yb = _attn_prompt(qkv, rel_bias, bucket_tiles, B)
        x1, h2 = _mix(xp, u, y_ssm, yb, gates, prm, MIX_ROWS)
        hr, hi = _tiles_to_state(hfin)
        p_re.append(hr)
        p_im.append(hi)
        qkv_prompt.append(qkv)

        h0 = _state_to_tiles(state_ssm_re[l], state_ssm_im[l])
        ys_ssm, hfin = _ssm(us, h0, SB, prm)
        tok = jnp.transpose(qkvs.reshape(3, N_ATT_GROUPS, H, SB, S, HEAD_DIM), (0, 1, 3, 4, 2, 5))
        outs, lses = [], []
        for g, (window, dil) in enumerate(WINDOWS):
            nk_g, nv_g = tok[1, g], tok[2, g]
            new_k[g].append(nk_g)
            new_v[g].append(nv_g)
            o_g, lse_g = _attn_sample(tok[0, g], caches_k[g], caches_v[g], nk_g, nv_g, bias_sample[g], l,
                                      window, dil)
            outs.append(o_g)
            lses.append(lse_g)
        ybs = _merge_groups(outs, lses).reshape(SB * S, ATT_WIDTH)
        x1s, h2s = _mix(xs, us, ys_ssm, ybs, gates_s, prm, SB * S)
        hr, hi = _tiles_to_state(hfin)
        s_re.append(hr)
        s_im.append(hi)

        next_gain = norm_mix[l + 1] if l + 1 < depth else None
        res = _ffn(h2, x1, h2s, x1s, w_ffn_gate, w_ffn_up, w_ffn_down, l, next_gain, FFN_ROWS)
        if next_gain is None:
            xp, xs = res
        else:
            xp, xs, hp, hs = res

    shifted = []
    for g in range(N_ATT_GROUPS):
        shifted += _cache_shift(caches_k[g], caches_v[g], jnp.stack(new_k[g]), jnp.stack(new_v[g]))
    tails = [t.reshape(depth, B, window, H, HEAD_DIM)
             for g, (window, _) in enumerate(WINDOWS) for t in _kv_tails(qkv_prompt, g, window, B)]

    return (xp.reshape(B, T, D_MODEL), xs.reshape(SB, S, D_MODEL),
            jnp.stack(p_re), jnp.stack(p_im), *tails,
            jnp.stack(s_re), jnp.stack(s_im), *shifted)
```

```python
import functools
import math

import numpy as np
import jax
import jax.numpy as jnp
from jax import lax
from jax.experimental import pallas as pl
from jax.experimental.pallas import tpu as pltpu

F32 = jnp.float32
BF16 = jnp.bfloat16

D_MODEL = 2048
HEAD_DIM = 128
HEADS_PER_GROUP = 8
WINDOWS = ((128, 1), (512, 4), (2048, 16))
N_ATT_GROUPS = len(WINDOWS)
N_ATT_HEADS = N_ATT_GROUPS * HEADS_PER_GROUP
ATT_WIDTH = HEADS_PER_GROUP * HEAD_DIM
QKV_WIDTH = N_ATT_HEADS * HEAD_DIM
Q_BLOCK = 128
SSM_WIDTH = D_MODEL // 2
SSM_GROUP = 16
N_SSM_GROUPS = SSM_WIDTH // SSM_GROUP
SSM_STATE = 64
IN_WIDTH = SSM_WIDTH + 3 * QKV_WIDTH + 2 * D_MODEL
FFN_HIDDEN = 5632
N_BUCKETS = 32
REL_MAX_DISTANCE = 2048
EPS = 1e-6
NEG_INF = -1e30

LANES = 128
SUBLANES = 8
VMEM_LIMIT_BYTES = 60 * 1024 * 1024

SSM_CHUNK = 8
SSM_RANGE_GROUPS = LANES // SSM_GROUP
N_SSM_RANGES = N_SSM_GROUPS // SSM_RANGE_GROUPS
RANGE_STATE = SSM_RANGE_GROUPS * SSM_STATE
COL_BLOCK = 512
FFN_BLOCK = 256
ATTN_UNROLL = 16
ATTN_FIRST_GROUP = 2
CACHE_ROWS = 1024
IN_PROJ_ROWS = 2048
FFN_ROWS = 1024
MIX_ROWS = 256
ROW_SUB = 256
FFN_ROW_SUB = 512
MIX_ROW_SUB = 256


def _cparams(sem):
    return pltpu.CompilerParams(dimension_semantics=sem, vmem_limit_bytes=VMEM_LIMIT_BYTES)


def _dot(a, b):
    return jnp.dot(a, b, preferred_element_type=F32)


def _dot_nt(a, b):
    return lax.dot_general(a, b, (((1,), (1,)), ((), ())), preferred_element_type=F32)


def _ssm_prep_kernel(*refs):
    for g in range(SSM_RANGE_GROUPS):
        _ssm_prep_group(*[ref.at[g] for ref in refs])


def _ssm_prep_group(lre_ref, lim_ref, ldt_ref, btre_ref, btim_ref, cre_ref, cim_ref,
                    kp_ref, bpre_ref, bpim_ref, are_ref, aim_ref, lnre_ref, lnim_ref):
    L = SSM_CHUNK
    lr = lre_ref[...]
    li = lim_ref[...]
    dt = jnp.exp(ldt_ref[...])
    er = jnp.exp(lr * dt)
    lbr = er * jnp.cos(li * dt)
    lbi = er * jnp.sin(li * dt)
    nr = lbr - 1.0
    dd = lr * lr + li * li
    rr = (nr * lr + lbi * li) / dd
    ri = (lbi * lr - nr * li) / dd
    btr = btre_ref[...]
    bti = btim_ref[...]
    bbr = rr * btr - ri * bti
    bbi = rr * bti + ri * btr
    cr = cre_ref[...]
    ci = cim_ref[...]
    pr = [jnp.ones_like(lbr)]
    pi = [jnp.zeros_like(lbr)]
    for _ in range(L):
        pr.append(pr[-1] * lbr - pi[-1] * lbi)
        pi.append(pr[-2] * lbi + pi[-1] * lbr)
    xr = [cr * pr[t] - ci * pi[t] for t in range(L + 1)]
    xi = [cr * pi[t] + ci * pr[t] for t in range(L + 1)]
    xr_k = jnp.concatenate(xr[:L], axis=0)
    xi_k = jnp.concatenate(xi[:L], axis=0)
    hp = lax.Precision.HIGHEST
    kp = (lax.dot_general(bbr, xr_k, (((1,), (1,)), ((), ())), precision=hp, preferred_element_type=F32)
          - lax.dot_general(bbi, xi_k, (((1,), (1,)), ((), ())), precision=hp, preferred_element_type=F32))
    lane = lax.broadcasted_iota(jnp.int32, kp.shape, 1)
    blocks = [kp] + [jnp.where(lane >= j * SSM_GROUP, pltpu.roll(kp, j * SSM_GROUP, axis=1), 0.0)
                     for j in range(1, L)]
    kp_ref[...] = jnp.concatenate(blocks, axis=0)
    are_ref[...] = jnp.concatenate(xr[1:], axis=0)
    aim_ref[...] = jnp.concatenate(xi[1:], axis=0)
    bpre_ref[...] = jnp.concatenate([pr[L - 1 - j] * bbr - pi[L - 1 - j] * bbi for j in range(L)], axis=0)
    bpim_ref[...] = jnp.concatenate([pr[L - 1 - j] * bbi + pi[L - 1 - j] * bbr for j in range(L)], axis=0)
    lnre_ref[...] = pr[L]
    lnim_ref[...] = pi[L]


def _ssm_prep(lam_re, lam_im, log_dt, b_re, b_im, c_re, c_im):
    G, P, C, L = N_SSM_GROUPS, SSM_STATE, SSM_GROUP, SSM_CHUNK
    row = lambda a: a.reshape(G, 1, -1)
    bt_re = jnp.swapaxes(b_re, 1, 2)
    bt_im = jnp.swapaxes(b_im, 1, 2)
    GL = SSM_RANGE_GROUPS
    vec = pl.BlockSpec((GL, 1, P), lambda r: (r, 0, 0))
    mat = pl.BlockSpec((GL, C, P), lambda r: (r, 0, 0))
    big = pl.BlockSpec((GL, L * C, P), lambda r: (r, 0, 0))
    kp, bpre, bpim, are, aim, lnre, lnim = pl.pallas_call(
        _ssm_prep_kernel,
        grid=(G // GL,),
        in_specs=[vec, vec, pl.BlockSpec((GL, 1, 1), lambda r: (r, 0, 0)), mat, mat, mat, mat],
        out_specs=[pl.BlockSpec((GL, L * C, L * C), lambda r: (r, 0, 0)), big, big, big, big, vec, vec],
        out_shape=[jax.ShapeDtypeStruct((G, L * C, L * C), F32)] + [jax.ShapeDtypeStruct((G, L * C, P), F32)] * 4
                  + [jax.ShapeDtypeStruct((G, 1, P), F32)] * 2,
        compiler_params=_cparams(("arbitrary",)),
        name="ssm_prep",
    )(row(lam_re), row(lam_im), log_dt.reshape(G, 1, 1), bt_re, bt_im, c_re, c_im)

    R, GL = N_SSM_RANGES, SSM_RANGE_GROUPS

    def by_range(a, n_outer):
        n_inner = a.shape[1] // n_outer
        a = a.reshape(R, GL, n_outer, n_inner, LANES)
        return jnp.transpose(a, (0, 2, 1, 3, 4)).reshape(R, n_outer * GL * n_inner, LANES).astype(BF16)

    ek = by_range(kp, L)
    eb = by_range(jnp.concatenate([bpre, bpim], axis=-1), L)
    a_t = jnp.swapaxes(jnp.concatenate([are, -aim], axis=-1), 1, 2)
    ea = by_range(a_t, 2)

    half = RANGE_STATE // LANES
    lnr_t = lnre.reshape(R, half, LANES)
    lni_t = lnim.reshape(R, half, LANES)
    lnr = jnp.concatenate([lnr_t, lnr_t], axis=1)
    lni = jnp.concatenate([-lni_t, lni_t], axis=1)
    return ek, eb, ea, lnr, lni


def _ssm_select_matrices():
    q = np.arange(SSM_CHUNK * LANES)
    r = np.arange(LANES)
    sel_k = (r[:, None] // SSM_GROUP == q[None, :] // LANES) & (r[:, None] % SSM_GROUP == q[None, :] % SSM_GROUP)
    sel_b = (r[:, None] // SSM_STATE == q[None, :] // RANGE_STATE) & (r[:, None] % SSM_STATE == q[None, :] % SSM_STATE)
    return jnp.asarray(sel_k, BF16), jnp.asarray(sel_b, BF16)


HEADS_PER_BLOCK = COL_BLOCK // HEAD_DIM
_SEG_U = 0
_SEG_Q = SSM_WIDTH // COL_BLOCK
_SEG_K = _SEG_Q + QKV_WIDTH // COL_BLOCK
_SEG_V = _SEG_K + QKV_WIDTH // COL_BLOCK
_SEG_GATE = _SEG_V + QKV_WIDTH // COL_BLOCK
_SEG_END = IN_WIDTH // COL_BLOCK


def _head_norm(res, gain):
    outs = []
    for h in range(HEADS_PER_BLOCK):
        t = res[:, h * HEAD_DIM:(h + 1) * HEAD_DIM]
        ms = jnp.mean(t * t, axis=-1, keepdims=True)
        outs.append(t * lax.rsqrt(ms + EPS) * gain)
    return outs


def _rmsnorm_rows(x, gain):
    ms = jnp.mean(x * x, axis=-1, keepdims=True)
    return (x * lax.rsqrt(ms + EPS) * gain).astype(BF16)


def _rmsnorm_kernel(x_ref, g_ref, o_ref):
    o_ref[...] = _rmsnorm_rows(x_ref[...], g_ref[...])


def _rmsnorm(x, gain, tm):
    M, D = x.shape
    return pl.pallas_call(
        _rmsnorm_kernel,
        grid=(M // tm,),
        in_specs=[pl.BlockSpec((tm, D), lambda i: (i, 0)), pl.BlockSpec((1, D), lambda i: (0, 0))],
        out_specs=pl.BlockSpec((tm, D), lambda i: (i, 0)),
        out_shape=jax.ShapeDtypeStruct((M, D), BF16),
        compiler_params=_cparams(("arbitrary",)),
        name="rmsnorm",
    )(x, gain.reshape(1, D))


def _in_proj_kernel(h_ref, hs_ref, w_ref, qn_ref, kn_ref,
                    u_ref, qkv_ref, gate_ref, us_ref, qkvs_ref, gates_s_ref, w_scr):
    i = pl.program_id(0)
    j = pl.program_id(1)
    tm = h_ref.shape[0]
    sub = min(tm, ROW_SUB)

    def row_blocks():
        w_scr[...] = w_ref[...].astype(BF16)
        for r in range(tm // sub):
            rows = pl.ds(r * sub, sub)
            yield rows, _dot(h_ref[rows, :], w_scr[...])

    def sample_res():
        return _dot(hs_ref[...], w_scr[...])

    @pl.when(j < _SEG_Q)
    def _():
        for rows, res in row_blocks():
            u_ref[rows, :] = res

        @pl.when(i == 0)
        def _():
            us_ref[j] = sample_res()

    def split_heads(res, gain_ref):
        if gain_ref is not None:
            return _head_norm(res, gain_ref[...])
        return [res[:, h * HEAD_DIM:(h + 1) * HEAD_DIM] for h in range(HEADS_PER_BLOCK)]

    def store_heads(gain_ref):
        for rows, res in row_blocks():
            for h, t in enumerate(split_heads(res, gain_ref)):
                qkv_ref[h, rows, :] = t

        @pl.when(i == 0)
        def _():
            base = (j - _SEG_Q) * HEADS_PER_BLOCK
            for h, t in enumerate(split_heads(sample_res(), gain_ref)):
                qkvs_ref[base + h] = t

    pl.when((j >= _SEG_Q) & (j < _SEG_K))(functools.partial(store_heads, qn_ref))
    pl.when((j >= _SEG_K) & (j < _SEG_V))(functools.partial(store_heads, kn_ref))
    pl.when((j >= _SEG_V) & (j < _SEG_GATE))(functools.partial(store_heads, None))

    @pl.when(j >= _SEG_GATE)
    def _():
        for rows, res in row_blocks():
            gate_ref[rows, :] = jax.nn.sigmoid(res).astype(gate_ref.dtype)

        @pl.when(i == 0)
        def _():
            gates_s_ref[j - _SEG_GATE] = jax.nn.sigmoid(sample_res()).astype(gates_s_ref.dtype)


def _in_proj(h, hs, w, layer, q_gain, k_gain, tm):
    M, Ms = h.shape[0], hs.shape[0]
    n_heads = 3 * N_ATT_HEADS
    n_gate = _SEG_END - _SEG_GATE
    whole = lambda shape: pl.BlockSpec(shape, lambda i, j: (0,) * len(shape))
    u, qkv, gates, us, qkvs, gates_s = pl.pallas_call(
        _in_proj_kernel,
        grid=(M // tm, _SEG_END),
        in_specs=[
            pl.BlockSpec((tm, D_MODEL), lambda i, j: (i, 0), pipeline_mode=pl.Buffered(1)),
            whole((Ms, D_MODEL)),
            pl.BlockSpec((None, D_MODEL, COL_BLOCK), lambda i, j: (layer, 0, j)),
            pl.BlockSpec((1, HEAD_DIM), lambda i, j: (0, 0)),
            pl.BlockSpec((1, HEAD_DIM), lambda i, j: (0, 0)),
        ],
        out_specs=[
            pl.BlockSpec((tm, COL_BLOCK), lambda i, j: (i, jnp.clip(j, 0, _SEG_Q - 1))),
            pl.BlockSpec((HEADS_PER_BLOCK, tm, HEAD_DIM),
                         lambda i, j: (jnp.clip(j - _SEG_Q, 0, _SEG_GATE - _SEG_Q - 1), i, 0)),
            pl.BlockSpec((tm, COL_BLOCK), lambda i, j: (i, jnp.clip(j - _SEG_GATE, 0, _SEG_END - _SEG_GATE - 1))),
            whole((_SEG_Q, Ms, COL_BLOCK)),
            whole((n_heads, Ms, HEAD_DIM)),
            whole((n_gate, Ms, COL_BLOCK)),
        ],
        out_shape=[
            jax.ShapeDtypeStruct((M, SSM_WIDTH), F32),
            jax.ShapeDtypeStruct((n_heads, M, HEAD_DIM), F32),
            jax.ShapeDtypeStruct((M, 2 * D_MODEL), BF16),
            jax.ShapeDtypeStruct((_SEG_Q, Ms, COL_BLOCK), F32),
            jax.ShapeDtypeStruct((n_heads, Ms, HEAD_DIM), F32),
            jax.ShapeDtypeStruct((n_gate, Ms, COL_BLOCK), BF16),
        ],
        scratch_shapes=[pltpu.VMEM((D_MODEL, COL_BLOCK), BF16)],
        compiler_params=_cparams(("arbitrary", "arbitrary")),
        name="in_proj",
    )(h, hs, w, q_gain.reshape(1, HEAD_DIM), k_gain.reshape(1, HEAD_DIM))
    us = jnp.swapaxes(us, 0, 1).reshape(Ms, SSM_WIDTH)
    gates_s = jnp.swapaxes(gates_s, 0, 1).reshape(Ms, 2 * D_MODEL)
    return (u, qkv, gates), (us, qkvs, gates_s)


def _expand_block_diag(e_ref, sel_ref, row_shift, col_shift, out_scr):
    n = out_scr.shape[0]
    for c in range(n // LANES):
        rows = pl.ds(c * LANES, LANES)
        full = _dot(e_ref[rows, :], sel_ref[...])
        row = lax.broadcasted_iota(jnp.int32, full.shape, 0) + c * LANES
        col = lax.broadcasted_iota(jnp.int32, full.shape, 1)
        keep = ((row >> row_shift) & (SSM_RANGE_GROUPS - 1)) == ((col >> col_shift) & (SSM_RANGE_GROUPS - 1))
        out_scr[rows, :] = jnp.where(keep, full, 0.0).astype(BF16)


def _ssm_chunk_kernel(groups, *refs):
    n = len(groups)
    ins, refs = refs[:2 * n], refs[2 * n:]
    (ek_ref, eb_ref, ea_ref, selk_ref, selb_ref, lnr_ref, lni_ref), refs = refs[:7], refs[7:]
    outs, refs = refs[:2 * n], refs[2 * n:]
    s_scrs, (ktoe_ref, bcat_ref, acat_ref) = refs[:n], refs[n:]
    lg_c = SSM_GROUP.bit_length() - 1
    lg_p = SSM_STATE.bit_length() - 1
    _expand_block_diag(ek_ref, selk_ref, lg_c, lg_c, ktoe_ref)
    _expand_block_diag(eb_ref, selb_ref, lg_c, lg_p, bcat_ref)
    _expand_block_diag(ea_ref, selk_ref, lg_p, lg_c, acat_ref)
    for g, (nseq, nk) in enumerate(groups):
        _ssm_chunk_group(nseq, nk, ins[2 * g], ins[2 * g + 1], ktoe_ref, bcat_ref, acat_ref, lnr_ref, lni_ref,
                         outs[2 * g], outs[2 * g + 1], s_scrs[g])


def _ssm_chunk_group(nseq, nk, u_ref, h0_ref, ktoe_ref, bcat_ref, acat_ref, lnr_ref, lni_ref,
                     y_ref, hfin_ref, s_scr):
    L = SSM_CHUNK
    rows = nseq * nk
    nsub = 2 * RANGE_STATE // LANES
    ucat = jnp.concatenate([u_ref[pl.ds(j, rows, stride=L), :] for j in range(L)], axis=-1).astype(BF16)
    s = _dot(ucat, bcat_ref[...])
    for n in range(nsub):
        s_scr[pl.ds(n, rows, stride=nsub), :] = s[:, n * LANES:(n + 1) * LANES]
    y_ref_intra = _dot(ucat, ktoe_ref[...])

    lnr = lnr_ref[...]
    lni = lni_ref[...]

    def step(k, hs):
        new = []
        for q in range(nseq):
            off = pl.multiple_of((q * nk + k) * nsub, nsub)
            h = hs[q]
            sk = s_scr[pl.ds(off, nsub), :]
            s_scr[pl.ds(off, nsub), :] = h
            new.append(h * lnr + pltpu.roll(h, nsub // 2, axis=0) * lni + sk)
        return tuple(new)

    hs = lax.fori_loop(0, nk, step, tuple(h0_ref[q] for q in range(nseq)))
    for q in range(nseq):
        hfin_ref[q] = hs[q]

    hprev = jnp.concatenate([s_scr[pl.ds(n, rows, stride=nsub), :] for n in range(nsub)], axis=-1).astype(BF16)
    y = y_ref_intra + _dot(hprev, acat_ref[...])
    for j in range(L):
        y_ref[pl.ds(j, rows, stride=L), :] = y[:, j * LANES:(j + 1) * LANES]


def _ssm_chunk(token_groups, ek, eb, ea, sel_k, sel_b, lnr, lni):
    R = N_SSM_RANGES
    nsub = 2 * RANGE_STATE // LANES
    wide = SSM_CHUNK * LANES
    assert wide == 2 * RANGE_STATE
    wspec = lambda shape: pl.BlockSpec((None,) + shape, lambda r: (r, 0, 0))
    sel_spec = pl.BlockSpec((LANES, wide), lambda r: (0, 0))
    groups, args, in_specs, out_specs, out_shape, scratch = [], [], [], [], [], []
    for u, h0, nseq in token_groups:
        M = u.shape[0]
        nk = M // nseq // SSM_CHUNK
        groups.append((nseq, nk))
        args += [u, h0]
        tok_spec = pl.BlockSpec((M, LANES), lambda r: (0, r))
        state_spec = pl.BlockSpec((nseq, None, nsub, LANES), lambda r: (0, r, 0, 0))
        in_specs += [tok_spec, state_spec]
        out_specs += [tok_spec, state_spec]
        out_shape += [jax.ShapeDtypeStruct((M, SSM_WIDTH), F32), jax.ShapeDtypeStruct((nseq, R, nsub, LANES), F32)]
        scratch.append(pltpu.VMEM((nseq * nk * nsub, LANES), F32))
    res = pl.pallas_call(
        functools.partial(_ssm_chunk_kernel, tuple(groups)),
        grid=(R,),
        in_specs=in_specs + [wspec((wide, LANES))] * 3 + [sel_spec, sel_spec] + [wspec((nsub, LANES))] * 2,
        out_specs=out_specs,
        out_shape=out_shape,
        scratch_shapes=scratch + [pltpu.VMEM((wide, wide), BF16)] * 3,
        compiler_params=_cparams(("arbitrary",)),
        name="ssm_chunk",
    )(*args, ek, eb, ea, sel_k, sel_b, lnr, lni)
    return [(res[2 * g], res[2 * g + 1]) for g in range(len(token_groups))]


def _state_to_tiles(re, im):
    N = re.shape[0]
    half = RANGE_STATE // LANES
    return jnp.concatenate([re.reshape(N, N_SSM_RANGES, half, LANES),
                            im.reshape(N, N_SSM_RANGES, half, LANES)], axis=2)


def _tiles_to_state(t):
    N = t.shape[0]
    half = RANGE_STATE // LANES
    return (t[:, :, :half].reshape(N, N_SSM_GROUPS, SSM_STATE),
            t[:, :, half:].reshape(N, N_SSM_GROUPS, SSM_STATE))


def _bucket_np(dist):
    max_exact = N_BUCKETS // 2
    n = np.maximum(dist, 0)
    nf = np.maximum(n, 1).astype(np.float64)
    large = max_exact + (np.log(nf / max_exact) / math.log(REL_MAX_DISTANCE / max_exact)
                         * (N_BUCKETS - max_exact)).astype(np.int32)
    large = np.minimum(large, N_BUCKETS - 1)
    return np.where(n < max_exact, n, large)


def _prompt_bucket_tile_np(group):
    window, dil = WINDOWS[group]
    a = np.arange(Q_BLOCK)[:, None]
    c = np.arange(2 * Q_BLOCK)[None, :]
    rel = a - c + Q_BLOCK
    K = window // dil + 1
    valid = (rel >= 0) & (rel < K)
    return np.where(valid, _bucket_np(np.clip(rel, 0, K - 1) * dil), -1)


def _prompt_buckets_present(group):
    tile = _prompt_bucket_tile_np(group)
    return [int(t) for t in np.unique(tile[tile >= 0])]


def _prompt_bucket_tiles():
    return jnp.asarray(np.stack([_prompt_bucket_tile_np(g) for g in range(N_ATT_GROUPS)]), jnp.int32)


def _attn_group_of_step(step):
    return (step + ATTN_FIRST_GROUP) % N_ATT_GROUPS


def _attn_prompt_kernel(T, tab_ref, q_ref, k_ref, v_ref, bkt_ref, o_ref,
                        m_scr, l_scr, acc_scr, s_scr, p_scr, mb_scr, bias_scr):
    h = pl.program_id(1)
    step = pl.program_id(2)
    scale = HEAD_DIM ** -0.5
    nblk = T // Q_BLOCK

    def run_group(first, group):
        dil = WINDOWS[group][1]
        bkt = bkt_ref[...]
        col = group * HEADS_PER_GROUP + h
        bias = jnp.full(bkt.shape, NEG_INF, F32)
        for t in _prompt_buckets_present(group):
            bias = jnp.where(bkt == t, tab_ref[t, col], bias)
        bias_scr[...] = bias

        per_class = nblk // dil
        run = min(per_class, ATTN_UNROLL)
        runs_per_iter = ATTN_UNROLL // run
        whole_class = run == per_class

        def block_rows(r, n):
            return pl.ds(r + n * (Q_BLOCK * dil), Q_BLOCK, stride=dil)

        def runs(it):
            for j in range(runs_per_iter):
                i0 = it * ATTN_UNROLL + j * run
                yield i0, i0 // per_class, (0 if whole_class else i0 % per_class)

        def tiles(ref, r, n0, augment):
            out = []
            for u in range(-1, run):
                if u < 0 and whole_class:
                    out.append(None)
                    continue
                n = jnp.maximum(n0 + u, 0) if u < 0 else n0 + u
                t = ref[block_rows(r, n), :].astype(BF16)
                if augment:
                    t = jnp.concatenate([t, jnp.ones((Q_BLOCK, HEAD_DIM), BF16)], axis=1)
                out.append(t)
            return out

        def scores(it, carry):
            for i0, r, n0 in runs(it):
                kt = tiles(k_ref, r, n0, False)
                for u in range(run):
                    q = q_ref[block_rows(r, n0 + u), :].astype(BF16)
                    s_r = _dot_nt(q, kt[u + 1]) * scale + bias_scr[:, Q_BLOCK:]
                    if kt[u] is None:
                        s_l = jnp.full((Q_BLOCK, Q_BLOCK), NEG_INF, F32)
                    else:
                        bias_l = bias_scr[:, :Q_BLOCK]
                        if u == 0:
                            bias_l = jnp.where(n0 == 0, NEG_INF, bias_l)
                        s_l = _dot_nt(q, kt[u]) * scale + bias_l
                    s_scr[i0 + u, :, :Q_BLOCK] = s_l
                    s_scr[i0 + u, :, Q_BLOCK:] = s_r
                    m = jnp.maximum(jnp.max(s_l, axis=-1, keepdims=True), jnp.max(s_r, axis=-1, keepdims=True))
                    mb_scr[i0 + u] = jnp.broadcast_to(m, (Q_BLOCK, HEAD_DIM))
            return carry

        def probs(i, carry):
            mb = mb_scr[i]
            p_scr[i, :, :Q_BLOCK] = jnp.exp(s_scr[i, :, :Q_BLOCK] - mb).astype(BF16)
            p_scr[i, :, Q_BLOCK:] = jnp.exp(s_scr[i, :, Q_BLOCK:] - mb).astype(BF16)
            return carry

        def values(it, carry):
            for i0, r, n0 in runs(it):
                vt = tiles(v_ref, r, n0, True)
                for u in range(run):
                    i = i0 + u
                    ol = _dot(p_scr[i, :, Q_BLOCK:], vt[u + 1])
                    if vt[u] is not None:
                        ol = ol + _dot(p_scr[i, :, :Q_BLOCK], vt[u])
                    o = ol[:, :HEAD_DIM]
                    lb = ol[:, HEAD_DIM:]
                    mb = mb_scr[i]
                    sl_q = block_rows(r, n0 + u)
                    if first:
                        m_scr[sl_q, :] = mb
                        l_scr[sl_q, :] = lb
                        acc_scr[sl_q, :] = o
                    else:
                        m0 = m_scr[sl_q, :]
                        mn = jnp.maximum(m0, mb)
                        a0 = jnp.exp(m0 - mn)
                        a1 = jnp.exp(mb - mn)
                        m_scr[sl_q, :] = mn
                        l_scr[sl_q, :] = a0 * l_scr[sl_q, :] + a1 * lb
                        acc_scr[sl_q, :] = a0 * acc_scr[sl_q, :] + a1 * o
            return carry

        lax.fori_loop(0, nblk // ATTN_UNROLL, scores, 0)
        lax.fori_loop(0, nblk, probs, 0, unroll=ATTN_UNROLL)
        lax.fori_loop(0, nblk // ATTN_UNROLL, values, 0)

    for s in range(N_ATT_GROUPS):
        pl.when(step == s)(functools.partial(run_group, s == 0, (s + ATTN_FIRST_GROUP) % N_ATT_GROUPS))

    @pl.when(step == N_ATT_GROUPS - 1)
    def _():
        o_ref[...] = (acc_scr[...] / l_scr[...]).astype(o_ref.dtype)


def _attn_prompt(qkv, rel_bias, bucket_tiles, nbatch):
    M = qkv.shape[1]
    T = M // nbatch
    H, G = HEADS_PER_GROUP, N_ATT_GROUPS
    nblk = T // Q_BLOCK

    def qkv_spec(which):
        return pl.BlockSpec((None, T, HEAD_DIM),
                            lambda b, h, s: (which * N_ATT_HEADS + _attn_group_of_step(s) * H + h, b, 0))

    return pl.pallas_call(
        functools.partial(_attn_prompt_kernel, T),
        grid=(nbatch, H, G),
        in_specs=[pl.BlockSpec(memory_space=pltpu.SMEM),
                  qkv_spec(0), qkv_spec(1), qkv_spec(2),
                  pl.BlockSpec((None, Q_BLOCK, 2 * Q_BLOCK), lambda b, h, s: (_attn_group_of_step(s), 0, 0))],
        out_specs=pl.BlockSpec((T, HEAD_DIM), lambda b, h, s: (b, h)),
        out_shape=jax.ShapeDtypeStruct((M, ATT_WIDTH), BF16),
        scratch_shapes=[pltpu.VMEM((T, HEAD_DIM), F32)] * 3
                       + [pltpu.VMEM((nblk, Q_BLOCK, 2 * Q_BLOCK), F32),
                          pltpu.VMEM((nblk, Q_BLOCK, 2 * Q_BLOCK), BF16),
                          pltpu.VMEM((nblk, Q_BLOCK, HEAD_DIM), F32),
                          pltpu.VMEM((Q_BLOCK, 2 * Q_BLOCK), F32)],
        compiler_params=_cparams(("arbitrary", "arbitrary", "arbitrary")),
        name="attn_prompt",
    )(rel_bias.astype(F32), qkv, qkv, qkv, bucket_tiles)


def _attn_sample_kernel(window, dil, q_ref, ck_ref, cv_ref, nk_ref, nv_ref, bias_ref, o_ref, lse_ref):
    S = q_ref.shape[0]
    K = window // dil + 1
    scale = HEAD_DIM ** -0.5
    bias = bias_ref[...]
    by_residue = len(ck_ref.shape) == 4

    def buffered(ref, s, n):
        return ref[pl.ds(0, n), s] if by_residue else ref[pl.ds(s, n, stride=dil)]

    for s in range(S):
        n_c = (window - 1 - s) // dil + 1
        qs = q_ref[s]
        new_rows = [s + j * dil - window for j in range(n_c, K)]
        kk = jnp.concatenate([buffered(ck_ref, s, n_c)] + [nk_ref[pl.ds(i, 1)] for i in new_rows], axis=0)
        vv = jnp.concatenate([buffered(cv_ref, s, n_c)] + [nv_ref[pl.ds(i, 1)] for i in new_rows], axis=0)
        lg = jnp.sum(kk * qs[None], axis=-1, keepdims=True) * scale + bias
        m = jnp.max(lg, axis=0)
        p = jnp.exp(lg - m[None])
        l = jnp.sum(p, axis=0)
        o_ref[s] = jnp.sum(p * vv, axis=0) / l
        lse_ref[s] = m + jnp.log(l)


def _attn_sample(q, cache_k, cache_v, new_k, new_v, bias, layer, window, dil):
    B, S = q.shape[0], q.shape[1]
    H = HEADS_PER_GROUP
    K = window // dil + 1
    small = pl.BlockSpec((None, S, H, HEAD_DIM), lambda b: (b, 0, 0, 0))
    if dil > S:
        depth = cache_k.shape[0]
        cache_k = cache_k.reshape(depth, B, window // dil, dil, H, HEAD_DIM)
        cache_v = cache_v.reshape(depth, B, window // dil, dil, H, HEAD_DIM)
        cache = pl.BlockSpec((None, None, window // dil, S, H, HEAD_DIM), lambda b: (layer, b, 0, 0, 0, 0))
    else:
        cache = pl.BlockSpec((None, None, window, H, HEAD_DIM), lambda b: (layer, b, 0, 0, 0))
    return pl.pallas_call(
        functools.partial(_attn_sample_kernel, window, dil),
        grid=(B,),
        in_specs=[small, cache, cache, small, small, pl.BlockSpec((K, H, HEAD_DIM), lambda b: (0, 0, 0))],
        out_specs=[small, small],
        out_shape=[jax.ShapeDtypeStruct((B, S, H, HEAD_DIM), F32)] * 2,
        compiler_params=_cparams(("arbitrary",)),
        name="attn_sample_w%d" % window,
    )(q, cache_k, cache_v, new_k, new_v, bias)


def _merge_groups_kernel(o0, o1, o2, l0, l1, l2, y_ref):
    a, b, c = l0[...], l1[...], l2[...]
    m = jnp.maximum(jnp.maximum(a, b), c)
    ea, eb, ec = jnp.exp(a - m), jnp.exp(b - m), jnp.exp(c - m)
    y_ref[...] = ((ea * o0[...] + eb * o1[...] + ec * o2[...]) / (ea + eb + ec)).astype(y_ref.dtype)


def _merge_groups(outs, lses):
    shape = outs[0].shape
    return pl.pallas_call(
        _merge_groups_kernel,
        out_shape=jax.ShapeDtypeStruct(shape, BF16),
        name="merge_groups",
    )(*outs, *lses)


def _sample_bias(rel_bias):
    out = []
    for g, (window, dil) in enumerate(WINDOWS):
        K = window // dil + 1
        steps = (K - 1) - np.arange(K)
        tab = rel_bias[:, g * HEADS_PER_GROUP:(g + 1) * HEADS_PER_GROUP].astype(F32)
        b = tab[_bucket_np(steps * dil)]
        out.append(jnp.broadcast_to(b[:, :, None], (K, HEADS_PER_GROUP, HEAD_DIM)))
    return out


def _gelu_tanh(x):
    return 0.5 * x * (1.0 + jnp.tanh(math.sqrt(2.0 / math.pi) * (x + 0.044715 * (x * x * x))))


def _mix_out_kernel(y_ref, u_ref, yb_ref, ga_ref, gb_ref, x_ref, d_ref, wglu_ref, bglu_ref, wa_ref, wb_ref,
                    wout_ref, gn_ref, x1_ref, h2_ref):
    tm = x_ref.shape[0]
    sub = min(tm, MIX_ROW_SUB)
    for r in range(tm // sub):
        rows = pl.ds(r * sub, sub)
        y = y_ref[rows, :] + d_ref[...] * u_ref[rows, :]
        z = _gelu_tanh(y)
        ya = z * jax.nn.sigmoid(_dot(z.astype(BF16), wglu_ref[...]) + bglu_ref[...])
        mix = (ga_ref[rows, :] * _dot(ya.astype(BF16), wa_ref[...])
               + gb_ref[rows, :] * _dot(yb_ref[rows, :], wb_ref[...]))
        x1 = x_ref[rows, :] + _dot(mix.astype(BF16), wout_ref[...])
        x1_ref[rows, :] = x1
        h2_ref[rows, :] = _rmsnorm_rows(x1, gn_ref[...])


def _mix_out(y_ssm, u, yb, gates, x, ssm_d, wglu, bglu, wa, wb, wout, layer, gain, tm):
    M = x.shape[0]
    row = lambda w: pl.BlockSpec((tm, w), lambda i: (i, 0))
    vec = lambda b: pl.BlockSpec((1, b), lambda i: (0, 0), pipeline_mode=pl.Buffered(1))
    full = lambda a, b: pl.BlockSpec((None, a, b), lambda i: (layer, 0, 0), pipeline_mode=pl.Buffered(1))
    return pl.pallas_call(
        _mix_out_kernel,
        grid=(M // tm,),
        in_specs=[row(SSM_WIDTH), row(SSM_WIDTH), row(ATT_WIDTH),
                  pl.BlockSpec((tm, D_MODEL), lambda i: (i, 0)),
                  pl.BlockSpec((tm, D_MODEL), lambda i: (i, 1)),
                  row(D_MODEL),
                  vec(SSM_WIDTH), full(SSM_WIDTH, SSM_WIDTH), vec(SSM_WIDTH),
                  full(SSM_WIDTH, D_MODEL), full(ATT_WIDTH, D_MODEL), full(D_MODEL, D_MODEL), vec(D_MODEL)],
        out_specs=[row(D_MODEL), row(D_MODEL)],
        out_shape=[jax.ShapeDtypeStruct((M, D_MODEL), F32), jax.ShapeDtypeStruct((M, D_MODEL), BF16)],
        compiler_params=_cparams(("arbitrary",)),
        name="mix_out",
    )(y_ssm, u, yb, gates, gates, x, ssm_d.reshape(1, -1), wglu, bglu.reshape(1, -1), wa, wb, wout,
      gain.reshape(1, -1))


def _ffn_kernel(emit_norm, h_ref, x_ref, hs_ref, xs_ref, wg_ref, wu_ref, wd_ref, gn_ref, *rest):
    if emit_norm:
        o_ref, os_ref, hn_ref, hns_ref, wg_scr, wu_scr, wd_scr = rest
    else:
        o_ref, os_ref, wg_scr, wu_scr, wd_scr = rest
    i = pl.program_id(0)
    f = pl.program_id(1)
    last = pl.num_programs(1) - 1

    @pl.when(f == 0)
    def _():
        o_ref[...] = x_ref[...]

    @pl.when((f == 0) & (i == 0))
    def _():
        os_ref[...] = xs_ref[...]

    def swiglu(h):
        a = jax.nn.silu(_dot(h, wg_scr[...])) * _dot(h, wu_scr[...])
        return _dot(a.astype(BF16), wd_scr[...])

    tm = h_ref.shape[0]
    sub = min(tm, FFN_ROW_SUB)
    wg_scr[...] = wg_ref[...].astype(BF16)
    wu_scr[...] = wu_ref[...].astype(BF16)
    wd_scr[...] = wd_ref[...].astype(BF16)
    for r in range(tm // sub):
        rows = pl.ds(r * sub, sub)
        o_ref[rows, :] += swiglu(h_ref[rows, :])

    @pl.when(i == 0)
    def _():
        os_ref[...] += swiglu(hs_ref[...])

    if emit_norm:
        @pl.when(f == last)
        def _():
            for r in range(tm // sub):
                rows = pl.ds(r * sub, sub)
                hn_ref[rows, :] = _rmsnorm_rows(o_ref[rows, :], gn_ref[...])

        @pl.when((f == last) & (i == 0))
        def _():
            hns_ref[...] = _rmsnorm_rows(os_ref[...], gn_ref[...])


def _ffn(h2, x1, h2s, x1s, wg, wu, wd, layer, next_gain, tm):
    M, Ms = x1.shape[0], x1s.shape[0]
    nf = FFN_HIDDEN // FFN_BLOCK
    emit_norm = next_gain is not None
    gain = next_gain if emit_norm else jnp.ones((D_MODEL,), F32)
    row = pl.BlockSpec((tm, D_MODEL), lambda i, f: (i, 0))
    whole = pl.BlockSpec((Ms, D_MODEL), lambda i, f: (0, 0))
    out_specs = [row, whole]
    out_shape = [jax.ShapeDtypeStruct((M, D_MODEL), F32), jax.ShapeDtypeStruct((Ms, D_MODEL), F32)]
    if emit_norm:
        out_specs += [row, whole]
        out_shape += [jax.ShapeDtypeStruct((M, D_MODEL), BF16), jax.ShapeDtypeStruct((Ms, D_MODEL), BF16)]
    return pl.pallas_call(
        functools.partial(_ffn_kernel, emit_norm),
        grid=(M // tm, nf),
        in_specs=[row,
                  pl.BlockSpec((tm, D_MODEL), lambda i, f: (i, 0), pipeline_mode=pl.Buffered(1)),
                  whole, whole,
                  pl.BlockSpec((None, D_MODEL, FFN_BLOCK), lambda i, f: (layer, 0, f)),
                  pl.BlockSpec((None, D_MODEL, FFN_BLOCK), lambda i, f: (layer, 0, f)),
                  pl.BlockSpec((None, FFN_BLOCK, D_MODEL), lambda i, f: (layer, f, 0)),
                  pl.BlockSpec((1, D_MODEL), lambda i, f: (0, 0))],
        out_specs=out_specs,
        out_shape=out_shape,
        scratch_shapes=[pltpu.VMEM((D_MODEL, FFN_BLOCK), BF16), pltpu.VMEM((D_MODEL, FFN_BLOCK), BF16),
                        pltpu.VMEM((FFN_BLOCK, D_MODEL), BF16)],
        compiler_params=_cparams(("arbitrary", "arbitrary")),
        name="ffn",
    )(h2, x1, h2s, x1s, wg, wu, wd, gain.reshape(1, D_MODEL))


def _cache_shift_kernel(ck_ref, cv_ref, hk_ref, hv_ref, nk_ref, nv_ref, ok_ref, ov_ref):
    c = pl.program_id(2)
    last = pl.num_programs(2) - 1
    R = ck_ref.shape[0]
    S = nk_ref.shape[0]
    for cache, halo, new, out in ((ck_ref, hk_ref, nk_ref, ok_ref), (cv_ref, hv_ref, nv_ref, ov_ref)):
        out[pl.ds(0, R - S)] = cache[pl.ds(S, R - S)]

        @pl.when(c == last)
        def _():
            out[pl.ds(R - S, S)] = new[...]

        @pl.when(c < last)
        def _():
            out[pl.ds(R - S, S)] = halo[...]


def _cache_shift(cache_k, cache_v, new_k, new_v):
    depth, B, W, H, E = cache_k.shape
    S = new_k.shape[2]
    R = min(W, CACHE_ROWS)
    nchunks = W // R
    blk = pl.BlockSpec((None, None, R, H, E), lambda l, b, c: (l, b, c, 0, 0))
    halo = pl.BlockSpec((None, None, None, S, H, E),
                        lambda l, b, c: (l, b, jnp.minimum((c + 1) * (R // S), W // S - 1), 0, 0, 0))
    new = pl.BlockSpec((None, None, S, H, E), lambda l, b, c: (l, b, 0, 0, 0))
    as_rows = lambda a: a.reshape(depth, B, W // S, S, H, E)
    return pl.pallas_call(
        _cache_shift_kernel,
        grid=(depth, B, nchunks),
        in_specs=[blk, blk, halo, halo, new, new],
        out_specs=[blk, blk],
        out_shape=[jax.ShapeDtypeStruct(cache_k.shape, cache_k.dtype)] * 2,
        compiler_params=_cparams(("arbitrary", "arbitrary", "arbitrary")),
        name="cache_shift_w%d" % W,
    )(cache_k, cache_v, as_rows(cache_k), as_rows(cache_v), new_k, new_v)


def _kv_tails_kernel(*refs):
    depth = (len(refs) - 2) // 2
    ok_ref, ov_ref = refs[-2:]
    H, R = refs[0].shape[0], refs[0].shape[1]
    for l in range(depth):
        @pl.when(pl.program_id(0) == l)
        def _():
            for src, out in ((refs[2 * l], ok_ref), (refs[2 * l + 1], ov_ref)):
                for h in range(H):
                    out[pl.ds(h, R, stride=H), :] = src[h]


def _kv_tails(qkv_layers, g, window, nbatch):
    depth = len(qkv_layers)
    M = qkv_layers[0].shape[1]
    T = M // nbatch
    H, E = HEADS_PER_GROUP, HEAD_DIM
    R = min(window, CACHE_ROWS)
    nchunks = window // R
    first = (T - window) // R

    def src(layer, which):
        def index(l, b, c):
            bb = jnp.where(l < layer, 0, jnp.where(l > layer, nbatch - 1, b))
            cc = jnp.where(l < layer, 0, jnp.where(l > layer, nchunks - 1, c))
            return (which * N_ATT_GROUPS + g, bb * (T // R) + first + cc, 0)
        return pl.BlockSpec((H, R, E), index)

    args, in_specs = [], []
    for layer, qkv in enumerate(qkv_layers):
        args += [qkv, qkv]
        in_specs += [src(layer, 1), src(layer, 2)]
    out_spec = pl.BlockSpec((None, R * H, E), lambda l, b, c: (l, b * nchunks + c, 0))
    return pl.pallas_call(
        _kv_tails_kernel,
        grid=(depth, nbatch, nchunks),
        in_specs=in_specs,
        out_specs=[out_spec, out_spec],
        out_shape=[jax.ShapeDtypeStruct((depth, nbatch * window * H, E), qkv_layers[0].dtype)] * 2,
        compiler_params=_cparams(("arbitrary", "arbitrary", "arbitrary")),
        name="kv_tails_w%d" % window,
    )(*args)


def _mix(x2d, u, y_ssm, yb, gates, prm, tm):
    return _mix_out(y_ssm, u, yb, gates, x2d, prm["ssm_d"], prm["w_glu"], prm["b_glu"], prm["w_branch_a"],
                    prm["w_branch_b"], prm["w_out"], prm["layer"], prm["norm_ffn"], tm)


def kernel(x_prompt, x_sample, state_ssm_re, state_ssm_im, cache_k_w128, cache_v_w128, cache_k_w512, cache_v_w512,
           cache_k_w2048, cache_v_w2048, rel_bias, norm_mix, norm_ffn, q_norm, k_norm, w_in, ssm_lambda_re,
           ssm_lambda_im, ssm_log_dt, ssm_b_re, ssm_b_im, ssm_c_re, ssm_c_im, ssm_d, w_glu, b_glu, w_branch_a,
           w_branch_b, w_out, w_ffn_gate, w_ffn_up, w_ffn_down):
    depth = w_in.shape[0]
    B, T, _ = x_prompt.shape
    SB, S, _ = x_sample.shape
    H = HEADS_PER_GROUP
    caches_k = (cache_k_w128, cache_k_w512, cache_k_w2048)
    caches_v = (cache_v_w128, cache_v_w512, cache_v_w2048)

    bucket_tiles = _prompt_bucket_tiles()
    bias_sample = _sample_bias(rel_bias)
    sel_k, sel_b = _ssm_select_matrices()

    xp = x_prompt.reshape(B * T, D_MODEL)
    xs = x_sample.reshape(SB * S, D_MODEL)
    p_re, p_im, s_re, s_im = [], [], [], []
    qkv_prompt = []
    new_k = [[] for _ in WINDOWS]
    new_v = [[] for _ in WINDOWS]

    weights = {
        "w_glu": w_glu.astype(BF16), "w_branch_a": w_branch_a.astype(BF16),
        "w_branch_b": w_branch_b.astype(BF16), "w_out": w_out.astype(BF16),
    }

    hp = _rmsnorm(xp, norm_mix[0], FFN_ROWS)
    hs = _rmsnorm(xs, norm_mix[0], SB * S)

    for l in range(depth):
        ek, eb, ea, lnr, lni = _ssm_prep(ssm_lambda_re[l], ssm_lambda_im[l], ssm_log_dt[l], ssm_b_re[l],
                                         ssm_b_im[l], ssm_c_re[l], ssm_c_im[l])
        prm = dict(weights)
        prm.update({
            "layer": l, "norm_ffn": norm_ffn[l], "ssm_d": ssm_d[l], "b_glu": b_glu[l],
        })

        (u, qkv, gates), (us, qkvs, gates_s) = _in_proj(hp, hs, w_in, l, q_norm[l], k_norm[l], IN_PROJ_ROWS)

        zeros = jnp.zeros((B, N_SSM_RANGES, 2 * RANGE_STATE // LANES, LANES), F32)
        h0 = _state_to_tiles(state_ssm_re[l], state_ssm_im[l])
        (y_ssm, hfin), (ys_ssm, hfin_s) = _ssm_chunk([(u, zeros, B), (us, h0, SB)], ek, eb, ea, sel_k, sel_b,
                                                     lnr, lni)

        yb = _attn_prompt(qkv, rel_bias, bucket_tiles, B)
        x1, h2 = _mix(xp, u, y_ssm, yb, gates, prm, MIX_ROWS)
        hr, hi = _tiles_to_state(hfin)
        p_re.append(hr)
        p_im.append(hi)
        qkv_prompt.append(qkv)

        tok = jnp.transpose(qkvs.reshape(3, N_ATT_GROUPS, H, SB, S, HEAD_DIM), (0, 1, 3, 4, 2, 5))
        outs, lses = [], []
        for g, (window, dil) in enumerate(WINDOWS):
            nk_g, nv_g = tok[1, g], tok[2, g]
            new_k[g].append(nk_g)
            new_v[g].append(nv_g)
            o_g, lse_g = _attn_sample(tok[0, g], caches_k[g], caches_v[g], nk_g, nv_g, bias_sample[g], l,
                                      window, dil)
            outs.append(o_g)
            lses.append(lse_g)
        ybs = _merge_groups(outs, lses).reshape(SB * S, ATT_WIDTH)
        x1s, h2s = _mix(xs, us, ys_ssm, ybs, gates_s, prm, SB * S)
        hr, hi = _tiles_to_state(hfin_s)
        s_re.append(hr)
        s_im.append(hi)

        next_gain = norm_mix[l + 1] if l + 1 < depth else None
        res = _ffn(h2, x1, h2s, x1s, w_ffn_gate, w_ffn_up, w_ffn_down, l, next_gain, FFN_ROWS)
        if next_gain is None:
            xp, xs = res
        else:
            xp, xs, hp, hs = res

    shifted = []
    for g in range(N_ATT_GROUPS):
        shifted += _cache_shift(caches_k[g], caches_v[g], jnp.stack(new_k[g]), jnp.stack(new_v[g]))
    tails = [t.reshape(depth, B, window, H, HEAD_DIM)
             for g, (window, _) in enumerate(WINDOWS) for t in _kv_tails(qkv_prompt, g, window, B)]

    return (xp.reshape(B, T, D_MODEL), xs.reshape(SB, S, D_MODEL),
            jnp.stack(p_re), jnp.stack(p_im), *tails,
            jnp.stack(s_re), jnp.stack(s_im), *shifted)
```

```python
import functools
import math

import numpy as np
import jax
import jax.numpy as jnp
from jax import lax
from jax.experimental import pallas as pl
from jax.experimental.pallas import tpu as pltpu

F32 = jnp.float32
BF16 = jnp.bfloat16

D_MODEL = 2048
HEAD_DIM = 128
HEADS_PER_GROUP = 8
WINDOWS = ((128, 1), (512, 4), (2048, 16))
N_ATT_GROUPS = len(WINDOWS)
N_ATT_HEADS = N_ATT_GROUPS * HEADS_PER_GROUP
ATT_WIDTH = HEADS_PER_GROUP * HEAD_DIM
QKV_WIDTH = N_ATT_HEADS * HEAD_DIM
Q_BLOCK = 128
SSM_WIDTH = D_MODEL // 2
SSM_GROUP = 16
N_SSM_GROUPS = SSM_WIDTH // SSM_GROUP
SSM_STATE = 64
IN_WIDTH = SSM_WIDTH + 3 * QKV_WIDTH + 2 * D_MODEL
FFN_HIDDEN = 5632
N_BUCKETS = 32
REL_MAX_DISTANCE = 2048
EPS = 1e-6
NEG_INF = -1e30

LANES = 128
SUBLANES = 8
VMEM_LIMIT_BYTES = 60 * 1024 * 1024

SSM_CHUNK = 8
SSM_RANGE_GROUPS = LANES // SSM_GROUP
N_SSM_RANGES = N_SSM_GROUPS // SSM_RANGE_GROUPS
RANGE_STATE = SSM_RANGE_GROUPS * SSM_STATE
COL_BLOCK = 512
FFN_BLOCK = 256
ATTN_UNROLL = 16
CACHE_ROWS = 1024
IN_PROJ_ROWS = 2048
FFN_ROWS = 1024
MIX_ROWS = 256
ROW_SUB = 256
FFN_ROW_SUB = 512
MIX_ROW_SUB = 256


def _cparams(sem):
    return pltpu.CompilerParams(dimension_semantics=sem, vmem_limit_bytes=VMEM_LIMIT_BYTES)


def _dot(a, b):
    return jnp.dot(a, b, preferred_element_type=F32)


def _dot_nt(a, b):
    return lax.dot_general(a, b, (((1,), (1,)), ((), ())), preferred_element_type=F32)


def _ssm_prep_kernel(*refs):
    for g in range(SSM_RANGE_GROUPS):
        _ssm_prep_group(*[ref.at[g] for ref in refs])


def _ssm_prep_group(lre_ref, lim_ref, ldt_ref, btre_ref, btim_ref, cre_ref, cim_ref,
                    kp_ref, bpre_ref, bpim_ref, are_ref, aim_ref, lnre_ref, lnim_ref):
    L = SSM_CHUNK
    lr = lre_ref[...]
    li = lim_ref[...]
    dt = jnp.exp(ldt_ref[...])
    er = jnp.exp(lr * dt)
    lbr = er * jnp.cos(li * dt)
    lbi = er * jnp.sin(li * dt)
    nr = lbr - 1.0
    dd = lr * lr + li * li
    rr = (nr * lr + lbi * li) / dd
    ri = (lbi * lr - nr * li) / dd
    btr = btre_ref[...]
    bti = btim_ref[...]
    bbr = rr * btr - ri * bti
    bbi = rr * bti + ri * btr
    cr = cre_ref[...]
    ci = cim_ref[...]
    pr = [jnp.ones_like(lbr)]
    pi = [jnp.zeros_like(lbr)]
    for _ in range(L):
        pr.append(pr[-1] * lbr - pi[-1] * lbi)
        pi.append(pr[-2] * lbi + pi[-1] * lbr)
    xr = [cr * pr[t] - ci * pi[t] for t in range(L + 1)]
    xi = [cr * pi[t] + ci * pr[t] for t in range(L + 1)]
    xr_k = jnp.concatenate(xr[:L], axis=0)
    xi_k = jnp.concatenate(xi[:L], axis=0)
    hp = lax.Precision.HIGHEST
    kp = (lax.dot_general(bbr, xr_k, (((1,), (1,)), ((), ())), precision=hp, preferred_element_type=F32)
          - lax.dot_general(bbi, xi_k, (((1,), (1,)), ((), ())), precision=hp, preferred_element_type=F32))
    lane = lax.broadcasted_iota(jnp.int32, kp.shape, 1)
    blocks = [kp] + [jnp.where(lane >= j * SSM_GROUP, pltpu.roll(kp, j * SSM_GROUP, axis=1), 0.0)
                     for j in range(1, L)]
    kp_ref[...] = jnp.concatenate(blocks, axis=0)
    are_ref[...] = jnp.concatenate(xr[1:], axis=0)
    aim_ref[...] = jnp.concatenate(xi[1:], axis=0)
    bpre_ref[...] = jnp.concatenate([pr[L - 1 - j] * bbr - pi[L - 1 - j] * bbi for j in range(L)], axis=0)
    bpim_ref[...] = jnp.concatenate([pr[L - 1 - j] * bbi + pi[L - 1 - j] * bbr for j in range(L)], axis=0)
    lnre_ref[...] = pr[L]
    lnim_ref[...] = pi[L]


def _ssm_prep(lam_re, lam_im, log_dt, b_re, b_im, c_re, c_im):
    G, P, C, L = N_SSM_GROUPS, SSM_STATE, SSM_GROUP, SSM_CHUNK
    row = lambda a: a.reshape(G, 1, -1)
    bt_re = jnp.swapaxes(b_re, 1, 2)
    bt_im = jnp.swapaxes(b_im, 1, 2)
    GL = SSM_RANGE_GROUPS
    vec = pl.BlockSpec((GL, 1, P), lambda r: (r, 0, 0))
    mat = pl.BlockSpec((GL, C, P), lambda r: (r, 0, 0))
    big = pl.BlockSpec((GL, L * C, P), lambda r: (r, 0, 0))
    kp, bpre, bpim, are, aim, lnre, lnim = pl.pallas_call(
        _ssm_prep_kernel,
        grid=(G // GL,),
        in_specs=[vec, vec, pl.BlockSpec((GL, 1, 1), lambda r: (r, 0, 0)), mat, mat, mat, mat],
        out_specs=[pl.BlockSpec((GL, L * C, L * C), lambda r: (r, 0, 0)), big, big, big, big, vec, vec],
        out_shape=[jax.ShapeDtypeStruct((G, L * C, L * C), F32)] + [jax.ShapeDtypeStruct((G, L * C, P), F32)] * 4
                  + [jax.ShapeDtypeStruct((G, 1, P), F32)] * 2,
        compiler_params=_cparams(("arbitrary",)),
        name="ssm_prep",
    )(row(lam_re), row(lam_im), log_dt.reshape(G, 1, 1), bt_re, bt_im, c_re, c_im)

    R, GL = N_SSM_RANGES, SSM_RANGE_GROUPS

    def by_range(a, n_outer):
        n_inner = a.shape[1] // n_outer
        a = a.reshape(R, GL, n_outer, n_inner, LANES)
        return jnp.transpose(a, (0, 2, 1, 3, 4)).reshape(R, n_outer * GL * n_inner, LANES).astype(BF16)

    ek = by_range(kp, L)
    eb = by_range(jnp.concatenate([bpre, bpim], axis=-1), L)
    a_t = jnp.swapaxes(jnp.concatenate([are, -aim], axis=-1), 1, 2)
    ea = by_range(a_t, 2)

    half = RANGE_STATE // LANES
    lnr_t = lnre.reshape(R, half, LANES)
    lni_t = lnim.reshape(R, half, LANES)
    lnr = jnp.concatenate([lnr_t, lnr_t], axis=1)
    lni = jnp.concatenate([-lni_t, lni_t], axis=1)
    return ek, eb, ea, lnr, lni


def _ssm_select_matrices():
    q = np.arange(SSM_CHUNK * LANES)
    r = np.arange(LANES)
    sel_k = (r[:, None] // SSM_GROUP == q[None, :] // LANES) & (r[:, None] % SSM_GROUP == q[None, :] % SSM_GROUP)
    sel_b = (r[:, None] // SSM_STATE == q[None, :] // RANGE_STATE) & (r[:, None] % SSM_STATE == q[None, :] % SSM_STATE)
    return jnp.asarray(sel_k, BF16), jnp.asarray(sel_b, BF16)


HEADS_PER_BLOCK = COL_BLOCK // HEAD_DIM
_SEG_U = 0
_SEG_Q = SSM_WIDTH // COL_BLOCK
_SEG_K = _SEG_Q + QKV_WIDTH // COL_BLOCK
_SEG_V = _SEG_K + QKV_WIDTH // COL_BLOCK
_SEG_GATE = _SEG_V + QKV_WIDTH // COL_BLOCK
_SEG_END = IN_WIDTH // COL_BLOCK


def _head_norm(res, gain):
    outs = []
    for h in range(HEADS_PER_BLOCK):
        t = res[:, h * HEAD_DIM:(h + 1) * HEAD_DIM]
        ms = jnp.mean(t * t, axis=-1, keepdims=True)
        outs.append(t * lax.rsqrt(ms + EPS) * gain)
    return outs


def _rmsnorm_rows(x, gain):
    ms = jnp.mean(x * x, axis=-1, keepdims=True)
    return (x * lax.rsqrt(ms + EPS) * gain).astype(BF16)


def _rmsnorm_kernel(x_ref, g_ref, o_ref):
    o_ref[...] = _rmsnorm_rows(x_ref[...], g_ref[...])


def _rmsnorm(x, gain, tm):
    M, D = x.shape
    return pl.pallas_call(
        _rmsnorm_kernel,
        grid=(M // tm,),
        in_specs=[pl.BlockSpec((tm, D), lambda i: (i, 0)), pl.BlockSpec((1, D), lambda i: (0, 0))],
        out_specs=pl.BlockSpec((tm, D), lambda i: (i, 0)),
        out_shape=jax.ShapeDtypeStruct((M, D), BF16),
        compiler_params=_cparams(("arbitrary",)),
        name="rmsnorm",
    )(x, gain.reshape(1, D))


def _in_proj_kernel(h_ref, hs_ref, w_ref, qn_ref, kn_ref,
                    u_ref, qkv_ref, gate_ref, us_ref, qkvs_ref, gates_s_ref, w_scr):
    i = pl.program_id(0)
    j = pl.program_id(1)
    tm = h_ref.shape[0]
    sub = min(tm, ROW_SUB)

    def row_blocks():
        w_scr[...] = w_ref[...].astype(BF16)
        for r in range(tm // sub):
            rows = pl.ds(r * sub, sub)
            yield rows, _dot(h_ref[rows, :], w_scr[...])

    def sample_res():
        return _dot(hs_ref[...], w_scr[...])

    @pl.when(j < _SEG_Q)
    def _():
        for rows, res in row_blocks():
            u_ref[rows, :] = res

        @pl.when(i == 0)
        def _():
            us_ref[j] = sample_res()

    def split_heads(res, gain_ref):
        if gain_ref is not None:
            return _head_norm(res, gain_ref[...])
        return [res[:, h * HEAD_DIM:(h + 1) * HEAD_DIM] for h in range(HEADS_PER_BLOCK)]

    def store_heads(gain_ref):
        for rows, res in row_blocks():
            for h, t in enumerate(split_heads(res, gain_ref)):
                qkv_ref[h, rows, :] = t

        @pl.when(i == 0)
        def _():
            base = (j - _SEG_Q) * HEADS_PER_BLOCK
            for h, t in enumerate(split_heads(sample_res(), gain_ref)):
                qkvs_ref[base + h] = t

    pl.when((j >= _SEG_Q) & (j < _SEG_K))(functools.partial(store_heads, qn_ref))
    pl.when((j >= _SEG_K) & (j < _SEG_V))(functools.partial(store_heads, kn_ref))
    pl.when((j >= _SEG_V) & (j < _SEG_GATE))(functools.partial(store_heads, None))

    @pl.when(j >= _SEG_GATE)
    def _():
        for rows, res in row_blocks():
            gate_ref[rows, :] = jax.nn.sigmoid(res).astype(gate_ref.dtype)

        @pl.when(i == 0)
        def _():
            gates_s_ref[j - _SEG_GATE] = jax.nn.sigmoid(sample_res()).astype(gates_s_ref.dtype)


def _in_proj(h, hs, w, layer, q_gain, k_gain, tm):
    M, Ms = h.shape[0], hs.shape[0]
    n_heads = 3 * N_ATT_HEADS
    n_gate = _SEG_END - _SEG_GATE
    whole = lambda shape: pl.BlockSpec(shape, lambda i, j: (0,) * len(shape))
    u, qkv, gates, us, qkvs, gates_s = pl.pallas_call(
        _in_proj_kernel,
        grid=(M // tm, _SEG_END),
        in_specs=[
            pl.BlockSpec((tm, D_MODEL), lambda i, j: (i, 0), pipeline_mode=pl.Buffered(1)),
            whole((Ms, D_MODEL)),
            pl.BlockSpec((None, D_MODEL, COL_BLOCK), lambda i, j: (layer, 0, j)),
            pl.BlockSpec((1, HEAD_DIM), lambda i, j: (0, 0)),
            pl.BlockSpec((1, HEAD_DIM), lambda i, j: (0, 0)),
        ],
        out_specs=[
            pl.BlockSpec((tm, COL_BLOCK), lambda i, j: (i, jnp.clip(j, 0, _SEG_Q - 1))),
            pl.BlockSpec((HEADS_PER_BLOCK, tm, HEAD_DIM),
                         lambda i, j: (jnp.clip(j - _SEG_Q, 0, _SEG_GATE - _SEG_Q - 1), i, 0)),
            pl.BlockSpec((tm, COL_BLOCK), lambda i, j: (i, jnp.clip(j - _SEG_GATE, 0, _SEG_END - _SEG_GATE - 1))),
            whole((_SEG_Q, Ms, COL_BLOCK)),
            whole((n_heads, Ms, HEAD_DIM)),
            whole((n_gate, Ms, COL_BLOCK)),
        ],
        out_shape=[
            jax.ShapeDtypeStruct((M, SSM_WIDTH), F32),
            jax.ShapeDtypeStruct((n_heads, M, HEAD_DIM), F32),
            jax.ShapeDtypeStruct((M, 2 * D_MODEL), BF16),
            jax.ShapeDtypeStruct((_SEG_Q, Ms, COL_BLOCK), F32),
            jax.ShapeDtypeStruct((n_heads, Ms, HEAD_DIM), F32),
            jax.ShapeDtypeStruct((n_gate, Ms, COL_BLOCK), BF16),
        ],
        scratch_shapes=[pltpu.VMEM((D_MODEL, COL_BLOCK), BF16)],
        compiler_params=_cparams(("arbitrary", "arbitrary")),
        name="in_proj",
    )(h, hs, w, q_gain.reshape(1, HEAD_DIM), k_gain.reshape(1, HEAD_DIM))
    us = jnp.swapaxes(us, 0, 1).reshape(Ms, SSM_WIDTH)
    gates_s = jnp.swapaxes(gates_s, 0, 1).reshape(Ms, 2 * D_MODEL)
    return (u, qkv, gates), (us, qkvs, gates_s)


def _expand_block_diag(e_ref, sel_ref, row_shift, col_shift, out_scr):
    n = out_scr.shape[0]
    for c in range(n // LANES):
        rows = pl.ds(c * LANES, LANES)
        full = _dot(e_ref[rows, :], sel_ref[...])
        row = lax.broadcasted_iota(jnp.int32, full.shape, 0) + c * LANES
        col = lax.broadcasted_iota(jnp.int32, full.shape, 1)
        keep = ((row >> row_shift) & (SSM_RANGE_GROUPS - 1)) == ((col >> col_shift) & (SSM_RANGE_GROUPS - 1))
        out_scr[rows, :] = jnp.where(keep, full, 0.0).astype(BF16)


def _ssm_chunk_kernel(groups, *refs):
    n = len(groups)
    ins, refs = refs[:2 * n], refs[2 * n:]
    (ek_ref, eb_ref, ea_ref, selk_ref, selb_ref, lnr_ref, lni_ref), refs = refs[:7], refs[7:]
    outs, refs = refs[:2 * n], refs[2 * n:]
    s_scrs, (ktoe_ref, bcat_ref, acat_ref) = refs[:n], refs[n:]
    lg_c = SSM_GROUP.bit_length() - 1
    lg_p = SSM_STATE.bit_length() - 1
    _expand_block_diag(ek_ref, selk_ref, lg_c, lg_c, ktoe_ref)
    _expand_block_diag(eb_ref, selb_ref, lg_c, lg_p, bcat_ref)
    _expand_block_diag(ea_ref, selk_ref, lg_p, lg_c, acat_ref)
    for g, (nseq, nk) in enumerate(groups):
        _ssm_chunk_group(nseq, nk, ins[2 * g], ins[2 * g + 1], ktoe_ref, bcat_ref, acat_ref, lnr_ref, lni_ref,
                         outs[2 * g], outs[2 * g + 1], s_scrs[g])


def _ssm_chunk_group(nseq, nk, u_ref, h0_ref, ktoe_ref, bcat_ref, acat_ref, lnr_ref, lni_ref,
                     y_ref, hfin_ref, s_scr):
    L = SSM_CHUNK
    rows = nseq * nk
    nsub = 2 * RANGE_STATE // LANES
    ucat = jnp.concatenate([u_ref[pl.ds(j, rows, stride=L), :] for j in range(L)], axis=-1).astype(BF16)
    s = _dot(ucat, bcat_ref[...])
    for n in range(nsub):
        s_scr[pl.ds(n, rows, stride=nsub), :] = s[:, n * LANES:(n + 1) * LANES]
    y_ref_intra = _dot(ucat, ktoe_ref[...])

    lnr = lnr_ref[...]
    lni = lni_ref[...]

    def step(k, hs):
        new = []
        for q in range(nseq):
            off = pl.multiple_of((q * nk + k) * nsub, nsub)
            h = hs[q]
            sk = s_scr[pl.ds(off, nsub), :]
            s_scr[pl.ds(off, nsub), :] = h
            new.append(h * lnr + pltpu.roll(h, nsub // 2, axis=0) * lni + sk)
        return tuple(new)

    hs = lax.fori_loop(0, nk, step, tuple(h0_ref[q] for q in range(nseq)))
    for q in range(nseq):
        hfin_ref[q] = hs[q]

    hprev = jnp.concatenate([s_scr[pl.ds(n, rows, stride=nsub), :] for n in range(nsub)], axis=-1).astype(BF16)
    y = y_ref_intra + _dot(hprev, acat_ref[...])
    for j in range(L):
        y_ref[pl.ds(j, rows, stride=L), :] = y[:, j * LANES:(j + 1) * LANES]


def _ssm_chunk(token_groups, ek, eb, ea, sel_k, sel_b, lnr, lni):
    R = N_SSM_RANGES
    nsub = 2 * RANGE_STATE // LANES
    wide = SSM_CHUNK * LANES
    assert wide == 2 * RANGE_STATE
    wspec = lambda shape: pl.BlockSpec((None,) + shape, lambda r: (r, 0, 0))
    sel_spec = pl.BlockSpec((LANES, wide), lambda r: (0, 0))
    groups, args, in_specs, out_specs, out_shape, scratch = [], [], [], [], [], []
    for u, h0, nseq in token_groups:
        M = u.shape[0]
        nk = M // nseq // SSM_CHUNK
        groups.append((nseq, nk))
        args += [u, h0]
        tok_spec = pl.BlockSpec((M, LANES), lambda r: (0, r))
        state_spec = pl.BlockSpec((nseq, None, nsub, LANES), lambda r: (0, r, 0, 0))
        in_specs += [tok_spec, state_spec]
        out_specs += [tok_spec, state_spec]
        out_shape += [jax.ShapeDtypeStruct((M, SSM_WIDTH), F32), jax.ShapeDtypeStruct((nseq, R, nsub, LANES), F32)]
        scratch.append(pltpu.VMEM((nseq * nk * nsub, LANES), F32))
    res = pl.pallas_call(
        functools.partial(_ssm_chunk_kernel, tuple(groups)),
        grid=(R,),
        in_specs=in_specs + [wspec((wide, LANES))] * 3 + [sel_spec, sel_spec] + [wspec((nsub, LANES))] * 2,
        out_specs=out_specs,
        out_shape=out_shape,
        scratch_shapes=scratch + [pltpu.VMEM((wide, wide), BF16)] * 3,
        compiler_params=_cparams(("arbitrary",)),
        name="ssm_chunk",
    )(*args, ek, eb, ea, sel_k, sel_b, lnr, lni)
    return [(res[2 * g], res[2 * g + 1]) for g in range(len(token_groups))]


def _state_to_tiles(re, im):
    N = re.shape[0]
    half = RANGE_STATE // LANES
    return jnp.concatenate([re.reshape(N, N_SSM_RANGES, half, LANES),
                            im.reshape(N, N_SSM_RANGES, half, LANES)], axis=2)


def _tiles_to_state(t):
    N = t.shape[0]
    half = RANGE_STATE // LANES
    return (t[:, :, :half].reshape(N, N_SSM_GROUPS, SSM_STATE),
            t[:, :, half:].reshape(N, N_SSM_GROUPS, SSM_STATE))


def _bucket_np(dist):
    max_exact = N_BUCKETS // 2
    n = np.maximum(dist, 0)
    nf = np.maximum(n, 1).astype(np.float64)
    large = max_exact + (np.log(nf / max_exact) / math.log(REL_MAX_DISTANCE / max_exact)
                         * (N_BUCKETS - max_exact)).astype(np.int32)
    large = np.minimum(large, N_BUCKETS - 1)
    return np.where(n < max_exact, n, large)


def _prompt_bucket_tile_np(group):
    window, dil = WINDOWS[group]
    a = np.arange(Q_BLOCK)[:, None]
    c = np.arange(2 * Q_BLOCK)[None, :]
    rel = a - c + Q_BLOCK
    K = window // dil + 1
    valid = (rel >= 0) & (rel < K)
    return np.where(valid, _bucket_np(np.clip(rel, 0, K - 1) * dil), -1)


def _prompt_buckets_present(group):
    tile = _prompt_bucket_tile_np(group)
    return [int(t) for t in np.unique(tile[tile >= 0])]


def _prompt_bucket_tiles():
    return jnp.asarray(np.stack([_prompt_bucket_tile_np(g) for g in range(N_ATT_GROUPS)]), jnp.int32)


def _attn_group_of_step(step):
    return N_ATT_GROUPS - 1 - step


def _attn_prompt_kernel(T, tab_ref, q_ref, k_ref, v_ref, bkt_ref, o_ref,
                        m_scr, l_scr, acc_scr, s_scr, p_scr, mb_scr, lb_scr, bias_scr):
    h = pl.program_id(1)
    step = pl.program_id(2)
    scale = HEAD_DIM ** -0.5
    nblk = T // Q_BLOCK

    def run_group(first, last, group):
        dil = WINDOWS[group][1]
        assert not last or dil == 1
        bkt = bkt_ref[...]
        col = group * HEADS_PER_GROUP + h
        bias = jnp.full(bkt.shape, NEG_INF, F32)
        for t in _prompt_buckets_present(group):
            bias = jnp.where(bkt == t, tab_ref[t, col], bias)
        bias_scr[...] = bias

        per_class = nblk // dil
        run = min(per_class, ATTN_UNROLL)
        runs_per_iter = ATTN_UNROLL // run
        whole_class = run == per_class

        def block_rows(r, n):
            if dil == 1:
                return pl.ds(pl.multiple_of(n * Q_BLOCK, Q_BLOCK), Q_BLOCK)
            return pl.ds(r + n * (Q_BLOCK * dil), Q_BLOCK, stride=dil)

        def runs(it):
            for j in range(runs_per_iter):
                i0 = it * ATTN_UNROLL + j * run
                yield i0, i0 // per_class, (0 if whole_class else i0 % per_class)

        def tiles(ref, r, n0):
            out = []
            for u in range(-1, run):
                if u < 0 and whole_class:
                    out.append(None)
                    continue
                n = jnp.maximum(n0 + u, 0) if u < 0 else n0 + u
                out.append(ref[block_rows(r, n), :].astype(BF16))
            return out

        def scores(it, carry):
            for i0, r, n0 in runs(it):
                kt = tiles(k_ref, r, n0)
                for u in range(run):
                    q = q_ref[block_rows(r, n0 + u), :].astype(BF16)
                    s_r = _dot_nt(q, kt[u + 1]) * scale + bias_scr[:, Q_BLOCK:]
                    if kt[u] is None:
                        s_l = jnp.full((Q_BLOCK, Q_BLOCK), NEG_INF, F32)
                    else:
                        bias_l = bias_scr[:, :Q_BLOCK]
                        if u == 0:
                            bias_l = jnp.where(n0 == 0, NEG_INF, bias_l)
                        s_l = _dot_nt(q, kt[u]) * scale + bias_l
                    s_scr[i0 + u, :, :Q_BLOCK] = s_l
                    s_scr[i0 + u, :, Q_BLOCK:] = s_r
                    m = jnp.maximum(jnp.max(s_l, axis=-1, keepdims=True), jnp.max(s_r, axis=-1, keepdims=True))
                    mb_scr[i0 + u] = jnp.broadcast_to(m, (Q_BLOCK, HEAD_DIM))
            return carry

        def probs(i, carry):
            mb = mb_scr[i]
            p_l = jnp.exp(s_scr[i, :, :Q_BLOCK] - mb)
            p_r = jnp.exp(s_scr[i, :, Q_BLOCK:] - mb)
            p_scr[i, :, :Q_BLOCK] = p_l.astype(BF16)
            p_scr[i, :, Q_BLOCK:] = p_r.astype(BF16)
            lb_scr[i] = jnp.broadcast_to(jnp.sum(p_l + p_r, axis=-1, keepdims=True), (Q_BLOCK, HEAD_DIM))
            return carry

        def values(it, carry):
            for i0, r, n0 in runs(it):
                vt = tiles(v_ref, r, n0)
                for u in range(run):
                    i = i0 + u
                    o = _dot(p_scr[i, :, Q_BLOCK:], vt[u + 1])
                    if vt[u] is not None:
                        o = o + _dot(p_scr[i, :, :Q_BLOCK], vt[u])
                    lb = lb_scr[i]
                    mb = mb_scr[i]
                    sl_q = block_rows(r, n0 + u)
                    if first:
                        m_scr[sl_q, :] = mb
                        l_scr[sl_q, :] = lb
                        acc_scr[sl_q, :] = o
                        continue
                    m0 = m_scr[sl_q, :]
                    mn = jnp.maximum(m0, mb)
                    a0 = jnp.exp(m0 - mn)
                    a1 = jnp.exp(mb - mn)
                    l_new = a0 * l_scr[sl_q, :] + a1 * lb
                    acc_new = a0 * acc_scr[sl_q, :] + a1 * o
                    if last:
                        o_ref[sl_q, :] = (acc_new / l_new).astype(o_ref.dtype)
                    else:
                        m_scr[sl_q, :] = mn
                        l_scr[sl_q, :] = l_new
                        acc_scr[sl_q, :] = acc_new
            return carry

        lax.fori_loop(0, nblk // ATTN_UNROLL, scores, 0)
        lax.fori_loop(0, nblk, probs, 0, unroll=ATTN_UNROLL)
        lax.fori_loop(0, nblk // ATTN_UNROLL, values, 0)

    for s in range(N_ATT_GROUPS):
        pl.when(step == s)(functools.partial(run_group, s == 0, s == N_ATT_GROUPS - 1, _attn_group_of_step(s)))


def _attn_prompt(qkv, rel_bias, bucket_tiles, nbatch):
    M = qkv.shape[1]
    T = M // nbatch
    H, G = HEADS_PER_GROUP, N_ATT_GROUPS
    nblk = T // Q_BLOCK

    def qkv_spec(which):
        return pl.BlockSpec((None, T, HEAD_DIM),
                            lambda b, h, s: (which * N_ATT_HEADS + _attn_group_of_step(s) * H + h, b, 0))

    return pl.pallas_call(
        functools.partial(_attn_prompt_kernel, T),
        grid=(nbatch, H, G),
        in_specs=[pl.BlockSpec(memory_space=pltpu.SMEM),
                  qkv_spec(0), qkv_spec(1), qkv_spec(2),
                  pl.BlockSpec((None, Q_BLOCK, 2 * Q_BLOCK), lambda b, h, s: (_attn_group_of_step(s), 0, 0))],
        out_specs=pl.BlockSpec((T, HEAD_DIM), lambda b, h, s: (b, h)),
        out_shape=jax.ShapeDtypeStruct((M, ATT_WIDTH), BF16),
        scratch_shapes=[pltpu.VMEM((T, HEAD_DIM), F32)] * 3
                       + [pltpu.VMEM((nblk, Q_BLOCK, 2 * Q_BLOCK), F32),
                          pltpu.VMEM((nblk, Q_BLOCK, 2 * Q_BLOCK), BF16),
                          pltpu.VMEM((nblk, Q_BLOCK, HEAD_DIM), F32),
                          pltpu.VMEM((nblk, Q_BLOCK, HEAD_DIM), F32),
                          pltpu.VMEM((Q_BLOCK, 2 * Q_BLOCK), F32)],
        compiler_params=_cparams(("arbitrary", "arbitrary", "arbitrary")),
        name="attn_prompt",
    )(rel_bias.astype(F32), qkv, qkv, qkv, bucket_tiles)


def _attn_sample_kernel(window, dil, q_ref, ck_ref, cv_ref, nk_ref, nv_ref, bias_ref, o_ref, lse_ref):
    S = q_ref.shape[0]
    K = window // dil + 1
    scale = HEAD_DIM ** -0.5
    bias = bias_ref[...]
    by_residue = len(ck_ref.shape) == 4

    def buffered(ref, s, n):
        return ref[pl.ds(0, n), s] if by_residue else ref[pl.ds(s, n, stride=dil)]

    for s in range(S):
        n_c = (window - 1 - s) // dil + 1
        qs = q_ref[s]
        new_rows = [s + j * dil - window for j in range(n_c, K)]
        kk = jnp.concatenate([buffered(ck_ref, s, n_c)] + [nk_ref[pl.ds(i, 1)] for i in new_rows], axis=0)
        vv = jnp.concatenate([buffered(cv_ref, s, n_c)] + [nv_ref[pl.ds(i, 1)] for i in new_rows], axis=0)
        lg = jnp.sum(kk * qs[None], axis=-1, keepdims=True) * scale + bias
        m = jnp.max(lg, axis=0)
        p = jnp.exp(lg - m[None])
        l = jnp.sum(p, axis=0)
        o_ref[s] = jnp.sum(p * vv, axis=0) / l
        lse_ref[s] = m + jnp.log(l)


def _attn_sample(q, cache_k, cache_v, new_k, new_v, bias, layer, window, dil):
    B, S = q.shape[0], q.shape[1]
    H = HEADS_PER_GROUP
    K = window // dil + 1
    small = pl.BlockSpec((None, S, H, HEAD_DIM), lambda b: (b, 0, 0, 0))
    if dil > S:
        depth = cache_k.shape[0]
        cache_k = cache_k.reshape(depth, B, window // dil, dil, H, HEAD_DIM)
        cache_v = cache_v.reshape(depth, B, window // dil, dil, H, HEAD_DIM)
        cache = pl.BlockSpec((None, None, window // dil, S, H, HEAD_DIM), lambda b: (layer, b, 0, 0, 0, 0))
    else:
        cache = pl.BlockSpec((None, None, window, H, HEAD_DIM), lambda b: (layer, b, 0, 0, 0))
    return pl.pallas_call(
        functools.partial(_attn_sample_kernel, window, dil),
        grid=(B,),
        in_specs=[small, cache, cache, small, small, pl.BlockSpec((K, H, HEAD_DIM), lambda b: (0, 0, 0))],
        out_specs=[small, small],
        out_shape=[jax.ShapeDtypeStruct((B, S, H, HEAD_DIM), F32)] * 2,
        compiler_params=_cparams(("arbitrary",)),
        name="attn_sample_w%d" % window,
    )(q, cache_k, cache_v, new_k, new_v, bias)


def _merge_groups_kernel(o0, o1, o2, l0, l1, l2, y_ref):
    a, b, c = l0[...], l1[...], l2[...]
    m = jnp.maximum(jnp.maximum(a, b), c)
    ea, eb, ec = jnp.exp(a - m), jnp.exp(b - m), jnp.exp(c - m)
    y_ref[...] = ((ea * o0[...] + eb * o1[...] + ec * o2[...]) / (ea + eb + ec)).astype(y_ref.dtype)


def _merge_groups(outs, lses):
    shape = outs[0].shape
    return pl.pallas_call(
        _merge_groups_kernel,
        out_shape=jax.ShapeDtypeStruct(shape, BF16),
        name="merge_groups",
    )(*outs, *lses)


def _sample_bias(rel_bias):
    out = []
    for g, (window, dil) in enumerate(WINDOWS):
        K = window // dil + 1
        steps = (K - 1) - np.arange(K)
        tab = rel_bias[:, g * HEADS_PER_GROUP:(g + 1) * HEADS_PER_GROUP].astype(F32)
        b = tab[_bucket_np(steps * dil)]
        out.append(jnp.broadcast_to(b[:, :, None], (K, HEADS_PER_GROUP, HEAD_DIM)))
    return out


def _gelu_tanh(x):
    return 0.5 * x * (1.0 + jnp.tanh(math.sqrt(2.0 / math.pi) * (x + 0.044715 * (x * x * x))))


def _mix_out_kernel(y_ref, u_ref, yb_ref, ga_ref, gb_ref, x_ref, d_ref, wglu_ref, bglu_ref, wa_ref, wb_ref,
                    wout_ref, gn_ref, x1_ref, h2_ref):
    tm = x_ref.shape[0]
    sub = min(tm, MIX_ROW_SUB)
    for r in range(tm // sub):
        rows = pl.ds(r * sub, sub)
        y = y_ref[rows, :] + d_ref[...] * u_ref[rows, :]
        z = _gelu_tanh(y)
        ya = z * jax.nn.sigmoid(_dot(z.astype(BF16), wglu_ref[...]) + bglu_ref[...])
        mix = (ga_ref[rows, :] * _dot(ya.astype(BF16), wa_ref[...])
               + gb_ref[rows, :] * _dot(yb_ref[rows, :], wb_ref[...]))
        x1 = x_ref[rows, :] + _dot(mix.astype(BF16), wout_ref[...])
        x1_ref[rows, :] = x1
        h2_ref[rows, :] = _rmsnorm_rows(x1, gn_ref[...])


def _mix_out(y_ssm, u, yb, gates, x, ssm_d, wglu, bglu, wa, wb, wout, layer, gain, tm):
    M = x.shape[0]
    row = lambda w: pl.BlockSpec((tm, w), lambda i: (i, 0))
    vec = lambda b: pl.BlockSpec((1, b), lambda i: (0, 0), pipeline_mode=pl.Buffered(1))
    full = lambda a, b: pl.BlockSpec((None, a, b), lambda i: (layer, 0, 0), pipeline_mode=pl.Buffered(1))
    return pl.pallas_call(
        _mix_out_kernel,
        grid=(M // tm,),
        in_specs=[row(SSM_WIDTH), row(SSM_WIDTH), row(ATT_WIDTH),
                  pl.BlockSpec((tm, D_MODEL), lambda i: (i, 0)),
                  pl.BlockSpec((tm, D_MODEL), lambda i: (i, 1)),
                  row(D_MODEL),
                  vec(SSM_WIDTH), full(SSM_WIDTH, SSM_WIDTH), vec(SSM_WIDTH),
                  full(SSM_WIDTH, D_MODEL), full(ATT_WIDTH, D_MODEL), full(D_MODEL, D_MODEL), vec(D_MODEL)],
        out_specs=[row(D_MODEL), row(D_MODEL)],
        out_shape=[jax.ShapeDtypeStruct((M, D_MODEL), F32), jax.ShapeDtypeStruct((M, D_MODEL), BF16)],
        compiler_params=_cparams(("arbitrary",)),
        name="mix_out",
    )(y_ssm, u, yb, gates, gates, x, ssm_d.reshape(1, -1), wglu, bglu.reshape(1, -1), wa, wb, wout,
      gain.reshape(1, -1))


def _ffn_kernel(emit_norm, h_ref, x_ref, hs_ref, xs_ref, wg_ref, wu_ref, wd_ref, gn_ref, *rest):
    if emit_norm:
        o_ref, os_ref, hn_ref, hns_ref, wg_scr, wu_scr, wd_scr = rest
    else:
        o_ref, os_ref, wg_scr, wu_scr, wd_scr = rest
    i = pl.program_id(0)
    f = pl.program_id(1)
    last = pl.num_programs(1) - 1

    @pl.when(f == 0)
    def _():
        o_ref[...] = x_ref[...]

    @pl.when((f == 0) & (i == 0))
    def _():
        os_ref[...] = xs_ref[...]

    def swiglu(h):
        a = jax.nn.silu(_dot(h, wg_scr[...])) * _dot(h, wu_scr[...])
        return _dot(a.astype(BF16), wd_scr[...])

    tm = h_ref.shape[0]
    sub = min(tm, FFN_ROW_SUB)
    wg_scr[...] = wg_ref[...].astype(BF16)
    wu_scr[...] = wu_ref[...].astype(BF16)
    wd_scr[...] = wd_ref[...].astype(BF16)
    for r in range(tm // sub):
        rows = pl.ds(r * sub, sub)
        o_ref[rows, :] += swiglu(h_ref[rows, :])

    @pl.when(i == 0)
    def _():
        os_ref[...] += swiglu(hs_ref[...])

    if emit_norm:
        @pl.when(f == last)
        def _():
            for r in range(tm // sub):
                rows = pl.ds(r * sub, sub)
                hn_ref[rows, :] = _rmsnorm_rows(o_ref[rows, :], gn_ref[...])

        @pl.when((f == last) & (i == 0))
        def _():
            hns_ref[...] = _rmsnorm_rows(os_ref[...], gn_ref[...])


def _ffn(h2, x1, h2s, x1s, wg, wu, wd, layer, next_gain, tm):
    M, Ms = x1.shape[0], x1s.shape[0]
    nf = FFN_HIDDEN // FFN_BLOCK
    emit_norm = next_gain is not None
    gain = next_gain if emit_norm else jnp.ones((D_MODEL,), F32)
    row = pl.BlockSpec((tm, D_MODEL), lambda i, f: (i, 0))
    whole = pl.BlockSpec((Ms, D_MODEL), lambda i, f: (0, 0))
    out_specs = [row, whole]
    out_shape = [jax.ShapeDtypeStruct((M, D_MODEL), F32), jax.ShapeDtypeStruct((Ms, D_MODEL), F32)]
    if emit_norm:
        out_specs += [row, whole]
        out_shape += [jax.ShapeDtypeStruct((M, D_MODEL), BF16), jax.ShapeDtypeStruct((Ms, D_MODEL), BF16)]
    return pl.pallas_call(
        functools.partial(_ffn_kernel, emit_norm),
        grid=(M // tm, nf),
        in_specs=[row,
                  pl.BlockSpec((tm, D_MODEL), lambda i, f: (i, 0), pipeline_mode=pl.Buffered(1)),
                  whole, whole,
                  pl.BlockSpec((None, D_MODEL, FFN_BLOCK), lambda i, f: (layer, 0, f)),
                  pl.BlockSpec((None, D_MODEL, FFN_BLOCK), lambda i, f: (layer, 0, f)),
                  pl.BlockSpec((None, FFN_BLOCK, D_MODEL), lambda i, f: (layer, f, 0)),
                  pl.BlockSpec((1, D_MODEL), lambda i, f: (0, 0))],
        out_specs=out_specs,
        out_shape=out_shape,
        scratch_shapes=[pltpu.VMEM((D_MODEL, FFN_BLOCK), BF16), pltpu.VMEM((D_MODEL, FFN_BLOCK), BF16),
                        pltpu.VMEM((FFN_BLOCK, D_MODEL), BF16)],
        compiler_params=_cparams(("arbitrary", "arbitrary")),
        name="ffn",
    )(h2, x1, h2s, x1s, wg, wu, wd, gain.reshape(1, D_MODEL))


def _cache_shift_kernel(ck_ref, cv_ref, hk_ref, hv_ref, nk_ref, nv_ref, ok_ref, ov_ref):
    c = pl.program_id(2)
    last = pl.num_programs(2) - 1
    R = ck_ref.shape[0]
    S = nk_ref.shape[0]
    for cache, halo, new, out in ((ck_ref, hk_ref, nk_ref, ok_ref), (cv_ref, hv_ref, nv_ref, ov_ref)):
        out[pl.ds(0, R - S)] = cache[pl.ds(S, R - S)]

        @pl.when(c == last)
        def _():
            out[pl.ds(R - S, S)] = new[...]

        @pl.when(c < last)
        def _():
            out[pl.ds(R - S, S)] = halo[...]


def _cache_shift(cache_k, cache_v, new_k, new_v):
    depth, B, W, H, E = cache_k.shape
    S = new_k.shape[2]
    R = min(W, CACHE_ROWS)
    nchunks = W // R
    blk = pl.BlockSpec((None, None, R, H, E), lambda l, b, c: (l, b, c, 0, 0))
    halo = pl.BlockSpec((None, None, None, S, H, E),
                        lambda l, b, c: (l, b, jnp.minimum((c + 1) * (R // S), W // S - 1), 0, 0, 0))
    new = pl.BlockSpec((None, None, S, H, E), lambda l, b, c: (l, b, 0, 0, 0))
    as_rows = lambda a: a.reshape(depth, B, W // S, S, H, E)
    return pl.pallas_call(
        _cache_shift_kernel,
        grid=(depth, B, nchunks),
        in_specs=[blk, blk, halo, halo, new, new],
        out_specs=[blk, blk],
        out_shape=[jax.ShapeDtypeStruct(cache_k.shape, cache_k.dtype)] * 2,
        compiler_params=_cparams(("arbitrary", "arbitrary", "arbitrary")),
        name="cache_shift_w%d" % W,
    )(cache_k, cache_v, as_rows(cache_k), as_rows(cache_v), new_k, new_v)


def _kv_tails_kernel(*refs):
    depth = (len(refs) - 2) // 2
    ok_ref, ov_ref = refs[-2:]
    H, R = refs[0].shape[0], refs[0].shape[1]
    for l in range(depth):
        @pl.when(pl.program_id(0) == l)
        def _():
            for src, out in ((refs[2 * l], ok_ref), (refs[2 * l + 1], ov_ref)):
                for h in range(H):
                    out[pl.ds(h, R, stride=H), :] = src[h]


def _kv_tails(qkv_layers, g, window, nbatch):
    depth = len(qkv_layers)
    M = qkv_layers[0].shape[1]
    T = M // nbatch
    H, E = HEADS_PER_GROUP, HEAD_DIM
    R = min(window, CACHE_ROWS)
    nchunks = window // R
    first = (T - window) // R

    def src(layer, which):
        def index(l, b, c):
            bb = jnp.where(l < layer, 0, jnp.where(l > layer, nbatch - 1, b))
            cc = jnp.where(l < layer, 0, jnp.where(l > layer, nchunks - 1, c))
            return (which * N_ATT_GROUPS + g, bb * (T // R) + first + cc, 0)
        return pl.BlockSpec((H, R, E), index)

    args, in_specs = [], []
    for layer, qkv in enumerate(qkv_layers):
        args += [qkv, qkv]
        in_specs += [src(layer, 1), src(layer, 2)]
    out_spec = pl.BlockSpec((None, R * H, E), lambda l, b, c: (l, b * nchunks + c, 0))
    return pl.pallas_call(
        _kv_tails_kernel,
        grid=(depth, nbatch, nchunks),
        in_specs=in_specs,
        out_specs=[out_spec, out_spec],
        out_shape=[jax.ShapeDtypeStruct((depth, nbatch * window * H, E), qkv_layers[0].dtype)] * 2,
        compiler_params=_cparams(("arbitrary", "arbitrary", "arbitrary")),
        name="kv_tails_w%d" % window,
    )(*args)


def _mix(x2d, u, y_ssm, yb, gates, prm, tm):
    return _mix_out(y_ssm, u, yb, gates, x2d, prm["ssm_d"], prm["w_glu"], prm["b_glu"], prm["w_branch_a"],
                    prm["w_branch_b"], prm["w_out"], prm["layer"], prm["norm_ffn"], tm)


def kernel(x_prompt, x_sample, state_ssm_re, state_ssm_im, cache_k_w128, cache_v_w128, cache_k_w512, cache_v_w512,
           cache_k_w2048, cache_v_w2048, rel_bias, norm_mix, norm_ffn, q_norm, k_norm, w_in, ssm_lambda_re,
           ssm_lambda_im, ssm_log_dt, ssm_b_re, ssm_b_im, ssm_c_re, ssm_c_im, ssm_d, w_glu, b_glu, w_branch_a,
           w_branch_b, w_out, w_ffn_gate, w_ffn_up, w_ffn_down):
    depth = w_in.shape[0]
    B, T, _ = x_prompt.shape
    SB, S, _ = x_sample.shape
    H = HEADS_PER_GROUP
    caches_k = (cache_k_w128, cache_k_w512, cache_k_w2048)
    caches_v = (cache_v_w128, cache_v_w512, cache_v_w2048)

    bucket_tiles = _prompt_bucket_tiles()
    bias_sample = _sample_bias(rel_bias)
    sel_k, sel_b = _ssm_select_matrices()

    xp = x_prompt.reshape(B * T, D_MODEL)
    xs = x_sample.reshape(SB * S, D_MODEL)
    p_re, p_im, s_re, s_im = [], [], [], []
    qkv_prompt = []
    new_k = [[] for _ in WINDOWS]
    new_v = [[] for _ in WINDOWS]

    weights = {
        "w_glu": w_glu.astype(BF16), "w_branch_a": w_branch_a.astype(BF16),
        "w_branch_b": w_branch_b.astype(BF16), "w_out": w_out.astype(BF16),
    }

    hp = _rmsnorm(xp, norm_mix[0], FFN_ROWS)
    hs = _rmsnorm(xs, norm_mix[0], SB * S)

    for l in range(depth):
        ek, eb, ea, lnr, lni = _ssm_prep(ssm_lambda_re[l], ssm_lambda_im[l], ssm_log_dt[l], ssm_b_re[l],
                                         ssm_b_im[l], ssm_c_re[l], ssm_c_im[l])
        prm = dict(weights)
        prm.update({
            "layer": l, "norm_ffn": norm_ffn[l], "ssm_d": ssm_d[l], "b_glu": b_glu[l],
        })

        (u, qkv, gates), (us, qkvs, gates_s) = _in_proj(hp, hs, w_in, l, q_norm[l], k_norm[l], IN_PROJ_ROWS)

        zeros = jnp.zeros((B, N_SSM_RANGES, 2 * RANGE_STATE // LANES, LANES), F32)
        h0 = _state_to_tiles(state_ssm_re[l], state_ssm_im[l])
        (y_ssm, hfin), (ys_ssm, hfin_s) = _ssm_chunk([(u, zeros, B), (us, h0, SB)], ek, eb, ea, sel_k, sel_b,
                                                     lnr, lni)

        yb = _attn_prompt(qkv, rel_bias, bucket_tiles, B)
        x1, h2 = _mix(xp, u, y_ssm, yb, gates, prm, MIX_ROWS)
        hr, hi = _tiles_to_state(hfin)
        p_re.append(hr)
        p_im.append(hi)
        qkv_prompt.append(qkv)

        tok = jnp.transpose(qkvs.reshape(3, N_ATT_GROUPS, H, SB, S, HEAD_DIM), (0, 1, 3, 4, 2, 5))
        outs, lses = [], []
        for g, (window, dil) in enumerate(WINDOWS):
            nk_g, nv_g = tok[1, g], tok[2, g]
            new_k[g].append(nk_g)
            new_v[g].append(nv_g)
            o_g, lse_g = _attn_sample(tok[0, g], caches_k[g], caches_v[g], nk_g, nv_g, bias_sample[g], l,
                                      window, dil)
            outs.append(o_g)
            lses.append(lse_g)
        ybs = _merge_groups(outs, lses).reshape(SB * S, ATT_WIDTH)
        x1s, h2s = _mix(xs, us, ys_ssm, ybs, gates_s, prm, SB * S)
        hr, hi = _tiles_to_state(hfin_s)
        s_re.append(hr)
        s_im.append(hi)

        next_gain = norm_mix[l + 1] if l + 1 < depth else None
        res = _ffn(h2, x1, h2s, x1s, w_ffn_gate, w_ffn_up, w_ffn_down, l, next_gain, FFN_ROWS)
        if next_gain is None:
            xp, xs = res
        else:
            xp, xs, hp, hs = res

    shifted = []
    for g in range(N_ATT_GROUPS):
        shifted += _cache_shift(caches_k[g], caches_v[g], jnp.stack(new_k[g]), jnp.stack(new_v[g]))
    tails = [t.reshape(depth, B, window, H, HEAD_DIM)
             for g, (window, _) in enumerate(WINDOWS) for t in _kv_tails(qkv_prompt, g, window, B)]

    return (xp.reshape(B, T, D_MODEL), xs.reshape(SB, S, D_MODEL),
            jnp.stack(p_re), jnp.stack(p_im), *tails,
            jnp.stack(s_re), jnp.stack(s_im), *shifted)
```

```python
import functools
import math

import numpy as np
import jax
import jax.numpy as jnp
from jax import lax
from jax.experimental import pallas as pl
from jax.experimental.pallas import tpu as pltpu

F32 = jnp.float32
BF16 = jnp.bfloat16

D_MODEL = 2048
HEAD_DIM = 128
HEADS_PER_GROUP = 8
WINDOWS = ((128, 1), (512, 4), (2048, 16))
N_ATT_GROUPS = len(WINDOWS)
N_ATT_HEADS = N_ATT_GROUPS * HEADS_PER_GROUP
ATT_WIDTH = HEADS_PER_GROUP * HEAD_DIM
QKV_WIDTH = N_ATT_HEADS * HEAD_DIM
Q_BLOCK = 128
SSM_WIDTH = D_MODEL // 2
SSM_GROUP = 16
N_SSM_GROUPS = SSM_WIDTH // SSM_GROUP
SSM_STATE = 64
IN_WIDTH = SSM_WIDTH + 3 * QKV_WIDTH + 2 * D_MODEL
FFN_HIDDEN = 5632
N_BUCKETS = 32
REL_MAX_DISTANCE = 2048
EPS = 1e-6
NEG_INF = -1e30

LANES = 128
SUBLANES = 8
VMEM_LIMIT_BYTES = 60 * 1024 * 1024

SSM_CHUNK = 8
SSM_RANGE_GROUPS = LANES // SSM_GROUP
N_SSM_RANGES = N_SSM_GROUPS // SSM_RANGE_GROUPS
RANGE_STATE = SSM_RANGE_GROUPS * SSM_STATE
COL_BLOCK = 512
FFN_BLOCK = 256
ATTN_UNROLL = 16
ATTN_FIRST_GROUP = 2
CACHE_ROWS = 1024
IN_PROJ_ROWS = 2048
FFN_ROWS = 1024
MIX_ROWS = 256
ROW_SUB = 256
FFN_ROW_SUB = 512
MIX_ROW_SUB = 256


def _cparams(sem):
    return pltpu.CompilerParams(dimension_semantics=sem, vmem_limit_bytes=VMEM_LIMIT_BYTES)


def _dot(a, b):
    return jnp.dot(a, b, preferred_element_type=F32)


def _dot_nt(a, b):
    return lax.dot_general(a, b, (((1,), (1,)), ((), ())), preferred_element_type=F32)


def _ssm_prep_kernel(*refs):
    for g in range(SSM_RANGE_GROUPS):
        _ssm_prep_group(*[ref.at[g] for ref in refs])


def _ssm_prep_group(lre_ref, lim_ref, ldt_ref, btre_ref, btim_ref, cre_ref, cim_ref,
                    kp_ref, bpre_ref, bpim_ref, are_ref, aim_ref, lnre_ref, lnim_ref):
    L = SSM_CHUNK
    lr = lre_ref[...]
    li = lim_ref[...]
    dt = jnp.exp(ldt_ref[...])
    er = jnp.exp(lr * dt)
    lbr = er * jnp.cos(li * dt)
    lbi = er * jnp.sin(li * dt)
    nr = lbr - 1.0
    dd = lr * lr + li * li
    rr = (nr * lr + lbi * li) / dd
    ri = (lbi * lr - nr * li) / dd
    btr = btre_ref[...]
    bti = btim_ref[...]
    bbr = rr * btr - ri * bti
    bbi = rr * bti + ri * btr
    cr = cre_ref[...]
    ci = cim_ref[...]
    pr = [jnp.ones_like(lbr)]
    pi = [jnp.zeros_like(lbr)]
    for _ in range(L):
        pr.append(pr[-1] * lbr - pi[-1] * lbi)
        pi.append(pr[-2] * lbi + pi[-1] * lbr)
    xr = [cr * pr[t] - ci * pi[t] for t in range(L + 1)]
    xi = [cr * pi[t] + ci * pr[t] for t in range(L + 1)]
    xr_k = jnp.concatenate(xr[:L], axis=0)
    xi_k = jnp.concatenate(xi[:L], axis=0)
    hp = lax.Precision.HIGHEST
    kp = (lax.dot_general(bbr, xr_k, (((1,), (1,)), ((), ())), precision=hp, preferred_element_type=F32)
          - lax.dot_general(bbi, xi_k, (((1,), (1,)), ((), ())), precision=hp, preferred_element_type=F32))
    lane = lax.broadcasted_iota(jnp.int32, kp.shape, 1)
    blocks = [kp] + [jnp.where(lane >= j * SSM_GROUP, pltpu.roll(kp, j * SSM_GROUP, axis=1), 0.0)
                     for j in range(1, L)]
    kp_ref[...] = jnp.concatenate(blocks, axis=0)
    are_ref[...] = jnp.concatenate(xr[1:], axis=0)
    aim_ref[...] = jnp.concatenate(xi[1:], axis=0)
    bpre_ref[...] = jnp.concatenate([pr[L - 1 - j] * bbr - pi[L - 1 - j] * bbi for j in range(L)], axis=0)
    bpim_ref[...] = jnp.concatenate([pr[L - 1 - j] * bbi + pi[L - 1 - j] * bbr for j in range(L)], axis=0)
    lnre_ref[...] = pr[L]
    lnim_ref[...] = pi[L]


def _ssm_prep(lam_re, lam_im, log_dt, b_re, b_im, c_re, c_im):
    G, P, C, L = N_SSM_GROUPS, SSM_STATE, SSM_GROUP, SSM_CHUNK
    row = lambda a: a.reshape(G, 1, -1)
    bt_re = jnp.swapaxes(b_re, 1, 2)
    bt_im = jnp.swapaxes(b_im, 1, 2)
    GL = SSM_RANGE_GROUPS
    vec = pl.BlockSpec((GL, 1, P), lambda r: (r, 0, 0))
    mat = pl.BlockSpec((GL, C, P), lambda r: (r, 0, 0))
    big = pl.BlockSpec((GL, L * C, P), lambda r: (r, 0, 0))
    kp, bpre, bpim, are, aim, lnre, lnim = pl.pallas_call(
        _ssm_prep_kernel,
        grid=(G // GL,),
        in_specs=[vec, vec, pl.BlockSpec((GL, 1, 1), lambda r: (r, 0, 0)), mat, mat, mat, mat],
        out_specs=[pl.BlockSpec((GL, L * C, L * C), lambda r: (r, 0, 0)), big, big, big, big, vec, vec],
        out_shape=[jax.ShapeDtypeStruct((G, L * C, L * C), F32)] + [jax.ShapeDtypeStruct((G, L * C, P), F32)] * 4
                  + [jax.ShapeDtypeStruct((G, 1, P), F32)] * 2,
        compiler_params=_cparams(("arbitrary",)),
        name="ssm_prep",
    )(row(lam_re), row(lam_im), log_dt.reshape(G, 1, 1), bt_re, bt_im, c_re, c_im)

    R, GL = N_SSM_RANGES, SSM_RANGE_GROUPS

    def by_range(a, n_outer):
        n_inner = a.shape[1] // n_outer
        a = a.reshape(R, GL, n_outer, n_inner, LANES)
        return jnp.transpose(a, (0, 2, 1, 3, 4)).reshape(R, n_outer * GL * n_inner, LANES).astype(BF16)

    ek = by_range(kp, L)
    eb = by_range(jnp.concatenate([bpre, bpim], axis=-1), L)
    a_t = jnp.swapaxes(jnp.concatenate([are, -aim], axis=-1), 1, 2)
    ea = by_range(a_t, 2)

    half = RANGE_STATE // LANES
    lnr_t = lnre.reshape(R, half, LANES)
    lni_t = lnim.reshape(R, half, LANES)
    lnr = jnp.concatenate([lnr_t, lnr_t], axis=1)
    lni = jnp.concatenate([-lni_t, lni_t], axis=1)
    return ek, eb, ea, lnr, lni


def _ssm_select_matrices():
    q = np.arange(SSM_CHUNK * LANES)
    r = np.arange(LANES)
    sel_k = (r[:, None] // SSM_GROUP == q[None, :] // LANES) & (r[:, None] % SSM_GROUP == q[None, :] % SSM_GROUP)
    sel_b = (r[:, None] // SSM_STATE == q[None, :] // RANGE_STATE) & (r[:, None] % SSM_STATE == q[None, :] % SSM_STATE)
    return jnp.asarray(sel_k, BF16), jnp.asarray(sel_b, BF16)


HEADS_PER_BLOCK = COL_BLOCK // HEAD_DIM
_SEG_U = 0
_SEG_Q = SSM_WIDTH // COL_BLOCK
_SEG_K = _SEG_Q + QKV_WIDTH // COL_BLOCK
_SEG_V = _SEG_K + QKV_WIDTH // COL_BLOCK
_SEG_GATE = _SEG_V + QKV_WIDTH // COL_BLOCK
_SEG_END = IN_WIDTH // COL_BLOCK


def _head_norm(res, gain):
    outs = []
    for h in range(HEADS_PER_BLOCK):
        t = res[:, h * HEAD_DIM:(h + 1) * HEAD_DIM]
        ms = jnp.mean(t * t, axis=-1, keepdims=True)
        outs.append(t * lax.rsqrt(ms + EPS) * gain)
    return outs


def _rmsnorm_rows(x, gain):
    ms = jnp.mean(x * x, axis=-1, keepdims=True)
    return (x * lax.rsqrt(ms + EPS) * gain).astype(BF16)


def _rmsnorm_kernel(x_ref, g_ref, o_ref):
    o_ref[...] = _rmsnorm_rows(x_ref[...], g_ref[...])


def _rmsnorm(x, gain, tm):
    M, D = x.shape
    return pl.pallas_call(
        _rmsnorm_kernel,
        grid=(M // tm,),
        in_specs=[pl.BlockSpec((tm, D), lambda i: (i, 0)), pl.BlockSpec((1, D), lambda i: (0, 0))],
        out_specs=pl.BlockSpec((tm, D), lambda i: (i, 0)),
        out_shape=jax.ShapeDtypeStruct((M, D), BF16),
        compiler_params=_cparams(("arbitrary",)),
        name="rmsnorm",
    )(x, gain.reshape(1, D))


def _in_proj_kernel(normalize, h_ref, hs_ref, g_ref, w_ref, qn_ref, kn_ref,
                    u_ref, qkv_ref, gate_ref, us_ref, qkvs_ref, gates_s_ref, w_scr, *norm_scr):
    i = pl.program_id(0)
    j = pl.program_id(1)
    tm = h_ref.shape[0]
    sub = min(tm, ROW_SUB)

    if normalize:
        h_src, hs_src = norm_scr

        @pl.when(j == 0)
        def _():
            for r in range(tm // sub):
                rows = pl.ds(r * sub, sub)
                h_src[rows, :] = _rmsnorm_rows(h_ref[rows, :], g_ref[...])

        @pl.when((j == 0) & (i == 0))
        def _():
            hs_src[...] = _rmsnorm_rows(hs_ref[...], g_ref[...])
    else:
        h_src, hs_src = h_ref, hs_ref

    def row_blocks():
        w_scr[...] = w_ref[...].astype(BF16)
        for r in range(tm // sub):
            rows = pl.ds(r * sub, sub)
            yield rows, _dot(h_src[rows, :], w_scr[...])

    def sample_res():
        return _dot(hs_src[...], w_scr[...])

    @pl.when(j < _SEG_Q)
    def _():
        for rows, res in row_blocks():
            u_ref[rows, :] = res

        @pl.when(i == 0)
        def _():
            us_ref[j] = sample_res()

    def split_heads(res, gain_ref):
        if gain_ref is not None:
            return _head_norm(res, gain_ref[...])
        return [res[:, h * HEAD_DIM:(h + 1) * HEAD_DIM] for h in range(HEADS_PER_BLOCK)]

    def store_heads(gain_ref):
        for rows, res in row_blocks():
            for h, t in enumerate(split_heads(res, gain_ref)):
                qkv_ref[h, rows, :] = t

        @pl.when(i == 0)
        def _():
            base = (j - _SEG_Q) * HEADS_PER_BLOCK
            for h, t in enumerate(split_heads(sample_res(), gain_ref)):
                qkvs_ref[base + h] = t

    pl.when((j >= _SEG_Q) & (j < _SEG_K))(functools.partial(store_heads, qn_ref))
    pl.when((j >= _SEG_K) & (j < _SEG_V))(functools.partial(store_heads, kn_ref))
    pl.when((j >= _SEG_V) & (j < _SEG_GATE))(functools.partial(store_heads, None))

    @pl.when(j >= _SEG_GATE)
    def _():
        for rows, res in row_blocks():
            gate_ref[rows, :] = jax.nn.sigmoid(res).astype(gate_ref.dtype)

        @pl.when(i == 0)
        def _():
            gates_s_ref[j - _SEG_GATE] = jax.nn.sigmoid(sample_res()).astype(gates_s_ref.dtype)


def _in_proj(h, hs, norm_gain, w, layer, q_gain, k_gain, tm):
    M, Ms = h.shape[0], hs.shape[0]
    n_heads = 3 * N_ATT_HEADS
    n_gate = _SEG_END - _SEG_GATE
    normalize = norm_gain is not None
    gain = norm_gain if normalize else jnp.ones((D_MODEL,), F32)
    norm_scratch = [pltpu.VMEM((tm, D_MODEL), BF16), pltpu.VMEM((Ms, D_MODEL), BF16)] if normalize else []
    whole = lambda shape: pl.BlockSpec(shape, lambda i, j: (0,) * len(shape))
    u, qkv, gates, us, qkvs, gates_s = pl.pallas_call(
        functools.partial(_in_proj_kernel, normalize),
        grid=(M // tm, _SEG_END),
        in_specs=[
            pl.BlockSpec((tm, D_MODEL), lambda i, j: (i, 0), pipeline_mode=pl.Buffered(1)),
            whole((Ms, D_MODEL)),
            pl.BlockSpec((1, D_MODEL), lambda i, j: (0, 0)),
            pl.BlockSpec((None, D_MODEL, COL_BLOCK), lambda i, j: (layer, 0, j)),
            pl.BlockSpec((1, HEAD_DIM), lambda i, j: (0, 0)),
            pl.BlockSpec((1, HEAD_DIM), lambda i, j: (0, 0)),
        ],
        out_specs=[
            pl.BlockSpec((tm, COL_BLOCK), lambda i, j: (i, jnp.clip(j, 0, _SEG_Q - 1))),
            pl.BlockSpec((HEADS_PER_BLOCK, tm, HEAD_DIM),
                         lambda i, j: (jnp.clip(j - _SEG_Q, 0, _SEG_GATE - _SEG_Q - 1), i, 0)),
            pl.BlockSpec((tm, COL_BLOCK), lambda i, j: (i, jnp.clip(j - _SEG_GATE, 0, _SEG_END - _SEG_GATE - 1))),
            whole((_SEG_Q, Ms, COL_BLOCK)),
            whole((n_heads, Ms, HEAD_DIM)),
            whole((n_gate, Ms, COL_BLOCK)),
        ],
        out_shape=[
            jax.ShapeDtypeStruct((M, SSM_WIDTH), F32),
            jax.ShapeDtypeStruct((n_heads, M, HEAD_DIM), F32),
            jax.ShapeDtypeStruct((M, 2 * D_MODEL), BF16),
            jax.ShapeDtypeStruct((_SEG_Q, Ms, COL_BLOCK), F32),
            jax.ShapeDtypeStruct((n_heads, Ms, HEAD_DIM), F32),
            jax.ShapeDtypeStruct((n_gate, Ms, COL_BLOCK), BF16),
        ],
        scratch_shapes=[pltpu.VMEM((D_MODEL, COL_BLOCK), BF16)] + norm_scratch,
        compiler_params=_cparams(("arbitrary", "arbitrary")),
        name="in_proj",
    )(h, hs, gain.reshape(1, D_MODEL), w, q_gain.reshape(1, HEAD_DIM), k_gain.reshape(1, HEAD_DIM))
    us = jnp.swapaxes(us, 0, 1).reshape(Ms, SSM_WIDTH)
    gates_s = jnp.swapaxes(gates_s, 0, 1).reshape(Ms, 2 * D_MODEL)
    return (u, qkv, gates), (us, qkvs, gates_s)


def _expand_block_diag(e_ref, sel_ref, row_shift, col_shift, out_scr):
    n = out_scr.shape[0]
    for c in range(n // LANES):
        rows = pl.ds(c * LANES, LANES)
        full = _dot(e_ref[rows, :], sel_ref[...])
        row = lax.broadcasted_iota(jnp.int32, full.shape, 0) + c * LANES
        col = lax.broadcasted_iota(jnp.int32, full.shape, 1)
        keep = ((row >> row_shift) & (SSM_RANGE_GROUPS - 1)) == ((col >> col_shift) & (SSM_RANGE_GROUPS - 1))
        out_scr[rows, :] = jnp.where(keep, full, 0.0).astype(BF16)


def _ssm_chunk_kernel(groups, *refs):
    n = len(groups)
    ins, refs = refs[:2 * n], refs[2 * n:]
    (ek_ref, eb_ref, ea_ref, selk_ref, selb_ref, lnr_ref, lni_ref), refs = refs[:7], refs[7:]
    outs, refs = refs[:2 * n], refs[2 * n:]
    s_scrs, (ktoe_ref, bcat_ref, acat_ref) = refs[:n], refs[n:]
    lg_c = SSM_GROUP.bit_length() - 1
    lg_p = SSM_STATE.bit_length() - 1
    _expand_block_diag(ek_ref, selk_ref, lg_c, lg_c, ktoe_ref)
    _expand_block_diag(eb_ref, selb_ref, lg_c, lg_p, bcat_ref)
    _expand_block_diag(ea_ref, selk_ref, lg_p, lg_c, acat_ref)
    for g, (nseq, nk) in enumerate(groups):
        _ssm_chunk_group(nseq, nk, ins[2 * g], ins[2 * g + 1], ktoe_ref, bcat_ref, acat_ref, lnr_ref, lni_ref,
                         outs[2 * g], outs[2 * g + 1], s_scrs[g])


def _ssm_chunk_group(nseq, nk, u_ref, h0_ref, ktoe_ref, bcat_ref, acat_ref, lnr_ref, lni_ref,
                     y_ref, hfin_ref, s_scr):
    L = SSM_CHUNK
    rows = nseq * nk
    nsub = 2 * RANGE_STATE // LANES
    ucat = jnp.concatenate([u_ref[pl.ds(j, rows, stride=L), :] for j in range(L)], axis=-1).astype(BF16)
    s = _dot(ucat, bcat_ref[...])
    for n in range(nsub):
        s_scr[pl.ds(n, rows, stride=nsub), :] = s[:, n * LANES:(n + 1) * LANES]
    y_ref_intra = _dot(ucat, ktoe_ref[...])

    lnr = lnr_ref[...]
    lni = lni_ref[...]

    def step(k, hs):
        new = []
        for q in range(nseq):
            off = pl.multiple_of((q * nk + k) * nsub, nsub)
            h = hs[q]
            sk = s_scr[pl.ds(off, nsub), :]
            s_scr[pl.ds(off, nsub), :] = h
            new.append(h * lnr + pltpu.roll(h, nsub // 2, axis=0) * lni + sk)
        return tuple(new)

    hs = lax.fori_loop(0, nk, step, tuple(h0_ref[q] for q in range(nseq)))
    for q in range(nseq):
        hfin_ref[q] = hs[q]

    hprev = jnp.concatenate([s_scr[pl.ds(n, rows, stride=nsub), :] for n in range(nsub)], axis=-1).astype(BF16)
    y = y_ref_intra + _dot(hprev, acat_ref[...])
    for j in range(L):
        y_ref[pl.ds(j, rows, stride=L), :] = y[:, j * LANES:(j + 1) * LANES]


def _ssm_chunk(token_groups, ek, eb, ea, sel_k, sel_b, lnr, lni):
    R = N_SSM_RANGES
    nsub = 2 * RANGE_STATE // LANES
    wide = SSM_CHUNK * LANES
    assert wide == 2 * RANGE_STATE
    wspec = lambda shape: pl.BlockSpec((None,) + shape, lambda r: (r, 0, 0))
    sel_spec = pl.BlockSpec((LANES, wide), lambda r: (0, 0))
    groups, args, in_specs, out_specs, out_shape, scratch = [], [], [], [], [], []
    for u, h0, nseq in token_groups:
        M = u.shape[0]
        nk = M // nseq // SSM_CHUNK
        groups.append((nseq, nk))
        args += [u, h0]
        tok_spec = pl.BlockSpec((M, LANES), lambda r: (0, r))
        state_spec = pl.BlockSpec((nseq, None, nsub, LANES), lambda r: (0, r, 0, 0))
        in_specs += [tok_spec, state_spec]
        out_specs += [tok_spec, state_spec]
        out_shape += [jax.ShapeDtypeStruct((M, SSM_WIDTH), F32), jax.ShapeDtypeStruct((nseq, R, nsub, LANES), F32)]
        scratch.append(pltpu.VMEM((nseq * nk * nsub, LANES), F32))
    res = pl.pallas_call(
        functools.partial(_ssm_chunk_kernel, tuple(groups)),
        grid=(R,),
        in_specs=in_specs + [wspec((wide, LANES))] * 3 + [sel_spec, sel_spec] + [wspec((nsub, LANES))] * 2,
        out_specs=out_specs,
        out_shape=out_shape,
        scratch_shapes=scratch + [pltpu.VMEM((wide, wide), BF16)] * 3,
        compiler_params=_cparams(("arbitrary",)),
        name="ssm_chunk",
    )(*args, ek, eb, ea, sel_k, sel_b, lnr, lni)
    return [(res[2 * g], res[2 * g + 1]) for g in range(len(token_groups))]


def _state_to_tiles(re, im):
    N = re.shape[0]
    half = RANGE_STATE // LANES
    return jnp.concatenate([re.reshape(N, N_SSM_RANGES, half, LANES),
                            im.reshape(N, N_SSM_RANGES, half, LANES)], axis=2)


def _tiles_to_state(t):
    N = t.shape[0]
    half = RANGE_STATE // LANES
    return (t[:, :, :half].reshape(N, N_SSM_GROUPS, SSM_STATE),
            t[:, :, half:].reshape(N, N_SSM_GROUPS, SSM_STATE))


def _bucket_np(dist):
    max_exact = N_BUCKETS // 2
    n = np.maximum(dist, 0)
    nf = np.maximum(n, 1).astype(np.float64)
    large = max_exact + (np.log(nf / max_exact) / math.log(REL_MAX_DISTANCE / max_exact)
                         * (N_BUCKETS - max_exact)).astype(np.int32)
    large = np.minimum(large, N_BUCKETS - 1)
    return np.where(n < max_exact, n, large)


def _prompt_bucket_tile_np(group):
    window, dil = WINDOWS[group]
    a = np.arange(Q_BLOCK)[:, None]
    c = np.arange(2 * Q_BLOCK)[None, :]
    rel = a - c + Q_BLOCK
    K = window // dil + 1
    valid = (rel >= 0) & (rel < K)
    return np.where(valid, _bucket_np(np.clip(rel, 0, K - 1) * dil), -1)


def _prompt_buckets_present(group):
    tile = _prompt_bucket_tile_np(group)
    return [int(t) for t in np.unique(tile[tile >= 0])]


def _prompt_bucket_tiles():
    return jnp.asarray(np.stack([_prompt_bucket_tile_np(g) for g in range(N_ATT_GROUPS)]), jnp.int32)


def _attn_group_of_step(step):
    return (step + ATTN_FIRST_GROUP) % N_ATT_GROUPS


def _attn_prompt_kernel(T, tab_ref, q_ref, k_ref, v_ref, bkt_ref, o_ref,
                        m_scr, l_scr, acc_scr, s_scr, p_scr, mb_scr, bias_scr):
    h = pl.program_id(1)
    step = pl.program_id(2)
    scale = HEAD_DIM ** -0.5
    nblk = T // Q_BLOCK

    def run_group(first, group):
        dil = WINDOWS[group][1]
        bkt = bkt_ref[...]
        col = group * HEADS_PER_GROUP + h
        bias = jnp.full(bkt.shape, NEG_INF, F32)
        for t in _prompt_buckets_present(group):
            bias = jnp.where(bkt == t, tab_ref[t, col], bias)
        bias_scr[...] = bias

        per_class = nblk // dil
        run = min(per_class, ATTN_UNROLL)
        runs_per_iter = ATTN_UNROLL // run
        whole_class = run == per_class

        def block_rows(r, n):
            return pl.ds(r + n * (Q_BLOCK * dil), Q_BLOCK, stride=dil)

        def runs(it):
            for j in range(runs_per_iter):
                i0 = it * ATTN_UNROLL + j * run
                yield i0, i0 // per_class, (0 if whole_class else i0 % per_class)

        def tiles(ref, r, n0, augment):
            out = []
            for u in range(-1, run):
                if u < 0 and whole_class:
                    out.append(None)
                    continue
                n = jnp.maximum(n0 + u, 0) if u < 0 else n0 + u
                t = ref[block_rows(r, n), :].astype(BF16)
                if augment:
                    t = jnp.concatenate([t, jnp.ones((Q_BLOCK, HEAD_DIM), BF16)], axis=1)
                out.append(t)
            return out

        def scores(it, carry):
            for i0, r, n0 in runs(it):
                kt = tiles(k_ref, r, n0, False)
                for u in range(run):
                    q = q_ref[block_rows(r, n0 + u), :].astype(BF16)
                    s_r = _dot_nt(q, kt[u + 1]) * scale + bias_scr[:, Q_BLOCK:]
                    if kt[u] is None:
                        s_l = jnp.full((Q_BLOCK, Q_BLOCK), NEG_INF, F32)
                    else:
                        bias_l = bias_scr[:, :Q_BLOCK]
                        if u == 0:
                            bias_l = jnp.where(n0 == 0, NEG_INF, bias_l)
                        s_l = _dot_nt(q, kt[u]) * scale + bias_l
                    s_scr[i0 + u, :, :Q_BLOCK] = s_l
                    s_scr[i0 + u, :, Q_BLOCK:] = s_r
                    m = jnp.maximum(jnp.max(s_l, axis=-1, keepdims=True), jnp.max(s_r, axis=-1, keepdims=True))
                    mb_scr[i0 + u] = jnp.broadcast_to(m, (Q_BLOCK, HEAD_DIM))
            return carry

        def probs(i, carry):
            mb = mb_scr[i]
            p_scr[i, :, :Q_BLOCK] = jnp.exp(s_scr[i, :, :Q_BLOCK] - mb).astype(BF16)
            p_scr[i, :, Q_BLOCK:] = jnp.exp(s_scr[i, :, Q_BLOCK:] - mb).astype(BF16)
            return carry

        def values(it, carry):
            for i0, r, n0 in runs(it):
                vt = tiles(v_ref, r, n0, True)
                for u in range(run):
                    i = i0 + u
                    ol = _dot(p_scr[i, :, Q_BLOCK:], vt[u + 1])
                    if vt[u] is not None:
                        ol = ol + _dot(p_scr[i, :, :Q_BLOCK], vt[u])
                    o = ol[:, :HEAD_DIM]
                    lb = ol[:, HEAD_DIM:]
                    mb = mb_scr[i]
                    sl_q = block_rows(r, n0 + u)
                    if first:
                        m_scr[sl_q, :] = mb
                        l_scr[sl_q, :] = lb
                        acc_scr[sl_q, :] = o
                    else:
                        m0 = m_scr[sl_q, :]
                        mn = jnp.maximum(m0, mb)
                        a0 = jnp.exp(m0 - mn)
                        a1 = jnp.exp(mb - mn)
                        m_scr[sl_q, :] = mn
                        l_scr[sl_q, :] = a0 * l_scr[sl_q, :] + a1 * lb
                        acc_scr[sl_q, :] = a0 * acc_scr[sl_q, :] + a1 * o
            return carry

        lax.fori_loop(0, nblk // ATTN_UNROLL, scores, 0)
        lax.fori_loop(0, nblk, probs, 0, unroll=ATTN_UNROLL)
        lax.fori_loop(0, nblk // ATTN_UNROLL, values, 0)

    for s in range(N_ATT_GROUPS):
        pl.when(step == s)(functools.partial(run_group, s == 0, (s + ATTN_FIRST_GROUP) % N_ATT_GROUPS))

    @pl.when(step == N_ATT_GROUPS - 1)
    def _():
        o_ref[...] = (acc_scr[...] / l_scr[...]).astype(o_ref.dtype)


def _attn_prompt(qkv, rel_bias, bucket_tiles, nbatch):
    M = qkv.shape[1]
    T = M // nbatch
    H, G = HEADS_PER_GROUP, N_ATT_GROUPS
    nblk = T // Q_BLOCK

    def qkv_spec(which):
        return pl.BlockSpec((None, T, HEAD_DIM),
                            lambda b, h, s: (which * N_ATT_HEADS + _attn_group_of_step(s) * H + h, b, 0))

    return pl.pallas_call(
        functools.partial(_attn_prompt_kernel, T),
        grid=(nbatch, H, G),
        in_specs=[pl.BlockSpec(memory_space=pltpu.SMEM),
                  qkv_spec(0), qkv_spec(1), qkv_spec(2),
                  pl.BlockSpec((None, Q_BLOCK, 2 * Q_BLOCK), lambda b, h, s: (_attn_group_of_step(s), 0, 0))],
        out_specs=pl.BlockSpec((T, HEAD_DIM), lambda b, h, s: (b, h)),
        out_shape=jax.ShapeDtypeStruct((M, ATT_WIDTH), BF16),
        scratch_shapes=[pltpu.VMEM((T, HEAD_DIM), F32)] * 3
                       + [pltpu.VMEM((nblk, Q_BLOCK, 2 * Q_BLOCK), F32),
                          pltpu.VMEM((nblk, Q_BLOCK, 2 * Q_BLOCK), BF16),
                          pltpu.VMEM((nblk, Q_BLOCK, HEAD_DIM), F32),
                          pltpu.VMEM((Q_BLOCK, 2 * Q_BLOCK), F32)],
        compiler_params=_cparams(("arbitrary", "arbitrary", "arbitrary")),
        name="attn_prompt",
    )(rel_bias.astype(F32), qkv, qkv, qkv, bucket_tiles)


def _attn_sample_kernel(window, dil, q_ref, ck_ref, cv_ref, nk_ref, nv_ref, bias_ref, o_ref, lse_ref):
    S = q_ref.shape[0]
    K = window // dil + 1
    scale = HEAD_DIM ** -0.5
    bias = bias_ref[...]
    by_residue = len(ck_ref.shape) == 4

    def buffered(ref, s, n):
        return ref[pl.ds(0, n), s] if by_residue else ref[pl.ds(s, n, stride=dil)]

    for s in range(S):
        n_c = (window - 1 - s) // dil + 1
        qs = q_ref[s]
        new_rows = [s + j * dil - window for j in range(n_c, K)]
        kk = jnp.concatenate([buffered(ck_ref, s, n_c)] + [nk_ref[pl.ds(i, 1)] for i in new_rows], axis=0)
        vv = jnp.concatenate([buffered(cv_ref, s, n_c)] + [nv_ref[pl.ds(i, 1)] for i in new_rows], axis=0)
        lg = jnp.sum(kk * qs[None], axis=-1, keepdims=True) * scale + bias
        m = jnp.max(lg, axis=0)
        p = jnp.exp(lg - m[None])
        l = jnp.sum(p, axis=0)
        o_ref[s] = jnp.sum(p * vv, axis=0) / l
        lse_ref[s] = m + jnp.log(l)


def _attn_sample(q, cache_k, cache_v, new_k, new_v, bias, layer, window, dil):
    B, S = q.shape[0], q.shape[1]
    H = HEADS_PER_GROUP
    K = window // dil + 1
    small = pl.BlockSpec((None, S, H, HEAD_DIM), lambda b: (b, 0, 0, 0))
    if dil > S:
        depth = cache_k.shape[0]
        cache_k = cache_k.reshape(depth, B, window // dil, dil, H, HEAD_DIM)
        cache_v = cache_v.reshape(depth, B, window // dil, dil, H, HEAD_DIM)
        cache = pl.BlockSpec((None, None, window // dil, S, H, HEAD_DIM), lambda b: (layer, b, 0, 0, 0, 0))
    else:
        cache = pl.BlockSpec((None, None, window, H, HEAD_DIM), lambda b: (layer, b, 0, 0, 0))
    return pl.pallas_call(
        functools.partial(_attn_sample_kernel, window, dil),
        grid=(B,),
        in_specs=[small, cache, cache, small, small, pl.BlockSpec((K, H, HEAD_DIM), lambda b: (0, 0, 0))],
        out_specs=[small, small],
        out_shape=[jax.ShapeDtypeStruct((B, S, H, HEAD_DIM), F32)] * 2,
        compiler_params=_cparams(("arbitrary",)),
        name="attn_sample_w%d" % window,
    )(q, cache_k, cache_v, new_k, new_v, bias)


def _merge_groups_kernel(o0, o1, o2, l0, l1, l2, y_ref):
    a, b, c = l0[...], l1[...], l2[...]
    m = jnp.maximum(jnp.maximum(a, b), c)
    ea, eb, ec = jnp.exp(a - m), jnp.exp(b - m), jnp.exp(c - m)
    y_ref[...] = ((ea * o0[...] + eb * o1[...] + ec * o2[...]) / (ea + eb + ec)).astype(y_ref.dtype)


def _merge_groups(outs, lses):
    shape = outs[0].shape
    return pl.pallas_call(
        _merge_groups_kernel,
        out_shape=jax.ShapeDtypeStruct(shape, BF16),
        name="merge_groups",
    )(*outs, *lses)


def _sample_bias(rel_bias):
    out = []
    for g, (window, dil) in enumerate(WINDOWS):
        K = window // dil + 1
        steps = (K - 1) - np.arange(K)
        tab = rel_bias[:, g * HEADS_PER_GROUP:(g + 1) * HEADS_PER_GROUP].astype(F32)
        b = tab[_bucket_np(steps * dil)]
        out.append(jnp.broadcast_to(b[:, :, None], (K, HEADS_PER_GROUP, HEAD_DIM)))
    return out


def _gelu_tanh(x):
    return 0.5 * x * (1.0 + jnp.tanh(math.sqrt(2.0 / math.pi) * (x + 0.044715 * (x * x * x))))


def _mix_out_kernel(y_ref, u_ref, yb_ref, ga_ref, gb_ref, x_ref, d_ref, wglu_ref, bglu_ref, wa_ref, wb_ref,
                    wout_ref, gn_ref, x1_ref, h2_ref):
    tm = x_ref.shape[0]
    sub = min(tm, MIX_ROW_SUB)
    for r in range(tm // sub):
        rows = pl.ds(r * sub, sub)
        y = y_ref[rows, :] + d_ref[...] * u_ref[rows, :]
        z = _gelu_tanh(y)
        ya = z * jax.nn.sigmoid(_dot(z.astype(BF16), wglu_ref[...]) + bglu_ref[...])
        mix = (ga_ref[rows, :] * _dot(ya.astype(BF16), wa_ref[...])
               + gb_ref[rows, :] * _dot(yb_ref[rows, :], wb_ref[...]))
        x1 = x_ref[rows, :] + _dot(mix.astype(BF16), wout_ref[...])
        x1_ref[rows, :] = x1
        h2_ref[rows, :] = _rmsnorm_rows(x1, gn_ref[...])


def _mix_out(y_ssm, u, yb, gates, x, ssm_d, wglu, bglu, wa, wb, wout, layer, gain, tm):
    M = x.shape[0]
    row = lambda w: pl.BlockSpec((tm, w), lambda i: (i, 0))
    vec = lambda b: pl.BlockSpec((1, b), lambda i: (0, 0), pipeline_mode=pl.Buffered(1))
    full = lambda a, b: pl.BlockSpec((None, a, b), lambda i: (layer, 0, 0), pipeline_mode=pl.Buffered(1))
    return pl.pallas_call(
        _mix_out_kernel,
        grid=(M // tm,),
        in_specs=[row(SSM_WIDTH), row(SSM_WIDTH), row(ATT_WIDTH),
                  pl.BlockSpec((tm, D_MODEL), lambda i: (i, 0)),
                  pl.BlockSpec((tm, D_MODEL), lambda i: (i, 1)),
                  row(D_MODEL),
                  vec(SSM_WIDTH), full(SSM_WIDTH, SSM_WIDTH), vec(SSM_WIDTH),
                  full(SSM_WIDTH, D_MODEL), full(ATT_WIDTH, D_MODEL), full(D_MODEL, D_MODEL), vec(D_MODEL)],
        out_specs=[row(D_MODEL), row(D_MODEL)],
        out_shape=[jax.ShapeDtypeStruct((M, D_MODEL), F32), jax.ShapeDtypeStruct((M, D_MODEL), BF16)],
        compiler_params=_cparams(("arbitrary",)),
        name="mix_out",
    )(y_ssm, u, yb, gates, gates, x, ssm_d.reshape(1, -1), wglu, bglu.reshape(1, -1), wa, wb, wout,
      gain.reshape(1, -1))


def _ffn_kernel(emit_norm, h_ref, x_ref, hs_ref, xs_ref, wg_ref, wu_ref, wd_ref, gn_ref, *rest):
    if emit_norm:
        o_ref, os_ref, hn_ref, hns_ref, wg_scr, wu_scr, wd_scr = rest
    else:
        o_ref, os_ref, wg_scr, wu_scr, wd_scr = rest
    i = pl.program_id(0)
    f = pl.program_id(1)
    last = pl.num_programs(1) - 1

    @pl.when(f == 0)
    def _():
        o_ref[...] = x_ref[...]

    @pl.when((f == 0) & (i == 0))
    def _():
        os_ref[...] = xs_ref[...]

    def swiglu(h):
        a = jax.nn.silu(_dot(h, wg_scr[...])) * _dot(h, wu_scr[...])
        return _dot(a.astype(BF16), wd_scr[...])

    tm = h_ref.shape[0]
    sub = min(tm, FFN_ROW_SUB)
    wg_scr[...] = wg_ref[...].astype(BF16)
    wu_scr[...] = wu_ref[...].astype(BF16)
    wd_scr[...] = wd_ref[...].astype(BF16)
    for r in range(tm // sub):
        rows = pl.ds(r * sub, sub)
        o_ref[rows, :] += swiglu(h_ref[rows, :])

    @pl.when(i == 0)
    def _():
        os_ref[...] += swiglu(hs_ref[...])

    if emit_norm:
        @pl.when(f == last)
        def _():
            for r in range(tm // sub):
                rows = pl.ds(r * sub, sub)
                hn_ref[rows, :] = _rmsnorm_rows(o_ref[rows, :], gn_ref[...])

        @pl.when((f == last) & (i == 0))
        def _():
            hns_ref[...] = _rmsnorm_rows(os_ref[...], gn_ref[...])


def _ffn(h2, x1, h2s, x1s, wg, wu, wd, layer, next_gain, tm):
    M, Ms = x1.shape[0], x1s.shape[0]
    nf = FFN_HIDDEN // FFN_BLOCK
    emit_norm = next_gain is not None
    gain = next_gain if emit_norm else jnp.ones((D_MODEL,), F32)
    row = pl.BlockSpec((tm, D_MODEL), lambda i, f: (i, 0))
    whole = pl.BlockSpec((Ms, D_MODEL), lambda i, f: (0, 0))
    out_specs = [row, whole]
    out_shape = [jax.ShapeDtypeStruct((M, D_MODEL), F32), jax.ShapeDtypeStruct((Ms, D_MODEL), F32)]
    if emit_norm:
        out_specs += [row, whole]
        out_shape += [jax.ShapeDtypeStruct((M, D_MODEL), BF16), jax.ShapeDtypeStruct((Ms, D_MODEL), BF16)]
    return pl.pallas_call(
        functools.partial(_ffn_kernel, emit_norm),
        grid=(M // tm, nf),
        in_specs=[row,
                  pl.BlockSpec((tm, D_MODEL), lambda i, f: (i, 0), pipeline_mode=pl.Buffered(1)),
                  whole, whole,
                  pl.BlockSpec((None, D_MODEL, FFN_BLOCK), lambda i, f: (layer, 0, f)),
                  pl.BlockSpec((None, D_MODEL, FFN_BLOCK), lambda i, f: (layer, 0, f)),
                  pl.BlockSpec((None, FFN_BLOCK, D_MODEL), lambda i, f: (layer, f, 0)),
                  pl.BlockSpec((1, D_MODEL), lambda i, f: (0, 0))],
        out_specs=out_specs,
        out_shape=out_shape,
        scratch_shapes=[pltpu.VMEM((D_MODEL, FFN_BLOCK), BF16), pltpu.VMEM((D_MODEL, FFN_BLOCK), BF16),
                        pltpu.VMEM((FFN_BLOCK, D_MODEL), BF16)],
        compiler_params=_cparams(("arbitrary", "arbitrary")),
        name="ffn",
    )(h2, x1, h2s, x1s, wg, wu, wd, gain.reshape(1, D_MODEL))


def _cache_shift_kernel(ck_ref, cv_ref, hk_ref, hv_ref, nk_ref, nv_ref, ok_ref, ov_ref):
    c = pl.program_id(2)
    last = pl.num_programs(2) - 1
    R = ck_ref.shape[0]
    S = nk_ref.shape[0]
    for cache, halo, new, out in ((ck_ref, hk_ref, nk_ref, ok_ref), (cv_ref, hv_ref, nv_ref, ov_ref)):
        out[pl.ds(0, R - S)] = cache[pl.ds(S, R - S)]

        @pl.when(c == last)
        def _():
            out[pl.ds(R - S, S)] = new[...]

        @pl.when(c < last)
        def _():
            out[pl.ds(R - S, S)] = halo[...]


def _cache_shift(cache_k, cache_v, new_k, new_v):
    depth, B, W, H, E = cache_k.shape
    S = new_k.shape[2]
    R = min(W, CACHE_ROWS)
    nchunks = W // R
    blk = pl.BlockSpec((None, None, R, H, E), lambda l, b, c: (l, b, c, 0, 0))
    halo = pl.BlockSpec((None, None, None, S, H, E),
                        lambda l, b, c: (l, b, jnp.minimum((c + 1) * (R // S), W // S - 1), 0, 0, 0))
    new = pl.BlockSpec((None, None, S, H, E), lambda l, b, c: (l, b, 0, 0, 0))
    as_rows = lambda a: a.reshape(depth, B, W // S, S, H, E)
    return pl.pallas_call(
        _cache_shift_kernel,
        grid=(depth, B, nchunks),
        in_specs=[blk, blk, halo, halo, new, new],
        out_specs=[blk, blk],
        out_shape=[jax.ShapeDtypeStruct(cache_k.shape, cache_k.dtype)] * 2,
        compiler_params=_cparams(("arbitrary", "arbitrary", "arbitrary")),
        name="cache_shift_w%d" % W,
    )(cache_k, cache_v, as_rows(cache_k), as_rows(cache_v), new_k, new_v)


def _kv_tails_kernel(*refs):
    depth = (len(refs) - 2) // 2
    ok_ref, ov_ref = refs[-2:]
    H, R = refs[0].shape[0], refs[0].shape[1]
    for l in range(depth):
        @pl.when(pl.program_id(0) == l)
        def _():
            for src, out in ((refs[2 * l], ok_ref), (refs[2 * l + 1], ov_ref)):
                for h in range(H):
                    out[pl.ds(h, R, stride=H), :] = src[h]


def _kv_tails(qkv_layers, g, window, nbatch):
    depth = len(qkv_layers)
    M = qkv_layers[0].shape[1]
    T = M // nbatch
    H, E = HEADS_PER_GROUP, HEAD_DIM
    R = min(window, CACHE_ROWS)
    nchunks = window // R
    first = (T - window) // R

    def src(layer, which):
        def index(l, b, c):
            bb = jnp.where(l < layer, 0, jnp.where(l > layer, nbatch - 1, b))
            cc = jnp.where(l < layer, 0, jnp.where(l > layer, nchunks - 1, c))
            return (which * N_ATT_GROUPS + g, bb * (T // R) + first + cc, 0)
        return pl.BlockSpec((H, R, E), index)

    args, in_specs = [], []
    for layer, qkv in enumerate(qkv_layers):
        args += [qkv, qkv]
        in_specs += [src(layer, 1), src(layer, 2)]
    out_spec = pl.BlockSpec((None, R * H, E), lambda l, b, c: (l, b * nchunks + c, 0))
    return pl.pallas_call(
        _kv_tails_kernel,
        grid=(depth, nbatch, nchunks),
        in_specs=in_specs,
        out_specs=[out_spec, out_spec],
        out_shape=[jax.ShapeDtypeStruct((depth, nbatch * window * H, E), qkv_layers[0].dtype)] * 2,
        compiler_params=_cparams(("arbitrary", "arbitrary", "arbitrary")),
        name="kv_tails_w%d" % window,
    )(*args)


def _mix(x2d, u, y_ssm, yb, gates, prm, tm):
    return _mix_out(y_ssm, u, yb, gates, x2d, prm["ssm_d"], prm["w_glu"], prm["b_glu"], prm["w_branch_a"],
                    prm["w_branch_b"], prm["w_out"], prm["layer"], prm["norm_ffn"], tm)


def kernel(x_prompt, x_sample, state_ssm_re, state_ssm_im, cache_k_w128, cache_v_w128, cache_k_w512, cache_v_w512,
           cache_k_w2048, cache_v_w2048, rel_bias, norm_mix, norm_ffn, q_norm, k_norm, w_in, ssm_lambda_re,
           ssm_lambda_im, ssm_log_dt, ssm_b_re, ssm_b_im, ssm_c_re, ssm_c_im, ssm_d, w_glu, b_glu, w_branch_a,
           w_branch_b, w_out, w_ffn_gate, w_ffn_up, w_ffn_down):
    depth = w_in.shape[0]
    B, T, _ = x_prompt.shape
    SB, S, _ = x_sample.shape
    H = HEADS_PER_GROUP
    caches_k = (cache_k_w128, cache_k_w512, cache_k_w2048)
    caches_v = (cache_v_w128, cache_v_w512, cache_v_w2048)

    bucket_tiles = _prompt_bucket_tiles()
    bias_sample = _sample_bias(rel_bias)
    sel_k, sel_b = _ssm_select_matrices()

    xp = x_prompt.reshape(B * T, D_MODEL)
    xs = x_sample.reshape(SB * S, D_MODEL)
    p_re, p_im, s_re, s_im = [], [], [], []
    qkv_prompt = []
    new_k = [[] for _ in WINDOWS]
    new_v = [[] for _ in WINDOWS]

    weights = {
        "w_glu": w_glu.astype(BF16), "w_branch_a": w_branch_a.astype(BF16),
        "w_branch_b": w_branch_b.astype(BF16), "w_out": w_out.astype(BF16),
    }

    hp, hs = xp, xs

    for l in range(depth):
        ek, eb, ea, lnr, lni = _ssm_prep(ssm_lambda_re[l], ssm_lambda_im[l], ssm_log_dt[l], ssm_b_re[l],
                                         ssm_b_im[l], ssm_c_re[l], ssm_c_im[l])
        prm = dict(weights)
        prm.update({
            "layer": l, "norm_ffn": norm_ffn[l], "ssm_d": ssm_d[l], "b_glu": b_glu[l],
        })

        (u, qkv, gates), (us, qkvs, gates_s) = _in_proj(hp, hs, norm_mix[0] if l == 0 else None, w_in, l,
                                                        q_norm[l], k_norm[l], IN_PROJ_ROWS)

        zeros = jnp.zeros((B, N_SSM_RANGES, 2 * RANGE_STATE // LANES, LANES), F32)
        h0 = _state_to_tiles(state_ssm_re[l], state_ssm_im[l])
        (y_ssm, hfin), (ys_ssm, hfin_s) = _ssm_chunk([(u, zeros, B), (us, h0, SB)], ek, eb, ea, sel_k, sel_b,
                                                     lnr, lni)

        yb = _attn_prompt(qkv, rel_bias, bucket_tiles, B)
        x1, h2 = _mix(xp, u, y_ssm, yb, gates, prm, MIX_ROWS)
        hr, hi = _tiles_to_state(hfin)
        p_re.append(hr)
        p_im.append(hi)
        qkv_prompt.append(qkv)

        tok = jnp.transpose(qkvs.reshape(3, N_ATT_GROUPS, H, SB, S, HEAD_DIM), (0, 1, 3, 4, 2, 5))
        outs, lses = [], []
        for g, (window, dil) in enumerate(WINDOWS):
            nk_g, nv_g = tok[1, g], tok[2, g]
            new_k[g].append(nk_g)
            new_v[g].append(nv_g)
            o_g, lse_g = _attn_sample(tok[0, g], caches_k[g], caches_v[g], nk_g, nv_g, bias_sample[g], l,
                                      window, dil)
            outs.append(o_g)
            lses.append(lse_g)
        ybs = _merge_groups(outs, lses).reshape(SB * S, ATT_WIDTH)
        x1s, h2s = _mix(xs, us, ys_ssm, ybs, gates_s, prm, SB * S)
        hr, hi = _tiles_to_state(hfin_s)
        s_re.append(hr)
        s_im.append(hi)

        next_gain = norm_mix[l + 1] if l + 1 < depth else None
        res = _ffn(h2, x1, h2s, x1s, w_ffn_gate, w_ffn_up, w_ffn_down, l, next_gain, FFN_ROWS)
        if next_gain is None:
            xp, xs = res
        else:
            xp, xs, hp, hs = res

    shifted = []
    for g in range(N_ATT_GROUPS):
        shifted += _cache_shift(caches_k[g], caches_v[g], jnp.stack(new_k[g]), jnp.stack(new_v[g]))
    tails = [t.reshape(depth, B, window, H, HEAD_DIM)
             for g, (window, _) in enumerate(WINDOWS) for t in _kv_tails(qkv_prompt, g, window, B)]

    return (xp.reshape(B, T, D_MODEL), xs.reshape(SB, S, D_MODEL),
            jnp.stack(p_re), jnp.stack(p_im), *tails,
            jnp.stack(s_re), jnp.stack(s_im), *shifted)
```

```python
import functools
import math

import numpy as np
import jax
import jax.numpy as jnp
from jax import lax
from jax.experimental import pallas as pl
from jax.experimental.pallas import tpu as pltpu

F32 = jnp.float32
BF16 = jnp.bfloat16

D_MODEL = 2048
HEAD_DIM = 128
HEADS_PER_GROUP = 8
WINDOWS = ((128, 1), (512, 4), (2048, 16))
N_ATT_GROUPS = len(WINDOWS)
N_ATT_HEADS = N_ATT_GROUPS * HEADS_PER_GROUP
ATT_WIDTH = HEADS_PER_GROUP * HEAD_DIM
QKV_WIDTH = N_ATT_HEADS * HEAD_DIM
Q_BLOCK = 128
SSM_WIDTH = D_MODEL // 2
SSM_GROUP = 16
N_SSM_GROUPS = SSM_WIDTH // SSM_GROUP
SSM_STATE = 64
IN_WIDTH = SSM_WIDTH + 3 * QKV_WIDTH + 2 * D_MODEL
FFN_HIDDEN = 5632
N_BUCKETS = 32
REL_MAX_DISTANCE = 2048
EPS = 1e-6
NEG_INF = -1e30

LANES = 128
SUBLANES = 8
VMEM_LIMIT_BYTES = 60 * 1024 * 1024

SSM_CHUNK = 8
SSM_RANGE_GROUPS = LANES // SSM_GROUP
N_SSM_RANGES = N_SSM_GROUPS // SSM_RANGE_GROUPS
RANGE_STATE = SSM_RANGE_GROUPS * SSM_STATE
COL_BLOCK = 512
FFN_BLOCK = 256
ATTN_UNROLL = 32
ATTN_FIRST_GROUP = 2
CACHE_ROWS = 1024
IN_PROJ_ROWS = 2048
FFN_ROWS = 1024
MIX_ROWS = 256
ROW_SUB = 256
FFN_ROW_SUB = 512
MIX_ROW_SUB = 256


def _cparams(sem):
    return pltpu.CompilerParams(dimension_semantics=sem, vmem_limit_bytes=VMEM_LIMIT_BYTES)


def _dot(a, b):
    return jnp.dot(a, b, preferred_element_type=F32)


def _dot_nt(a, b):
    return lax.dot_general(a, b, (((1,), (1,)), ((), ())), preferred_element_type=F32)


def _ssm_prep_kernel(*refs):
    for g in range(SSM_RANGE_GROUPS):
        _ssm_prep_group(*[ref.at[g] for ref in refs])


def _ssm_prep_group(lre_ref, lim_ref, ldt_ref, btre_ref, btim_ref, cre_ref, cim_ref,
                    kp_ref, bpre_ref, bpim_ref, are_ref, aim_ref, lnre_ref, lnim_ref):
    L = SSM_CHUNK
    lr = lre_ref[...]
    li = lim_ref[...]
    dt = jnp.exp(ldt_ref[...])
    er = jnp.exp(lr * dt)
    lbr = er * jnp.cos(li * dt)
    lbi = er * jnp.sin(li * dt)
    nr = lbr - 1.0
    dd = lr * lr + li * li
    rr = (nr * lr + lbi * li) / dd
    ri = (lbi * lr - nr * li) / dd
    btr = btre_ref[...]
    bti = btim_ref[...]
    bbr = rr * btr - ri * bti
    bbi = rr * bti + ri * btr
    cr = cre_ref[...]
    ci = cim_ref[...]
    pr = [jnp.ones_like(lbr)]
    pi = [jnp.zeros_like(lbr)]
    for _ in range(L):
        pr.append(pr[-1] * lbr - pi[-1] * lbi)
        pi.append(pr[-2] * lbi + pi[-1] * lbr)
    xr = [cr * pr[t] - ci * pi[t] for t in range(L + 1)]
    xi = [cr * pi[t] + ci * pr[t] for t in range(L + 1)]
    xr_k = jnp.concatenate(xr[:L], axis=0)
    xi_k = jnp.concatenate(xi[:L], axis=0)
    hp = lax.Precision.HIGHEST
    kp = (lax.dot_general(bbr, xr_k, (((1,), (1,)), ((), ())), precision=hp, preferred_element_type=F32)
          - lax.dot_general(bbi, xi_k, (((1,), (1,)), ((), ())), precision=hp, preferred_element_type=F32))
    lane = lax.broadcasted_iota(jnp.int32, kp.shape, 1)
    blocks = [kp] + [jnp.where(lane >= j * SSM_GROUP, pltpu.roll(kp, j * SSM_GROUP, axis=1), 0.0)
                     for j in range(1, L)]
    kp_ref[...] = jnp.concatenate(blocks, axis=0)
    are_ref[...] = jnp.concatenate(xr[1:], axis=0)
    aim_ref[...] = jnp.concatenate(xi[1:], axis=0)
    bpre_ref[...] = jnp.concatenate([pr[L - 1 - j] * bbr - pi[L - 1 - j] * bbi for j in range(L)], axis=0)
    bpim_ref[...] = jnp.concatenate([pr[L - 1 - j] * bbi + pi[L - 1 - j] * bbr for j in range(L)], axis=0)
    lnre_ref[...] = pr[L]
    lnim_ref[...] = pi[L]


def _ssm_prep(lam_re, lam_im, log_dt, b_re, b_im, c_re, c_im):
    G, P, C, L = N_SSM_GROUPS, SSM_STATE, SSM_GROUP, SSM_CHUNK
    row = lambda a: a.reshape(G, 1, -1)
    bt_re = jnp.swapaxes(b_re, 1, 2)
    bt_im = jnp.swapaxes(b_im, 1, 2)
    GL = SSM_RANGE_GROUPS
    vec = pl.BlockSpec((GL, 1, P), lambda r: (r, 0, 0))
    mat = pl.BlockSpec((GL, C, P), lambda r: (r, 0, 0))
    big = pl.BlockSpec((GL, L * C, P), lambda r: (r, 0, 0))
    kp, bpre, bpim, are, aim, lnre, lnim = pl.pallas_call(
        _ssm_prep_kernel,
        grid=(G // GL,),
        in_specs=[vec, vec, pl.BlockSpec((GL, 1, 1), lambda r: (r, 0, 0)), mat, mat, mat, mat],
        out_specs=[pl.BlockSpec((GL, L * C, L * C), lambda r: (r, 0, 0)), big, big, big, big, vec, vec],
        out_shape=[jax.ShapeDtypeStruct((G, L * C, L * C), F32)] + [jax.ShapeDtypeStruct((G, L * C, P), F32)] * 4
                  + [jax.ShapeDtypeStruct((G, 1, P), F32)] * 2,
        compiler_params=_cparams(("arbitrary",)),
        name="ssm_prep",
    )(row(lam_re), row(lam_im), log_dt.reshape(G, 1, 1), bt_re, bt_im, c_re, c_im)

    R, GL = N_SSM_RANGES, SSM_RANGE_GROUPS

    def by_range(a, n_outer):
        n_inner = a.shape[1] // n_outer
        a = a.reshape(R, GL, n_outer, n_inner, LANES)
        return jnp.transpose(a, (0, 2, 1, 3, 4)).reshape(R, n_outer * GL * n_inner, LANES).astype(BF16)

    ek = by_range(kp, L)
    eb = by_range(jnp.concatenate([bpre, bpim], axis=-1), L)
    a_t = jnp.swapaxes(jnp.concatenate([are, -aim], axis=-1), 1, 2)
    ea = by_range(a_t, 2)

    half = RANGE_STATE // LANES
    lnr_t = lnre.reshape(R, half, LANES)
    lni_t = lnim.reshape(R, half, LANES)
    lnr = jnp.concatenate([lnr_t, lnr_t], axis=1)
    lni = jnp.concatenate([-lni_t, lni_t], axis=1)
    return ek, eb, ea, lnr, lni


def _ssm_select_matrices():
    q = np.arange(SSM_CHUNK * LANES)
    r = np.arange(LANES)
    sel_k = (r[:, None] // SSM_GROUP == q[None, :] // LANES) & (r[:, None] % SSM_GROUP == q[None, :] % SSM_GROUP)
    sel_b = (r[:, None] // SSM_STATE == q[None, :] // RANGE_STATE) & (r[:, None] % SSM_STATE == q[None, :] % SSM_STATE)
    return jnp.asarray(sel_k, BF16), jnp.asarray(sel_b, BF16)


HEADS_PER_BLOCK = COL_BLOCK // HEAD_DIM
_SEG_U = 0
_SEG_Q = SSM_WIDTH // COL_BLOCK
_SEG_K = _SEG_Q + QKV_WIDTH // COL_BLOCK
_SEG_V = _SEG_K + QKV_WIDTH // COL_BLOCK
_SEG_GATE = _SEG_V + QKV_WIDTH // COL_BLOCK
_SEG_END = IN_WIDTH // COL_BLOCK


def _head_norm(res, gain):
    outs = []
    for h in range(HEADS_PER_BLOCK):
        t = res[:, h * HEAD_DIM:(h + 1) * HEAD_DIM]
        ms = jnp.mean(t * t, axis=-1, keepdims=True)
        outs.append(t * lax.rsqrt(ms + EPS) * gain)
    return outs


def _rmsnorm_rows(x, gain):
    ms = jnp.mean(x * x, axis=-1, keepdims=True)
    return (x * lax.rsqrt(ms + EPS) * gain).astype(BF16)


def _in_proj_kernel(normalize, h_ref, hs_ref, g_ref, w_ref, qn_ref, kn_ref,
                    u_ref, qkv_ref, gate_ref, us_ref, qkvs_ref, gates_s_ref, w_scr, *norm_scr):
    i = pl.program_id(0)
    j = pl.program_id(1)
    tm = h_ref.shape[0]
    sub = min(tm, ROW_SUB)

    if normalize:
        h_src, hs_src = norm_scr

        @pl.when(j == 0)
        def _():
            for r in range(tm // sub):
                rows = pl.ds(r * sub, sub)
                h_src[rows, :] = _rmsnorm_rows(h_ref[rows, :], g_ref[...])

        @pl.when((j == 0) & (i == 0))
        def _():
            hs_src[...] = _rmsnorm_rows(hs_ref[...], g_ref[...])
    else:
        h_src, hs_src = h_ref, hs_ref

    def row_blocks():
        w_scr[...] = w_ref[...].astype(BF16)
        for r in range(tm // sub):
            rows = pl.ds(r * sub, sub)
            yield rows, _dot(h_src[rows, :], w_scr[...])

    def sample_res():
        return _dot(hs_src[...], w_scr[...])

    @pl.when(j < _SEG_Q)
    def _():
        for rows, res in row_blocks():
            u_ref[rows, :] = res

        @pl.when(i == 0)
        def _():
            us_ref[j] = sample_res()

    def split_heads(res, gain_ref):
        if gain_ref is not None:
            return _head_norm(res, gain_ref[...])
        return [res[:, h * HEAD_DIM:(h + 1) * HEAD_DIM] for h in range(HEADS_PER_BLOCK)]

    def store_heads(gain_ref):
        for rows, res in row_blocks():
            for h, t in enumerate(split_heads(res, gain_ref)):
                qkv_ref[h, rows, :] = t

        @pl.when(i == 0)
        def _():
            base = (j - _SEG_Q) * HEADS_PER_BLOCK
            for h, t in enumerate(split_heads(sample_res(), gain_ref)):
                qkvs_ref[base + h] = t

    pl.when((j >= _SEG_Q) & (j < _SEG_K))(functools.partial(store_heads, qn_ref))
    pl.when((j >= _SEG_K) & (j < _SEG_V))(functools.partial(store_heads, kn_ref))
    pl.when((j >= _SEG_V) & (j < _SEG_GATE))(functools.partial(store_heads, None))

    @pl.when(j >= _SEG_GATE)
    def _():
        for rows, res in row_blocks():
            gate_ref[rows, :] = jax.nn.sigmoid(res).astype(gate_ref.dtype)

        @pl.when(i == 0)
        def _():
            gates_s_ref[j - _SEG_GATE] = jax.nn.sigmoid(sample_res()).astype(gates_s_ref.dtype)


def _in_proj(h, hs, norm_gain, w, layer, q_gain, k_gain, tm):
    M, Ms = h.shape[0], hs.shape[0]
    n_heads = 3 * N_ATT_HEADS
    n_gate = _SEG_END - _SEG_GATE
    normalize = norm_gain is not None
    gain = norm_gain if normalize else jnp.ones((D_MODEL,), F32)
    norm_scratch = [pltpu.VMEM((tm, D_MODEL), BF16), pltpu.VMEM((Ms, D_MODEL), BF16)] if normalize else []
    whole = lambda shape: pl.BlockSpec(shape, lambda i, j: (0,) * len(shape))
    u, qkv, gates, us, qkvs, gates_s = pl.pallas_call(
        functools.partial(_in_proj_kernel, normalize),
        grid=(M // tm, _SEG_END),
        in_specs=[
            pl.BlockSpec((tm, D_MODEL), lambda i, j: (i, 0), pipeline_mode=pl.Buffered(1)),
            whole((Ms, D_MODEL)),
            pl.BlockSpec((1, D_MODEL), lambda i, j: (0, 0)),
            pl.BlockSpec((None, D_MODEL, COL_BLOCK), lambda i, j: (layer, 0, j)),
            pl.BlockSpec((1, HEAD_DIM), lambda i, j: (0, 0)),
            pl.BlockSpec((1, HEAD_DIM), lambda i, j: (0, 0)),
        ],
        out_specs=[
            pl.BlockSpec((tm, COL_BLOCK), lambda i, j: (i, jnp.clip(j, 0, _SEG_Q - 1))),
            pl.BlockSpec((HEADS_PER_BLOCK, tm, HEAD_DIM),
                         lambda i, j: (jnp.clip(j - _SEG_Q, 0, _SEG_GATE - _SEG_Q - 1), i, 0)),
            pl.BlockSpec((tm, COL_BLOCK), lambda i, j: (i, jnp.clip(j - _SEG_GATE, 0, _SEG_END - _SEG_GATE - 1))),
            whole((_SEG_Q, Ms, COL_BLOCK)),
            whole((n_heads, Ms, HEAD_DIM)),
            whole((n_gate, Ms, COL_BLOCK)),
        ],
        out_shape=[
            jax.ShapeDtypeStruct((M, SSM_WIDTH), F32),
            jax.ShapeDtypeStruct((n_heads, M, HEAD_DIM), F32),
            jax.ShapeDtypeStruct((M, 2 * D_MODEL), BF16),
            jax.ShapeDtypeStruct((_SEG_Q, Ms, COL_BLOCK), F32),
            jax.ShapeDtypeStruct((n_heads, Ms, HEAD_DIM), F32),
            jax.ShapeDtypeStruct((n_gate, Ms, COL_BLOCK), BF16),
        ],
        scratch_shapes=[pltpu.VMEM((D_MODEL, COL_BLOCK), BF16)] + norm_scratch,
        compiler_params=_cparams(("arbitrary", "arbitrary")),
        name="in_proj",
    )(h, hs, gain.reshape(1, D_MODEL), w, q_gain.reshape(1, HEAD_DIM), k_gain.reshape(1, HEAD_DIM))
    us = jnp.swapaxes(us, 0, 1).reshape(Ms, SSM_WIDTH)
    gates_s = jnp.swapaxes(gates_s, 0, 1).reshape(Ms, 2 * D_MODEL)
    return (u, qkv, gates), (us, qkvs, gates_s)


def _expand_block_diag(e_ref, sel_ref, row_shift, col_shift, out_scr):
    n = out_scr.shape[0]
    for c in range(n // LANES):
        rows = pl.ds(c * LANES, LANES)
        full = _dot(e_ref[rows, :], sel_ref[...])
        row = lax.broadcasted_iota(jnp.int32, full.shape, 0) + c * LANES
        col = lax.broadcasted_iota(jnp.int32, full.shape, 1)
        keep = ((row >> row_shift) & (SSM_RANGE_GROUPS - 1)) == ((col >> col_shift) & (SSM_RANGE_GROUPS - 1))
        out_scr[rows, :] = jnp.where(keep, full, 0.0).astype(BF16)


def _ssm_chunk_kernel(groups, *refs):
    n = len(groups)
    ins, refs = refs[:2 * n], refs[2 * n:]
    (ek_ref, eb_ref, ea_ref, selk_ref, selb_ref, lnr_ref, lni_ref), refs = refs[:7], refs[7:]
    outs, refs = refs[:2 * n], refs[2 * n:]
    s_scrs, (ktoe_ref, bcat_ref, acat_ref) = refs[:n], refs[n:]
    lg_c = SSM_GROUP.bit_length() - 1
    lg_p = SSM_STATE.bit_length() - 1
    _expand_block_diag(ek_ref, selk_ref, lg_c, lg_c, ktoe_ref)
    _expand_block_diag(eb_ref, selb_ref, lg_c, lg_p, bcat_ref)
    _expand_block_diag(ea_ref, selk_ref, lg_p, lg_c, acat_ref)
    for g, (nseq, nk) in enumerate(groups):
        _ssm_chunk_group(nseq, nk, ins[2 * g], ins[2 * g + 1], ktoe_ref, bcat_ref, acat_ref, lnr_ref, lni_ref,
                         outs[2 * g], outs[2 * g + 1], s_scrs[g])


def _ssm_chunk_group(nseq, nk, u_ref, h0_ref, ktoe_ref, bcat_ref, acat_ref, lnr_ref, lni_ref,
                     y_ref, hfin_ref, s_scr):
    L = SSM_CHUNK
    rows = nseq * nk
    nsub = 2 * RANGE_STATE // LANES
    ucat = jnp.concatenate([u_ref[pl.ds(j, rows, stride=L), :] for j in range(L)], axis=-1).astype(BF16)
    s = _dot(ucat, bcat_ref[...])
    for n in range(nsub):
        s_scr[pl.ds(n, rows, stride=nsub), :] = s[:, n * LANES:(n + 1) * LANES]
    y_ref_intra = _dot(ucat, ktoe_ref[...])

    lnr = lnr_ref[...]
    lni = lni_ref[...]

    def step(k, hs):
        new = []
        for q in range(nseq):
            off = pl.multiple_of((q * nk + k) * nsub, nsub)
            h = hs[q]
            sk = s_scr[pl.ds(off, nsub), :]
            s_scr[pl.ds(off, nsub), :] = h
            new.append(h * lnr + pltpu.roll(h, nsub // 2, axis=0) * lni + sk)
        return tuple(new)

    hs = lax.fori_loop(0, nk, step, tuple(h0_ref[q] for q in range(nseq)))
    for q in range(nseq):
        hfin_ref[q] = hs[q]

    hprev = jnp.concatenate([s_scr[pl.ds(n, rows, stride=nsub), :] for n in range(nsub)], axis=-1).astype(BF16)
    y = y_ref_intra + _dot(hprev, acat_ref[...])
    for j in range(L):
        y_ref[pl.ds(j, rows, stride=L), :] = y[:, j * LANES:(j + 1) * LANES]


def _ssm_chunk(token_groups, ek, eb, ea, sel_k, sel_b, lnr, lni):
    R = N_SSM_RANGES
    nsub = 2 * RANGE_STATE // LANES
    wide = SSM_CHUNK * LANES
    assert wide == 2 * RANGE_STATE
    wspec = lambda shape: pl.BlockSpec((None,) + shape, lambda r: (r, 0, 0))
    sel_spec = pl.BlockSpec((LANES, wide), lambda r: (0, 0))
    groups, args, in_specs, out_specs, out_shape, scratch = [], [], [], [], [], []
    for u, h0, nseq in token_groups:
        M = u.shape[0]
        nk = M // nseq // SSM_CHUNK
        groups.append((nseq, nk))
        args += [u, h0]
        tok_spec = pl.BlockSpec((M, LANES), lambda r: (0, r))
        state_spec = pl.BlockSpec((nseq, None, nsub, LANES), lambda r: (0, r, 0, 0))
        in_specs += [tok_spec, state_spec]
        out_specs += [tok_spec, state_spec]
        out_shape += [jax.ShapeDtypeStruct((M, SSM_WIDTH), F32), jax.ShapeDtypeStruct((nseq, R, nsub, LANES), F32)]
        scratch.append(pltpu.VMEM((nseq * nk * nsub, LANES), F32))
    res = pl.pallas_call(
        functools.partial(_ssm_chunk_kernel, tuple(groups)),
        grid=(R,),
        in_specs=in_specs + [wspec((wide, LANES))] * 3 + [sel_spec, sel_spec] + [wspec((nsub, LANES))] * 2,
        out_specs=out_specs,
        out_shape=out_shape,
        scratch_shapes=scratch + [pltpu.VMEM((wide, wide), BF16)] * 3,
        compiler_params=_cparams(("arbitrary",)),
        name="ssm_chunk",
    )(*args, ek, eb, ea, sel_k, sel_b, lnr, lni)
    return [(res[2 * g], res[2 * g + 1]) for g in range(len(token_groups))]


def _state_to_tiles(re, im):
    N = re.shape[0]
    half = RANGE_STATE // LANES
    return jnp.concatenate([re.reshape(N, N_SSM_RANGES, half, LANES),
                            im.reshape(N, N_SSM_RANGES, half, LANES)], axis=2)


def _tiles_to_state(t):
    N = t.shape[0]
    half = RANGE_STATE // LANES
    return (t[:, :, :half].reshape(N, N_SSM_GROUPS, SSM_STATE),
            t[:, :, half:].reshape(N, N_SSM_GROUPS, SSM_STATE))


def _bucket_np(dist):
    max_exact = N_BUCKETS // 2
    n = np.maximum(dist, 0)
    nf = np.maximum(n, 1).astype(np.float64)
    large = max_exact + (np.log(nf / max_exact) / math.log(REL_MAX_DISTANCE / max_exact)
                         * (N_BUCKETS - max_exact)).astype(np.int32)
    large = np.minimum(large, N_BUCKETS - 1)
    return np.where(n < max_exact, n, large)


def _prompt_bucket_tile_np(group):
    window, dil = WINDOWS[group]
    a = np.arange(Q_BLOCK)[:, None]
    c = np.arange(2 * Q_BLOCK)[None, :]
    rel = a - c + Q_BLOCK
    K = window // dil + 1
    valid = (rel >= 0) & (rel < K)
    return np.where(valid, _bucket_np(np.clip(rel, 0, K - 1) * dil), -1)


def _prompt_buckets_present(group):
    tile = _prompt_bucket_tile_np(group)
    return [int(t) for t in np.unique(tile[tile >= 0])]


def _prompt_bucket_tiles():
    return jnp.asarray(np.stack([_prompt_bucket_tile_np(g) for g in range(N_ATT_GROUPS)]), jnp.int32)


def _attn_group_of_step(step):
    return (step + ATTN_FIRST_GROUP) % N_ATT_GROUPS


def _attn_prompt_kernel(T, tab_ref, q_ref, k_ref, v_ref, bkt_ref, o_ref,
                        m_scr, l_scr, acc_scr, s_scr, p_scr, mb_scr, bias_scr):
    h = pl.program_id(1)
    step = pl.program_id(2)
    scale = HEAD_DIM ** -0.5
    nblk = T // Q_BLOCK

    def run_group(first, group):
        dil = WINDOWS[group][1]
        bkt = bkt_ref[...]
        col = group * HEADS_PER_GROUP + h
        bias = jnp.full(bkt.shape, NEG_INF, F32)
        for t in _prompt_buckets_present(group):
            bias = jnp.where(bkt == t, tab_ref[t, col], bias)
        bias_scr[...] = bias

        per_class = nblk // dil
        run = min(per_class, ATTN_UNROLL)
        runs_per_iter = ATTN_UNROLL // run
        whole_class = run == per_class

        def block_rows(r, n):
            return pl.ds(r + n * (Q_BLOCK * dil), Q_BLOCK, stride=dil)

        def runs(it):
            for j in range(runs_per_iter):
                i0 = it * ATTN_UNROLL + j * run
                yield i0, i0 // per_class, (0 if whole_class else i0 % per_class)

        def tiles(ref, r, n0, augment):
            out = []
            for u in range(-1, run):
                if u < 0 and whole_class:
                    out.append(None)
                    continue
                n = jnp.maximum(n0 + u, 0) if u < 0 else n0 + u
                t = ref[block_rows(r, n), :].astype(BF16)
                if augment:
                    t = jnp.concatenate([t, jnp.ones((Q_BLOCK, HEAD_DIM), BF16)], axis=1)
                out.append(t)
            return out

        def scores(it, carry):
            for i0, r, n0 in runs(it):
                kt = tiles(k_ref, r, n0, False)
                for u in range(run):
                    q = q_ref[block_rows(r, n0 + u), :].astype(BF16)
                    s_r = _dot_nt(q, kt[u + 1]) * scale + bias_scr[:, Q_BLOCK:]
                    if kt[u] is None:
                        s_l = jnp.full((Q_BLOCK, Q_BLOCK), NEG_INF, F32)
                    else:
                        bias_l = bias_scr[:, :Q_BLOCK]
                        if u == 0:
                            bias_l = jnp.where(n0 == 0, NEG_INF, bias_l)
                        s_l = _dot_nt(q, kt[u]) * scale + bias_l
                    s_scr[i0 + u, :, :Q_BLOCK] = s_l
                    s_scr[i0 + u, :, Q_BLOCK:] = s_r
                    m = jnp.maximum(jnp.max(s_l, axis=-1, keepdims=True), jnp.max(s_r, axis=-1, keepdims=True))
                    mb_scr[i0 + u] = jnp.broadcast_to(m, (Q_BLOCK, HEAD_DIM))
            return carry

        def probs(i, carry):
            mb = mb_scr[i]
            p_scr[i, :, :Q_BLOCK] = jnp.exp(s_scr[i, :, :Q_BLOCK] - mb).astype(BF16)
            p_scr[i, :, Q_BLOCK:] = jnp.exp(s_scr[i, :, Q_BLOCK:] - mb).astype(BF16)
            return carry

        def values(it, carry):
            for i0, r, n0 in runs(it):
                vt = tiles(v_ref, r, n0, True)
                for u in range(run):
                    i = i0 + u
                    ol = _dot(p_scr[i, :, Q_BLOCK:], vt[u + 1])
                    if vt[u] is not None:
                        ol = ol + _dot(p_scr[i, :, :Q_BLOCK], vt[u])
                    o = ol[:, :HEAD_DIM]
                    lb = ol[:, HEAD_DIM:]
                    mb = mb_scr[i]
                    sl_q = block_rows(r, n0 + u)
                    if first:
                        m_scr[sl_q, :] = mb
                        l_scr[sl_q, :] = lb
                        acc_scr[sl_q, :] = o
                    else:
                        m0 = m_scr[sl_q, :]
                        mn = jnp.maximum(m0, mb)
                        a0 = jnp.exp(m0 - mn)
                        a1 = jnp.exp(mb - mn)
                        m_scr[sl_q, :] = mn
                        l_scr[sl_q, :] = a0 * l_scr[sl_q, :] + a1 * lb
                        acc_scr[sl_q, :] = a0 * acc_scr[sl_q, :] + a1 * o
            return carry

        lax.fori_loop(0, nblk // ATTN_UNROLL, scores, 0)
        lax.fori_loop(0, nblk, probs, 0, unroll=ATTN_UNROLL)
        lax.fori_loop(0, nblk // ATTN_UNROLL, values, 0)

    for s in range(N_ATT_GROUPS):
        pl.when(step == s)(functools.partial(run_group, s == 0, (s + ATTN_FIRST_GROUP) % N_ATT_GROUPS))

    @pl.when(step == N_ATT_GROUPS - 1)
    def _():
        o_ref[...] = (acc_scr[...] / l_scr[...]).astype(o_ref.dtype)


def _attn_prompt(qkv, rel_bias, bucket_tiles, nbatch):
    M = qkv.shape[1]
    T = M // nbatch
    H, G = HEADS_PER_GROUP, N_ATT_GROUPS
    nblk = T // Q_BLOCK

    def qkv_spec(which):
        return pl.BlockSpec((None, T, HEAD_DIM),
                            lambda b, h, s: (which * N_ATT_HEADS + _attn_group_of_step(s) * H + h, b, 0))

    return pl.pallas_call(
        functools.partial(_attn_prompt_kernel, T),
        grid=(nbatch, H, G),
        in_specs=[pl.BlockSpec(memory_space=pltpu.SMEM),
                  qkv_spec(0), qkv_spec(1), qkv_spec(2),
                  pl.BlockSpec((None, Q_BLOCK, 2 * Q_BLOCK), lambda b, h, s: (_attn_group_of_step(s), 0, 0))],
        out_specs=pl.BlockSpec((T, HEAD_DIM), lambda b, h, s: (b, h)),
        out_shape=jax.ShapeDtypeStruct((M, ATT_WIDTH), BF16),
        scratch_shapes=[pltpu.VMEM((T, HEAD_DIM), F32)] * 3
                       + [pltpu.VMEM((nblk, Q_BLOCK, 2 * Q_BLOCK), F32),
                          pltpu.VMEM((nblk, Q_BLOCK, 2 * Q_BLOCK), BF16),
                          pltpu.VMEM((nblk, Q_BLOCK, HEAD_DIM), F32),
                          pltpu.VMEM((Q_BLOCK, 2 * Q_BLOCK), F32)],
        compiler_params=_cparams(("arbitrary", "arbitrary", "arbitrary")),
        name="attn_prompt",
    )(rel_bias.astype(F32), qkv, qkv, qkv, bucket_tiles)


def _attn_sample_kernel(window, dil, q_ref, ck_ref, cv_ref, nk_ref, nv_ref, bias_ref, o_ref, lse_ref):
    S = q_ref.shape[0]
    K = window // dil + 1
    scale = HEAD_DIM ** -0.5
    bias = bias_ref[...]
    by_residue = len(ck_ref.shape) == 4

    def buffered(ref, s, n):
        return ref[pl.ds(0, n), s] if by_residue else ref[pl.ds(s, n, stride=dil)]

    for s in range(S):
        n_c = (window - 1 - s) // dil + 1
        qs = q_ref[s]
        new_rows = [s + j * dil - window for j in range(n_c, K)]
        kk = jnp.concatenate([buffered(ck_ref, s, n_c)] + [nk_ref[pl.ds(i, 1)] for i in new_rows], axis=0)
        vv = jnp.concatenate([buffered(cv_ref, s, n_c)] + [nv_ref[pl.ds(i, 1)] for i in new_rows], axis=0)
        lg = jnp.sum(kk * qs[None], axis=-1, keepdims=True) * scale + bias
        m = jnp.max(lg, axis=0)
        p = jnp.exp(lg - m[None])
        l = jnp.sum(p, axis=0)
        o_ref[s] = jnp.sum(p * vv, axis=0) / l
        lse_ref[s] = m + jnp.log(l)


def _attn_sample(q, cache_k, cache_v, new_k, new_v, bias, layer, window, dil):
    B, S = q.shape[0], q.shape[1]
    H = HEADS_PER_GROUP
    K = window // dil + 1
    small = pl.BlockSpec((None, S, H, HEAD_DIM), lambda b: (b, 0, 0, 0))
    if dil > S:
        depth = cache_k.shape[0]
        cache_k = cache_k.reshape(depth, B, window // dil, dil, H, HEAD_DIM)
        cache_v = cache_v.reshape(depth, B, window // dil, dil, H, HEAD_DIM)
        cache = pl.BlockSpec((None, None, window // dil, S, H, HEAD_DIM), lambda b: (layer, b, 0, 0, 0, 0))
    else:
        cache = pl.BlockSpec((None, None, window, H, HEAD_DIM), lambda b: (layer, b, 0, 0, 0))
    return pl.pallas_call(
        functools.partial(_attn_sample_kernel, window, dil),
        grid=(B,),
        in_specs=[small, cache, cache, small, small, pl.BlockSpec((K, H, HEAD_DIM), lambda b: (0, 0, 0))],
        out_specs=[small, small],
        out_shape=[jax.ShapeDtypeStruct((B, S, H, HEAD_DIM), F32)] * 2,
        compiler_params=_cparams(("arbitrary",)),
        name="attn_sample_w%d" % window,
    )(q, cache_k, cache_v, new_k, new_v, bias)


def _merge_groups_kernel(o0, o1, o2, l0, l1, l2, y_ref):
    a, b, c = l0[...], l1[...], l2[...]
    m = jnp.maximum(jnp.maximum(a, b), c)
    ea, eb, ec = jnp.exp(a - m), jnp.exp(b - m), jnp.exp(c - m)
    y_ref[...] = ((ea * o0[...] + eb * o1[...] + ec * o2[...]) / (ea + eb + ec)).astype(y_ref.dtype)


def _merge_groups(outs, lses):
    shape = outs[0].shape
    return pl.pallas_call(
        _merge_groups_kernel,
        out_shape=jax.ShapeDtypeStruct(shape, BF16),
        name="merge_groups",
    )(*outs, *lses)


def _sample_bias(rel_bias):
    out = []
    for g, (window, dil) in enumerate(WINDOWS):
        K = window // dil + 1
        steps = (K - 1) - np.arange(K)
        tab = rel_bias[:, g * HEADS_PER_GROUP:(g + 1) * HEADS_PER_GROUP].astype(F32)
        b = tab[_bucket_np(steps * dil)]
        out.append(jnp.broadcast_to(b[:, :, None], (K, HEADS_PER_GROUP, HEAD_DIM)))
    return out


def _gelu_tanh(x):
    return 0.5 * x * (1.0 + jnp.tanh(math.sqrt(2.0 / math.pi) * (x + 0.044715 * (x * x * x))))


def _mix_out_kernel(y_ref, u_ref, yb_ref, ga_ref, gb_ref, x_ref, d_ref, wglu_ref, bglu_ref, wa_ref, wb_ref,
                    wout_ref, gn_ref, x1_ref, h2_ref):
    tm = x_ref.shape[0]
    sub = min(tm, MIX_ROW_SUB)
    for r in range(tm // sub):
        rows = pl.ds(r * sub, sub)
        y = y_ref[rows, :] + d_ref[...] * u_ref[rows, :]
        z = _gelu_tanh(y)
        ya = z * jax.nn.sigmoid(_dot(z.astype(BF16), wglu_ref[...]) + bglu_ref[...])
        mix = (ga_ref[rows, :] * _dot(ya.astype(BF16), wa_ref[...])
               + gb_ref[rows, :] * _dot(yb_ref[rows, :], wb_ref[...]))
        x1 = x_ref[rows, :] + _dot(mix.astype(BF16), wout_ref[...])
        x1_ref[rows, :] = x1
        h2_ref[rows, :] = _rmsnorm_rows(x1, gn_ref[...])


def _mix_out(y_ssm, u, yb, gates, x, ssm_d, wglu, bglu, wa, wb, wout, layer, gain, tm):
    M = x.shape[0]
    row = lambda w: pl.BlockSpec((tm, w), lambda i: (i, 0))
    vec = lambda b: pl.BlockSpec((1, b), lambda i: (0, 0), pipeline_mode=pl.Buffered(1))
    full = lambda a, b: pl.BlockSpec((None, a, b), lambda i: (layer, 0, 0), pipeline_mode=pl.Buffered(1))
    return pl.pallas_call(
        _mix_out_kernel,
        grid=(M // tm,),
        in_specs=[row(SSM_WIDTH), row(SSM_WIDTH), row(ATT_WIDTH),
                  pl.BlockSpec((tm, D_MODEL), lambda i: (i, 0)),
                  pl.BlockSpec((tm, D_MODEL), lambda i: (i, 1)),
                  row(D_MODEL),
                  vec(SSM_WIDTH), full(SSM_WIDTH, SSM_WIDTH), vec(SSM_WIDTH),
                  full(SSM_WIDTH, D_MODEL), full(ATT_WIDTH, D_MODEL), full(D_MODEL, D_MODEL), vec(D_MODEL)],
        out_specs=[row(D_MODEL), row(D_MODEL)],
        out_shape=[jax.ShapeDtypeStruct((M, D_MODEL), F32), jax.ShapeDtypeStruct((M, D_MODEL), BF16)],
        compiler_params=_cparams(("arbitrary",)),
        name="mix_out",
    )(y_ssm, u, yb, gates, gates, x, ssm_d.reshape(1, -1), wglu, bglu.reshape(1, -1), wa, wb, wout,
      gain.reshape(1, -1))


def _ffn_kernel(emit_norm, h_ref, x_ref, hs_ref, xs_ref, wg_ref, wu_ref, wd_ref, gn_ref, *rest):
    if emit_norm:
        o_ref, os_ref, hn_ref, hns_ref, wg_scr, wu_scr, wd_scr = rest
    else:
        o_ref, os_ref, wg_scr, wu_scr, wd_scr = rest
    i = pl.program_id(0)
    f = pl.program_id(1)
    last = pl.num_programs(1) - 1

    @pl.when(f == 0)
    def _():
        o_ref[...] = x_ref[...]

    @pl.when((f == 0) & (i == 0))
    def _():
        os_ref[...] = xs_ref[...]

    def swiglu(h):
        a = jax.nn.silu(_dot(h, wg_scr[...])) * _dot(h, wu_scr[...])
        return _dot(a.astype(BF16), wd_scr[...])

    tm = h_ref.shape[0]
    sub = min(tm, FFN_ROW_SUB)
    wg_scr[...] = wg_ref[...].astype(BF16)
    wu_scr[...] = wu_ref[...].astype(BF16)
    wd_scr[...] = wd_ref[...].astype(BF16)
    for r in range(tm // sub):
        rows = pl.ds(r * sub, sub)
        o_ref[rows, :] += swiglu(h_ref[rows, :])

    @pl.when(i == 0)
    def _():
        os_ref[...] += swiglu(hs_ref[...])

    if emit_norm:
        @pl.when(f == last)
        def _():
            for r in range(tm // sub):
                rows = pl.ds(r * sub, sub)
                hn_ref[rows, :] = _rmsnorm_rows(o_ref[rows, :], gn_ref[...])

        @pl.when((f == last) & (i == 0))
        def _():
            hns_ref[...] = _rmsnorm_rows(os_ref[...], gn_ref[...])


def _ffn(h2, x1, h2s, x1s, wg, wu, wd, layer, next_gain, tm):
    M, Ms = x1.shape[0], x1s.shape[0]
    nf = FFN_HIDDEN // FFN_BLOCK
    emit_norm = next_gain is not None
    gain = next_gain if emit_norm else jnp.ones((D_MODEL,), F32)
    row = pl.BlockSpec((tm, D_MODEL), lambda i, f: (i, 0))
    whole = pl.BlockSpec((Ms, D_MODEL), lambda i, f: (0, 0))
    out_specs = [row, whole]
    out_shape = [jax.ShapeDtypeStruct((M, D_MODEL), F32), jax.ShapeDtypeStruct((Ms, D_MODEL), F32)]
    if emit_norm:
        out_specs += [row, whole]
        out_shape += [jax.ShapeDtypeStruct((M, D_MODEL), BF16), jax.ShapeDtypeStruct((Ms, D_MODEL), BF16)]
    return pl.pallas_call(
        functools.partial(_ffn_kernel, emit_norm),
        grid=(M // tm, nf),
        in_specs=[row,
                  pl.BlockSpec((tm, D_MODEL), lambda i, f: (i, 0), pipeline_mode=pl.Buffered(1)),
                  whole, whole,
                  pl.BlockSpec((None, D_MODEL, FFN_BLOCK), lambda i, f: (layer, 0, f)),
                  pl.BlockSpec((None, D_MODEL, FFN_BLOCK), lambda i, f: (layer, 0, f)),
                  pl.BlockSpec((None, FFN_BLOCK, D_MODEL), lambda i, f: (layer, f, 0)),
                  pl.BlockSpec((1, D_MODEL), lambda i, f: (0, 0))],
        out_specs=out_specs,
        out_shape=out_shape,
        scratch_shapes=[pltpu.VMEM((D_MODEL, FFN_BLOCK), BF16), pltpu.VMEM((D_MODEL, FFN_BLOCK), BF16),
                        pltpu.VMEM((FFN_BLOCK, D_MODEL), BF16)],
        compiler_params=_cparams(("arbitrary", "arbitrary")),
        name="ffn",
    )(h2, x1, h2s, x1s, wg, wu, wd, gain.reshape(1, D_MODEL))


def _cache_shift_kernel(ck_ref, cv_ref, hk_ref, hv_ref, nk_ref, nv_ref, ok_ref, ov_ref):
    c = pl.program_id(2)
    last = pl.num_programs(2) - 1
    R = ck_ref.shape[0]
    S = nk_ref.shape[0]
    for cache, halo, new, out in ((ck_ref, hk_ref, nk_ref, ok_ref), (cv_ref, hv_ref, nv_ref, ov_ref)):
        out[pl.ds(0, R - S)] = cache[pl.ds(S, R - S)]

        @pl.when(c == last)
        def _():
            out[pl.ds(R - S, S)] = new[...]

        @pl.when(c < last)
        def _():
            out[pl.ds(R - S, S)] = halo[...]


def _cache_shift(cache_k, cache_v, new_k, new_v):
    depth, B, W, H, E = cache_k.shape
    S = new_k.shape[2]
    R = min(W, CACHE_ROWS)
    nchunks = W // R
    blk = pl.BlockSpec((None, None, R, H, E), lambda l, b, c: (l, b, c, 0, 0))
    halo = pl.BlockSpec((None, None, None, S, H, E),
                        lambda l, b, c: (l, b, jnp.minimum((c + 1) * (R // S), W // S - 1), 0, 0, 0))
    new = pl.BlockSpec((None, None, S, H, E), lambda l, b, c: (l, b, 0, 0, 0))
    as_rows = lambda a: a.reshape(depth, B, W // S, S, H, E)
    return pl.pallas_call(
        _cache_shift_kernel,
        grid=(depth, B, nchunks),
        in_specs=[blk, blk, halo, halo, new, new],
        out_specs=[blk, blk],
        out_shape=[jax.ShapeDtypeStruct(cache_k.shape, cache_k.dtype)] * 2,
        compiler_params=_cparams(("arbitrary", "arbitrary", "arbitrary")),
        name="cache_shift_w%d" % W,
    )(cache_k, cache_v, as_rows(cache_k), as_rows(cache_v), new_k, new_v)


def _kv_tails_kernel(*refs):
    depth = (len(refs) - 2) // 2
    ok_ref, ov_ref = refs[-2:]
    H, R = refs[0].shape[0], refs[0].shape[1]
    for l in range(depth):
        @pl.when(pl.program_id(0) == l)
        def _():
            for src, out in ((refs[2 * l], ok_ref), (refs[2 * l + 1], ov_ref)):
                for h in range(H):
                    out[pl.ds(h, R, stride=H), :] = src[h]


def _kv_tails(qkv_layers, g, window, nbatch):
    depth = len(qkv_layers)
    M = qkv_layers[0].shape[1]
    T = M // nbatch
    H, E = HEADS_PER_GROUP, HEAD_DIM
    R = min(window, CACHE_ROWS)
    nchunks = window // R
    first = (T - window) // R

    def src(layer, which):
        def index(l, b, c):
            bb = jnp.where(l < layer, 0, jnp.where(l > layer, nbatch - 1, b))
            cc = jnp.where(l < layer, 0, jnp.where(l > layer, nchunks - 1, c))
            return (which * N_ATT_GROUPS + g, bb * (T // R) + first + cc, 0)
        return pl.BlockSpec((H, R, E), index)

    args, in_specs = [], []
    for layer, qkv in enumerate(qkv_layers):
        args += [qkv, qkv]
        in_specs += [src(layer, 1), src(layer, 2)]
    out_spec = pl.BlockSpec((None, R * H, E), lambda l, b, c: (l, b * nchunks + c, 0))
    return pl.pallas_call(
        _kv_tails_kernel,
        grid=(depth, nbatch, nchunks),
        in_specs=in_specs,
        out_specs=[out_spec, out_spec],
        out_shape=[jax.ShapeDtypeStruct((depth, nbatch * window * H, E), qkv_layers[0].dtype)] * 2,
        compiler_params=_cparams(("arbitrary", "arbitrary", "arbitrary")),
        name="kv_tails_w%d" % window,
    )(*args)


def _mix(x2d, u, y_ssm, yb, gates, prm, tm):
    return _mix_out(y_ssm, u, yb, gates, x2d, prm["ssm_d"], prm["w_glu"], prm["b_glu"], prm["w_branch_a"],
                    prm["w_branch_b"], prm["w_out"], prm["layer"], prm["norm_ffn"], tm)


def kernel(x_prompt, x_sample, state_ssm_re, state_ssm_im, cache_k_w128, cache_v_w128, cache_k_w512, cache_v_w512,
           cache_k_w2048, cache_v_w2048, rel_bias, norm_mix, norm_ffn, q_norm, k_norm, w_in, ssm_lambda_re,
           ssm_lambda_im, ssm_log_dt, ssm_b_re, ssm_b_im, ssm_c_re, ssm_c_im, ssm_d, w_glu, b_glu, w_branch_a,
           w_branch_b, w_out, w_ffn_gate, w_ffn_up, w_ffn_down):
    depth = w_in.shape[0]
    B, T, _ = x_prompt.shape
    SB, S, _ = x_sample.shape
    H = HEADS_PER_GROUP
    caches_k = (cache_k_w128, cache_k_w512, cache_k_w2048)
    caches_v = (cache_v_w128, cache_v_w512, cache_v_w2048)

    bucket_tiles = _prompt_bucket_tiles()
    bias_sample = _sample_bias(rel_bias)
    sel_k, sel_b = _ssm_select_matrices()

    xp = x_prompt.reshape(B * T, D_MODEL)
    xs = x_sample.reshape(SB * S, D_MODEL)
    p_re, p_im, s_re, s_im = [], [], [], []
    qkv_prompt = []
    new_k = [[] for _ in WINDOWS]
    new_v = [[] for _ in WINDOWS]

    weights = {
        "w_glu": w_glu.astype(BF16), "w_branch_a": w_branch_a.astype(BF16),
        "w_branch_b": w_branch_b.astype(BF16), "w_out": w_out.astype(BF16),
    }

    hp, hs = xp, xs

    for l in range(depth):
        ek, eb, ea, lnr, lni = _ssm_prep(ssm_lambda_re[l], ssm_lambda_im[l], ssm_log_dt[l], ssm_b_re[l],
                                         ssm_b_im[l], ssm_c_re[l], ssm_c_im[l])
        prm = dict(weights)
        prm.update({
            "layer": l, "norm_ffn": norm_ffn[l], "ssm_d": ssm_d[l], "b_glu": b_glu[l],
        })

        (u, qkv, gates), (us, qkvs, gates_s) = _in_proj(hp, hs, norm_mix[0] if l == 0 else None, w_in, l,
                                                        q_norm[l], k_norm[l], IN_PROJ_ROWS)

        zeros = jnp.zeros((B, N_SSM_RANGES, 2 * RANGE_STATE // LANES, LANES), F32)
        h0 = _state_to_tiles(state_ssm_re[l], state_ssm_im[l])
        (y_ssm, hfin), (ys_ssm, hfin_s) = _ssm_chunk([(u, zeros, B), (us, h0, SB)], ek, eb, ea, sel_k, sel_b,
                                                     lnr, lni)

        yb = _attn_prompt(qkv, rel_bias, bucket_tiles, B)
        x1, h2 = _mix(xp, u, y_ssm, yb, gates, prm, MIX_ROWS)
        hr, hi = _tiles_to_state(hfin)
        p_re.append(hr)
        p_im.append(hi)
        qkv_prompt.append(qkv)

        tok = jnp.transpose(qkvs.reshape(3, N_ATT_GROUPS, H, SB, S, HEAD_DIM), (0, 1, 3, 4, 2, 5))
        outs, lses = [], []
        for g, (window, dil) in enumerate(WINDOWS):
            nk_g, nv_g = tok[1, g], tok[2, g]
            new_k[g].append(nk_g)
            new_v[g].append(nv_g)
            o_g, lse_g = _attn_sample(tok[0, g], caches_k[g], caches_v[g], nk_g, nv_g, bias_sample[g], l,
                                      window, dil)
            outs.append(o_g)
            lses.append(lse_g)
        ybs = _merge_groups(outs, lses).reshape(SB * S, ATT_WIDTH)
        x1s, h2s = _mix(xs, us, ys_ssm, ybs, gates_s, prm, SB * S)
        hr, hi = _tiles_to_state(hfin_s)
        s_re.append(hr)
        s_im.append(hi)

        next_gain = norm_mix[l + 1] if l + 1 < depth else None
        res = _ffn(h2, x1, h2s, x1s, w_ffn_gate, w_ffn_up, w_ffn_down, l, next_gain, FFN_ROWS)
        if next_gain is None:
            xp, xs = res
        else:
            xp, xs, hp, hs = res

    shifted = []
    for g in range(N_ATT_GROUPS):
        shifted += _cache_shift(caches_k[g], caches_v[g], jnp.stack(new_k[g]), jnp.stack(new_v[g]))
    tails = [t.reshape(depth, B, window, H, HEAD_DIM)
             for g, (window, _) in enumerate(WINDOWS) for t in _kv_tails(qkv_prompt, g, window, B)]

    return (xp.reshape(B, T, D_MODEL), xs.reshape(SB, S, D_MODEL),
            jnp.stack(p_re), jnp.stack(p_im), *tails,
            jnp.stack(s_re), jnp.stack(s_im), *shifted)
```

```python
import functools
import math

import numpy as np
import jax
import jax.numpy as jnp
from jax import lax
from jax.experimental import pallas as pl
from jax.experimental.pallas import tpu as pltpu

F32 = jnp.float32
BF16 = jnp.bfloat16

D_MODEL = 2048
HEAD_DIM = 128
HEADS_PER_GROUP = 8
WINDOWS = ((128, 1), (512, 4), (2048, 16))
N_ATT_GROUPS = len(WINDOWS)
N_ATT_HEADS = N_ATT_GROUPS * HEADS_PER_GROUP
ATT_WIDTH = HEADS_PER_GROUP * HEAD_DIM
QKV_WIDTH = N_ATT_HEADS * HEAD_DIM
Q_BLOCK = 128
SSM_WIDTH = D_MODEL // 2
SSM_GROUP = 16
N_SSM_GROUPS = SSM_WIDTH // SSM_GROUP
SSM_STATE = 64
IN_WIDTH = SSM_WIDTH + 3 * QKV_WIDTH + 2 * D_MODEL
FFN_HIDDEN = 5632
N_BUCKETS = 32
REL_MAX_DISTANCE = 2048
EPS = 1e-6
NEG_INF = -1e30

LANES = 128
SUBLANES = 8
VMEM_LIMIT_BYTES = 60 * 1024 * 1024

SSM_CHUNK = 8
SSM_RANGE_GROUPS = LANES // SSM_GROUP
N_SSM_RANGES = N_SSM_GROUPS // SSM_RANGE_GROUPS
RANGE_STATE = SSM_RANGE_GROUPS * SSM_STATE
COL_BLOCK = 512
FFN_BLOCK = 256
ATTN_UNROLL = 32
ATTN_FIRST_GROUP = 2
CACHE_ROWS = 1024
IN_PROJ_ROWS = 2048
FFN_ROWS = 1024
MIX_ROWS = 256
ROW_SUB = 256
FFN_ROW_SUB = 512
MIX_ROW_SUB = 256


def _cparams(sem):
    return pltpu.CompilerParams(dimension_semantics=sem, vmem_limit_bytes=VMEM_LIMIT_BYTES)


def _dot(a, b):
    return jnp.dot(a, b, preferred_element_type=F32)


def _dot_nt(a, b):
    return lax.dot_general(a, b, (((1,), (1,)), ((), ())), preferred_element_type=F32)


def _ssm_prep_kernel(*refs):
    for g in range(SSM_RANGE_GROUPS):
        _ssm_prep_group(*[ref.at[g] for ref in refs])


def _ssm_prep_group(lre_ref, lim_ref, ldt_ref, btre_ref, btim_ref, cre_ref, cim_ref,
                    kp_ref, bpre_ref, bpim_ref, are_ref, aim_ref, lnre_ref, lnim_ref):
    L = SSM_CHUNK
    lr = lre_ref[...]
    li = lim_ref[...]
    dt = jnp.exp(ldt_ref[...])
    er = jnp.exp(lr * dt)
    lbr = er * jnp.cos(li * dt)
    lbi = er * jnp.sin(li * dt)
    nr = lbr - 1.0
    dd = lr * lr + li * li
    rr = (nr * lr + lbi * li) / dd
    ri = (lbi * lr - nr * li) / dd
    btr = btre_ref[...]
    bti = btim_ref[...]
    bbr = rr * btr - ri * bti
    bbi = rr * bti + ri * btr
    cr = cre_ref[...]
    ci = cim_ref[...]
    pr = [jnp.ones_like(lbr)]
    pi = [jnp.zeros_like(lbr)]
    for _ in range(L):
        pr.append(pr[-1] * lbr - pi[-1] * lbi)
        pi.append(pr[-2] * lbi + pi[-1] * lbr)
    xr = [cr * pr[t] - ci * pi[t] for t in range(L + 1)]
    xi = [cr * pi[t] + ci * pr[t] for t in range(L + 1)]
    xr_k = jnp.concatenate(xr[:L], axis=0)
    xi_k = jnp.concatenate(xi[:L], axis=0)
    hp = lax.Precision.HIGHEST
    kp = (lax.dot_general(bbr, xr_k, (((1,), (1,)), ((), ())), precision=hp, preferred_element_type=F32)
          - lax.dot_general(bbi, xi_k, (((1,), (1,)), ((), ())), precision=hp, preferred_element_type=F32))
    lane = lax.broadcasted_iota(jnp.int32, kp.shape, 1)
    blocks = [kp] + [jnp.where(lane >= j * SSM_GROUP, pltpu.roll(kp, j * SSM_GROUP, axis=1), 0.0)
                     for j in range(1, L)]
    kp_ref[...] = jnp.concatenate(blocks, axis=0)
    are_ref[...] = jnp.concatenate(xr[1:], axis=0)
    aim_ref[...] = jnp.concatenate(xi[1:], axis=0)
    bpre_ref[...] = jnp.concatenate([pr[L - 1 - j] * bbr - pi[L - 1 - j] * bbi for j in range(L)], axis=0)
    bpim_ref[...] = jnp.concatenate([pr[L - 1 - j] * bbi + pi[L - 1 - j] * bbr for j in range(L)], axis=0)
    lnre_ref[...] = pr[L]
    lnim_ref[...] = pi[L]


def _ssm_prep(lam_re, lam_im, log_dt, b_re, b_im, c_re, c_im):
    G, P, C, L = N_SSM_GROUPS, SSM_STATE, SSM_GROUP, SSM_CHUNK
    row = lambda a: a.reshape(G, 1, -1)
    bt_re = jnp.swapaxes(b_re, 1, 2)
    bt_im = jnp.swapaxes(b_im, 1, 2)
    GL = SSM_RANGE_GROUPS
    vec = pl.BlockSpec((GL, 1, P), lambda r: (r, 0, 0))
    mat = pl.BlockSpec((GL, C, P), lambda r: (r, 0, 0))
    big = pl.BlockSpec((GL, L * C, P), lambda r: (r, 0, 0))
    kp, bpre, bpim, are, aim, lnre, lnim = pl.pallas_call(
        _ssm_prep_kernel,
        grid=(G // GL,),
        in_specs=[vec, vec, pl.BlockSpec((GL, 1, 1), lambda r: (r, 0, 0)), mat, mat, mat, mat],
        out_specs=[pl.BlockSpec((GL, L * C, L * C), lambda r: (r, 0, 0)), big, big, big, big, vec, vec],
        out_shape=[jax.ShapeDtypeStruct((G, L * C, L * C), F32)] + [jax.ShapeDtypeStruct((G, L * C, P), F32)] * 4
                  + [jax.ShapeDtypeStruct((G, 1, P), F32)] * 2,
        compiler_params=_cparams(("arbitrary",)),
        name="ssm_prep",
    )(row(lam_re), row(lam_im), log_dt.reshape(G, 1, 1), bt_re, bt_im, c_re, c_im)

    R, GL = N_SSM_RANGES, SSM_RANGE_GROUPS

    def by_range(a, n_outer):
        n_inner = a.shape[1] // n_outer
        a = a.reshape(R, GL, n_outer, n_inner, LANES)
        return jnp.transpose(a, (0, 2, 1, 3, 4)).reshape(R, n_outer * GL * n_inner, LANES).astype(BF16)

    ek = by_range(kp, L)
    eb = by_range(jnp.concatenate([bpre, bpim], axis=-1), L)
    a_t = jnp.swapaxes(jnp.concatenate([are, -aim], axis=-1), 1, 2)
    ea = by_range(a_t, 2)

    half = RANGE_STATE // LANES
    lnr_t = lnre.reshape(R, half, LANES)
    lni_t = lnim.reshape(R, half, LANES)
    lnr = jnp.concatenate([lnr_t, lnr_t], axis=1)
    lni = jnp.concatenate([-lni_t, lni_t], axis=1)
    return ek, eb, ea, lnr, lni


def _ssm_select_matrices():
    q = np.arange(SSM_CHUNK * LANES)
    r = np.arange(LANES)
    sel_k = (r[:, None] // SSM_GROUP == q[None, :] // LANES) & (r[:, None] % SSM_GROUP == q[None, :] % SSM_GROUP)
    sel_b = (r[:, None] // SSM_STATE == q[None, :] // RANGE_STATE) & (r[:, None] % SSM_STATE == q[None, :] % SSM_STATE)
    return jnp.asarray(sel_k, BF16), jnp.asarray(sel_b, BF16)


HEADS_PER_BLOCK = COL_BLOCK // HEAD_DIM
_SEG_U = 0
_SEG_Q = SSM_WIDTH // COL_BLOCK
_SEG_K = _SEG_Q + QKV_WIDTH // COL_BLOCK
_SEG_V = _SEG_K + QKV_WIDTH // COL_BLOCK
_SEG_GATE = _SEG_V + QKV_WIDTH // COL_BLOCK
_SEG_END = IN_WIDTH // COL_BLOCK


def _head_norm(res, gain):
    outs = []
    for h in range(HEADS_PER_BLOCK):
        t = res[:, h * HEAD_DIM:(h + 1) * HEAD_DIM]
        ms = jnp.mean(t * t, axis=-1, keepdims=True)
        outs.append(t * lax.rsqrt(ms + EPS) * gain)
    return outs


def _rmsnorm_rows(x, gain):
    ms = jnp.mean(x * x, axis=-1, keepdims=True)
    return (x * lax.rsqrt(ms + EPS) * gain).astype(BF16)


def _in_proj_kernel(normalize, h_ref, hs_ref, g_ref, w_ref, qn_ref, kn_ref,
                    u_ref, qkv_ref, gate_ref, us_ref, qkvs_ref, gates_s_ref, w_scr, *norm_scr):
    i = pl.program_id(0)
    j = pl.program_id(1)
    tm = h_ref.shape[0]
    sub = min(tm, ROW_SUB)

    if normalize:
        h_src, hs_src = norm_scr

        @pl.when(j == 0)
        def _():
            for r in range(tm // sub):
                rows = pl.ds(r * sub, sub)
                h_src[rows, :] = _rmsnorm_rows(h_ref[rows, :], g_ref[...])

        @pl.when((j == 0) & (i == 0))
        def _():
            hs_src[...] = _rmsnorm_rows(hs_ref[...], g_ref[...])
    else:
        h_src, hs_src = h_ref, hs_ref

    def row_blocks():
        w_scr[...] = w_ref[...].astype(BF16)
        for r in range(tm // sub):
            rows = pl.ds(r * sub, sub)
            yield rows, _dot(h_src[rows, :], w_scr[...])

    def sample_res():
        return _dot(hs_src[...], w_scr[...])

    @pl.when(j < _SEG_Q)
    def _():
        for rows, res in row_blocks():
            u_ref[rows, :] = res

        @pl.when(i == 0)
        def _():
            us_ref[j] = sample_res()

    def split_heads(res, gain_ref):
        if gain_ref is not None:
            return _head_norm(res, gain_ref[...])
        return [res[:, h * HEAD_DIM:(h + 1) * HEAD_DIM] for h in range(HEADS_PER_BLOCK)]

    def store_heads(gain_ref):
        for rows, res in row_blocks():
            for h, t in enumerate(split_heads(res, gain_ref)):
                qkv_ref[h, rows, :] = t

        @pl.when(i == 0)
        def _():
            base = (j - _SEG_Q) * HEADS_PER_BLOCK
            for h, t in enumerate(split_heads(sample_res(), gain_ref)):
                qkvs_ref[base + h] = t

    pl.when((j >= _SEG_Q) & (j < _SEG_K))(functools.partial(store_heads, qn_ref))
    pl.when((j >= _SEG_K) & (j < _SEG_V))(functools.partial(store_heads, kn_ref))
    pl.when((j >= _SEG_V) & (j < _SEG_GATE))(functools.partial(store_heads, None))

    @pl.when(j >= _SEG_GATE)
    def _():
        for rows, res in row_blocks():
            gate_ref[rows, :] = jax.nn.sigmoid(res).astype(gate_ref.dtype)

        @pl.when(i == 0)
        def _():
            gates_s_ref[j - _SEG_GATE] = jax.nn.sigmoid(sample_res()).astype(gates_s_ref.dtype)


def _in_proj(h, hs, norm_gain, w, layer, q_gain, k_gain, tm):
    M, Ms = h.shape[0], hs.shape[0]
    n_heads = 3 * N_ATT_HEADS
    n_gate = _SEG_END - _SEG_GATE
    normalize = norm_gain is not None
    gain = norm_gain if normalize else jnp.ones((D_MODEL,), F32)
    norm_scratch = [pltpu.VMEM((tm, D_MODEL), BF16), pltpu.VMEM((Ms, D_MODEL), BF16)] if normalize else []
    whole = lambda shape: pl.BlockSpec(shape, lambda i, j: (0,) * len(shape))
    u, qkv, gates, us, qkvs, gates_s = pl.pallas_call(
        functools.partial(_in_proj_kernel, normalize),
        grid=(M // tm, _SEG_END),
        in_specs=[
            pl.BlockSpec((tm, D_MODEL), lambda i, j: (i, 0), pipeline_mode=pl.Buffered(1)),
            whole((Ms, D_MODEL)),
            pl.BlockSpec((1, D_MODEL), lambda i, j: (0, 0)),
            pl.BlockSpec((None, D_MODEL, COL_BLOCK), lambda i, j: (layer, 0, j)),
            pl.BlockSpec((1, HEAD_DIM), lambda i, j: (0, 0)),
            pl.BlockSpec((1, HEAD_DIM), lambda i, j: (0, 0)),
        ],
        out_specs=[
            pl.BlockSpec((tm, COL_BLOCK), lambda i, j: (i, jnp.clip(j, 0, _SEG_Q - 1))),
            pl.BlockSpec((HEADS_PER_BLOCK, tm, HEAD_DIM),
                         lambda i, j: (jnp.clip(j - _SEG_Q, 0, _SEG_GATE - _SEG_Q - 1), i, 0)),
            pl.BlockSpec((tm, COL_BLOCK), lambda i, j: (i, jnp.clip(j - _SEG_GATE, 0, _SEG_END - _SEG_GATE - 1))),
            whole((_SEG_Q, Ms, COL_BLOCK)),
            whole((n_heads, Ms, HEAD_DIM)),
            whole((n_gate, Ms, COL_BLOCK)),
        ],
        out_shape=[
            jax.ShapeDtypeStruct((M, SSM_WIDTH), F32),
            jax.ShapeDtypeStruct((n_heads, M, HEAD_DIM), F32),
            jax.ShapeDtypeStruct((M, 2 * D_MODEL), BF16),
            jax.ShapeDtypeStruct((_SEG_Q, Ms, COL_BLOCK), F32),
            jax.ShapeDtypeStruct((n_heads, Ms, HEAD_DIM), F32),
            jax.ShapeDtypeStruct((n_gate, Ms, COL_BLOCK), BF16),
        ],
        scratch_shapes=[pltpu.VMEM((D_MODEL, COL_BLOCK), BF16)] + norm_scratch,
        compiler_params=_cparams(("arbitrary", "arbitrary")),
        name="in_proj",
    )(h, hs, gain.reshape(1, D_MODEL), w, q_gain.reshape(1, HEAD_DIM), k_gain.reshape(1, HEAD_DIM))
    us = jnp.swapaxes(us, 0, 1).reshape(Ms, SSM_WIDTH)
    gates_s = jnp.swapaxes(gates_s, 0, 1).reshape(Ms, 2 * D_MODEL)
    return (u, qkv, gates), (us, qkvs, gates_s)


def _expand_block_diag(e_ref, sel_ref, row_shift, col_shift, out_scr):
    n = out_scr.shape[0]
    for c in range(n // LANES):
        rows = pl.ds(c * LANES, LANES)
        full = _dot(e_ref[rows, :], sel_ref[...])
        row = lax.broadcasted_iota(jnp.int32, full.shape, 0) + c * LANES
        col = lax.broadcasted_iota(jnp.int32, full.shape, 1)
        keep = ((row >> row_shift) & (SSM_RANGE_GROUPS - 1)) == ((col >> col_shift) & (SSM_RANGE_GROUPS - 1))
        out_scr[rows, :] = jnp.where(keep, full, 0.0).astype(BF16)


def _ssm_chunk_kernel(groups, *refs):
    n = len(groups)
    ins, refs = refs[:2 * n], refs[2 * n:]
    (ek_ref, eb_ref, ea_ref, selk_ref, selb_ref, lnr_ref, lni_ref), refs = refs[:7], refs[7:]
    outs, refs = refs[:2 * n], refs[2 * n:]
    s_scrs, (ktoe_ref, bcat_ref, acat_ref) = refs[:n], refs[n:]
    lg_c = SSM_GROUP.bit_length() - 1
    lg_p = SSM_STATE.bit_length() - 1
    _expand_block_diag(ek_ref, selk_ref, lg_c, lg_c, ktoe_ref)
    _expand_block_diag(eb_ref, selb_ref, lg_c, lg_p, bcat_ref)
    _expand_block_diag(ea_ref, selk_ref, lg_p, lg_c, acat_ref)
    for g, (nseq, nk) in enumerate(groups):
        _ssm_chunk_group(nseq, nk, ins[2 * g], ins[2 * g + 1], ktoe_ref, bcat_ref, acat_ref, lnr_ref, lni_ref,
                         outs[2 * g], outs[2 * g + 1], s_scrs[g])


def _ssm_chunk_group(nseq, nk, u_ref, h0_ref, ktoe_ref, bcat_ref, acat_ref, lnr_ref, lni_ref,
                     y_ref, hfin_ref, s_scr):
    L = SSM_CHUNK
    rows = nseq * nk
    nsub = 2 * RANGE_STATE // LANES
    ucat = jnp.concatenate([u_ref[pl.ds(j, rows, stride=L), :] for j in range(L)], axis=-1).astype(BF16)
    s = _dot(ucat, bcat_ref[...])
    for n in range(nsub):
        s_scr[pl.ds(n, rows, stride=nsub), :] = s[:, n * LANES:(n + 1) * LANES]
    y_ref_intra = _dot(ucat, ktoe_ref[...])

    lnr = lnr_ref[...]
    lni = lni_ref[...]

    def step(k, hs):
        new = []
        for q in range(nseq):
            off = pl.multiple_of((q * nk + k) * nsub, nsub)
            h = hs[q]
            sk = s_scr[pl.ds(off, nsub), :]
            s_scr[pl.ds(off, nsub), :] = h
            new.append(h * lnr + pltpu.roll(h, nsub // 2, axis=0) * lni + sk)
        return tuple(new)

    hs = lax.fori_loop(0, nk, step, tuple(h0_ref[q] for q in range(nseq)))
    for q in range(nseq):
        hfin_ref[q] = hs[q]

    hprev = jnp.concatenate([s_scr[pl.ds(n, rows, stride=nsub), :] for n in range(nsub)], axis=-1).astype(BF16)
    y = y_ref_intra + _dot(hprev, acat_ref[...])
    for j in range(L):
        y_ref[pl.ds(j, rows, stride=L), :] = y[:, j * LANES:(j + 1) * LANES]


def _ssm_chunk(token_groups, ek, eb, ea, sel_k, sel_b, lnr, lni):
    R = N_SSM_RANGES
    nsub = 2 * RANGE_STATE // LANES
    wide = SSM_CHUNK * LANES
    assert wide == 2 * RANGE_STATE
    wspec = lambda shape: pl.BlockSpec((None,) + shape, lambda r: (r, 0, 0))
    sel_spec = pl.BlockSpec((LANES, wide), lambda r: (0, 0))
    groups, args, in_specs, out_specs, out_shape, scratch = [], [], [], [], [], []
    for u, h0, nseq in token_groups:
        M = u.shape[0]
        nk = M // nseq // SSM_CHUNK
        groups.append((nseq, nk))
        args += [u, h0]
        tok_spec = pl.BlockSpec((M, LANES), lambda r: (0, r))
        state_spec = pl.BlockSpec((nseq, None, nsub, LANES), lambda r: (0, r, 0, 0))
        in_specs += [tok_spec, state_spec]
        out_specs += [tok_spec, state_spec]
        out_shape += [jax.ShapeDtypeStruct((M, SSM_WIDTH), F32), jax.ShapeDtypeStruct((nseq, R, nsub, LANES), F32)]
        scratch.append(pltpu.VMEM((nseq * nk * nsub, LANES), F32))
    res = pl.pallas_call(
        functools.partial(_ssm_chunk_kernel, tuple(groups)),
        grid=(R,),
        in_specs=in_specs + [wspec((wide, LANES))] * 3 + [sel_spec, sel_spec] + [wspec((nsub, LANES))] * 2,
        out_specs=out_specs,
        out_shape=out_shape,
        scratch_shapes=scratch + [pltpu.VMEM((wide, wide), BF16)] * 3,
        compiler_params=_cparams(("arbitrary",)),
        name="ssm_chunk",
    )(*args, ek, eb, ea, sel_k, sel_b, lnr, lni)
    return [(res[2 * g], res[2 * g + 1]) for g in range(len(token_groups))]


def _state_to_tiles(re, im):
    N = re.shape[0]
    half = RANGE_STATE // LANES
    return jnp.concatenate([re.reshape(N, N_SSM_RANGES, half, LANES),
                            im.reshape(N, N_SSM_RANGES, half, LANES)], axis=2)


def _tiles_to_state(t):
    N = t.shape[0]
    half = RANGE_STATE // LANES
    return (t[:, :, :half].reshape(N, N_SSM_GROUPS, SSM_STATE),
            t[:, :, half:].reshape(N, N_SSM_GROUPS, SSM_STATE))


def _bucket_np(dist):
    max_exact = N_BUCKETS // 2
    n = np.maximum(dist, 0)
    nf = np.maximum(n, 1).astype(np.float64)
    large = max_exact + (np.log(nf / max_exact) / math.log(REL_MAX_DISTANCE / max_exact)
                         * (N_BUCKETS - max_exact)).astype(np.int32)
    large = np.minimum(large, N_BUCKETS - 1)
    return np.where(n < max_exact, n, large)


def _prompt_bucket_tile_np(group):
    window, dil = WINDOWS[group]
    a = np.arange(Q_BLOCK)[:, None]
    c = np.arange(2 * Q_BLOCK)[None, :]
    rel = a - c + Q_BLOCK
    K = window // dil + 1
    valid = (rel >= 0) & (rel < K)
    return np.where(valid, _bucket_np(np.clip(rel, 0, K - 1) * dil), -1)


def _prompt_buckets_present(group):
    tile = _prompt_bucket_tile_np(group)
    return [int(t) for t in np.unique(tile[tile >= 0])]


def _prompt_bucket_tiles():
    return jnp.asarray(np.stack([_prompt_bucket_tile_np(g) for g in range(N_ATT_GROUPS)]), jnp.int32)


def _attn_group_of_step(step):
    return (step + ATTN_FIRST_GROUP) % N_ATT_GROUPS


def _attn_prompt_kernel(T, tab_ref, q_ref, k_ref, v_ref, bkt_ref, o_ref,
                        m_scr, l_scr, acc_scr, p_scr, mb_scr, bias_scr):
    h = pl.program_id(1)
    step = pl.program_id(2)
    scale = HEAD_DIM ** -0.5
    nblk = T // Q_BLOCK

    def run_group(first, group):
        dil = WINDOWS[group][1]
        bkt = bkt_ref[...]
        col = group * HEADS_PER_GROUP + h
        bias = jnp.full(bkt.shape, NEG_INF, F32)
        for t in _prompt_buckets_present(group):
            bias = jnp.where(bkt == t, tab_ref[t, col], bias)
        bias_scr[...] = bias

        per_class = nblk // dil
        run = min(per_class, ATTN_UNROLL)
        runs_per_iter = ATTN_UNROLL // run
        whole_class = run == per_class

        def block_rows(r, n):
            return pl.ds(r + n * (Q_BLOCK * dil), Q_BLOCK, stride=dil)

        def runs(it):
            for j in range(runs_per_iter):
                i0 = it * ATTN_UNROLL + j * run
                yield i0, i0 // per_class, (0 if whole_class else i0 % per_class)

        def tiles(ref, r, n0, augment):
            out = []
            for u in range(-1, run):
                if u < 0 and whole_class:
                    out.append(None)
                    continue
                n = jnp.maximum(n0 + u, 0) if u < 0 else n0 + u
                t = ref[block_rows(r, n), :].astype(BF16)
                if augment:
                    t = jnp.concatenate([t, jnp.ones((Q_BLOCK, HEAD_DIM), BF16)], axis=1)
                out.append(t)
            return out

        def scores(it, carry):
            for i0, r, n0 in runs(it):
                kt = tiles(k_ref, r, n0, False)
                for u in range(run):
                    q = q_ref[block_rows(r, n0 + u), :].astype(BF16)
                    s_r = _dot_nt(q, kt[u + 1]) * scale + bias_scr[:, Q_BLOCK:]
                    if kt[u] is None:
                        s_l = jnp.full((Q_BLOCK, Q_BLOCK), NEG_INF, F32)
                    else:
                        bias_l = bias_scr[:, :Q_BLOCK]
                        if u == 0:
                            bias_l = jnp.where(n0 == 0, NEG_INF, bias_l)
                        s_l = _dot_nt(q, kt[u]) * scale + bias_l
                    m = jnp.maximum(jnp.max(s_l, axis=-1, keepdims=True), jnp.max(s_r, axis=-1, keepdims=True))
                    mb_scr[i0 + u] = jnp.broadcast_to(m, (Q_BLOCK, HEAD_DIM))
                    p_scr[i0 + u, :, :Q_BLOCK] = jnp.exp(s_l - m).astype(BF16)
                    p_scr[i0 + u, :, Q_BLOCK:] = jnp.exp(s_r - m).astype(BF16)
            return carry

        def values(it, carry):
            for i0, r, n0 in runs(it):
                vt = tiles(v_ref, r, n0, True)
                for u in range(run):
                    i = i0 + u
                    ol = _dot(p_scr[i, :, Q_BLOCK:], vt[u + 1])
                    if vt[u] is not None:
                        ol = ol + _dot(p_scr[i, :, :Q_BLOCK], vt[u])
                    o = ol[:, :HEAD_DIM]
                    lb = ol[:, HEAD_DIM:]
                    mb = mb_scr[i]
                    sl_q = block_rows(r, n0 + u)
                    if first:
                        m_scr[sl_q, :] = mb
                        l_scr[sl_q, :] = lb
                        acc_scr[sl_q, :] = o
                    else:
                        m0 = m_scr[sl_q, :]
                        mn = jnp.maximum(m0, mb)
                        a0 = jnp.exp(m0 - mn)
                        a1 = jnp.exp(mb - mn)
                        m_scr[sl_q, :] = mn
                        l_scr[sl_q, :] = a0 * l_scr[sl_q, :] + a1 * lb
                        acc_scr[sl_q, :] = a0 * acc_scr[sl_q, :] + a1 * o
            return carry

        lax.fori_loop(0, nblk // ATTN_UNROLL, scores, 0)
        lax.fori_loop(0, nblk // ATTN_UNROLL, values, 0)

    for s in range(N_ATT_GROUPS):
        pl.when(step == s)(functools.partial(run_group, s == 0, (s + ATTN_FIRST_GROUP) % N_ATT_GROUPS))

    @pl.when(step == N_ATT_GROUPS - 1)
    def _():
        o_ref[...] = (acc_scr[...] / l_scr[...]).astype(o_ref.dtype)


def _attn_prompt(qkv, rel_bias, bucket_tiles, nbatch):
    M = qkv.shape[1]
    T = M // nbatch
    H, G = HEADS_PER_GROUP, N_ATT_GROUPS
    nblk = T // Q_BLOCK

    def qkv_spec(which):
        return pl.BlockSpec((None, T, HEAD_DIM),
                            lambda b, h, s: (which * N_ATT_HEADS + _attn_group_of_step(s) * H + h, b, 0))

    return pl.pallas_call(
        functools.partial(_attn_prompt_kernel, T),
        grid=(nbatch, H, G),
        in_specs=[pl.BlockSpec(memory_space=pltpu.SMEM),
                  qkv_spec(0), qkv_spec(1), qkv_spec(2),
                  pl.BlockSpec((None, Q_BLOCK, 2 * Q_BLOCK), lambda b, h, s: (_attn_group_of_step(s), 0, 0))],
        out_specs=pl.BlockSpec((T, HEAD_DIM), lambda b, h, s: (b, h)),
        out_shape=jax.ShapeDtypeStruct((M, ATT_WIDTH), BF16),
        scratch_shapes=[pltpu.VMEM((T, HEAD_DIM), F32)] * 3
                       + [pltpu.VMEM((nblk, Q_BLOCK, 2 * Q_BLOCK), BF16),
                          pltpu.VMEM((nblk, Q_BLOCK, HEAD_DIM), F32),
                          pltpu.VMEM((Q_BLOCK, 2 * Q_BLOCK), F32)],
        compiler_params=_cparams(("arbitrary", "arbitrary", "arbitrary")),
        name="attn_prompt",
    )(rel_bias.astype(F32), qkv, qkv, qkv, bucket_tiles)


def _attn_sample_kernel(window, dil, q_ref, ck_ref, cv_ref, nk_ref, nv_ref, bias_ref, o_ref, lse_ref):
    S = q_ref.shape[0]
    K = window // dil + 1
    scale = HEAD_DIM ** -0.5
    bias = bias_ref[...]
    by_residue = len(ck_ref.shape) == 4

    def buffered(ref, s, n):
        return ref[pl.ds(0, n), s] if by_residue else ref[pl.ds(s, n, stride=dil)]

    for s in range(S):
        n_c = (window - 1 - s) // dil + 1
        qs = q_ref[s]
        new_rows = [s + j * dil - window for j in range(n_c, K)]
        kk = jnp.concatenate([buffered(ck_ref, s, n_c)] + [nk_ref[pl.ds(i, 1)] for i in new_rows], axis=0)
        vv = jnp.concatenate([buffered(cv_ref, s, n_c)] + [nv_ref[pl.ds(i, 1)] for i in new_rows], axis=0)
        lg = jnp.sum(kk * qs[None], axis=-1, keepdims=True) * scale + bias
        m = jnp.max(lg, axis=0)
        p = jnp.exp(lg - m[None])
        l = jnp.sum(p, axis=0)
        o_ref[s] = jnp.sum(p * vv, axis=0) / l
        lse_ref[s] = m + jnp.log(l)


def _attn_sample(q, cache_k, cache_v, new_k, new_v, bias, layer, window, dil):
    B, S = q.shape[0], q.shape[1]
    H = HEADS_PER_GROUP
    K = window // dil + 1
    small = pl.BlockSpec((None, S, H, HEAD_DIM), lambda b: (b, 0, 0, 0))
    if dil > S:
        depth = cache_k.shape[0]
        cache_k = cache_k.reshape(depth, B, window // dil, dil, H, HEAD_DIM)
        cache_v = cache_v.reshape(depth, B, window // dil, dil, H, HEAD_DIM)
        cache = pl.BlockSpec((None, None, window // dil, S, H, HEAD_DIM), lambda b: (layer, b, 0, 0, 0, 0))
    else:
        cache = pl.BlockSpec((None, None, window, H, HEAD_DIM), lambda b: (layer, b, 0, 0, 0))
    return pl.pallas_call(
        functools.partial(_attn_sample_kernel, window, dil),
        grid=(B,),
        in_specs=[small, cache, cache, small, small, pl.BlockSpec((K, H, HEAD_DIM), lambda b: (0, 0, 0))],
        out_specs=[small, small],
        out_shape=[jax.ShapeDtypeStruct((B, S, H, HEAD_DIM), F32)] * 2,
        compiler_params=_cparams(("arbitrary",)),
        name="attn_sample_w%d" % window,
    )(q, cache_k, cache_v, new_k, new_v, bias)


def _merge_groups_kernel(o0, o1, o2, l0, l1, l2, y_ref):
    a, b, c = l0[...], l1[...], l2[...]
    m = jnp.maximum(jnp.maximum(a, b), c)
    ea, eb, ec = jnp.exp(a - m), jnp.exp(b - m), jnp.exp(c - m)
    y_ref[...] = ((ea * o0[...] + eb * o1[...] + ec * o2[...]) / (ea + eb + ec)).astype(y_ref.dtype)


def _merge_groups(outs, lses):
    shape = outs[0].shape
    return pl.pallas_call(
        _merge_groups_kernel,
        out_shape=jax.ShapeDtypeStruct(shape, BF16),
        name="merge_groups",
    )(*outs, *lses)


def _sample_bias(rel_bias):
    out = []
    for g, (window, dil) in enumerate(WINDOWS):
        K = window // dil + 1
        steps = (K - 1) - np.arange(K)
        tab = rel_bias[:, g * HEADS_PER_GROUP:(g + 1) * HEADS_PER_GROUP].astype(F32)
        b = tab[_bucket_np(steps * dil)]
        out.append(jnp.broadcast_to(b[:, :, None], (K, HEADS_PER_GROUP, HEAD_DIM)))
    return out


def _gelu_tanh(x):
    return 0.5 * x * (1.0 + jnp.tanh(math.sqrt(2.0 / math.pi) * (x + 0.044715 * (x * x * x))))


def _mix_out_kernel(y_ref, u_ref, yb_ref, ga_ref, gb_ref, x_ref, d_ref, wglu_ref, bglu_ref, wa_ref, wb_ref,
                    wout_ref, gn_ref, x1_ref, h2_ref):
    tm = x_ref.shape[0]
    sub = min(tm, MIX_ROW_SUB)
    for r in range(tm // sub):
        rows = pl.ds(r * sub, sub)
        y = y_ref[rows, :] + d_ref[...] * u_ref[rows, :]
        z = _gelu_tanh(y)
        ya = z * jax.nn.sigmoid(_dot(z.astype(BF16), wglu_ref[...]) + bglu_ref[...])
        mix = (ga_ref[rows, :] * _dot(ya.astype(BF16), wa_ref[...])
               + gb_ref[rows, :] * _dot(yb_ref[rows, :], wb_ref[...]))
        x1 = x_ref[rows, :] + _dot(mix.astype(BF16), wout_ref[...])
        x1_ref[rows, :] = x1
        h2_ref[rows, :] = _rmsnorm_rows(x1, gn_ref[...])


def _mix_out(y_ssm, u, yb, gates, x, ssm_d, wglu, bglu, wa, wb, wout, layer, gain, tm):
    M = x.shape[0]
    row = lambda w: pl.BlockSpec((tm, w), lambda i: (i, 0))
    vec = lambda b: pl.BlockSpec((1, b), lambda i: (0, 0), pipeline_mode=pl.Buffered(1))
    full = lambda a, b: pl.BlockSpec((None, a, b), lambda i: (layer, 0, 0), pipeline_mode=pl.Buffered(1))
    return pl.pallas_call(
        _mix_out_kernel,
        grid=(M // tm,),
        in_specs=[row(SSM_WIDTH), row(SSM_WIDTH), row(ATT_WIDTH),
                  pl.BlockSpec((tm, D_MODEL), lambda i: (i, 0)),
                  pl.BlockSpec((tm, D_MODEL), lambda i: (i, 1)),
                  row(D_MODEL),
                  vec(SSM_WIDTH), full(SSM_WIDTH, SSM_WIDTH), vec(SSM_WIDTH),
                  full(SSM_WIDTH, D_MODEL), full(ATT_WIDTH, D_MODEL), full(D_MODEL, D_MODEL), vec(D_MODEL)],
        out_specs=[row(D_MODEL), row(D_MODEL)],
        out_shape=[jax.ShapeDtypeStruct((M, D_MODEL), F32), jax.ShapeDtypeStruct((M, D_MODEL), BF16)],
        compiler_params=_cparams(("arbitrary",)),
        name="mix_out",
    )(y_ssm, u, yb, gates, gates, x, ssm_d.reshape(1, -1), wglu, bglu.reshape(1, -1), wa, wb, wout,
      gain.reshape(1, -1))


def _ffn_kernel(emit_norm, h_ref, x_ref, hs_ref, xs_ref, wg_ref, wu_ref, wd_ref, gn_ref, *rest):
    if emit_norm:
        o_ref, os_ref, hn_ref, hns_ref, wg_scr, wu_scr, wd_scr = rest
    else:
        o_ref, os_ref, wg_scr, wu_scr, wd_scr = rest
    i = pl.program_id(0)
    f = pl.program_id(1)
    last = pl.num_programs(1) - 1

    @pl.when(f == 0)
    def _():
        o_ref[...] = x_ref[...]

    @pl.when((f == 0) & (i == 0))
    def _():
        os_ref[...] = xs_ref[...]

    def swiglu(h):
        a = jax.nn.silu(_dot(h, wg_scr[...])) * _dot(h, wu_scr[...])
        return _dot(a.astype(BF16), wd_scr[...])

    tm = h_ref.shape[0]
    sub = min(tm, FFN_ROW_SUB)
    wg_scr[...] = wg_ref[...].astype(BF16)
    wu_scr[...] = wu_ref[...].astype(BF16)
    wd_scr[...] = wd_ref[...].astype(BF16)
    for r in range(tm // sub):
        rows = pl.ds(r * sub, sub)
        o_ref[rows, :] += swiglu(h_ref[rows, :])

    @pl.when(i == 0)
    def _():
        os_ref[...] += swiglu(hs_ref[...])

    if emit_norm:
        @pl.when(f == last)
        def _():
            for r in range(tm // sub):
                rows = pl.ds(r * sub, sub)
                hn_ref[rows, :] = _rmsnorm_rows(o_ref[rows, :], gn_ref[...])

        @pl.when((f == last) & (i == 0))
        def _():
            hns_ref[...] = _rmsnorm_rows(os_ref[...], gn_ref[...])


def _ffn(h2, x1, h2s, x1s, wg, wu, wd, layer, next_gain, tm):
    M, Ms = x1.shape[0], x1s.shape[0]
    nf = FFN_HIDDEN // FFN_BLOCK
    emit_norm = next_gain is not None
    gain = next_gain if emit_norm else jnp.ones((D_MODEL,), F32)
    row = pl.BlockSpec((tm, D_MODEL), lambda i, f: (i, 0))
    whole = pl.BlockSpec((Ms, D_MODEL), lambda i, f: (0, 0))
    out_specs = [row, whole]
    out_shape = [jax.ShapeDtypeStruct((M, D_MODEL), F32), jax.ShapeDtypeStruct((Ms, D_MODEL), F32)]
    if emit_norm:
        out_specs += [row, whole]
        out_shape += [jax.ShapeDtypeStruct((M, D_MODEL), BF16), jax.ShapeDtypeStruct((Ms, D_MODEL), BF16)]
    return pl.pallas_call(
        functools.partial(_ffn_kernel, emit_norm),
        grid=(M // tm, nf),
        in_specs=[row,
                  pl.BlockSpec((tm, D_MODEL), lambda i, f: (i, 0), pipeline_mode=pl.Buffered(1)),
                  whole, whole,
                  pl.BlockSpec((None, D_MODEL, FFN_BLOCK), lambda i, f: (layer, 0, f)),
                  pl.BlockSpec((None, D_MODEL, FFN_BLOCK), lambda i, f: (layer, 0, f)),
                  pl.BlockSpec((None, FFN_BLOCK, D_MODEL), lambda i, f: (layer, f, 0)),
                  pl.BlockSpec((1, D_MODEL), lambda i, f: (0, 0))],
        out_specs=out_specs,
        out_shape=out_shape,
        scratch_shapes=[pltpu.VMEM((D_MODEL, FFN_BLOCK), BF16), pltpu.VMEM((D_MODEL, FFN_BLOCK), BF16),
                        pltpu.VMEM((FFN_BLOCK, D_MODEL), BF16)],
        compiler_params=_cparams(("arbitrary", "arbitrary")),
        name="ffn",
    )(h2, x1, h2s, x1s, wg, wu, wd, gain.reshape(1, D_MODEL))


def _cache_shift_kernel(ck_ref, cv_ref, hk_ref, hv_ref, nk_ref, nv_ref, ok_ref, ov_ref):
    c = pl.program_id(2)
    last = pl.num_programs(2) - 1
    R = ck_ref.shape[0]
    S = nk_ref.shape[0]
    for cache, halo, new, out in ((ck_ref, hk_ref, nk_ref, ok_ref), (cv_ref, hv_ref, nv_ref, ov_ref)):
        out[pl.ds(0, R - S)] = cache[pl.ds(S, R - S)]

        @pl.when(c == last)
        def _():
            out[pl.ds(R - S, S)] = new[...]

        @pl.when(c < last)
        def _():
            out[pl.ds(R - S, S)] = halo[...]


def _cache_shift(cache_k, cache_v, new_k, new_v):
    depth, B, W, H, E = cache_k.shape
    S = new_k.shape[2]
    R = min(W, CACHE_ROWS)
    nchunks = W // R
    blk = pl.BlockSpec((None, None, R, H, E), lambda l, b, c: (l, b, c, 0, 0))
    halo = pl.BlockSpec((None, None, None, S, H, E),
                        lambda l, b, c: (l, b, jnp.minimum((c + 1) * (R // S), W // S - 1), 0, 0, 0))
    new = pl.BlockSpec((None, None, S, H, E), lambda l, b, c: (l, b, 0, 0, 0))
    as_rows = lambda a: a.reshape(depth, B, W // S, S, H, E)
    return pl.pallas_call(
        _cache_shift_kernel,
        grid=(depth, B, nchunks),
        in_specs=[blk, blk, halo, halo, new, new],
        out_specs=[blk, blk],
        out_shape=[jax.ShapeDtypeStruct(cache_k.shape, cache_k.dtype)] * 2,
        compiler_params=_cparams(("arbitrary", "arbitrary", "arbitrary")),
        name="cache_shift_w%d" % W,
    )(cache_k, cache_v, as_rows(cache_k), as_rows(cache_v), new_k, new_v)


def _kv_tails_kernel(*refs):
    depth = (len(refs) - 2) // 2
    ok_ref, ov_ref = refs[-2:]
    H, R = refs[0].shape[0], refs[0].shape[1]
    for l in range(depth):
        @pl.when(pl.program_id(0) == l)
        def _():
            for src, out in ((refs[2 * l], ok_ref), (refs[2 * l + 1], ov_ref)):
                for h in range(H):
                    out[pl.ds(h, R, stride=H), :] = src[h]


def _kv_tails(qkv_layers, g, window, nbatch):
    depth = len(qkv_layers)
    M = qkv_layers[0].shape[1]
    T = M // nbatch
    H, E = HEADS_PER_GROUP, HEAD_DIM
    R = min(window, CACHE_ROWS)
    nchunks = window // R
    first = (T - window) // R

    def src(layer, which):
        def index(l, b, c):
            bb = jnp.where(l < layer, 0, jnp.where(l > layer, nbatch - 1, b))
            cc = jnp.where(l < layer, 0, jnp.where(l > layer, nchunks - 1, c))
            return (which * N_ATT_GROUPS + g, bb * (T // R) + first + cc, 0)
        return pl.BlockSpec((H, R, E), index)

    args, in_specs = [], []
    for layer, qkv in enumerate(qkv_layers):
        args += [qkv, qkv]
        in_specs += [src(layer, 1), src(layer, 2)]
    out_spec = pl.BlockSpec((None, R * H, E), lambda l, b, c: (l, b * nchunks + c, 0))
    return pl.pallas_call(
        _kv_tails_kernel,
        grid=(depth, nbatch, nchunks),
        in_specs=in_specs,
        out_specs=[out_spec, out_spec],
        out_shape=[jax.ShapeDtypeStruct((depth, nbatch * window * H, E), qkv_layers[0].dtype)] * 2,
        compiler_params=_cparams(("arbitrary", "arbitrary", "arbitrary")),
        name="kv_tails_w%d" % window,
    )(*args)


def _mix(x2d, u, y_ssm, yb, gates, prm, tm):
    return _mix_out(y_ssm, u, yb, gates, x2d, prm["ssm_d"], prm["w_glu"], prm["b_glu"], prm["w_branch_a"],
                    prm["w_branch_b"], prm["w_out"], prm["layer"], prm["norm_ffn"], tm)


def kernel(x_prompt, x_sample, state_ssm_re, state_ssm_im, cache_k_w128, cache_v_w128, cache_k_w512, cache_v_w512,
           cache_k_w2048, cache_v_w2048, rel_bias, norm_mix, norm_ffn, q_norm, k_norm, w_in, ssm_lambda_re,
           ssm_lambda_im, ssm_log_dt, ssm_b_re, ssm_b_im, ssm_c_re, ssm_c_im, ssm_d, w_glu, b_glu, w_branch_a,
           w_branch_b, w_out, w_ffn_gate, w_ffn_up, w_ffn_down):
    depth = w_in.shape[0]
    B, T, _ = x_prompt.shape
    SB, S, _ = x_sample.shape
    H = HEADS_PER_GROUP
    caches_k = (cache_k_w128, cache_k_w512, cache_k_w2048)
    caches_v = (cache_v_w128, cache_v_w512, cache_v_w2048)

    bucket_tiles = _prompt_bucket_tiles()
    bias_sample = _sample_bias(rel_bias)
    sel_k, sel_b = _ssm_select_matrices()

    xp = x_prompt.reshape(B * T, D_MODEL)
    xs = x_sample.reshape(SB * S, D_MODEL)
    p_re, p_im, s_re, s_im = [], [], [], []
    qkv_prompt = []
    new_k = [[] for _ in WINDOWS]
    new_v = [[] for _ in WINDOWS]

    weights = {
        "w_glu": w_glu.astype(BF16), "w_branch_a": w_branch_a.astype(BF16),
        "w_branch_b": w_branch_b.astype(BF16), "w_out": w_out.astype(BF16),
    }

    hp, hs = xp, xs

    for l in range(depth):
        ek, eb, ea, lnr, lni = _ssm_prep(ssm_lambda_re[l], ssm_lambda_im[l], ssm_log_dt[l], ssm_b_re[l],
                                         ssm_b_im[l], ssm_c_re[l], ssm_c_im[l])
        prm = dict(weights)
        prm.update({
            "layer": l, "norm_ffn": norm_ffn[l], "ssm_d": ssm_d[l], "b_glu": b_glu[l],
        })

        (u, qkv, gates), (us, qkvs, gates_s) = _in_proj(hp, hs, norm_mix[0] if l == 0 else None, w_in, l,
                                                        q_norm[l], k_norm[l], IN_PROJ_ROWS)

        zeros = jnp.zeros((B, N_SSM_RANGES, 2 * RANGE_STATE // LANES, LANES), F32)
        h0 = _state_to_tiles(state_ssm_re[l], state_ssm_im[l])
        (y_ssm, hfin), (ys_ssm, hfin_s) = _ssm_chunk([(u, zeros, B), (us, h0, SB)], ek, eb, ea, sel_k, sel_b,
                                                     lnr, lni)

        yb = _attn_prompt(qkv, rel_bias, bucket_tiles, B)
        x1, h2 = _mix(xp, u, y_ssm, yb, gates, prm, MIX_ROWS)
        hr, hi = _tiles_to_state(hfin)
        p_re.append(hr)
        p_im.append(hi)
        qkv_prompt.append(qkv)

        tok = jnp.transpose(qkvs.reshape(3, N_ATT_GROUPS, H, SB, S, HEAD_DIM), (0, 1, 3, 4, 2, 5))
        outs, lses = [], []
        for g, (window, dil) in enumerate(WINDOWS):
            nk_g, nv_g = tok[1, g], tok[2, g]
            new_k[g].append(nk_g)
            new_v[g].append(nv_g)
            o_g, lse_g = _attn_sample(tok[0, g], caches_k[g], caches_v[g], nk_g, nv_g, bias_sample[g], l,
                                      window, dil)
            outs.append(o_g)
            lses.append(lse_g)
        ybs = _merge_groups(outs, lses).reshape(SB * S, ATT_WIDTH)
        x1s, h2s = _mix(xs, us, ys_ssm, ybs, gates_s, prm, SB * S)
        hr, hi = _tiles_to_state(hfin_s)
        s_re.append(hr)
        s_im.append(hi)

        next_gain = norm_mix[l + 1] if l + 1 < depth else None
        res = _ffn(h2, x1, h2s, x1s, w_ffn_gate, w_ffn_up, w_ffn_down, l, next_gain, FFN_ROWS)
        if next_gain is None:
            xp, xs = res
        else:
            xp, xs, hp, hs = res

    shifted = []
    for g in range(N_ATT_GROUPS):
        shifted += _cache_shift(caches_k[g], caches_v[g], jnp.stack(new_k[g]), jnp.stack(new_v[g]))
    tails = [t.reshape(depth, B, window, H, HEAD_DIM)
             for g, (window, _) in enumerate(WINDOWS) for t in _kv_tails(qkv_prompt, g, window, B)]

    return (xp.reshape(B, T, D_MODEL), xs.reshape(SB, S, D_MODEL),
            jnp.stack(p_re), jnp.stack(p_im), *tails,
            jnp.stack(s_re), jnp.stack(s_im), *shifted)
```
